```python
import jax, jax.numpy as jnp
from jax import lax
import numpy as np

D_MODEL = 2048
BATCH = 1
SEQ = 8192
DEPTH = 1

CHUNK = 64
N_MEM = 256
LN_EPS = 1e-5
ALPHA = (2 * DEPTH) ** 0.25
BETA = (8 * DEPTH) ** -0.25

D_LRU = D_MODEL
LRU_HEADS = 16
LRU_HEAD_DIM = D_LRU // LRU_HEADS
CONV_WIDTH = 4
LRU_C = 8.0

D_RWKV = D_MODEL
RWKV_HEAD_DIM = 64
RWKV_HEADS = D_RWKV // RWKV_HEAD_DIM
D_DECAY_LORA = 96
D_AAA_LORA = 96
D_GATE_LORA = 256
GN_EPS = 64e-5
D_RWKV_IN = 3 * D_RWKV + D_DECAY_LORA + D_AAA_LORA + D_GATE_LORA

N_IN = 2 * D_LRU + D_RWKV_IN + 2 * D_MODEL

XATTN_HEADS = 4
XATTN_HEAD_DIM = D_MODEL // XATTN_HEADS

D_FF = -(-(8 * D_MODEL) // (3 * 256)) * 256

kernel_name = 'hybrid_rglru_rwkv7_deepnorm_encoder'


def layer_norm(x, g, b, eps=LN_EPS):
    xf = x.astype(jnp.float32)
    mu = jnp.mean(xf, axis=-1, keepdims=True)
    var = jnp.mean(jnp.square(xf - mu), axis=-1, keepdims=True)
    y = (xf - mu) * lax.rsqrt(var + eps) * g.astype(jnp.float32) + b.astype(jnp.float32)
    return y.astype(x.dtype)


def causal_depthwise_conv(u, w, b):
    t = u.shape[1]
    up = jnp.pad(u, ((0, 0), (CONV_WIDTH - 1, 0), (0, 0)))
    out = b
    for j in range(CONV_WIDTH):
        out = out + up[:, j:j + t] * w[j]
    return out


def chunked_linear_scan(a, b):
    bsz, t, c = a.shape
    nc = t // CHUNK
    a = a.reshape(bsz, nc, CHUNK, c)
    b = b.reshape(bsz, nc, CHUNK, c)

    def combine(lo, hi):
        return (lo[0] * hi[0], hi[0] * lo[1] + hi[1])

    a_cum, h_loc = lax.associative_scan(combine, (a, b), axis=2)
    _, h_end = lax.associative_scan(combine, (a_cum[:, :, -1], h_loc[:, :, -1]), axis=1)
    h_in = jnp.pad(h_end[:, :-1], ((0, 0), (1, 0), (0, 0)))
    return (h_loc + a_cum * h_in[:, :, None]).reshape(bsz, t, c)


def rg_lru(u, wa, ba, wx, bx, lam):
    bsz, t, _ = u.shape
    uh = u.reshape(bsz, t, LRU_HEADS, LRU_HEAD_DIM)
    r = jax.nn.sigmoid(jnp.einsum('btgi,gij->btgj', uh, wa) + ba).reshape(bsz, t, D_LRU)
    i = jax.nn.sigmoid(jnp.einsum('btgi,gij->btgj', uh, wx) + bx).reshape(bsz, t, D_LRU)
    log_a = -LRU_C * r.astype(jnp.float32) * jax.nn.softplus(-lam.astype(jnp.float32))
    a = jnp.exp(log_a)
    gated_x = jnp.sqrt(-jnp.expm1(2.0 * log_a)) * (i * u).astype(jnp.float32)
    return chunked_linear_scan(a, gated_x)


def rwkv7_time_mix(z, mu, w0, wB, a0, aB, gB, k_k, k_a, r_k, gn_g, gn_b):
    f32 = jnp.float32
    bsz, t, _ = z.shape
    z_prev = jnp.pad(z, ((0, 0), (1, 0), (0, 0)))[:, :-1]
    z = z + (z_prev - z) * mu
    r, k, v, w_lo, a_lo, g_lo = jnp.split(
        z, [D_RWKV, 2 * D_RWKV, 3 * D_RWKV, 3 * D_RWKV + D_DECAY_LORA,
            3 * D_RWKV + D_DECAY_LORA + D_AAA_LORA], axis=-1)
    w_log = -jax.nn.softplus(-(w0 + jnp.tanh(w_lo) @ wB).astype(f32)) - 0.5
    decay = jnp.exp(-jnp.exp(w_log))
    a = jax.nn.sigmoid((a0 + a_lo @ aB).astype(f32))
    g = jax.nn.sigmoid(g_lo) @ gB
    kf = k.astype(f32)
    kk = (kf * k_k.astype(f32)).reshape(bsz, t, RWKV_HEADS, RWKV_HEAD_DIM)
    kk = kk / jnp.maximum(jnp.linalg.norm(kk, axis=-1, keepdims=True), 1e-12)
    kf = kf * (1.0 + (a - 1.0) * k_a.astype(f32))

    def heads(y):
        return y.astype(f32).reshape(bsz, t, RWKV_HEADS, RWKV_HEAD_DIM)

    r_h, k_h, v_h, w_h, a_h = heads(r), heads(kf), heads(v), heads(decay), heads(a)
    b_h = kk * a_h

    def step(S, inp):
        r_t, w_t, k_t, v_t, kk_t, b_t = inp
        sa = jnp.einsum('bhvk,bhk->bhv', S, kk_t)
        S = S * w_t[:, :, None, :] - sa[..., None] * b_t[:, :, None, :] + v_t[..., None] * k_t[:, :, None, :]
        return S, jnp.einsum('bhvk,bhk->bhv', S, r_t)

    S0 = jnp.zeros((bsz, RWKV_HEADS, RWKV_HEAD_DIM, RWKV_HEAD_DIM), f32)
    xs = tuple(jnp.moveaxis(y, 1, 0) for y in (r_h, w_h, k_h, v_h, kk, b_h))
    _, o = lax.scan(step, S0, xs)
    o = jnp.moveaxis(o, 0, 1)
    m = jnp.mean(o, axis=-1, keepdims=True)
    var = jnp.mean(jnp.square(o - m), axis=-1, keepdims=True)
    o = (o - m) * lax.rsqrt(var + GN_EPS) * gn_g.astype(f32).reshape(RWKV_HEADS, RWKV_HEAD_DIM) \
        + gn_b.astype(f32).reshape(RWKV_HEADS, RWKV_HEAD_DIM)
    o = o + jnp.sum(r_h * k_h * r_k.astype(f32), axis=-1, keepdims=True) * v_h
    return (o.reshape(bsz, t, D_RWKV) * g.astype(f32)).astype(z.dtype)


def hybrid_mixer(h, w_in, conv_w, conv_b, lru_wa, lru_ba, lru_wx, lru_bx, lru_lambda,
                 rw_mu, rw_w0, rw_wB, rw_a0, rw_aB, rw_gB, rw_kk, rw_ka, rw_rk,
                 rw_gn_g, rw_gn_b, w_out):
    zin = h @ w_in
    u, gate_lru, z_rw, g_a, g_b = jnp.split(
        zin, [D_LRU, 2 * D_LRU, 2 * D_LRU + D_RWKV_IN, 2 * D_LRU + D_RWKV_IN + D_MODEL], axis=-1)
    u = causal_depthwise_conv(u, conv_w, conv_b)
    y_a = jax.nn.gelu(gate_lru) * rg_lru(u, lru_wa, lru_ba, lru_wx, lru_bx, lru_lambda).astype(h.dtype)
    y_b = rwkv7_time_mix(z_rw, rw_mu, rw_w0, rw_wB, rw_a0, rw_aB, rw_gB, rw_kk, rw_ka, rw_rk,
                         rw_gn_g, rw_gn_b)
    y = jax.nn.sigmoid(g_a) * y_a + jax.nn.sigmoid(g_b) * y_b
    return y @ w_out


def memory_cross_attention(h, mem, wq, wk, wv, wo):
    bsz, t, _ = h.shape
    q = (h @ wq).reshape(bsz, t, XATTN_HEADS, XATTN_HEAD_DIM)
    k = (mem @ wk).reshape(bsz, N_MEM, XATTN_HEADS, XATTN_HEAD_DIM)
    v = (mem @ wv).reshape(bsz, N_MEM, XATTN_HEADS, XATTN_HEAD_DIM)
    s = jnp.einsum('bthd,bmhd->bhtm', q, k).astype(jnp.float32) * (XATTN_HEAD_DIM ** -0.5)
    p = jax.nn.softmax(s, axis=-1).astype(h.dtype)
    o = jnp.einsum('bhtm,bmhd->bthd', p, v).reshape(bsz, t, D_MODEL)
    return o @ wo


def swiglu(h, wg, wu, wd):
    return (jax.nn.silu(h @ wg) * (h @ wu)) @ wd


def setup_inputs(seed: int = 0) -> dict:
    key = jax.random.key(seed)
    ks = iter(jax.random.split(key, 48))
    f32 = jnp.float32
    L = DEPTH

    def nrm(shape, scale):
        return scale * jax.random.normal(next(ks), shape, f32)

    def gain(shape):
        return 1.0 + nrm(shape, 0.02)

    u = jax.random.uniform(next(ks), (L, D_LRU), f32, 0.9, 0.999)
    a_init = u ** (1.0 / LRU_C)
    lru_lambda = jnp.log(a_init) - jnp.log1p(-a_init)
    n = jnp.arange(D_RWKV, dtype=f32) / (D_RWKV - 1)
    rw_w0 = -6.5 + 5.0 * n ** 0.85 + nrm((L, D_RWKV), 0.1)
    rw_mu = jax.random.uniform(next(ks), (L, D_RWKV_IN), f32)
    return {
        'x': nrm((BATCH, SEQ, D_MODEL), 1.0),
        'mem': nrm((BATCH, N_MEM, D_MODEL), 1.0),
        'ln_in_g': gain((D_MODEL,)),
        'ln_in_b': nrm((D_MODEL,), 0.02),
        'w_in': nrm((L, D_MODEL, N_IN), D_MODEL ** -0.5),
        'conv_w': nrm((L, CONV_WIDTH, D_LRU), CONV_WIDTH ** -0.5),
        'conv_b': nrm((L, D_LRU), 0.02),
        'lru_wa': nrm((L, LRU_HEADS, LRU_HEAD_DIM, LRU_HEAD_DIM), LRU_HEAD_DIM ** -0.5),
        'lru_ba': nrm((L, LRU_HEADS, LRU_HEAD_DIM), 0.02),
        'lru_wx': nrm((L, LRU_HEADS, LRU_HEAD_DIM, LRU_HEAD_DIM), LRU_HEAD_DIM ** -0.5),
        'lru_bx': nrm((L, LRU_HEADS, LRU_HEAD_DIM), 0.02),
        'lru_lambda': lru_lambda,
        'rw_mu': rw_mu,
        'rw_w0': rw_w0,
        'rw_wB': nrm((L, D_DECAY_LORA, D_RWKV), 0.5 * D_DECAY_LORA ** -0.5),
        'rw_a0': nrm((L, D_RWKV), 0.1),
        'rw_aB': nrm((L, D_AAA_LORA, D_RWKV), 0.5 * D_AAA_LORA ** -0.5),
        'rw_gB': nrm((L, D_GATE_LORA, D_RWKV), D_GATE_LORA ** -0.5),
        'rw_kk': 0.85 + nrm((L, D_RWKV), 0.02),
        'rw_ka': gain((L, D_RWKV)),
        'rw_rk': nrm((L, RWKV_HEADS, RWKV_HEAD_DIM), 0.1),
        'rw_gn_g': gain((L, D_RWKV)),
        'rw_gn_b': nrm((L, D_RWKV), 0.02),
        'w_out': nrm((L, D_MODEL, D_MODEL), BETA * D_MODEL ** -0.5),
        'ln1_g': gain((L, D_MODEL)),
        'ln1_b': nrm((L, D_MODEL), 0.02),
        'xa_wq': nrm((L, D_MODEL, D_MODEL), D_MODEL ** -0.5),
        'xa_wk': nrm((L, D_MODEL, D_MODEL), D_MODEL ** -0.5),
        'xa_wv': nrm((L, D_MODEL, D_MODEL), D_MODEL ** -0.5),
        'xa_wo': nrm((L, D_MODEL, D_MODEL), BETA * D_MODEL ** -0.5),
        'ln2_g': gain((L, D_MODEL)),
        'ln2_b': nrm((L, D_MODEL), 0.02),
        'ffn_wg': nrm((L, D_MODEL, D_FF), D_MODEL ** -0.5),
        'ffn_wu': nrm((L, D_MODEL, D_FF), D_MODEL ** -0.5),
        'ffn_wd': nrm((L, D_FF, D_MODEL), BETA * D_FF ** -0.5),
        'ln3_g': gain((L, D_MODEL)),
        'ln3_b': nrm((L, D_MODEL), 0.02),
    }


def reference(x, mem, ln_in_g, ln_in_b, w_in, conv_w, conv_b, lru_wa, lru_ba, lru_wx, lru_bx,
              lru_lambda, rw_mu, rw_w0, rw_wB, rw_a0, rw_aB, rw_gB, rw_kk, rw_ka, rw_rk,
              rw_gn_g, rw_gn_b, w_out, ln1_g, ln1_b, xa_wq, xa_wk, xa_wv, xa_wo, ln2_g, ln2_b,
              ffn_wg, ffn_wu, ffn_wd, ln3_g, ln3_b):
    h = layer_norm(x, ln_in_g, ln_in_b)
    for l in range(DEPTH):
        mix = hybrid_mixer(h, w_in[l], conv_w[l], conv_b[l], lru_wa[l], lru_ba[l], lru_wx[l],
                           lru_bx[l], lru_lambda[l], rw_mu[l], rw_w0[l], rw_wB[l], rw_a0[l],
                           rw_aB[l], rw_gB[l], rw_kk[l], rw_ka[l], rw_rk[l], rw_gn_g[l],
                           rw_gn_b[l], w_out[l])
        h = layer_norm(ALPHA * h + mix, ln1_g[l], ln1_b[l])
        xa = memory_cross_attention(h, mem, xa_wq[l], xa_wk[l], xa_wv[l], xa_wo[l])
        h = layer_norm(ALPHA * h + xa, ln2_g[l], ln2_b[l])
        ff = swiglu(h, ffn_wg[l], ffn_wu[l], ffn_wd[l])
        h = layer_norm(ALPHA * h + ff, ln3_g[l], ln3_b[l])
    return h
```

```python
import functools

import jax
import jax.numpy as jnp
from jax import lax
from jax.experimental import pallas as pl
from jax.experimental.pallas import tpu as pltpu

F32 = jnp.float32
BF16 = jnp.bfloat16

D_MODEL = 2048
LN_EPS = 1e-5
ALPHA = 2.0 ** 0.25

LRU_HEADS = 16
LRU_HEAD_DIM = 128
CONV_WIDTH = 4
LRU_C = 8.0

RWKV_HEAD_DIM = 64
D_LORA_PAD = 128
D_GATE_LORA = 256
GN_EPS = 64e-5
CHUNK = 64
GROUP = 256
N_GROUPS = D_MODEL // GROUP

XATTN_HEADS = 4
XATTN_HEAD_DIM = 512
N_MEM = 256
D_FF = 5632

VMEM_LIMIT = 56 * 1024 * 1024


def _dot(a, b):
    return jnp.dot(a, b, preferred_element_type=F32)


def _dot_nt(a, b):
    return lax.dot_general(a, b, (((1,), (1,)), ((), ())), preferred_element_type=F32)


def _dot_tn(a, b):
    return lax.dot_general(a, b, (((0,), (0,)), ((), ())), preferred_element_type=F32)


def _split2(x):
    hi = x.astype(BF16)
    lo = (x - hi.astype(F32)).astype(BF16)
    return hi, lo


def _split3(x):
    hi = x.astype(BF16)
    r1 = x - hi.astype(F32)
    mid = r1.astype(BF16)
    lo = (r1 - mid.astype(F32)).astype(BF16)
    return hi, mid, lo


def _sigmoid(x):
    return 1.0 / (1.0 + jnp.exp(-x))


def _softplus(x):
    return jnp.maximum(x, 0.0) + jnp.log1p(jnp.exp(-jnp.abs(x)))


def _gelu_tanh(x):
    c = 0.7978845608028654
    return 0.5 * x * (1.0 + jnp.tanh(c * (x + 0.044715 * (x * x * x))))


def _layer_norm(x, g, b):
    mu = jnp.mean(x, axis=-1, keepdims=True)
    xc = x - mu
    var = jnp.mean(xc * xc, axis=-1, keepdims=True)
    return xc * lax.rsqrt(var + LN_EPS) * g + b


def _shift_rows(z, prev8, s):
    rolled = pltpu.roll(z, s, 0)
    row8 = lax.broadcasted_iota(jnp.int32, prev8.shape, 0)
    head = jnp.where(row8 < s, pltpu.roll(prev8, s, 0), rolled[:8])
    return jnp.concatenate([head, rolled[8:]], axis=0)


def _params(sem):
    return pltpu.CompilerParams(dimension_semantics=sem, vmem_limit_bytes=VMEM_LIMIT)


def _in_proj_kernel(x_ref, g_ref, b_ref, w_ref, h_ref, z_ref, hb_ref):
    @pl.when(pl.program_id(1) == 0)
    def _():
        h = _layer_norm(x_ref[...], g_ref[...], b_ref[...])
        h_ref[...] = h
        hb_ref[...] = h.astype(BF16)

    z_ref[...] = _dot(hb_ref[...], w_ref[...])


def _in_proj(x, g, b, w, tm=1024, tn=512):
    t, d = x.shape
    n = w.shape[1]
    tm = min(tm, t)
    return pl.pallas_call(
        _in_proj_kernel,
        grid=(t // tm, n // tn),
        in_specs=[
            pl.BlockSpec((tm, d), lambda i, j: (i, 0)),
            pl.BlockSpec((1, d), lambda i, j: (0, 0)),
            pl.BlockSpec((1, d), lambda i, j: (0, 0)),
            pl.BlockSpec((d, tn), lambda i, j: (0, j)),
        ],
        out_specs=[
            pl.BlockSpec((tm, d), lambda i, j: (i, 0)),
            pl.BlockSpec((tm, tn), lambda i, j: (i, j)),
        ],
        out_shape=[
            jax.ShapeDtypeStruct((t, d), F32),
            jax.ShapeDtypeStruct((t, n), F32),
        ],
        scratch_shapes=[pltpu.VMEM((tm, d), BF16)],
        compiler_params=_params(("arbitrary", "arbitrary")),
        name="in_proj",
    )(x, g, b, w)


def _lru_kernel(u_ref, gate_ref, ga_ref, cw_ref, cb_ref, wa_ref, ba_ref, wx_ref, bx_ref,
                lam_ref, out_ref, ucarry, hcarry):
    @pl.when(pl.program_id(0) == 0)
    def _():
        ucarry[...] = jnp.zeros_like(ucarry)
        hcarry[...] = jnp.zeros_like(hcarry)

    u = u_ref[...]
    tm = u.shape[0]
    prev8 = ucarry[...]
    conv = cb_ref[...] + u * cw_ref[CONV_WIDTH - 1:CONV_WIDTH, :]
    for s in range(1, CONV_WIDTH):
        conv = conv + _shift_rows(u, prev8, s) * cw_ref[CONV_WIDTH - 1 - s:CONV_WIDTH - s, :]
    ucarry[...] = u[tm - 8:, :]

    cb16 = conv.astype(BF16)
    r_parts, i_parts = [], []
    for g in range(LRU_HEADS):
        ug = cb16[:, g * LRU_HEAD_DIM:(g + 1) * LRU_HEAD_DIM]
        r_parts.append(_dot(ug, wa_ref[g]))
        i_parts.append(_dot(ug, wx_ref[g]))
    r = _sigmoid(jnp.concatenate(r_parts, axis=1) + ba_ref[...])
    ig = _sigmoid(jnp.concatenate(i_parts, axis=1) + bx_ref[...])

    log_a = (-LRU_C) * r * _softplus(-lam_ref[...])
    a_c = jnp.exp(log_a)
    b_c = jnp.sqrt(-jnp.tanh(log_a) * (a_c * a_c + 1.0)) * (ig * conv)

    row = lax.broadcasted_iota(jnp.int32, a_c.shape, 0)
    d = 1
    while d < tm:
        a_sh = pltpu.roll(a_c, d, 0)
        b_sh = pltpu.roll(b_c, d, 0)
        m = row >= d
        b_c = jnp.where(m, a_c * b_sh + b_c, b_c)
        a_c = jnp.where(m, a_c * a_sh, a_c)
        d *= 2
    h = b_c + a_c * hcarry[...]
    hcarry[...] = h[tm - 1:tm, :]

    y = _gelu_tanh(gate_ref[...]) * h
    out_ref[...] = _sigmoid(ga_ref[...]) * y


def _lru(zin, conv_w, conv_b, wa, ba, wx, bx, lam, tm=256):
    t = zin.shape[0]
    d = D_MODEL
    row = lambda i: (0, 0)
    return pl.pallas_call(
        _lru_kernel,
        grid=(t // tm,),
        in_specs=[
            pl.BlockSpec((tm, d), lambda i: (i, 0)),
            pl.BlockSpec((tm, d), lambda i: (i, 1)),
            pl.BlockSpec((tm, d), lambda i: (i, 2)),
            pl.BlockSpec((CONV_WIDTH, d), row),
            pl.BlockSpec((1, d), row),
            pl.BlockSpec((LRU_HEADS, LRU_HEAD_DIM, LRU_HEAD_DIM), lambda i: (0, 0, 0)),
            pl.BlockSpec((1, d), row),
            pl.BlockSpec((LRU_HEADS, LRU_HEAD_DIM, LRU_HEAD_DIM), lambda i: (0, 0, 0)),
            pl.BlockSpec((1, d), row),
            pl.BlockSpec((1, d), row),
        ],
        out_specs=pl.BlockSpec((tm, d), lambda i: (i, 0)),
        out_shape=jax.ShapeDtypeStruct((t, d), F32),
        scratch_shapes=[pltpu.VMEM((8, d), F32), pltpu.VMEM((1, d), F32)],
        compiler_params=_params(("arbitrary",)),
        name="lru",
    )(zin, zin, zin, conv_w, conv_b, wa, ba, wx, bx, lam)


def _block_ones(n, seg):
    r = lax.broadcasted_iota(jnp.int32, (n, n), 0) // seg
    c = lax.broadcasted_iota(jnp.int32, (n, n), 1) // seg
    return jnp.where(r == c, 1.0, 0.0).astype(BF16)


def _seg_sum(x, ones_bd):
    parts = []
    for g in range(x.shape[1] // GROUP):
        hi, lo = _split2(x[:, g * GROUP:(g + 1) * GROUP])
        parts.append(_dot(hi, ones_bd) + _dot(lo, ones_bd))
    return jnp.concatenate(parts, axis=1)


def _rwkv_prep_kernel(zr_ref, zk_ref, zv_ref, zl_ref, mu_ref, mul_ref, w0_ref, wb_ref, a0_ref,
                      ab_ref, gb_ref, kk_ref, ka_ref,
                      r_out, lw_out, kf_out, v_out, kn_out, b_out, g_out,
                      cr, ck, cv, cl):
    @pl.when(pl.program_id(0) == 0)
    def _():
        cr[...] = jnp.zeros_like(cr)
        ck[...] = jnp.zeros_like(ck)
        cv[...] = jnp.zeros_like(cv)
        cl[...] = jnp.zeros_like(cl)

    d = D_MODEL

    def shifted(z_ref, carry, mu):
        z = z_ref[...]
        tm = z.shape[0]
        zp = _shift_rows(z, carry[...], 1)
        carry[...] = z[tm - 8:, :]
        return z + (zp - z) * mu

    r = shifted(zr_ref, cr, mu_ref[:, 0:d])
    k = shifted(zk_ref, ck, mu_ref[:, d:2 * d])
    v = shifted(zv_ref, cv, mu_ref[:, 2 * d:3 * d])
    zl = shifted(zl_ref, cl, mul_ref[...])
    w_lo = zl[:, 0:D_LORA_PAD]
    a_lo = zl[:, D_LORA_PAD:2 * D_LORA_PAD]
    g_lo = zl[:, 2 * D_LORA_PAD:]

    w_log = -_softplus(-(w0_ref[...] + _dot(jnp.tanh(w_lo).astype(BF16), wb_ref[...]))) - 0.5
    lw_out[...] = -jnp.exp(w_log)
    a = _sigmoid(a0_ref[...] + _dot(a_lo.astype(BF16), ab_ref[...]))
    g_out[...] = _dot(_sigmoid(g_lo).astype(BF16), gb_ref[...])

    ones_bd = _block_ones(GROUP, RWKV_HEAD_DIM)
    kk = k * kk_ref[...]
    norm = jnp.sqrt(_seg_sum(kk * kk, ones_bd))
    kn = kk / jnp.maximum(norm, 1e-12)
    r_out[...] = r
    v_out[...] = v
    kn_out[...] = kn
    b_out[...] = kn * a
    kf_out[...] = k * (1.0 + (a - 1.0) * ka_ref[...])


def _rwkv_prep(zin, mu_rkv, mu_l, w0, wb, a0, ab, gb, k_k, k_a, tm=256):
    t = zin.shape[0]
    d = D_MODEL
    dl = 2 * D_LORA_PAD + D_GATE_LORA
    row = lambda i: (0, 0)
    out = jax.ShapeDtypeStruct((t, d), F32)
    blk = pl.BlockSpec((tm, d), lambda i: (i, 0))
    return pl.pallas_call(
        _rwkv_prep_kernel,
        grid=(t // tm,),
        in_specs=[
            pl.BlockSpec((tm, d), lambda i: (i, 4)),
            pl.BlockSpec((tm, d), lambda i: (i, 5)),
            pl.BlockSpec((tm, d), lambda i: (i, 6)),
            pl.BlockSpec((tm, dl), lambda i: (i, 7 * d // dl)),
            pl.BlockSpec((1, 3 * d), row),
            pl.BlockSpec((1, dl), row),
            pl.BlockSpec((1, d), row),
            pl.BlockSpec((D_LORA_PAD, d), row),
            pl.BlockSpec((1, d), row),
            pl.BlockSpec((D_LORA_PAD, d), row),
            pl.BlockSpec((D_GATE_LORA, d), row),
            pl.BlockSpec((1, d), row),
            pl.BlockSpec((1, d), row),
        ],
        out_specs=[blk] * 7,
        out_shape=[out] * 7,
        scratch_shapes=[pltpu.VMEM((8, d), F32), pltpu.VMEM((8, d), F32),
                        pltpu.VMEM((8, d), F32), pltpu.VMEM((8, dl), F32)],
        compiler_params=_params(("arbitrary",)),
        name="rwkv_prep",
    )(zin, zin, zin, zin, mu_rkv, mu_l, w0, wb, a0, ab, gb, k_k, k_a)


def _rwkv_rec_kernel(r_ref, lw_ref, kf_ref, v_ref, kn_ref, b_ref, g_ref, zgb_ref,
                     rk_ref, gng_ref, gnb_ref, out_ref, s_ref):
    @pl.when(pl.program_id(0) == 0)
    def _():
        s_ref[...] = jnp.zeros_like(s_ref)

    c = CHUNK
    hd = RWKV_HEAD_DIM
    lw = lw_ref[...]
    rows = lax.broadcasted_iota(jnp.int32, (c, c), 0)
    cols = lax.broadcasted_iota(jnp.int32, (c, c), 1)
    tri = jnp.where(cols <= rows, 1.0, 0.0).astype(BF16)
    hi, mid, lo = _split3(lw)
    cum = _dot(tri, hi) + _dot(tri, mid) + _dot(tri, lo)
    tot = cum[c - 1:c, :]
    p_in = jnp.exp(cum)
    p_ex = jnp.exp(cum - lw)
    p_inv = jnp.exp(-cum)
    p_end = jnp.exp(tot - cum)
    p_tot = jnp.exp(tot)

    r = r_ref[...]
    kf = kf_ref[...]
    v = v_ref[...]
    bb = b_ref[...]
    rq = r * p_in
    kap = kn_ref[...] * p_ex
    bet = bb * p_inv
    kt = kf * p_inv
    bhat = bb * p_end
    khat = kf * p_end

    t_c = lax.broadcasted_iota(jnp.int32, (c, GROUP), 0)
    s_c = lax.broadcasted_iota(jnp.int32, (c, GROUP), 1) % hd
    strict = s_c < t_c
    incl = s_c <= t_c
    eye_c = jnp.where(s_c == t_c, 1.0, 0.0)
    bd = (lax.broadcasted_iota(jnp.int32, (GROUP, GROUP), 0) // hd
          == lax.broadcasted_iota(jnp.int32, (GROUP, GROUP), 1) // hd)

    def expand(x_c):
        return jnp.where(bd, jnp.concatenate([x_c] * (GROUP // hd), axis=0), 0.0).astype(BF16)

    ones_bd = _block_ones(GROUP, hd)
    o_parts = []
    for g in range(N_GROUPS):
        sl = slice(g * GROUP, (g + 1) * GROUP)
        lhs2 = jnp.concatenate([kap[:, sl], rq[:, sl]], axis=0).astype(BF16)
        rhs2 = jnp.concatenate([expand(bet[:, sl]), expand(kt[:, sl])], axis=0)
        amat = _dot_nt(lhs2, rhs2)
        l_c = jnp.where(strict, amat[:c, :GROUP], 0.0)
        akk = jnp.where(strict, amat[:c, GROUP:], 0.0)
        arb = jnp.where(incl, amat[c:, :GROUP], 0.0)
        ark = jnp.where(incl, amat[c:, GROUP:], 0.0)

        x_c = eye_c - jnp.where((t_c >> 1) == (s_c >> 1), l_c, 0.0)
        for lvl in range(2, 7):
            cl = jnp.where(((t_c >> lvl) == (s_c >> lvl)) & ((t_c >> (lvl - 1)) != (s_c >> (lvl - 1))),
                           l_c, 0.0)
            y_c = _dot(x_c.astype(BF16), expand(cl))
            x_c = x_c - _dot(y_c.astype(BF16), expand(x_c))

        s0 = s_ref[g]
        rs = _dot_nt(lhs2, s0.astype(BF16))
        vg = v[:, sl]
        av = _dot(jnp.concatenate([akk, ark], axis=0).astype(BF16), expand(vg))
        u_c = _dot(x_c.astype(BF16), expand(rs[:c] + av[:c]))
        o_c = rs[c:] + av[c:] - _dot(arb.astype(BF16), expand(u_c))
        upd = _dot_tn(jnp.concatenate([u_c, vg], axis=0).astype(BF16),
                      jnp.concatenate([-bhat[:, sl], khat[:, sl]], axis=0).astype(BF16))
        s_ref[g] = s0 * p_tot[:, sl] + jnp.where(bd, upd, 0.0)

        hi_o, lo_o = _split2(o_c)
        mean = (_dot(hi_o, ones_bd) + _dot(lo_o, ones_bd)) * (1.0 / hd)
        dev = o_c - mean
        hi_d, lo_d = _split2(dev * dev)
        var = (_dot(hi_d, ones_bd) + _dot(lo_d, ones_bd)) * (1.0 / hd)
        o_n = dev * lax.rsqrt(var + GN_EPS) * gng_ref[:, sl] + gnb_ref[:, sl]
        hi_b, lo_b = _split2(r[:, sl] * kf[:, sl] * rk_ref[:, sl])
        bonus = _dot(hi_b, ones_bd) + _dot(lo_b, ones_bd)
        o_parts.append(o_n + bonus * vg)

    o = jnp.concatenate(o_parts, axis=1)
    out_ref[...] = _sigmoid(zgb_ref[...]) * (o * g_ref[...])


def _rwkv_rec(r, lw, kf, v, kn, b, g, zin, rk, gng, gnb):
    t, d = r.shape
    blk = pl.BlockSpec((CHUNK, d), lambda i: (i, 0))
    row = pl.BlockSpec((1, d), lambda i: (0, 0))
    return pl.pallas_call(
        _rwkv_rec_kernel,
        grid=(t // CHUNK,),
        in_specs=[blk] * 7 + [pl.BlockSpec((CHUNK, d), lambda i: (i, 3)), row, row, row],
        out_specs=blk,
        out_shape=jax.ShapeDtypeStruct((t, d), F32),
        scratch_shapes=[pltpu.VMEM((N_GROUPS, GROUP, GROUP), F32)],
        compiler_params=_params(("arbitrary",)),
        name="rwkv_rec",
    )(r, lw, kf, v, kn, b, g, zin, rk, gng, gnb)


def _out_proj_kernel(ya_ref, yb_ref, h_ref, w_ref, g_ref, b_ref, o_ref):
    y = (ya_ref[...] + yb_ref[...]).astype(BF16)
    mix = _dot(y, w_ref[...])
    o_ref[...] = _layer_norm(ALPHA * h_ref[...] + mix, g_ref[...], b_ref[...])


def _out_proj(ya, yb, h, w, g, b, tm=256):
    t, d = h.shape
    blk = pl.BlockSpec((tm, d), lambda i: (i, 0))
    row = pl.BlockSpec((1, d), lambda i: (0, 0))
    return pl.pallas_call(
        _out_proj_kernel,
        grid=(t // tm,),
        in_specs=[blk, blk, blk, pl.BlockSpec((d, d), lambda i: (0, 0)), row, row],
        out_specs=blk,
        out_shape=jax.ShapeDtypeStruct((t, d), F32),
        compiler_params=_params(("arbitrary",)),
        name="out_proj",
    )(ya, yb, h, w, g, b)


def _mem_kv_kernel(mem_ref, wk_ref, wv_ref, k_ref, v_ref):
    m = mem_ref[...].astype(BF16)
    k_ref[...] = _dot(m, wk_ref[...]).astype(BF16)
    v_ref[...] = _dot(m, wv_ref[...]).astype(BF16)


def _mem_kv(mem, wk, wv, tn=512):
    n, d = mem.shape
    return pl.pallas_call(
        _mem_kv_kernel,
        grid=(d // tn,),
        in_specs=[pl.BlockSpec((n, d), lambda j: (0, 0)),
                  pl.BlockSpec((d, tn), lambda j: (0, j)),
                  pl.BlockSpec((d, tn), lambda j: (0, j))],
        out_specs=[pl.BlockSpec((n, tn), lambda j: (0, j))] * 2,
        out_shape=[jax.ShapeDtypeStruct((n, d), BF16)] * 2,
        compiler_params=_params(("arbitrary",)),
        name="mem_kv",
    )(mem, wk, wv)


def _xattn_kernel(h_ref, wq_ref, k_ref, v_ref, wo_ref, g_ref, b_ref, o_ref):
    h = h_ref[...]
    q = _dot(h.astype(BF16), wq_ref[...]).astype(BF16)
    scale = XATTN_HEAD_DIM ** -0.5
    outs = []
    for hh in range(XATTN_HEADS):
        sl = slice(hh * XATTN_HEAD_DIM, (hh + 1) * XATTN_HEAD_DIM)
        s = _dot_nt(q[:, sl], k_ref[:, sl]) * scale
        e = jnp.exp(s - jnp.max(s, axis=-1, keepdims=True))
        p = e / jnp.sum(e, axis=-1, keepdims=True)
        outs.append(_dot(p.astype(BF16), v_ref[:, sl]))
    o = jnp.concatenate(outs, axis=1).astype(BF16)
    xa = _dot(o, wo_ref[...])
    o_ref[...] = _layer_norm(ALPHA * h + xa, g_ref[...], b_ref[...])


def _xattn(h, wq, kmem, vmem, wo, g, b, tm=256):
    t, d = h.shape
    n = kmem.shape[0]
    blk = pl.BlockSpec((tm, d), lambda i: (i, 0))
    row = pl.BlockSpec((1, d), lambda i: (0, 0))
    full = lambda i: (0, 0)
    return pl.pallas_call(
        _xattn_kernel,
        grid=(t // tm,),
        in_specs=[blk, pl.BlockSpec((d, d), full), pl.BlockSpec((n, d), full),
                  pl.BlockSpec((n, d), full), pl.BlockSpec((d, d), full), row, row],
        out_specs=blk,
        out_shape=jax.ShapeDtypeStruct((t, d), F32),
        compiler_params=_params(("arbitrary",)),
        name="xattn",
    )(h, wq, kmem, vmem, wo, g, b)


def _ffn_kernel(h_ref, wg_ref, wu_ref, wd_ref, g_ref, b_ref, o_ref, hb_ref, acc_ref):
    j = pl.program_id(1)

    @pl.when(j == 0)
    def _():
        hb_ref[...] = h_ref[...].astype(BF16)
        acc_ref[...] = jnp.zeros_like(acc_ref)

    hb = hb_ref[...]
    gate = _dot(hb, wg_ref[...])
    up = _dot(hb, wu_ref[...])
    act = (gate * _sigmoid(gate) * up).astype(BF16)
    acc_ref[...] += _dot(act, wd_ref[...])

    @pl.when(j == pl.num_programs(1) - 1)
    def _():
        o_ref[...] = _layer_norm(ALPHA * h_ref[...] + acc_ref[...], g_ref[...], b_ref[...])


def _ffn(h, wg, wu, wd, g, b, tm=512, tf=512):
    t, d = h.shape
    f = wg.shape[1]
    blk = pl.BlockSpec((tm, d), lambda i, j: (i, 0))
    row = pl.BlockSpec((1, d), lambda i, j: (0, 0))
    return pl.pallas_call(
        _ffn_kernel,
        grid=(t // tm, f // tf),
        in_specs=[blk,
                  pl.BlockSpec((d, tf), lambda i, j: (0, j)),
                  pl.BlockSpec((d, tf), lambda i, j: (0, j)),
                  pl.BlockSpec((tf, d), lambda i, j: (j, 0)),
                  row, row],
        out_specs=blk,
        out_shape=jax.ShapeDtypeStruct((t, d), F32),
        scratch_shapes=[pltpu.VMEM((tm, d), BF16), pltpu.VMEM((tm, d), F32)],
        compiler_params=_params(("arbitrary", "arbitrary")),
        name="ffn",
    )(h, wg, wu, wd, g, b)


def _pad_cols(w, n):
    return jnp.pad(w, ((0, 0), (0, n - w.shape[1])))


def _pad_rows(w, n):
    return jnp.pad(w, ((0, n - w.shape[0]), (0, 0)))


def _regroup_in_cols(w):
    d = D_MODEL
    o = 2 * d
    u, gate = w[:, :d], w[:, d:2 * d]
    rkv = w[:, o:o + 3 * d]
    w_lo = w[:, o + 3 * d:o + 3 * d + 96]
    a_lo = w[:, o + 3 * d + 96:o + 3 * d + 192]
    g_lo = w[:, o + 3 * d + 192:o + 3 * d + 192 + D_GATE_LORA]
    ga = w[:, o + 3 * d + 192 + D_GATE_LORA:o + 4 * d + 192 + D_GATE_LORA]
    gb = w[:, o + 4 * d + 192 + D_GATE_LORA:]
    return jnp.concatenate(
        [u, gate, ga, gb, rkv, _pad_cols(w_lo, D_LORA_PAD), _pad_cols(a_lo, D_LORA_PAD), g_lo], axis=1)


def kernel(x, mem, ln_in_g, ln_in_b, w_in, conv_w, conv_b, lru_wa, lru_ba, lru_wx, lru_bx, lru_lambda, rw_mu, rw_w0, rw_wB, rw_a0, rw_aB, rw_gB, rw_kk, rw_ka, rw_rk, rw_gn_g, rw_gn_b, w_out, ln1_g, ln1_b, xa_wq, xa_wk, xa_wv, xa_wo, ln2_g, ln2_b, ffn_wg, ffn_wu, ffn_wd, ln3_g, ln3_b):
    bsz, t, d = x.shape
    depth = w_in.shape[0]
    assert bsz == 1 and d == D_MODEL and t % 512 == 0
    row = lambda p: p.reshape(1, -1)

    h = None
    for l in range(depth):
        w_in_p = _regroup_in_cols(w_in[l]).astype(BF16)
        if l == 0:
            h, zin = _in_proj(x[0], row(ln_in_g), row(ln_in_b), w_in_p)
        else:
            raise NotImplementedError("DEPTH > 1 is not part of this problem")

        ya = _lru(zin, conv_w[l], row(conv_b[l]), lru_wa[l].astype(BF16), row(lru_ba[l]),
                  lru_wx[l].astype(BF16), row(lru_bx[l]), row(lru_lambda[l]))

        mu = _regroup_in_cols(jnp.pad(row(rw_mu[l]), ((0, 0), (2 * d, 2 * d))))
        mu_rkv, mu_l = mu[:, 4 * d:7 * d], mu[:, 7 * d:]
        r, lw, kf, v, kn, b, g = _rwkv_prep(
            zin, mu_rkv, mu_l, row(rw_w0[l]),
            _pad_rows(rw_wB[l], D_LORA_PAD).astype(BF16), row(rw_a0[l]),
            _pad_rows(rw_aB[l], D_LORA_PAD).astype(BF16), rw_gB[l].astype(BF16),
            row(rw_kk[l]), row(rw_ka[l]))
        yb = _rwkv_rec(r, lw, kf, v, kn, b, g, zin, row(rw_rk[l]), row(rw_gn_g[l]), row(rw_gn_b[l]))

        h = _out_proj(ya, yb, h, w_out[l].astype(BF16), row(ln1_g[l]), row(ln1_b[l]))
        kmem, vmem = _mem_kv(mem[0], xa_wk[l].astype(BF16), xa_wv[l].astype(BF16))
        h = _xattn(h, xa_wq[l].astype(BF16), kmem, vmem, xa_wo[l].astype(BF16),
                   row(ln2_g[l]), row(ln2_b[l]))
        h = _ffn(h, ffn_wg[l].astype(BF16), ffn_wu[l].astype(BF16), ffn_wd[l].astype(BF16),
                 row(ln3_g[l]), row(ln3_b[l]))
    return h[None]
```

```python
import functools

import jax
import jax.numpy as jnp
from jax import lax
from jax.experimental import pallas as pl
from jax.experimental.pallas import tpu as pltpu

F32 = jnp.float32
BF16 = jnp.bfloat16

D_MODEL = 2048
LN_EPS = 1e-5
ALPHA = 2.0 ** 0.25

LRU_HEADS = 16
LRU_HEAD_DIM = 128
CONV_WIDTH = 4
LRU_C = 8.0

RWKV_HEAD_DIM = 64
D_LORA_PAD = 128
D_GATE_LORA = 256
GN_EPS = 64e-5
CHUNK = 64
GROUP = 256
N_GROUPS = D_MODEL // GROUP

XATTN_HEADS = 4
XATTN_HEAD_DIM = 512
N_MEM = 256
D_FF = 5632

VMEM_LIMIT = 56 * 1024 * 1024


def _dot(a, b):
    return jnp.dot(a, b, preferred_element_type=F32)


def _dot_nt(a, b):
    return lax.dot_general(a, b, (((1,), (1,)), ((), ())), preferred_element_type=F32)


def _dot_tn(a, b):
    return lax.dot_general(a, b, (((0,), (0,)), ((), ())), preferred_element_type=F32)


def _split2(x):
    hi = x.astype(BF16)
    lo = (x - hi.astype(F32)).astype(BF16)
    return hi, lo


def _split3(x):
    hi = x.astype(BF16)
    r1 = x - hi.astype(F32)
    mid = r1.astype(BF16)
    lo = (r1 - mid.astype(F32)).astype(BF16)
    return hi, mid, lo


def _sigmoid(x):
    return 1.0 / (1.0 + jnp.exp(-x))


def _softplus(x):
    return jnp.maximum(x, 0.0) + jnp.log1p(jnp.exp(-jnp.abs(x)))


def _gelu_tanh(x):
    c = 0.7978845608028654
    return 0.5 * x * (1.0 + jnp.tanh(c * (x + 0.044715 * (x * x * x))))


def _layer_norm(x, g, b):
    mu = jnp.mean(x, axis=-1, keepdims=True)
    xc = x - mu
    var = jnp.mean(xc * xc, axis=-1, keepdims=True)
    return xc * lax.rsqrt(var + LN_EPS) * g + b


def _shift_rows(z, prev8, s):
    rolled = pltpu.roll(z, s, 0)
    row8 = lax.broadcasted_iota(jnp.int32, prev8.shape, 0)
    head = jnp.where(row8 < s, pltpu.roll(prev8, s, 0), rolled[:8])
    return jnp.concatenate([head, rolled[8:]], axis=0)


def _params(sem):
    return pltpu.CompilerParams(dimension_semantics=sem, vmem_limit_bytes=VMEM_LIMIT)


def _in_proj_kernel(x_ref, g_ref, b_ref, w_ref, h_ref, z_ref, hb_ref):
    @pl.when(pl.program_id(1) == 0)
    def _():
        h = _layer_norm(x_ref[...], g_ref[...], b_ref[...])
        h_ref[...] = h
        hb_ref[...] = h.astype(BF16)

    z_ref[...] = _dot(hb_ref[...], w_ref[...])


def _in_proj(x, g, b, w, tm=1024, tn=512):
    t, d = x.shape
    n = w.shape[1]
    tm = min(tm, t)
    return pl.pallas_call(
        _in_proj_kernel,
        grid=(t // tm, n // tn),
        in_specs=[
            pl.BlockSpec((tm, d), lambda i, j: (i, 0)),
            pl.BlockSpec((1, d), lambda i, j: (0, 0)),
            pl.BlockSpec((1, d), lambda i, j: (0, 0)),
            pl.BlockSpec((d, tn), lambda i, j: (0, j)),
        ],
        out_specs=[
            pl.BlockSpec((tm, d), lambda i, j: (i, 0)),
            pl.BlockSpec((tm, tn), lambda i, j: (i, j)),
        ],
        out_shape=[
            jax.ShapeDtypeStruct((t, d), F32),
            jax.ShapeDtypeStruct((t, n), F32),
        ],
        scratch_shapes=[pltpu.VMEM((tm, d), BF16)],
        compiler_params=_params(("arbitrary", "arbitrary")),
        name="in_proj",
    )(x, g, b, w)


def _lru_kernel(u_ref, gate_ref, ga_ref, cw_ref, cb_ref, wa_ref, ba_ref, wx_ref, bx_ref,
                lam_ref, out_ref, ucarry, hcarry):
    @pl.when(pl.program_id(0) == 0)
    def _():
        ucarry[...] = jnp.zeros_like(ucarry)
        hcarry[...] = jnp.zeros_like(hcarry)

    u = u_ref[...]
    tm = u.shape[0]
    prev8 = ucarry[...]
    conv = cb_ref[...] + u * cw_ref[CONV_WIDTH - 1:CONV_WIDTH, :]
    for s in range(1, CONV_WIDTH):
        conv = conv + _shift_rows(u, prev8, s) * cw_ref[CONV_WIDTH - 1 - s:CONV_WIDTH - s, :]
    ucarry[...] = u[tm - 8:, :]

    cb16 = conv.astype(BF16)
    r_parts, i_parts = [], []
    for g in range(LRU_HEADS):
        ug = cb16[:, g * LRU_HEAD_DIM:(g + 1) * LRU_HEAD_DIM]
        r_parts.append(_dot(ug, wa_ref[g]))
        i_parts.append(_dot(ug, wx_ref[g]))
    r = _sigmoid(jnp.concatenate(r_parts, axis=1) + ba_ref[...])
    ig = _sigmoid(jnp.concatenate(i_parts, axis=1) + bx_ref[...])

    log_a = (-LRU_C) * r * _softplus(-lam_ref[...])
    a_c = jnp.exp(log_a)
    b_c = jnp.sqrt(-jnp.tanh(log_a) * (a_c * a_c + 1.0)) * (ig * conv)

    row = lax.broadcasted_iota(jnp.int32, a_c.shape, 0)
    d = 1
    while d < tm:
        a_sh = pltpu.roll(a_c, d, 0)
        b_sh = pltpu.roll(b_c, d, 0)
        m = row >= d
        b_c = jnp.where(m, a_c * b_sh + b_c, b_c)
        a_c = jnp.where(m, a_c * a_sh, a_c)
        d *= 2
    h = b_c + a_c * hcarry[...]
    hcarry[...] = h[tm - 1:tm, :]

    y = _gelu_tanh(gate_ref[...]) * h
    out_ref[...] = _sigmoid(ga_ref[...]) * y


def _lru(zin, conv_w, conv_b, wa, ba, wx, bx, lam, tm=256):
    t = zin.shape[0]
    d = D_MODEL
    row = lambda i: (0, 0)
    return pl.pallas_call(
        _lru_kernel,
        grid=(t // tm,),
        in_specs=[
            pl.BlockSpec((tm, d), lambda i: (i, 0)),
            pl.BlockSpec((tm, d), lambda i: (i, 1)),
            pl.BlockSpec((tm, d), lambda i: (i, 2)),
            pl.BlockSpec((CONV_WIDTH, d), row),
            pl.BlockSpec((1, d), row),
            pl.BlockSpec((LRU_HEADS, LRU_HEAD_DIM, LRU_HEAD_DIM), lambda i: (0, 0, 0)),
            pl.BlockSpec((1, d), row),
            pl.BlockSpec((LRU_HEADS, LRU_HEAD_DIM, LRU_HEAD_DIM), lambda i: (0, 0, 0)),
            pl.BlockSpec((1, d), row),
            pl.BlockSpec((1, d), row),
        ],
        out_specs=pl.BlockSpec((tm, d), lambda i: (i, 0)),
        out_shape=jax.ShapeDtypeStruct((t, d), F32),
        scratch_shapes=[pltpu.VMEM((8, d), F32), pltpu.VMEM((1, d), F32)],
        compiler_params=_params(("arbitrary",)),
        name="lru",
    )(zin, zin, zin, conv_w, conv_b, wa, ba, wx, bx, lam)


def _block_ones(n, seg):
    r = lax.broadcasted_iota(jnp.int32, (n, n), 0) // seg
    c = lax.broadcasted_iota(jnp.int32, (n, n), 1) // seg
    return jnp.where(r == c, 1.0, 0.0).astype(BF16)


def _seg_sum(x, ones_bd):
    parts = []
    for g in range(x.shape[1] // GROUP):
        hi, lo = _split2(x[:, g * GROUP:(g + 1) * GROUP])
        parts.append(_dot(hi, ones_bd) + _dot(lo, ones_bd))
    return jnp.concatenate(parts, axis=1)


def _rwkv_prep_kernel(zr_ref, zk_ref, zv_ref, zl_ref, mu_ref, mul_ref, w0_ref, wb_ref, a0_ref,
                      ab_ref, gb_ref, kk_ref, ka_ref,
                      r_out, lw_out, kf_out, v_out, kn_out, b_out, g_out,
                      cr, ck, cv, cl):
    @pl.when(pl.program_id(0) == 0)
    def _():
        cr[...] = jnp.zeros_like(cr)
        ck[...] = jnp.zeros_like(ck)
        cv[...] = jnp.zeros_like(cv)
        cl[...] = jnp.zeros_like(cl)

    d = D_MODEL

    def shifted(z_ref, carry, mu):
        z = z_ref[...]
        tm = z.shape[0]
        zp = _shift_rows(z, carry[...], 1)
        carry[...] = z[tm - 8:, :]
        return z + (zp - z) * mu

    r = shifted(zr_ref, cr, mu_ref[:, 0:d])
    k = shifted(zk_ref, ck, mu_ref[:, d:2 * d])
    v = shifted(zv_ref, cv, mu_ref[:, 2 * d:3 * d])
    zl = shifted(zl_ref, cl, mul_ref[...])
    w_lo = zl[:, 0:D_LORA_PAD]
    a_lo = zl[:, D_LORA_PAD:2 * D_LORA_PAD]
    g_lo = zl[:, 2 * D_LORA_PAD:]

    w_log = -_softplus(-(w0_ref[...] + _dot(jnp.tanh(w_lo).astype(BF16), wb_ref[...]))) - 0.5
    lw_out[...] = -jnp.exp(w_log)
    a = _sigmoid(a0_ref[...] + _dot(a_lo.astype(BF16), ab_ref[...]))
    g_out[...] = _dot(_sigmoid(g_lo).astype(BF16), gb_ref[...])

    ones_bd = _block_ones(GROUP, RWKV_HEAD_DIM)
    kk = k * kk_ref[...]
    norm = jnp.sqrt(_seg_sum(kk * kk, ones_bd))
    kn = kk / jnp.maximum(norm, 1e-12)
    r_out[...] = r
    v_out[...] = v
    kn_out[...] = kn
    b_out[...] = kn * a
    kf_out[...] = k * (1.0 + (a - 1.0) * ka_ref[...])


def _rwkv_prep(zin, mu_rkv, mu_l, w0, wb, a0, ab, gb, k_k, k_a, tm=256):
    t = zin.shape[0]
    d = D_MODEL
    dl = 2 * D_LORA_PAD + D_GATE_LORA
    row = lambda i: (0, 0)
    out = jax.ShapeDtypeStruct((t, d), F32)
    blk = pl.BlockSpec((tm, d), lambda i: (i, 0))
    return pl.pallas_call(
        _rwkv_prep_kernel,
        grid=(t // tm,),
        in_specs=[
            pl.BlockSpec((tm, d), lambda i: (i, 4)),
            pl.BlockSpec((tm, d), lambda i: (i, 5)),
            pl.BlockSpec((tm, d), lambda i: (i, 6)),
            pl.BlockSpec((tm, dl), lambda i: (i, 7 * d // dl)),
            pl.BlockSpec((1, 3 * d), row),
            pl.BlockSpec((1, dl), row),
            pl.BlockSpec((1, d), row),
            pl.BlockSpec((D_LORA_PAD, d), row),
            pl.BlockSpec((1, d), row),
            pl.BlockSpec((D_LORA_PAD, d), row),
            pl.BlockSpec((D_GATE_LORA, d), row),
            pl.BlockSpec((1, d), row),
            pl.BlockSpec((1, d), row),
        ],
        out_specs=[blk] * 7,
        out_shape=[out] * 7,
        scratch_shapes=[pltpu.VMEM((8, d), F32), pltpu.VMEM((8, d), F32),
                        pltpu.VMEM((8, d), F32), pltpu.VMEM((8, dl), F32)],
        compiler_params=_params(("arbitrary",)),
        name="rwkv_prep",
    )(zin, zin, zin, zin, mu_rkv, mu_l, w0, wb, a0, ab, gb, k_k, k_a)


def _rwkv_rec_kernel(r_ref, lw_ref, kf_ref, v_ref, kn_ref, b_ref, g_ref, zgb_ref,
                     rk_ref, gng_ref, gnb_ref, out_ref, s_ref):
    @pl.when(pl.program_id(0) == 0)
    def _():
        s_ref[...] = jnp.zeros_like(s_ref)

    c = CHUNK
    hd = RWKV_HEAD_DIM
    lw = lw_ref[...]
    rows = lax.broadcasted_iota(jnp.int32, (c, c), 0)
    cols = lax.broadcasted_iota(jnp.int32, (c, c), 1)
    tri = jnp.where(cols <= rows, 1.0, 0.0).astype(BF16)
    hi, mid, lo = _split3(lw)
    cum = _dot(tri, hi) + _dot(tri, mid) + _dot(tri, lo)
    tot = cum[c - 1:c, :]
    p_in = jnp.exp(cum)
    p_ex = jnp.exp(cum - lw)
    p_inv = jnp.exp(-cum)
    p_end = jnp.exp(tot - cum)
    p_tot = jnp.exp(tot)

    r = r_ref[...]
    kf = kf_ref[...]
    v = v_ref[...]
    bb = b_ref[...]
    rq = r * p_in
    kap = kn_ref[...] * p_ex
    bet = bb * p_inv
    kt = kf * p_inv
    bhat = bb * p_end
    khat = kf * p_end

    t_c = lax.broadcasted_iota(jnp.int32, (c, GROUP), 0)
    s_c = lax.broadcasted_iota(jnp.int32, (c, GROUP), 1) % hd
    strict = s_c < t_c
    incl = s_c <= t_c
    eye_c = jnp.where(s_c == t_c, 1.0, 0.0)
    bd = (lax.broadcasted_iota(jnp.int32, (GROUP, GROUP), 0) // hd
          == lax.broadcasted_iota(jnp.int32, (GROUP, GROUP), 1) // hd)

    def expand(x_c):
        return jnp.where(bd, jnp.concatenate([x_c] * (GROUP // hd), axis=0), 0.0).astype(BF16)

    ones_bd = _block_ones(GROUP, hd)
    ng = N_GROUPS
    sls = [slice(g * GROUP, (g + 1) * GROUP) for g in range(ng)]

    def seg_sums(xs):
        hi, lo = _split2(jnp.concatenate(xs, axis=0))
        s = _dot(jnp.concatenate([hi, lo], axis=0), ones_bd)
        n = c * len(xs)
        s = s[:n] + s[n:]
        return [s[i * c:(i + 1) * c] for i in range(len(xs))]

    lhs2 = [jnp.concatenate([kap[:, sl], rq[:, sl]], axis=0).astype(BF16) for sl in sls]
    amat = [_dot_nt(lhs2[g], jnp.concatenate([expand(bet[:, sls[g]]), expand(kt[:, sls[g]])], axis=0))
            for g in range(ng)]
    l_c = [jnp.where(strict, a[:c, :GROUP], 0.0) for a in amat]
    a_lo = [jnp.concatenate([jnp.where(strict, a[:c, GROUP:], 0.0),
                             jnp.where(incl, a[c:, GROUP:], 0.0)], axis=0).astype(BF16) for a in amat]
    arb = [jnp.where(incl, a[c:, :GROUP], 0.0).astype(BF16) for a in amat]

    x_c = [eye_c - jnp.where((t_c >> 1) == (s_c >> 1), l, 0.0) for l in l_c]
    for lvl in range(2, 7):
        lmask = ((t_c >> lvl) == (s_c >> lvl)) & ((t_c >> (lvl - 1)) != (s_c >> (lvl - 1)))
        y_c = [_dot(x_c[g].astype(BF16), expand(jnp.where(lmask, l_c[g], 0.0))) for g in range(ng)]
        x_c = [x_c[g] - _dot(y_c[g].astype(BF16), expand(x_c[g])) for g in range(ng)]

    s0 = [s_ref[g] for g in range(ng)]
    rs = [_dot_nt(lhs2[g], s0[g].astype(BF16)) for g in range(ng)]
    av = [_dot(a_lo[g], expand(v[:, sls[g]])) for g in range(ng)]
    u_c = [_dot(x_c[g].astype(BF16), expand(rs[g][:c] + av[g][:c])) for g in range(ng)]
    o_c = [rs[g][c:] + av[g][c:] - _dot(arb[g], expand(u_c[g])) for g in range(ng)]
    for g in range(ng):
        upd = _dot_tn(jnp.concatenate([u_c[g], v[:, sls[g]]], axis=0).astype(BF16),
                      jnp.concatenate([-bhat[:, sls[g]], khat[:, sls[g]]], axis=0).astype(BF16))
        s_ref[g] = s0[g] * p_tot[:, sls[g]] + jnp.where(bd, upd, 0.0)

    sums = seg_sums(o_c + [r[:, sl] * kf[:, sl] * rk_ref[:, sl] for sl in sls])
    dev = [o_c[g] - sums[g] * (1.0 / hd) for g in range(ng)]
    var = seg_sums([dv * dv for dv in dev])
    o_parts = []
    for g in range(ng):
        o_n = dev[g] * lax.rsqrt(var[g] * (1.0 / hd) + GN_EPS) * gng_ref[:, sls[g]] + gnb_ref[:, sls[g]]
        o_parts.append(o_n + sums[ng + g] * v[:, sls[g]])
    o = jnp.concatenate(o_parts, axis=1)
    out_ref[...] = _sigmoid(zgb_ref[...]) * (o * g_ref[...])


def _rwkv_rec(r, lw, kf, v, kn, b, g, zin, rk, gng, gnb):
    t, d = r.shape
    blk = pl.BlockSpec((CHUNK, d), lambda i: (i, 0))
    row = pl.BlockSpec((1, d), lambda i: (0, 0))
    return pl.pallas_call(
        _rwkv_rec_kernel,
        grid=(t // CHUNK,),
        in_specs=[blk] * 7 + [pl.BlockSpec((CHUNK, d), lambda i: (i, 3)), row, row, row],
        out_specs=blk,
        out_shape=jax.ShapeDtypeStruct((t, d), F32),
        scratch_shapes=[pltpu.VMEM((N_GROUPS, GROUP, GROUP), F32)],
        compiler_params=_params(("arbitrary",)),
        name="rwkv_rec",
    )(r, lw, kf, v, kn, b, g, zin, rk, gng, gnb)


def _out_proj_kernel(ya_ref, yb_ref, h_ref, w_ref, g_ref, b_ref, o_ref):
    y = (ya_ref[...] + yb_ref[...]).astype(BF16)
    mix = _dot(y, w_ref[...])
    o_ref[...] = _layer_norm(ALPHA * h_ref[...] + mix, g_ref[...], b_ref[...])


def _out_proj(ya, yb, h, w, g, b, tm=256):
    t, d = h.shape
    blk = pl.BlockSpec((tm, d), lambda i: (i, 0))
    row = pl.BlockSpec((1, d), lambda i: (0, 0))
    return pl.pallas_call(
        _out_proj_kernel,
        grid=(t // tm,),
        in_specs=[blk, blk, blk, pl.BlockSpec((d, d), lambda i: (0, 0)), row, row],
        out_specs=blk,
        out_shape=jax.ShapeDtypeStruct((t, d), F32),
        compiler_params=_params(("arbitrary",)),
        name="out_proj",
    )(ya, yb, h, w, g, b)


def _mem_kv_kernel(mem_ref, wk_ref, wv_ref, k_ref, v_ref):
    m = mem_ref[...].astype(BF16)
    k_ref[...] = _dot(m, wk_ref[...]).astype(BF16)
    v_ref[...] = _dot(m, wv_ref[...]).astype(BF16)


def _mem_kv(mem, wk, wv, tn=512):
    n, d = mem.shape
    return pl.pallas_call(
        _mem_kv_kernel,
        grid=(d // tn,),
        in_specs=[pl.BlockSpec((n, d), lambda j: (0, 0)),
                  pl.BlockSpec((d, tn), lambda j: (0, j)),
                  pl.BlockSpec((d, tn), lambda j: (0, j))],
        out_specs=[pl.BlockSpec((n, tn), lambda j: (0, j))] * 2,
        out_shape=[jax.ShapeDtypeStruct((n, d), BF16)] * 2,
        compiler_params=_params(("arbitrary",)),
        name="mem_kv",
    )(mem, wk, wv)


def _xattn_kernel(h_ref, wq_ref, k_ref, v_ref, wo_ref, g_ref, b_ref, o_ref):
    h = h_ref[...]
    q = _dot(h.astype(BF16), wq_ref[...]).astype(BF16)
    scale = XATTN_HEAD_DIM ** -0.5
    outs = []
    for hh in range(XATTN_HEADS):
        sl = slice(hh * XATTN_HEAD_DIM, (hh + 1) * XATTN_HEAD_DIM)
        s = _dot_nt(q[:, sl], k_ref[:, sl]) * scale
        e = jnp.exp(s - jnp.max(s, axis=-1, keepdims=True))
        p = e / jnp.sum(e, axis=-1, keepdims=True)
        outs.append(_dot(p.astype(BF16), v_ref[:, sl]))
    o = jnp.concatenate(outs, axis=1).astype(BF16)
    xa = _dot(o, wo_ref[...])
    o_ref[...] = _layer_norm(ALPHA * h + xa, g_ref[...], b_ref[...])


def _xattn(h, wq, kmem, vmem, wo, g, b, tm=256):
    t, d = h.shape
    n = kmem.shape[0]
    blk = pl.BlockSpec((tm, d), lambda i: (i, 0))
    row = pl.BlockSpec((1, d), lambda i: (0, 0))
    full = lambda i: (0, 0)
    return pl.pallas_call(
        _xattn_kernel,
        grid=(t // tm,),
        in_specs=[blk, pl.BlockSpec((d, d), full), pl.BlockSpec((n, d), full),
                  pl.BlockSpec((n, d), full), pl.BlockSpec((d, d), full), row, row],
        out_specs=blk,
        out_shape=jax.ShapeDtypeStruct((t, d), F32),
        compiler_params=_params(("arbitrary",)),
        name="xattn",
    )(h, wq, kmem, vmem, wo, g, b)


def _ffn_kernel(h_ref, wg_ref, wu_ref, wd_ref, g_ref, b_ref, o_ref, hb_ref, acc_ref):
    j = pl.program_id(1)

    @pl.when(j == 0)
    def _():
        hb_ref[...] = h_ref[...].astype(BF16)
        acc_ref[...] = jnp.zeros_like(acc_ref)

    hb = hb_ref[...]
    gate = _dot(hb, wg_ref[...])
    up = _dot(hb, wu_ref[...])
    act = (gate * _sigmoid(gate) * up).astype(BF16)
    acc_ref[...] += _dot(act, wd_ref[...])

    @pl.when(j == pl.num_programs(1) - 1)
    def _():
        o_ref[...] = _layer_norm(ALPHA * h_ref[...] + acc_ref[...], g_ref[...], b_ref[...])


def _ffn(h, wg, wu, wd, g, b, tm=512, tf=512):
    t, d = h.shape
    f = wg.shape[1]
    blk = pl.BlockSpec((tm, d), lambda i, j: (i, 0))
    row = pl.BlockSpec((1, d), lambda i, j: (0, 0))
    return pl.pallas_call(
        _ffn_kernel,
        grid=(t // tm, f // tf),
        in_specs=[blk,
                  pl.BlockSpec((d, tf), lambda i, j: (0, j)),
                  pl.BlockSpec((d, tf), lambda i, j: (0, j)),
                  pl.BlockSpec((tf, d), lambda i, j: (j, 0)),
                  row, row],
        out_specs=blk,
        out_shape=jax.ShapeDtypeStruct((t, d), F32),
        scratch_shapes=[pltpu.VMEM((tm, d), BF16), pltpu.VMEM((tm, d), F32)],
        compiler_params=_params(("arbitrary", "arbitrary")),
        name="ffn",
    )(h, wg, wu, wd, g, b)


def _pad_cols(w, n):
    return jnp.pad(w, ((0, 0), (0, n - w.shape[1])))


def _pad_rows(w, n):
    return jnp.pad(w, ((0, n - w.shape[0]), (0, 0)))


def _regroup_in_cols(w):
    d = D_MODEL
    o = 2 * d
    u, gate = w[:, :d], w[:, d:2 * d]
    rkv = w[:, o:o + 3 * d]
    w_lo = w[:, o + 3 * d:o + 3 * d + 96]
    a_lo = w[:, o + 3 * d + 96:o + 3 * d + 192]
    g_lo = w[:, o + 3 * d + 192:o + 3 * d + 192 + D_GATE_LORA]
    ga = w[:, o + 3 * d + 192 + D_GATE_LORA:o + 4 * d + 192 + D_GATE_LORA]
    gb = w[:, o + 4 * d + 192 + D_GATE_LORA:]
    return jnp.concatenate(
        [u, gate, ga, gb, rkv, _pad_cols(w_lo, D_LORA_PAD), _pad_cols(a_lo, D_LORA_PAD), g_lo], axis=1)


def kernel(x, mem, ln_in_g, ln_in_b, w_in, conv_w, conv_b, lru_wa, lru_ba, lru_wx, lru_bx, lru_lambda, rw_mu, rw_w0, rw_wB, rw_a0, rw_aB, rw_gB, rw_kk, rw_ka, rw_rk, rw_gn_g, rw_gn_b, w_out, ln1_g, ln1_b, xa_wq, xa_wk, xa_wv, xa_wo, ln2_g, ln2_b, ffn_wg, ffn_wu, ffn_wd, ln3_g, ln3_b):
    bsz, t, d = x.shape
    depth = w_in.shape[0]
    assert bsz == 1 and d == D_MODEL and t % 512 == 0
    row = lambda p: p.reshape(1, -1)

    h = None
    for l in range(depth):
        w_in_p = _regroup_in_cols(w_in[l]).astype(BF16)
        if l == 0:
            h, zin = _in_proj(x[0], row(ln_in_g), row(ln_in_b), w_in_p)
        else:
            raise NotImplementedError("DEPTH > 1 is not part of this problem")

        ya = _lru(zin, conv_w[l], row(conv_b[l]), lru_wa[l].astype(BF16), row(lru_ba[l]),
                  lru_wx[l].astype(BF16), row(lru_bx[l]), row(lru_lambda[l]))

        mu = _regroup_in_cols(jnp.pad(row(rw_mu[l]), ((0, 0), (2 * d, 2 * d))))
        mu_rkv, mu_l = mu[:, 4 * d:7 * d], mu[:, 7 * d:]
        r, lw, kf, v, kn, b, g = _rwkv_prep(
            zin, mu_rkv, mu_l, row(rw_w0[l]),
            _pad_rows(rw_wB[l], D_LORA_PAD).astype(BF16), row(rw_a0[l]),
            _pad_rows(rw_aB[l], D_LORA_PAD).astype(BF16), rw_gB[l].astype(BF16),
            row(rw_kk[l]), row(rw_ka[l]))
        yb = _rwkv_rec(r, lw, kf, v, kn, b, g, zin, row(rw_rk[l]), row(rw_gn_g[l]), row(rw_gn_b[l]))

        h = _out_proj(ya, yb, h, w_out[l].astype(BF16), row(ln1_g[l]), row(ln1_b[l]))
        kmem, vmem = _mem_kv(mem[0], xa_wk[l].astype(BF16), xa_wv[l].astype(BF16))
        h = _xattn(h, xa_wq[l].astype(BF16), kmem, vmem, xa_wo[l].astype(BF16),
                   row(ln2_g[l]), row(ln2_b[l]))
        h = _ffn(h, ffn_wg[l].astype(BF16), ffn_wu[l].astype(BF16), ffn_wd[l].astype(BF16),
                 row(ln3_g[l]), row(ln3_b[l]))
    return h[None]
```

```python
import functools

import jax
import jax.numpy as jnp
from jax import lax
from jax.experimental import pallas as pl
from jax.experimental.pallas import tpu as pltpu

F32 = jnp.float32
BF16 = jnp.bfloat16

D_MODEL = 2048
LN_EPS = 1e-5
ALPHA = 2.0 ** 0.25

LRU_HEADS = 16
LRU_HEAD_DIM = 128
CONV_WIDTH = 4
LRU_C = 8.0

RWKV_HEAD_DIM = 64
D_LORA_PAD = 128
D_GATE_LORA = 256
GN_EPS = 64e-5
CHUNK = 64
GROUP = 256
N_GROUPS = D_MODEL // GROUP

XATTN_HEADS = 4
XATTN_HEAD_DIM = 512
N_MEM = 256
D_FF = 5632

VMEM_LIMIT = 56 * 1024 * 1024


def _dot(a, b):
    return jnp.dot(a, b, preferred_element_type=F32)


def _dot_nt(a, b):
    return lax.dot_general(a, b, (((1,), (1,)), ((), ())), preferred_element_type=F32)


def _dot_tn(a, b):
    return lax.dot_general(a, b, (((0,), (0,)), ((), ())), preferred_element_type=F32)


def _split2(x):
    hi = x.astype(BF16)
    lo = (x - hi.astype(F32)).astype(BF16)
    return hi, lo


def _split3(x):
    hi = x.astype(BF16)
    r1 = x - hi.astype(F32)
    mid = r1.astype(BF16)
    lo = (r1 - mid.astype(F32)).astype(BF16)
    return hi, mid, lo


def _sigmoid(x):
    return 1.0 / (1.0 + jnp.exp(-x))


def _softplus(x):
    return jnp.maximum(x, 0.0) + jnp.log1p(jnp.exp(-jnp.abs(x)))


def _gelu_tanh(x):
    c = 0.7978845608028654
    return 0.5 * x * (1.0 + jnp.tanh(c * (x + 0.044715 * (x * x * x))))


def _layer_norm(x, g, b):
    mu = jnp.mean(x, axis=-1, keepdims=True)
    xc = x - mu
    var = jnp.mean(xc * xc, axis=-1, keepdims=True)
    return xc * lax.rsqrt(var + LN_EPS) * g + b


def _shift_rows(z, prev8, s):
    rolled = pltpu.roll(z, s, 0)
    row8 = lax.broadcasted_iota(jnp.int32, prev8.shape, 0)
    head = jnp.where(row8 < s, pltpu.roll(prev8, s, 0), rolled[:8])
    return jnp.concatenate([head, rolled[8:]], axis=0)


def _params(sem):
    return pltpu.CompilerParams(dimension_semantics=sem, vmem_limit_bytes=VMEM_LIMIT)


IN_TN = 512
IN_MAIN_BLOCKS = 5 * D_MODEL // IN_TN
IN_TAIL_BLOCKS = (2 * D_MODEL + 2 * D_LORA_PAD + D_GATE_LORA) // IN_TN


def _in_out_block(j):
    b2 = 2 * D_MODEL // IN_TN
    return jnp.where(j < b2, j,
                     jnp.where(j < IN_MAIN_BLOCKS, j + b2,
                               jnp.where(j < IN_MAIN_BLOCKS + b2, j - (IN_MAIN_BLOCKS - b2), j)))


def _in_proj_kernel(x_ref, g_ref, b_ref, wm_ref, wt_ref, h_ref, z_ref, hb_ref):
    j = pl.program_id(1)

    @pl.when(j == 0)
    def _():
        h = _layer_norm(x_ref[...], g_ref[...], b_ref[...])
        h_ref[...] = h
        hb_ref[...] = h.astype(BF16)

    @pl.when(j < IN_MAIN_BLOCKS)
    def _():
        z_ref[...] = _dot(hb_ref[...], wm_ref[...])

    @pl.when(j >= IN_MAIN_BLOCKS)
    def _():
        z_ref[...] = _dot(hb_ref[...], wt_ref[...])


def _in_proj(x, g, b, w_main, w_tail, tm=1024):
    t, d = x.shape
    tn = IN_TN
    nb = IN_MAIN_BLOCKS + IN_TAIL_BLOCKS
    tm = min(tm, t)
    return pl.pallas_call(
        _in_proj_kernel,
        grid=(t // tm, nb),
        in_specs=[
            pl.BlockSpec((tm, d), lambda i, j: (i, 0)),
            pl.BlockSpec((1, d), lambda i, j: (0, 0)),
            pl.BlockSpec((1, d), lambda i, j: (0, 0)),
            pl.BlockSpec((d, tn), lambda i, j: (0, jnp.minimum(j, IN_MAIN_BLOCKS - 1))),
            pl.BlockSpec((d, tn), lambda i, j: (0, jnp.maximum(j - IN_MAIN_BLOCKS, 0))),
        ],
        out_specs=[
            pl.BlockSpec((tm, d), lambda i, j: (i, 0)),
            pl.BlockSpec((tm, tn), lambda i, j: (i, _in_out_block(j))),
        ],
        out_shape=[
            jax.ShapeDtypeStruct((t, d), F32),
            jax.ShapeDtypeStruct((t, nb * tn), F32),
        ],
        scratch_shapes=[pltpu.VMEM((tm, d), BF16)],
        compiler_params=_params(("arbitrary", "arbitrary")),
        name="in_proj",
    )(x, g, b, w_main, w_tail)


def _lru_kernel(u_ref, gate_ref, ga_ref, cw_ref, cb_ref, wa_ref, ba_ref, wx_ref, bx_ref,
                lam_ref, out_ref, ucarry, hcarry):
    @pl.when(pl.program_id(0) == 0)
    def _():
        ucarry[...] = jnp.zeros_like(ucarry)
        hcarry[...] = jnp.zeros_like(hcarry)

    u = u_ref[...]
    tm = u.shape[0]
    prev8 = ucarry[...]
    conv = cb_ref[...] + u * cw_ref[CONV_WIDTH - 1:CONV_WIDTH, :]
    for s in range(1, CONV_WIDTH):
        conv = conv + _shift_rows(u, prev8, s) * cw_ref[CONV_WIDTH - 1 - s:CONV_WIDTH - s, :]
    ucarry[...] = u[tm - 8:, :]

    cb16 = conv.astype(BF16)
    r_parts, i_parts = [], []
    for g in range(LRU_HEADS):
        ug = cb16[:, g * LRU_HEAD_DIM:(g + 1) * LRU_HEAD_DIM]
        r_parts.append(_dot(ug, wa_ref[g]))
        i_parts.append(_dot(ug, wx_ref[g]))
    r = _sigmoid(jnp.concatenate(r_parts, axis=1) + ba_ref[...])
    ig = _sigmoid(jnp.concatenate(i_parts, axis=1) + bx_ref[...])

    log_a = (-LRU_C) * r * _softplus(-lam_ref[...])
    a_c = jnp.exp(log_a)
    b_c = jnp.sqrt(-jnp.tanh(log_a) * (a_c * a_c + 1.0)) * (ig * conv)

    n8 = tm // 8
    a3 = a_c.reshape(n8, 8, a_c.shape[1])
    b3 = b_c.reshape(n8, 8, b_c.shape[1])
    sub = lax.broadcasted_iota(jnp.int32, a3.shape, 1)
    for d in (1, 2, 4):
        m = sub >= d
        a_sh = jnp.where(m, pltpu.roll(a3, d, 1), 1.0)
        b_sh = jnp.where(m, pltpu.roll(b3, d, 1), 0.0)
        b3 = a3 * b_sh + b3
        a3 = a3 * a_sh
    carry = hcarry[...]
    hs = []
    for i in range(n8):
        h_i = b3[i] + a3[i] * carry
        hs.append(h_i)
        carry = h_i[7:8, :]
    hcarry[...] = carry
    h = jnp.concatenate(hs, axis=0)

    y = _gelu_tanh(gate_ref[...]) * h
    out_ref[...] = _sigmoid(ga_ref[...]) * y


def _lru(zin, conv_w, conv_b, wa, ba, wx, bx, lam, tm=256):
    t = zin.shape[0]
    d = D_MODEL
    row = lambda i: (0, 0)
    return pl.pallas_call(
        _lru_kernel,
        grid=(t // tm,),
        in_specs=[
            pl.BlockSpec((tm, d), lambda i: (i, 0)),
            pl.BlockSpec((tm, d), lambda i: (i, 1)),
            pl.BlockSpec((tm, d), lambda i: (i, 2)),
            pl.BlockSpec((CONV_WIDTH, d), row),
            pl.BlockSpec((1, d), row),
            pl.BlockSpec((LRU_HEADS, LRU_HEAD_DIM, LRU_HEAD_DIM), lambda i: (0, 0, 0)),
            pl.BlockSpec((1, d), row),
            pl.BlockSpec((LRU_HEADS, LRU_HEAD_DIM, LRU_HEAD_DIM), lambda i: (0, 0, 0)),
            pl.BlockSpec((1, d), row),
            pl.BlockSpec((1, d), row),
        ],
        out_specs=pl.BlockSpec((tm, d), lambda i: (i, 0)),
        out_shape=jax.ShapeDtypeStruct((t, d), F32),
        scratch_shapes=[pltpu.VMEM((8, d), F32), pltpu.VMEM((1, d), F32)],
        compiler_params=_params(("arbitrary",)),
        name="lru",
    )(zin, zin, zin, conv_w, conv_b, wa, ba, wx, bx, lam)


def _block_ones(n, seg):
    r = lax.broadcasted_iota(jnp.int32, (n, n), 0) // seg
    c = lax.broadcasted_iota(jnp.int32, (n, n), 1) // seg
    return jnp.where(r == c, 1.0, 0.0).astype(BF16)


def _seg_sum(x, ones_bd):
    parts = []
    for g in range(x.shape[1] // GROUP):
        hi, lo = _split2(x[:, g * GROUP:(g + 1) * GROUP])
        parts.append(_dot(hi, ones_bd) + _dot(lo, ones_bd))
    return jnp.concatenate(parts, axis=1)


def _rwkv_prep_kernel(zr_ref, zk_ref, zv_ref, zl_ref, mu_ref, mul_ref, w0_ref, wb_ref, a0_ref,
                      ab_ref, gb_ref, kk_ref, ka_ref,
                      r_out, lw_out, kf_out, v_out, kn_out, b_out, g_out,
                      cr, ck, cv, cl):
    @pl.when(pl.program_id(0) == 0)
    def _():
        cr[...] = jnp.zeros_like(cr)
        ck[...] = jnp.zeros_like(ck)
        cv[...] = jnp.zeros_like(cv)
        cl[...] = jnp.zeros_like(cl)

    d = D_MODEL

    def shifted(z_ref, carry, mu):
        z = z_ref[...]
        tm = z.shape[0]
        zp = _shift_rows(z, carry[...], 1)
        carry[...] = z[tm - 8:, :]
        return z + (zp - z) * mu

    r = shifted(zr_ref, cr, mu_ref[:, 0:d])
    k = shifted(zk_ref, ck, mu_ref[:, d:2 * d])
    v = shifted(zv_ref, cv, mu_ref[:, 2 * d:3 * d])
    zl = shifted(zl_ref, cl, mul_ref[...])
    w_lo = zl[:, 0:D_LORA_PAD]
    a_lo = zl[:, D_LORA_PAD:2 * D_LORA_PAD]
    g_lo = zl[:, 2 * D_LORA_PAD:]

    w_log = -_softplus(-(w0_ref[...] + _dot(jnp.tanh(w_lo).astype(BF16), wb_ref[...]))) - 0.5
    lw_out[...] = -jnp.exp(w_log)
    a = _sigmoid(a0_ref[...] + _dot(a_lo.astype(BF16), ab_ref[...]))
    g_out[...] = _dot(_sigmoid(g_lo).astype(BF16), gb_ref[...])

    ones_bd = _block_ones(GROUP, RWKV_HEAD_DIM)
    kk = k * kk_ref[...]
    norm = jnp.sqrt(_seg_sum(kk * kk, ones_bd))
    kn = kk / jnp.maximum(norm, 1e-12)
    r_out[...] = r
    v_out[...] = v
    kn_out[...] = kn
    b_out[...] = kn * a
    kf_out[...] = k * (1.0 + (a - 1.0) * ka_ref[...])


def _rwkv_prep(zin, mu_rkv, mu_l, w0, wb, a0, ab, gb, k_k, k_a, tm=256):
    t = zin.shape[0]
    d = D_MODEL
    dl = 2 * D_LORA_PAD + D_GATE_LORA
    row = lambda i: (0, 0)
    out = jax.ShapeDtypeStruct((t, d), F32)
    blk = pl.BlockSpec((tm, d), lambda i: (i, 0))
    return pl.pallas_call(
        _rwkv_prep_kernel,
        grid=(t // tm,),
        in_specs=[
            pl.BlockSpec((tm, d), lambda i: (i, 4)),
            pl.BlockSpec((tm, d), lambda i: (i, 5)),
            pl.BlockSpec((tm, d), lambda i: (i, 6)),
            pl.BlockSpec((tm, dl), lambda i: (i, 7 * d // dl)),
            pl.BlockSpec((1, 3 * d), row),
            pl.BlockSpec((1, dl), row),
            pl.BlockSpec((1, d), row),
            pl.BlockSpec((D_LORA_PAD, d), row),
            pl.BlockSpec((1, d), row),
            pl.BlockSpec((D_LORA_PAD, d), row),
            pl.BlockSpec((D_GATE_LORA, d), row),
            pl.BlockSpec((1, d), row),
            pl.BlockSpec((1, d), row),
        ],
        out_specs=[blk] * 7,
        out_shape=[out] * 7,
        scratch_shapes=[pltpu.VMEM((8, d), F32), pltpu.VMEM((8, d), F32),
                        pltpu.VMEM((8, d), F32), pltpu.VMEM((8, dl), F32)],
        compiler_params=_params(("arbitrary",)),
        name="rwkv_prep",
    )(zin, zin, zin, zin, mu_rkv, mu_l, w0, wb, a0, ab, gb, k_k, k_a)


def _rwkv_rec_kernel(r_ref, lw_ref, kf_ref, v_ref, kn_ref, b_ref, g_ref, zgb_ref,
                     rk_ref, gng_ref, gnb_ref, out_ref, s_ref):
    @pl.when(pl.program_id(0) == 0)
    def _():
        s_ref[...] = jnp.zeros_like(s_ref)

    c = CHUNK
    hd = RWKV_HEAD_DIM
    lw = lw_ref[...]
    rows = lax.broadcasted_iota(jnp.int32, (c, c), 0)
    cols = lax.broadcasted_iota(jnp.int32, (c, c), 1)
    tri = jnp.where(cols <= rows, 1.0, 0.0).astype(BF16)
    hi, mid, lo = _split3(lw)
    cum = _dot(tri, hi) + _dot(tri, mid) + _dot(tri, lo)
    tot = cum[c - 1:c, :]
    p_in = jnp.exp(cum)
    p_ex = jnp.exp(cum - lw)
    p_inv = jnp.exp(-cum)
    p_end = jnp.exp(tot - cum)
    p_tot = jnp.exp(tot)

    r = r_ref[...]
    kf = kf_ref[...]
    v = v_ref[...]
    bb = b_ref[...]
    rq = r * p_in
    kap = kn_ref[...] * p_ex
    bet = bb * p_inv
    kt = kf * p_inv
    bhat = bb * p_end
    khat = kf * p_end

    t_c = lax.broadcasted_iota(jnp.int32, (c, GROUP), 0)
    s_c = lax.broadcasted_iota(jnp.int32, (c, GROUP), 1) % hd
    strict = s_c < t_c
    incl = s_c <= t_c
    eye_c = jnp.where(s_c == t_c, 1.0, 0.0)
    bd = (lax.broadcasted_iota(jnp.int32, (GROUP, GROUP), 0) // hd
          == lax.broadcasted_iota(jnp.int32, (GROUP, GROUP), 1) // hd)

    def expand(x_c):
        return jnp.where(bd, jnp.concatenate([x_c] * (GROUP // hd), axis=0), 0.0).astype(BF16)

    ones_bd = _block_ones(GROUP, hd)
    ng = N_GROUPS
    sls = [slice(g * GROUP, (g + 1) * GROUP) for g in range(ng)]

    def seg_sums(xs):
        hi, lo = _split2(jnp.concatenate(xs, axis=0))
        s = _dot(jnp.concatenate([hi, lo], axis=0), ones_bd)
        n = c * len(xs)
        s = s[:n] + s[n:]
        return [s[i * c:(i + 1) * c] for i in range(len(xs))]

    lhs2 = [jnp.concatenate([kap[:, sl], rq[:, sl]], axis=0).astype(BF16) for sl in sls]
    amat = [_dot_nt(lhs2[g], jnp.concatenate([expand(bet[:, sls[g]]), expand(kt[:, sls[g]])], axis=0))
            for g in range(ng)]
    l_c = [jnp.where(strict, a[:c, :GROUP], 0.0) for a in amat]
    a_lo = [jnp.concatenate([jnp.where(strict, a[:c, GROUP:], 0.0),
                             jnp.where(incl, a[c:, GROUP:], 0.0)], axis=0).astype(BF16) for a in amat]
    arb = [jnp.where(incl, a[c:, :GROUP], 0.0).astype(BF16) for a in amat]

    x_c = [eye_c - jnp.where((t_c >> 1) == (s_c >> 1), l, 0.0) for l in l_c]
    for lvl in range(2, 7):
        lmask = ((t_c >> lvl) == (s_c >> lvl)) & ((t_c >> (lvl - 1)) != (s_c >> (lvl - 1)))
        y_c = [_dot(x_c[g].astype(BF16), expand(jnp.where(lmask, l_c[g], 0.0))) for g in range(ng)]
        x_c = [x_c[g] - _dot(y_c[g].astype(BF16), expand(x_c[g])) for g in range(ng)]

    s0 = [s_ref[g] for g in range(ng)]
    rs = [_dot_nt(lhs2[g], s0[g].astype(BF16)) for g in range(ng)]
    av = [_dot(a_lo[g], expand(v[:, sls[g]])) for g in range(ng)]
    u_c = [_dot(x_c[g].astype(BF16), expand(rs[g][:c] + av[g][:c])) for g in range(ng)]
    o_c = [rs[g][c:] + av[g][c:] - _dot(arb[g], expand(u_c[g])) for g in range(ng)]
    for g in range(ng):
        upd = _dot_tn(jnp.concatenate([u_c[g], v[:, sls[g]]], axis=0).astype(BF16),
                      jnp.concatenate([-bhat[:, sls[g]], khat[:, sls[g]]], axis=0).astype(BF16))
        s_ref[g] = s0[g] * p_tot[:, sls[g]] + jnp.where(bd, upd, 0.0)

    sums = seg_sums(o_c + [r[:, sl] * kf[:, sl] * rk_ref[:, sl] for sl in sls])
    dev = [o_c[g] - sums[g] * (1.0 / hd) for g in range(ng)]
    var = seg_sums([dv * dv for dv in dev])
    o_parts = []
    for g in range(ng):
        o_n = dev[g] * lax.rsqrt(var[g] * (1.0 / hd) + GN_EPS) * gng_ref[:, sls[g]] + gnb_ref[:, sls[g]]
        o_parts.append(o_n + sums[ng + g] * v[:, sls[g]])
    o = jnp.concatenate(o_parts, axis=1)
    out_ref[...] = _sigmoid(zgb_ref[...]) * (o * g_ref[...])


def _rwkv_rec(r, lw, kf, v, kn, b, g, zin, rk, gng, gnb):
    t, d = r.shape
    blk = pl.BlockSpec((CHUNK, d), lambda i: (i, 0))
    row = pl.BlockSpec((1, d), lambda i: (0, 0))
    return pl.pallas_call(
        _rwkv_rec_kernel,
        grid=(t // CHUNK,),
        in_specs=[blk] * 7 + [pl.BlockSpec((CHUNK, d), lambda i: (i, 3)), row, row, row],
        out_specs=blk,
        out_shape=jax.ShapeDtypeStruct((t, d), F32),
        scratch_shapes=[pltpu.VMEM((N_GROUPS, GROUP, GROUP), F32)],
        compiler_params=_params(("arbitrary",)),
        name="rwkv_rec",
    )(r, lw, kf, v, kn, b, g, zin, rk, gng, gnb)


def _out_proj_kernel(ya_ref, yb_ref, h_ref, w_ref, g_ref, b_ref, o_ref):
    y = (ya_ref[...] + yb_ref[...]).astype(BF16)
    mix = _dot(y, w_ref[...])
    o_ref[...] = _layer_norm(ALPHA * h_ref[...] + mix, g_ref[...], b_ref[...])


def _resident(shape):
    return pl.BlockSpec(shape, lambda *_: (0,) * len(shape), pipeline_mode=pl.Buffered(1))


def _out_proj(ya, yb, h, w, g, b, tm=512):
    t, d = h.shape
    blk = pl.BlockSpec((tm, d), lambda i: (i, 0))
    row = pl.BlockSpec((1, d), lambda i: (0, 0))
    return pl.pallas_call(
        _out_proj_kernel,
        grid=(t // tm,),
        in_specs=[blk, blk, blk, _resident((d, d)), row, row],
        out_specs=blk,
        out_shape=jax.ShapeDtypeStruct((t, d), F32),
        compiler_params=_params(("arbitrary",)),
        name="out_proj",
    )(ya, yb, h, w, g, b)


def _mem_kv_kernel(mem_ref, wk_ref, wv_ref, k_ref, v_ref):
    m = mem_ref[...].astype(BF16)
    k_ref[...] = _dot(m, wk_ref[...]).astype(BF16)
    v_ref[...] = _dot(m, wv_ref[...]).astype(BF16)


def _mem_kv(mem, wk, wv, tn=512):
    n, d = mem.shape
    return pl.pallas_call(
        _mem_kv_kernel,
        grid=(d // tn,),
        in_specs=[pl.BlockSpec((n, d), lambda j: (0, 0)),
                  pl.BlockSpec((d, tn), lambda j: (0, j)),
                  pl.BlockSpec((d, tn), lambda j: (0, j))],
        out_specs=[pl.BlockSpec((n, tn), lambda j: (0, j))] * 2,
        out_shape=[jax.ShapeDtypeStruct((n, d), BF16)] * 2,
        compiler_params=_params(("arbitrary",)),
        name="mem_kv",
    )(mem, wk, wv)


def _xattn_kernel(h_ref, wq_ref, k_ref, v_ref, wo_ref, g_ref, b_ref, o_ref):
    h = h_ref[...]
    q = _dot(h.astype(BF16), wq_ref[...]).astype(BF16)
    scale = XATTN_HEAD_DIM ** -0.5
    outs = []
    for hh in range(XATTN_HEADS):
        sl = slice(hh * XATTN_HEAD_DIM, (hh + 1) * XATTN_HEAD_DIM)
        s = _dot_nt(q[:, sl], k_ref[:, sl]) * scale
        e = jnp.exp(s - jnp.max(s, axis=-1, keepdims=True))
        p = e / jnp.sum(e, axis=-1, keepdims=True)
        outs.append(_dot(p.astype(BF16), v_ref[:, sl]))
    o = jnp.concatenate(outs, axis=1).astype(BF16)
    xa = _dot(o, wo_ref[...])
    o_ref[...] = _layer_norm(ALPHA * h + xa, g_ref[...], b_ref[...])


def _xattn(h, wq, kmem, vmem, wo, g, b, tm=512):
    t, d = h.shape
    n = kmem.shape[0]
    blk = pl.BlockSpec((tm, d), lambda i: (i, 0))
    row = pl.BlockSpec((1, d), lambda i: (0, 0))
    return pl.pallas_call(
        _xattn_kernel,
        grid=(t // tm,),
        in_specs=[blk, _resident((d, d)), _resident((n, d)), _resident((n, d)), _resident((d, d)),
                  row, row],
        out_specs=blk,
        out_shape=jax.ShapeDtypeStruct((t, d), F32),
        compiler_params=_params(("arbitrary",)),
        name="xattn",
    )(h, wq, kmem, vmem, wo, g, b)


def _ffn_kernel(h_ref, wg_ref, wu_ref, wd_ref, g_ref, b_ref, o_ref, hb_ref, acc_ref):
    j = pl.program_id(1)

    @pl.when(j == 0)
    def _():
        hb_ref[...] = h_ref[...].astype(BF16)
        acc_ref[...] = jnp.zeros_like(acc_ref)

    hb = hb_ref[...]
    gate = _dot(hb, wg_ref[...])
    up = _dot(hb, wu_ref[...])
    act = (gate * _sigmoid(gate) * up).astype(BF16)
    acc_ref[...] += _dot(act, wd_ref[...])

    @pl.when(j == pl.num_programs(1) - 1)
    def _():
        o_ref[...] = _layer_norm(ALPHA * h_ref[...] + acc_ref[...], g_ref[...], b_ref[...])


def _ffn(h, wg, wu, wd, g, b, tm=512, tf=512):
    t, d = h.shape
    f = wg.shape[1]
    blk = pl.BlockSpec((tm, d), lambda i, j: (i, 0))
    row = pl.BlockSpec((1, d), lambda i, j: (0, 0))
    return pl.pallas_call(
        _ffn_kernel,
        grid=(t // tm, f // tf),
        in_specs=[blk,
                  pl.BlockSpec((d, tf), lambda i, j: (0, j)),
                  pl.BlockSpec((d, tf), lambda i, j: (0, j)),
                  pl.BlockSpec((tf, d), lambda i, j: (j, 0)),
                  row, row],
        out_specs=blk,
        out_shape=jax.ShapeDtypeStruct((t, d), F32),
        scratch_shapes=[pltpu.VMEM((tm, d), BF16), pltpu.VMEM((tm, d), F32)],
        compiler_params=_params(("arbitrary", "arbitrary")),
        name="ffn",
    )(h, wg, wu, wd, g, b)


def _pad_cols(w, n):
    return jnp.pad(w, ((0, 0), (0, n - w.shape[1])))


def _pad_rows(w, n):
    return jnp.pad(w, ((0, n - w.shape[0]), (0, 0)))


def _split_in_cols(w):
    d = D_MODEL
    o = 5 * d
    lo = w.shape[1] - 7 * d - D_GATE_LORA
    w_lo, a_lo = w[:, o:o + lo // 2], w[:, o + lo // 2:o + lo]
    g_lo = w[:, o + lo:o + lo + D_GATE_LORA]
    gates = w[:, o + lo + D_GATE_LORA:]
    tail = jnp.concatenate(
        [gates, _pad_cols(w_lo, D_LORA_PAD), _pad_cols(a_lo, D_LORA_PAD), g_lo], axis=1)
    return w[:, :o], tail


def kernel(x, mem, ln_in_g, ln_in_b, w_in, conv_w, conv_b, lru_wa, lru_ba, lru_wx, lru_bx, lru_lambda, rw_mu, rw_w0, rw_wB, rw_a0, rw_aB, rw_gB, rw_kk, rw_ka, rw_rk, rw_gn_g, rw_gn_b, w_out, ln1_g, ln1_b, xa_wq, xa_wk, xa_wv, xa_wo, ln2_g, ln2_b, ffn_wg, ffn_wu, ffn_wd, ln3_g, ln3_b):
    bsz, t, d = x.shape
    depth = w_in.shape[0]
    assert bsz == 1 and d == D_MODEL and t % 512 == 0
    row = lambda p: p.reshape(1, -1)

    h = None
    for l in range(depth):
        w_main, w_tail = _split_in_cols(w_in[l])
        if l == 0:
            h, zin = _in_proj(x[0], row(ln_in_g), row(ln_in_b), w_main.astype(BF16), w_tail.astype(BF16))
        else:
            raise NotImplementedError("DEPTH > 1 is not part of this problem")

        ya = _lru(zin, conv_w[l], row(conv_b[l]), lru_wa[l].astype(BF16), row(lru_ba[l]),
                  lru_wx[l].astype(BF16), row(lru_bx[l]), row(lru_lambda[l]))

        mu_main, mu_tail = _split_in_cols(jnp.pad(row(rw_mu[l]), ((0, 0), (2 * d, 2 * d))))
        mu_rkv, mu_l = mu_main[:, 2 * d:], mu_tail[:, 2 * d:]
        r, lw, kf, v, kn, b, g = _rwkv_prep(
            zin, mu_rkv, mu_l, row(rw_w0[l]),
            _pad_rows(rw_wB[l], D_LORA_PAD).astype(BF16), row(rw_a0[l]),
            _pad_rows(rw_aB[l], D_LORA_PAD).astype(BF16), rw_gB[l].astype(BF16),
            row(rw_kk[l]), row(rw_ka[l]))
        yb = _rwkv_rec(r, lw, kf, v, kn, b, g, zin, row(rw_rk[l]), row(rw_gn_g[l]), row(rw_gn_b[l]))

        h = _out_proj(ya, yb, h, w_out[l].astype(BF16), row(ln1_g[l]), row(ln1_b[l]))
        kmem, vmem = _mem_kv(mem[0], xa_wk[l].astype(BF16), xa_wv[l].astype(BF16))
        h = _xattn(h, xa_wq[l].astype(BF16), kmem, vmem, xa_wo[l].astype(BF16),
                   row(ln2_g[l]), row(ln2_b[l]))
        h = _ffn(h, ffn_wg[l].astype(BF16), ffn_wu[l].astype(BF16), ffn_wd[l].astype(BF16),
                 row(ln3_g[l]), row(ln3_b[l]))
    return h[None]
```

```python
import functools

import jax
import jax.numpy as jnp
from jax import lax
from jax.experimental import pallas as pl
from jax.experimental.pallas import tpu as pltpu

F32 = jnp.float32
BF16 = jnp.bfloat16

D_MODEL = 2048
LN_EPS = 1e-5
ALPHA = 2.0 ** 0.25

LRU_HEADS = 16
LRU_HEAD_DIM = 128
CONV_WIDTH = 4
LRU_C = 8.0

RWKV_HEAD_DIM = 64
D_LORA_PAD = 128
D_GATE_LORA = 256
GN_EPS = 64e-5
CHUNK = 64
GROUP = 256
N_GROUPS = D_MODEL // GROUP

XATTN_HEADS = 4
XATTN_HEAD_DIM = 512
N_MEM = 256
D_FF = 5632

VMEM_LIMIT = 56 * 1024 * 1024


def _dot(a, b):
    return jnp.dot(a, b, preferred_element_type=F32)


def _dot_nt(a, b):
    return lax.dot_general(a, b, (((1,), (1,)), ((), ())), preferred_element_type=F32)


def _dot_tn(a, b):
    return lax.dot_general(a, b, (((0,), (0,)), ((), ())), preferred_element_type=F32)


def _split2(x):
    hi = x.astype(BF16)
    lo = (x - hi.astype(F32)).astype(BF16)
    return hi, lo


def _split3(x):
    hi = x.astype(BF16)
    r1 = x - hi.astype(F32)
    mid = r1.astype(BF16)
    lo = (r1 - mid.astype(F32)).astype(BF16)
    return hi, mid, lo


def _sigmoid(x):
    return 1.0 / (1.0 + jnp.exp(-x))


def _softplus(x):
    return jnp.maximum(x, 0.0) + jnp.log1p(jnp.exp(-jnp.abs(x)))


def _gelu_tanh(x):
    c = 0.7978845608028654
    return 0.5 * x * (1.0 + jnp.tanh(c * (x + 0.044715 * (x * x * x))))


def _layer_norm(x, g, b):
    mu = jnp.mean(x, axis=-1, keepdims=True)
    xc = x - mu
    var = jnp.mean(xc * xc, axis=-1, keepdims=True)
    return xc * lax.rsqrt(var + LN_EPS) * g + b


def _shift_rows(z, prev8, s):
    rolled = pltpu.roll(z, s, 0)
    row8 = lax.broadcasted_iota(jnp.int32, prev8.shape, 0)
    head = jnp.where(row8 < s, pltpu.roll(prev8, s, 0), rolled[:8])
    return jnp.concatenate([head, rolled[8:]], axis=0)


def _params(sem):
    return pltpu.CompilerParams(dimension_semantics=sem, vmem_limit_bytes=VMEM_LIMIT)


IN_TN = 512
IN_MAIN_BLOCKS = 5 * D_MODEL // IN_TN
IN_TAIL_BLOCKS = (2 * D_MODEL + 2 * D_LORA_PAD + D_GATE_LORA) // IN_TN


def _in_out_block(j):
    b2 = 2 * D_MODEL // IN_TN
    return jnp.where(j < b2, j,
                     jnp.where(j < IN_MAIN_BLOCKS, j + b2,
                               jnp.where(j < IN_MAIN_BLOCKS + b2, j - (IN_MAIN_BLOCKS - b2), j)))


def _in_proj_kernel(x_ref, g_ref, b_ref, wm_ref, wt_ref, h_ref, z_ref, hb_ref):
    j = pl.program_id(1)

    @pl.when(j == 0)
    def _():
        h = _layer_norm(x_ref[...], g_ref[...], b_ref[...])
        h_ref[...] = h
        hb_ref[...] = h.astype(BF16)

    @pl.when(j < IN_MAIN_BLOCKS)
    def _():
        z_ref[...] = _dot(hb_ref[...], wm_ref[...])

    @pl.when(j >= IN_MAIN_BLOCKS)
    def _():
        z_ref[...] = _dot(hb_ref[...], wt_ref[...])


def _in_proj(x, g, b, w_main, w_tail, tm=1024):
    t, d = x.shape
    tn = IN_TN
    nb = IN_MAIN_BLOCKS + IN_TAIL_BLOCKS
    tm = min(tm, t)
    return pl.pallas_call(
        _in_proj_kernel,
        grid=(t // tm, nb),
        in_specs=[
            pl.BlockSpec((tm, d), lambda i, j: (i, 0)),
            pl.BlockSpec((1, d), lambda i, j: (0, 0)),
            pl.BlockSpec((1, d), lambda i, j: (0, 0)),
            pl.BlockSpec((d, tn), lambda i, j: (0, jnp.minimum(j, IN_MAIN_BLOCKS - 1))),
            pl.BlockSpec((d, tn), lambda i, j: (0, jnp.maximum(j - IN_MAIN_BLOCKS, 0))),
        ],
        out_specs=[
            pl.BlockSpec((tm, d), lambda i, j: (i, 0)),
            pl.BlockSpec((tm, tn), lambda i, j: (i, _in_out_block(j))),
        ],
        out_shape=[
            jax.ShapeDtypeStruct((t, d), F32),
            jax.ShapeDtypeStruct((t, nb * tn), F32),
        ],
        scratch_shapes=[pltpu.VMEM((tm, d), BF16)],
        compiler_params=_params(("arbitrary", "arbitrary")),
        name="in_proj",
    )(x, g, b, w_main, w_tail)


def _lru_kernel(u_ref, gate_ref, ga_ref, cw_ref, cb_ref, wa_ref, ba_ref, wx_ref, bx_ref,
                lam_ref, out_ref, ucarry, hcarry):
    @pl.when(pl.program_id(0) == 0)
    def _():
        ucarry[...] = jnp.zeros_like(ucarry)
        hcarry[...] = jnp.zeros_like(hcarry)

    u = u_ref[...]
    tm = u.shape[0]
    prev8 = ucarry[...]
    conv = cb_ref[...] + u * cw_ref[CONV_WIDTH - 1:CONV_WIDTH, :]
    for s in range(1, CONV_WIDTH):
        conv = conv + _shift_rows(u, prev8, s) * cw_ref[CONV_WIDTH - 1 - s:CONV_WIDTH - s, :]
    ucarry[...] = u[tm - 8:, :]

    cb16 = conv.astype(BF16)
    r_parts, i_parts = [], []
    for g in range(LRU_HEADS):
        ug = cb16[:, g * LRU_HEAD_DIM:(g + 1) * LRU_HEAD_DIM]
        r_parts.append(_dot(ug, wa_ref[g]))
        i_parts.append(_dot(ug, wx_ref[g]))
    r = _sigmoid(jnp.concatenate(r_parts, axis=1) + ba_ref[...])
    ig = _sigmoid(jnp.concatenate(i_parts, axis=1) + bx_ref[...])

    log_a = (-LRU_C) * r * _softplus(-lam_ref[...])
    a_c = jnp.exp(log_a)
    b_c = jnp.sqrt(-jnp.tanh(log_a) * (a_c * a_c + 1.0)) * (ig * conv)

    n8 = tm // 8
    a3 = a_c.reshape(n8, 8, a_c.shape[1])
    b3 = b_c.reshape(n8, 8, b_c.shape[1])
    sub = lax.broadcasted_iota(jnp.int32, a3.shape, 1)
    for d in (1, 2, 4):
        m = sub >= d
        a_sh = jnp.where(m, pltpu.roll(a3, d, 1), 1.0)
        b_sh = jnp.where(m, pltpu.roll(b3, d, 1), 0.0)
        b3 = a3 * b_sh + b3
        a3 = a3 * a_sh
    carry = hcarry[...]
    hs = []
    for i in range(n8):
        h_i = b3[i] + a3[i] * carry
        hs.append(h_i)
        carry = h_i[7:8, :]
    hcarry[...] = carry
    h = jnp.concatenate(hs, axis=0)

    y = _gelu_tanh(gate_ref[...]) * h
    out_ref[...] = _sigmoid(ga_ref[...]) * y


def _lru(zin, conv_w, conv_b, wa, ba, wx, bx, lam, tm=256):
    t = zin.shape[0]
    d = D_MODEL
    row = lambda i: (0, 0)
    return pl.pallas_call(
        _lru_kernel,
        grid=(t // tm,),
        in_specs=[
            pl.BlockSpec((tm, d), lambda i: (i, 0)),
            pl.BlockSpec((tm, d), lambda i: (i, 1)),
            pl.BlockSpec((tm, d), lambda i: (i, 2)),
            pl.BlockSpec((CONV_WIDTH, d), row),
            pl.BlockSpec((1, d), row),
            pl.BlockSpec((LRU_HEADS, LRU_HEAD_DIM, LRU_HEAD_DIM), lambda i: (0, 0, 0)),
            pl.BlockSpec((1, d), row),
            pl.BlockSpec((LRU_HEADS, LRU_HEAD_DIM, LRU_HEAD_DIM), lambda i: (0, 0, 0)),
            pl.BlockSpec((1, d), row),
            pl.BlockSpec((1, d), row),
        ],
        out_specs=pl.BlockSpec((tm, d), lambda i: (i, 0)),
        out_shape=jax.ShapeDtypeStruct((t, d), F32),
        scratch_shapes=[pltpu.VMEM((8, d), F32), pltpu.VMEM((1, d), F32)],
        compiler_params=_params(("arbitrary",)),
        name="lru",
    )(zin, zin, zin, conv_w, conv_b, wa, ba, wx, bx, lam)


def _block_ones(n, seg):
    r = lax.broadcasted_iota(jnp.int32, (n, n), 0) // seg
    c = lax.broadcasted_iota(jnp.int32, (n, n), 1) // seg
    return jnp.where(r == c, 1.0, 0.0).astype(BF16)


def _seg_sum(x, ones_bd):
    parts = []
    for g in range(x.shape[1] // GROUP):
        hi, lo = _split2(x[:, g * GROUP:(g + 1) * GROUP])
        parts.append(_dot(hi, ones_bd) + _dot(lo, ones_bd))
    return jnp.concatenate(parts, axis=1)


def _rwkv_prep_kernel(zr_ref, zk_ref, zv_ref, zl_ref, mu_ref, mul_ref, w0_ref, wb_ref, a0_ref,
                      ab_ref, gb_ref, kk_ref, ka_ref,
                      r_out, lw_out, kf_out, v_out, kn_out, b_out, g_out,
                      cr, ck, cv, cl):
    @pl.when(pl.program_id(0) == 0)
    def _():
        cr[...] = jnp.zeros_like(cr)
        ck[...] = jnp.zeros_like(ck)
        cv[...] = jnp.zeros_like(cv)
        cl[...] = jnp.zeros_like(cl)

    d = D_MODEL

    def shifted(z_ref, carry, mu):
        z = z_ref[...]
        tm = z.shape[0]
        zp = _shift_rows(z, carry[...], 1)
        carry[...] = z[tm - 8:, :]
        return z + (zp - z) * mu

    r = shifted(zr_ref, cr, mu_ref[:, 0:d])
    k = shifted(zk_ref, ck, mu_ref[:, d:2 * d])
    v = shifted(zv_ref, cv, mu_ref[:, 2 * d:3 * d])
    zl = shifted(zl_ref, cl, mul_ref[...])
    w_lo = zl[:, 0:D_LORA_PAD]
    a_lo = zl[:, D_LORA_PAD:2 * D_LORA_PAD]
    g_lo = zl[:, 2 * D_LORA_PAD:]

    w_log = -_softplus(-(w0_ref[...] + _dot(jnp.tanh(w_lo).astype(BF16), wb_ref[...]))) - 0.5
    lw_out[...] = -jnp.exp(w_log)
    a = _sigmoid(a0_ref[...] + _dot(a_lo.astype(BF16), ab_ref[...]))
    g_out[...] = _dot(_sigmoid(g_lo).astype(BF16), gb_ref[...])

    ones_bd = _block_ones(GROUP, RWKV_HEAD_DIM)
    kk = k * kk_ref[...]
    norm = jnp.sqrt(_seg_sum(kk * kk, ones_bd))
    kn = kk / jnp.maximum(norm, 1e-12)
    r_out[...] = r
    v_out[...] = v
    kn_out[...] = kn
    b_out[...] = kn * a
    kf_out[...] = k * (1.0 + (a - 1.0) * ka_ref[...])


def _rwkv_prep(zin, mu_rkv, mu_l, w0, wb, a0, ab, gb, k_k, k_a, tm=256):
    t = zin.shape[0]
    d = D_MODEL
    dl = 2 * D_LORA_PAD + D_GATE_LORA
    row = lambda i: (0, 0)
    out = jax.ShapeDtypeStruct((t, d), F32)
    blk = pl.BlockSpec((tm, d), lambda i: (i, 0))
    return pl.pallas_call(
        _rwkv_prep_kernel,
        grid=(t // tm,),
        in_specs=[
            pl.BlockSpec((tm, d), lambda i: (i, 4)),
            pl.BlockSpec((tm, d), lambda i: (i, 5)),
            pl.BlockSpec((tm, d), lambda i: (i, 6)),
            pl.BlockSpec((tm, dl), lambda i: (i, 7 * d // dl)),
            pl.BlockSpec((1, 3 * d), row),
            pl.BlockSpec((1, dl), row),
            pl.BlockSpec((1, d), row),
            pl.BlockSpec((D_LORA_PAD, d), row),
            pl.BlockSpec((1, d), row),
            pl.BlockSpec((D_LORA_PAD, d), row),
            pl.BlockSpec((D_GATE_LORA, d), row),
            pl.BlockSpec((1, d), row),
            pl.BlockSpec((1, d), row),
        ],
        out_specs=[blk] * 7,
        out_shape=[out] * 7,
        scratch_shapes=[pltpu.VMEM((8, d), F32), pltpu.VMEM((8, d), F32),
                        pltpu.VMEM((8, d), F32), pltpu.VMEM((8, dl), F32)],
        compiler_params=_params(("arbitrary",)),
        name="rwkv_prep",
    )(zin, zin, zin, zin, mu_rkv, mu_l, w0, wb, a0, ab, gb, k_k, k_a)


REC_ROWS = 256
REC_VMEM_LIMIT = 60 * 1024 * 1024


def _interleave(*gens):
    live = list(gens)
    while live:
        for gen in list(live):
            try:
                next(gen)
            except StopIteration:
                live.remove(gen)


def _chain(*gens):
    for gen in gens:
        yield from gen


def _rwkv_rec_kernel(r_ref, lw_ref, kf_ref, v_ref, kn_ref, b_ref, g_ref, zgb_ref,
                     rk_ref, gng_ref, gnb_ref, out_ref, s_ref):
    @pl.when(pl.program_id(0) == 0)
    def _():
        s_ref[...] = jnp.zeros_like(s_ref)

    c = CHUNK
    hd = RWKV_HEAD_DIM
    ng = N_GROUPS
    nch = r_ref.shape[0] // c
    sls = [slice(g * GROUP, (g + 1) * GROUP) for g in range(ng)]

    rows = lax.broadcasted_iota(jnp.int32, (c, c), 0)
    cols = lax.broadcasted_iota(jnp.int32, (c, c), 1)
    tri = jnp.where(cols <= rows, 1.0, 0.0).astype(BF16)
    t_c = lax.broadcasted_iota(jnp.int32, (c, GROUP), 0)
    s_c = lax.broadcasted_iota(jnp.int32, (c, GROUP), 1) % hd
    strict = s_c < t_c
    incl = s_c <= t_c
    eye_c = jnp.where(s_c == t_c, 1.0, 0.0)
    bd = (lax.broadcasted_iota(jnp.int32, (GROUP, GROUP), 0) // hd
          == lax.broadcasted_iota(jnp.int32, (GROUP, GROUP), 1) // hd)
    ones_bd = _block_ones(GROUP, hd)

    def expand(x_c):
        return jnp.where(bd, jnp.concatenate([x_c] * (GROUP // hd), axis=0), 0.0).astype(BF16)

    def seg_sums(xs):
        hi, lo = _split2(jnp.concatenate(xs, axis=0))
        s = _dot(jnp.concatenate([hi, lo], axis=0), ones_bd)
        n = c * len(xs)
        s = s[:n] + s[n:]
        return [s[i * c:(i + 1) * c] for i in range(len(xs))]

    state = [s_ref[g] for g in range(ng)]
    prep = [None] * nch

    def phase_a(ci):
        rs_ = slice(ci * c, (ci + 1) * c)
        lw = lw_ref[rs_, :]
        hi, mid, lo = _split3(lw)
        cum = _dot(tri, hi) + _dot(tri, mid) + _dot(tri, lo)
        tot = cum[c - 1:c, :]
        yield
        r = r_ref[rs_, :]
        kf = kf_ref[rs_, :]
        v = v_ref[rs_, :]
        bb = b_ref[rs_, :]
        p_inv = jnp.exp(-cum)
        p_end = jnp.exp(tot - cum)
        rq = r * jnp.exp(cum)
        kap = kn_ref[rs_, :] * jnp.exp(cum - lw)
        bet = bb * p_inv
        kt = kf * p_inv
        lhs2 = [jnp.concatenate([kap[:, sl], rq[:, sl]], axis=0).astype(BF16) for sl in sls]
        amat = [_dot_nt(lhs2[g], jnp.concatenate([expand(bet[:, sls[g]]), expand(kt[:, sls[g]])], axis=0))
                for g in range(ng)]
        yield
        l_c = [jnp.where(strict, a[:c, :GROUP], 0.0) for a in amat]
        a_lo = [jnp.concatenate([jnp.where(strict, a[:c, GROUP:], 0.0),
                                 jnp.where(incl, a[c:, GROUP:], 0.0)], axis=0).astype(BF16) for a in amat]
        arb = [jnp.where(incl, a[c:, :GROUP], 0.0).astype(BF16) for a in amat]
        av = [_dot(a_lo[g], expand(v[:, sls[g]])) for g in range(ng)]
        x_c = [eye_c - jnp.where((t_c >> 1) == (s_c >> 1), l, 0.0) for l in l_c]
        for lvl in range(2, 7):
            lmask = ((t_c >> lvl) == (s_c >> lvl)) & ((t_c >> (lvl - 1)) != (s_c >> (lvl - 1)))
            y_c = [_dot(x_c[g].astype(BF16), expand(jnp.where(lmask, l_c[g], 0.0))) for g in range(ng)]
            yield
            x_c = [x_c[g] - _dot(y_c[g].astype(BF16), expand(x_c[g])) for g in range(ng)]
            yield
        upd_rhs = [jnp.concatenate([-(bb * p_end)[:, sl], (kf * p_end)[:, sl]], axis=0).astype(BF16)
                   for sl in sls]
        prep[ci] = dict(lhs2=lhs2, av=av, arb=arb, x=[x.astype(BF16) for x in x_c], v=v,
                        upd_rhs=upd_rhs, p_tot=jnp.exp(tot), rkk=r * kf * rk_ref[...])

    def phase_b(ci):
        p = prep[ci]
        rs_ = slice(ci * c, (ci + 1) * c)
        v = p["v"]
        rs = [_dot_nt(p["lhs2"][g], state[g].astype(BF16)) for g in range(ng)]
        yield
        u_c = [_dot(p["x"][g], expand(rs[g][:c] + p["av"][g][:c])) for g in range(ng)]
        yield
        o_c = [rs[g][c:] + p["av"][g][c:] - _dot(p["arb"][g], expand(u_c[g])) for g in range(ng)]
        for g in range(ng):
            upd = _dot_tn(jnp.concatenate([u_c[g], v[:, sls[g]]], axis=0).astype(BF16), p["upd_rhs"][g])
            state[g] = state[g] * p["p_tot"][:, sls[g]] + jnp.where(bd, upd, 0.0)
        yield
        sums = seg_sums(o_c + [p["rkk"][:, sl] for sl in sls])
        dev = [o_c[g] - sums[g] * (1.0 / hd) for g in range(ng)]
        yield
        var = seg_sums([dv * dv for dv in dev])
        o_parts = []
        for g in range(ng):
            o_n = dev[g] * lax.rsqrt(var[g] * (1.0 / hd) + GN_EPS) * gng_ref[:, sls[g]] + gnb_ref[:, sls[g]]
            o_parts.append(o_n + sums[ng + g] * v[:, sls[g]])
        o = jnp.concatenate(o_parts, axis=1)
        out_ref[rs_, :] = _sigmoid(zgb_ref[rs_, :]) * (o * g_ref[rs_, :])

    pairs = [list(range(i, min(i + 2, nch))) for i in range(0, nch, 2)]
    _interleave(*[phase_a(ci) for ci in pairs[0]])
    for k in range(len(pairs)):
        b_gen = _chain(*[phase_b(ci) for ci in pairs[k]])
        a_gens = [phase_a(ci) for ci in pairs[k + 1]] if k + 1 < len(pairs) else []
        _interleave(b_gen, *a_gens)

    for g in range(ng):
        s_ref[g] = state[g]


def _rwkv_rec(r, lw, kf, v, kn, b, g, zin, rk, gng, gnb):
    t, d = r.shape
    br = min(REC_ROWS, t)
    blk = pl.BlockSpec((br, d), lambda i: (i, 0))
    row = pl.BlockSpec((1, d), lambda i: (0, 0))
    return pl.pallas_call(
        _rwkv_rec_kernel,
        grid=(t // br,),
        in_specs=[blk] * 7 + [pl.BlockSpec((br, d), lambda i: (i, 3)), row, row, row],
        out_specs=blk,
        out_shape=jax.ShapeDtypeStruct((t, d), F32),
        scratch_shapes=[pltpu.VMEM((N_GROUPS, GROUP, GROUP), F32)],
        compiler_params=pltpu.CompilerParams(dimension_semantics=("arbitrary",),
                                             vmem_limit_bytes=REC_VMEM_LIMIT),
        name="rwkv_rec",
    )(r, lw, kf, v, kn, b, g, zin, rk, gng, gnb)


def _out_proj_kernel(ya_ref, yb_ref, h_ref, w_ref, g_ref, b_ref, o_ref):
    y = (ya_ref[...] + yb_ref[...]).astype(BF16)
    mix = _dot(y, w_ref[...])
    o_ref[...] = _layer_norm(ALPHA * h_ref[...] + mix, g_ref[...], b_ref[...])


def _resident(shape):
    return pl.BlockSpec(shape, lambda *_: (0,) * len(shape), pipeline_mode=pl.Buffered(1))


def _out_proj(ya, yb, h, w, g, b, tm=512):
    t, d = h.shape
    blk = pl.BlockSpec((tm, d), lambda i: (i, 0))
    row = pl.BlockSpec((1, d), lambda i: (0, 0))
    return pl.pallas_call(
        _out_proj_kernel,
        grid=(t // tm,),
        in_specs=[blk, blk, blk, _resident((d, d)), row, row],
        out_specs=blk,
        out_shape=jax.ShapeDtypeStruct((t, d), F32),
        compiler_params=_params(("arbitrary",)),
        name="out_proj",
    )(ya, yb, h, w, g, b)


def _mem_kv_kernel(mem_ref, wk_ref, wv_ref, k_ref, v_ref):
    m = mem_ref[...].astype(BF16)
    k_ref[...] = _dot(m, wk_ref[...]).astype(BF16)
    v_ref[...] = _dot(m, wv_ref[...]).astype(BF16)


def _mem_kv(mem, wk, wv, tn=512):
    n, d = mem.shape
    return pl.pallas_call(
        _mem_kv_kernel,
        grid=(d // tn,),
        in_specs=[pl.BlockSpec((n, d), lambda j: (0, 0)),
                  pl.BlockSpec((d, tn), lambda j: (0, j)),
                  pl.BlockSpec((d, tn), lambda j: (0, j))],
        out_specs=[pl.BlockSpec((n, tn), lambda j: (0, j))] * 2,
        out_shape=[jax.ShapeDtypeStruct((n, d), BF16)] * 2,
        compiler_params=_params(("arbitrary",)),
        name="mem_kv",
    )(mem, wk, wv)


def _xattn_kernel(h_ref, wq_ref, k_ref, v_ref, wo_ref, g_ref, b_ref, o_ref):
    h = h_ref[...]
    q = _dot(h.astype(BF16), wq_ref[...]).astype(BF16)
    scale = XATTN_HEAD_DIM ** -0.5
    outs = []
    for hh in range(XATTN_HEADS):
        sl = slice(hh * XATTN_HEAD_DIM, (hh + 1) * XATTN_HEAD_DIM)
        s = _dot_nt(q[:, sl], k_ref[:, sl]) * scale
        e = jnp.exp(s - jnp.max(s, axis=-1, keepdims=True))
        p = e / jnp.sum(e, axis=-1, keepdims=True)
        outs.append(_dot(p.astype(BF16), v_ref[:, sl]))
    o = jnp.concatenate(outs, axis=1).astype(BF16)
    xa = _dot(o, wo_ref[...])
    o_ref[...] = _layer_norm(ALPHA * h + xa, g_ref[...], b_ref[...])


def _xattn(h, wq, kmem, vmem, wo, g, b, tm=512):
    t, d = h.shape
    n = kmem.shape[0]
    blk = pl.BlockSpec((tm, d), lambda i: (i, 0))
    row = pl.BlockSpec((1, d), lambda i: (0, 0))
    return pl.pallas_call(
        _xattn_kernel,
        grid=(t // tm,),
        in_specs=[blk, _resident((d, d)), _resident((n, d)), _resident((n, d)), _resident((d, d)),
                  row, row],
        out_specs=blk,
        out_shape=jax.ShapeDtypeStruct((t, d), F32),
        compiler_params=_params(("arbitrary",)),
        name="xattn",
    )(h, wq, kmem, vmem, wo, g, b)


def _ffn_kernel(h_ref, wg_ref, wu_ref, wd_ref, g_ref, b_ref, o_ref, hb_ref, acc_ref):
    j = pl.program_id(1)

    @pl.when(j == 0)
    def _():
        hb_ref[...] = h_ref[...].astype(BF16)
        acc_ref[...] = jnp.zeros_like(acc_ref)

    hb = hb_ref[...]
    gate = _dot(hb, wg_ref[...])
    up = _dot(hb, wu_ref[...])
    act = (gate * _sigmoid(gate) * up).astype(BF16)
    acc_ref[...] += _dot(act, wd_ref[...])

    @pl.when(j == pl.num_programs(1) - 1)
    def _():
        o_ref[...] = _layer_norm(ALPHA * h_ref[...] + acc_ref[...], g_ref[...], b_ref[...])


def _ffn(h, wg, wu, wd, g, b, tm=512, tf=512):
    t, d = h.shape
    f = wg.shape[1]
    blk = pl.BlockSpec((tm, d), lambda i, j: (i, 0))
    row = pl.BlockSpec((1, d), lambda i, j: (0, 0))
    return pl.pallas_call(
        _ffn_kernel,
        grid=(t // tm, f // tf),
        in_specs=[blk,
                  pl.BlockSpec((d, tf), lambda i, j: (0, j)),
                  pl.BlockSpec((d, tf), lambda i, j: (0, j)),
                  pl.BlockSpec((tf, d), lambda i, j: (j, 0)),
                  row, row],
        out_specs=blk,
        out_shape=jax.ShapeDtypeStruct((t, d), F32),
        scratch_shapes=[pltpu.VMEM((tm, d), BF16), pltpu.VMEM((tm, d), F32)],
        compiler_params=_params(("arbitrary", "arbitrary")),
        name="ffn",
    )(h, wg, wu, wd, g, b)


def _pad_cols(w, n):
    return jnp.pad(w, ((0, 0), (0, n - w.shape[1])))


def _pad_rows(w, n):
    return jnp.pad(w, ((0, n - w.shape[0]), (0, 0)))


def _split_in_cols(w):
    d = D_MODEL
    o = 5 * d
    lo = w.shape[1] - 7 * d - D_GATE_LORA
    w_lo, a_lo = w[:, o:o + lo // 2], w[:, o + lo // 2:o + lo]
    g_lo = w[:, o + lo:o + lo + D_GATE_LORA]
    gates = w[:, o + lo + D_GATE_LORA:]
    tail = jnp.concatenate(
        [gates, _pad_cols(w_lo, D_LORA_PAD), _pad_cols(a_lo, D_LORA_PAD), g_lo], axis=1)
    return w[:, :o], tail


def kernel(x, mem, ln_in_g, ln_in_b, w_in, conv_w, conv_b, lru_wa, lru_ba, lru_wx, lru_bx, lru_lambda, rw_mu, rw_w0, rw_wB, rw_a0, rw_aB, rw_gB, rw_kk, rw_ka, rw_rk, rw_gn_g, rw_gn_b, w_out, ln1_g, ln1_b, xa_wq, xa_wk, xa_wv, xa_wo, ln2_g, ln2_b, ffn_wg, ffn_wu, ffn_wd, ln3_g, ln3_b):
    bsz, t, d = x.shape
    depth = w_in.shape[0]
    assert bsz == 1 and d == D_MODEL and t % 512 == 0
    row = lambda p: p.reshape(1, -1)

    h = None
    for l in range(depth):
        w_all = w_in[l].astype(BF16)
        _, w_tail = _split_in_cols(w_all)
        if l == 0:
            h, zin = _in_proj(x[0], row(ln_in_g), row(ln_in_b), w_all, w_tail)
        else:
            raise NotImplementedError("DEPTH > 1 is not part of this problem")

        ya = _lru(zin, conv_w[l], row(conv_b[l]), lru_wa[l].astype(BF16), row(lru_ba[l]),
                  lru_wx[l].astype(BF16), row(lru_bx[l]), row(lru_lambda[l]))

        mu_main, mu_tail = _split_in_cols(jnp.pad(row(rw_mu[l]), ((0, 0), (2 * d, 2 * d))))
        mu_rkv, mu_l = mu_main[:, 2 * d:], mu_tail[:, 2 * d:]
        r, lw, kf, v, kn, b, g = _rwkv_prep(
            zin, mu_rkv, mu_l, row(rw_w0[l]),
            _pad_rows(rw_wB[l], D_LORA_PAD).astype(BF16), row(rw_a0[l]),
            _pad_rows(rw_aB[l], D_LORA_PAD).astype(BF16), rw_gB[l].astype(BF16),
            row(rw_kk[l]), row(rw_ka[l]))
        yb = _rwkv_rec(r, lw, kf, v, kn, b, g, zin, row(rw_rk[l]), row(rw_gn_g[l]), row(rw_gn_b[l]))

        h = _out_proj(ya, yb, h, w_out[l].astype(BF16), row(ln1_g[l]), row(ln1_b[l]))
        kmem, vmem = _mem_kv(mem[0], xa_wk[l].astype(BF16), xa_wv[l].astype(BF16))
        h = _xattn(h, xa_wq[l].astype(BF16), kmem, vmem, xa_wo[l].astype(BF16),
                   row(ln2_g[l]), row(ln2_b[l]))
        h = _ffn(h, ffn_wg[l].astype(BF16), ffn_wu[l].astype(BF16), ffn_wd[l].astype(BF16),
                 row(ln3_g[l]), row(ln3_b[l]))
    return h[None]
```

```python
import functools

import jax
import jax.numpy as jnp
from jax import lax
from jax.experimental import pallas as pl
from jax.experimental.pallas import tpu as pltpu

F32 = jnp.float32
BF16 = jnp.bfloat16

D_MODEL = 2048
LN_EPS = 1e-5
ALPHA = 2.0 ** 0.25

LRU_HEADS = 16
LRU_HEAD_DIM = 128
CONV_WIDTH = 4
LRU_C = 8.0

RWKV_HEAD_DIM = 64
D_LORA_PAD = 128
D_GATE_LORA = 256
GN_EPS = 64e-5
CHUNK = 64
GROUP = 256
N_GROUPS = D_MODEL // GROUP

XATTN_HEADS = 4
XATTN_HEAD_DIM = 512
N_MEM = 256
D_FF = 5632

VMEM_LIMIT = 56 * 1024 * 1024
VMEM_LIMIT_BIG_TILES = 60 * 1024 * 1024


def _dot(a, b):
    return jnp.dot(a, b, preferred_element_type=F32)


def _dot_nt(a, b):
    return lax.dot_general(a, b, (((1,), (1,)), ((), ())), preferred_element_type=F32)


def _dot_tn(a, b):
    return lax.dot_general(a, b, (((0,), (0,)), ((), ())), preferred_element_type=F32)


def _split2(x):
    hi = x.astype(BF16)
    lo = (x - hi.astype(F32)).astype(BF16)
    return hi, lo


def _split3(x):
    hi = x.astype(BF16)
    r1 = x - hi.astype(F32)
    mid = r1.astype(BF16)
    lo = (r1 - mid.astype(F32)).astype(BF16)
    return hi, mid, lo


def _sigmoid(x):
    return 1.0 / (1.0 + jnp.exp(-x))


def _softplus(x):
    return jnp.maximum(x, 0.0) + jnp.log1p(jnp.exp(-jnp.abs(x)))


def _gelu_tanh(x):
    c = 0.7978845608028654
    return 0.5 * x * (1.0 + jnp.tanh(c * (x + 0.044715 * (x * x * x))))


def _layer_norm(x, g, b):
    mu = jnp.mean(x, axis=-1, keepdims=True)
    xc = x - mu
    var = jnp.mean(xc * xc, axis=-1, keepdims=True)
    return xc * lax.rsqrt(var + LN_EPS) * g + b


def _shift_rows(z, prev8, s):
    rolled = pltpu.roll(z, s, 0)
    row8 = lax.broadcasted_iota(jnp.int32, prev8.shape, 0)
    head = jnp.where(row8 < s, pltpu.roll(prev8, s, 0), rolled[:8])
    return jnp.concatenate([head, rolled[8:]], axis=0)


def _params(sem, vmem_limit=VMEM_LIMIT):
    return pltpu.CompilerParams(dimension_semantics=sem, vmem_limit_bytes=vmem_limit)


IN_TN = 512
IN_MAIN_BLOCKS = 5 * D_MODEL // IN_TN
IN_TAIL_BLOCKS = (2 * D_MODEL + 2 * D_LORA_PAD + D_GATE_LORA) // IN_TN


def _in_out_block(j):
    b2 = 2 * D_MODEL // IN_TN
    return jnp.where(j < b2, j,
                     jnp.where(j < IN_MAIN_BLOCKS, j + b2,
                               jnp.where(j < IN_MAIN_BLOCKS + b2, j - (IN_MAIN_BLOCKS - b2), j)))


IN_D_BLOCKS = D_MODEL // IN_TN
CP_MU, CP_CONV_W, CP_CONV_B, CP_ROWS = 0, 1, 1 + CONV_WIDTH, 8


def _in_proj_kernel(x_ref, g_ref, b_ref, wm_ref, wt_ref, cp_ref, h_ref, z_ref, hb_ref, carry_ref):
    i = pl.program_id(0)
    j = pl.program_id(1)
    nd = IN_D_BLOCKS

    @pl.when(j == 0)
    def _():
        h = _layer_norm(x_ref[...], g_ref[...], b_ref[...])
        h_ref[...] = h
        hb_ref[...] = h.astype(BF16)

    @pl.when((i == 0) & (j == 0))
    def _():
        carry_ref[...] = jnp.zeros_like(carry_ref)

    def keep_tail(z):
        carry_ref[j] = z[z.shape[0] - 8:, :]

    @pl.when(j < nd)
    def _():
        z = _dot(hb_ref[...], wm_ref[...])
        prev8 = carry_ref[j]
        cw = CP_CONV_W + CONV_WIDTH - 1
        conv = cp_ref[CP_CONV_B:CP_CONV_B + 1, :] + z * cp_ref[cw:cw + 1, :]
        for s in range(1, CONV_WIDTH):
            conv = conv + _shift_rows(z, prev8, s) * cp_ref[cw - s:cw - s + 1, :]
        keep_tail(z)
        z_ref[...] = conv

    @pl.when((j >= nd) & (j < 2 * nd))
    def _():
        z_ref[...] = _gelu_tanh(_dot(hb_ref[...], wm_ref[...]))

    def token_shift(z):
        zp = _shift_rows(z, carry_ref[j], 1)
        keep_tail(z)
        return z + (zp - z) * cp_ref[CP_MU:CP_MU + 1, :]

    @pl.when((j >= 2 * nd) & (j < IN_MAIN_BLOCKS))
    def _():
        z_ref[...] = token_shift(_dot(hb_ref[...], wm_ref[...]))

    @pl.when((j >= IN_MAIN_BLOCKS) & (j < IN_MAIN_BLOCKS + 2 * nd))
    def _():
        z_ref[...] = _sigmoid(_dot(hb_ref[...], wt_ref[...]))

    @pl.when(j >= IN_MAIN_BLOCKS + 2 * nd)
    def _():
        z_ref[...] = token_shift(_dot(hb_ref[...], wt_ref[...]))


def _in_proj(x, g, b, w_main, w_tail, col_params, tm=1024):
    t, d = x.shape
    tn = IN_TN
    nb = IN_MAIN_BLOCKS + IN_TAIL_BLOCKS
    tm = min(tm, t)
    return pl.pallas_call(
        _in_proj_kernel,
        grid=(t // tm, nb),
        in_specs=[
            pl.BlockSpec((tm, d), lambda i, j: (i, 0)),
            pl.BlockSpec((1, d), lambda i, j: (0, 0)),
            pl.BlockSpec((1, d), lambda i, j: (0, 0)),
            pl.BlockSpec((d, tn), lambda i, j: (0, jnp.minimum(j, IN_MAIN_BLOCKS - 1))),
            pl.BlockSpec((d, tn), lambda i, j: (0, jnp.maximum(j - IN_MAIN_BLOCKS, 0))),
            pl.BlockSpec((CP_ROWS, tn), lambda i, j: (0, j)),
        ],
        out_specs=[
            pl.BlockSpec((tm, d), lambda i, j: (i, 0)),
            pl.BlockSpec((tm, tn), lambda i, j: (i, _in_out_block(j))),
        ],
        out_shape=[
            jax.ShapeDtypeStruct((t, d), F32),
            jax.ShapeDtypeStruct((t, nb * tn), F32),
        ],
        scratch_shapes=[pltpu.VMEM((tm, d), BF16), pltpu.VMEM((nb, 8, tn), F32)],
        compiler_params=_params(("arbitrary", "arbitrary")),
        name="in_proj",
    )(x, g, b, w_main, w_tail, col_params)


def _lru_kernel(u_ref, gate_ref, ga_ref, wa_ref, ba_ref, wx_ref, bx_ref,
                lam_ref, out_ref, hcarry):
    @pl.when(pl.program_id(0) == 0)
    def _():
        hcarry[...] = jnp.zeros_like(hcarry)

    conv = u_ref[...]
    tm = conv.shape[0]
    cb16 = conv.astype(BF16)
    r_parts, i_parts = [], []
    for g in range(LRU_HEADS):
        ug = cb16[:, g * LRU_HEAD_DIM:(g + 1) * LRU_HEAD_DIM]
        r_parts.append(_dot(ug, wa_ref[g]))
        i_parts.append(_dot(ug, wx_ref[g]))
    r = _sigmoid(jnp.concatenate(r_parts, axis=1) + ba_ref[...])
    ig = _sigmoid(jnp.concatenate(i_parts, axis=1) + bx_ref[...])

    log_a = (-LRU_C) * r * _softplus(-lam_ref[...])
    a_c = jnp.exp(log_a)
    b_c = jnp.sqrt(-jnp.tanh(log_a) * (a_c * a_c + 1.0)) * (ig * conv)

    n8 = tm // 8
    a3 = a_c.reshape(n8, 8, a_c.shape[1])
    b3 = b_c.reshape(n8, 8, b_c.shape[1])
    sub = lax.broadcasted_iota(jnp.int32, a3.shape, 1)
    for d in (1, 2, 4):
        m = sub >= d
        a_sh = jnp.where(m, pltpu.roll(a3, d, 1), 1.0)
        b_sh = jnp.where(m, pltpu.roll(b3, d, 1), 0.0)
        b3 = a3 * b_sh + b3
        a3 = a3 * a_sh
    carry = hcarry[...]
    hs = []
    for i in range(n8):
        h_i = b3[i] + a3[i] * carry
        hs.append(h_i)
        carry = h_i[7:8, :]
    hcarry[...] = carry
    h = jnp.concatenate(hs, axis=0)

    out_ref[...] = ga_ref[...] * (gate_ref[...] * h)


def _lru(zin, wa, ba, wx, bx, lam, tm=256):
    t = zin.shape[0]
    d = D_MODEL
    row = lambda i: (0, 0)
    return pl.pallas_call(
        _lru_kernel,
        grid=(t // tm,),
        in_specs=[
            pl.BlockSpec((tm, d), lambda i: (i, 0)),
            pl.BlockSpec((tm, d), lambda i: (i, 1)),
            pl.BlockSpec((tm, d), lambda i: (i, 2)),
            pl.BlockSpec((LRU_HEADS, LRU_HEAD_DIM, LRU_HEAD_DIM), lambda i: (0, 0, 0)),
            pl.BlockSpec((1, d), row),
            pl.BlockSpec((LRU_HEADS, LRU_HEAD_DIM, LRU_HEAD_DIM), lambda i: (0, 0, 0)),
            pl.BlockSpec((1, d), row),
            pl.BlockSpec((1, d), row),
        ],
        out_specs=pl.BlockSpec((tm, d), lambda i: (i, 0)),
        out_shape=jax.ShapeDtypeStruct((t, d), F32),
        scratch_shapes=[pltpu.VMEM((1, d), F32)],
        compiler_params=_params(("arbitrary",)),
        name="lru",
    )(zin, zin, zin, wa, ba, wx, bx, lam)


def _block_ones(n, seg):
    r = lax.broadcasted_iota(jnp.int32, (n, n), 0) // seg
    c = lax.broadcasted_iota(jnp.int32, (n, n), 1) // seg
    return jnp.where(r == c, 1.0, 0.0).astype(BF16)


def _seg_sum(x, ones_bd):
    parts = []
    for g in range(x.shape[1] // GROUP):
        hi, lo = _split2(x[:, g * GROUP:(g + 1) * GROUP])
        parts.append(_dot(hi, ones_bd) + _dot(lo, ones_bd))
    return jnp.concatenate(parts, axis=1)


def _rwkv_prep_kernel(zk_ref, zl_ref, w0_ref, wb_ref, a0_ref, ab_ref, gb_ref, kk_ref, ka_ref,
                      lw_out, kf_out, kn_out, b_out, g_out):
    k = zk_ref[...]
    zl = zl_ref[...]
    w_lo = zl[:, 0:D_LORA_PAD]
    a_lo = zl[:, D_LORA_PAD:2 * D_LORA_PAD]
    g_lo = zl[:, 2 * D_LORA_PAD:]

    w_log = -_softplus(-(w0_ref[...] + _dot(jnp.tanh(w_lo).astype(BF16), wb_ref[...]))) - 0.5
    lw_out[...] = -jnp.exp(w_log)
    a = _sigmoid(a0_ref[...] + _dot(a_lo.astype(BF16), ab_ref[...]))
    g_out[...] = _dot(_sigmoid(g_lo).astype(BF16), gb_ref[...])

    ones_bd = _block_ones(GROUP, RWKV_HEAD_DIM)
    kk = k * kk_ref[...]
    norm = jnp.sqrt(_seg_sum(kk * kk, ones_bd))
    kn = kk / jnp.maximum(norm, 1e-12)
    kn_out[...] = kn
    b_out[...] = kn * a
    kf_out[...] = k * (1.0 + (a - 1.0) * ka_ref[...])


def _rwkv_prep(zin, w0, wb, a0, ab, gb, k_k, k_a, tm=256):
    t = zin.shape[0]
    d = D_MODEL
    dl = 2 * D_LORA_PAD + D_GATE_LORA
    row = lambda i: (0, 0)
    out = jax.ShapeDtypeStruct((t, d), F32)
    blk = pl.BlockSpec((tm, d), lambda i: (i, 0))
    return pl.pallas_call(
        _rwkv_prep_kernel,
        grid=(t // tm,),
        in_specs=[
            pl.BlockSpec((tm, d), lambda i: (i, 5)),
            pl.BlockSpec((tm, dl), lambda i: (i, 7 * d // dl)),
            pl.BlockSpec((1, d), row),
            pl.BlockSpec((D_LORA_PAD, d), row),
            pl.BlockSpec((1, d), row),
            pl.BlockSpec((D_LORA_PAD, d), row),
            pl.BlockSpec((D_GATE_LORA, d), row),
            pl.BlockSpec((1, d), row),
            pl.BlockSpec((1, d), row),
        ],
        out_specs=[blk] * 5,
        out_shape=[out] * 5,
        compiler_params=_params(("arbitrary",)),
        name="rwkv_prep",
    )(zin, zin, w0, wb, a0, ab, gb, k_k, k_a)


REC_ROWS = 256


def _interleave(*gens):
    live = list(gens)
    while live:
        for gen in list(live):
            try:
                next(gen)
            except StopIteration:
                live.remove(gen)


def _chain(*gens):
    for gen in gens:
        yield from gen


def _rwkv_rec_kernel(r_ref, lw_ref, kf_ref, v_ref, kn_ref, b_ref, g_ref, zgb_ref,
                     rk_ref, gng_ref, gnb_ref, out_ref, s_ref):
    @pl.when(pl.program_id(0) == 0)
    def _():
        s_ref[...] = jnp.zeros_like(s_ref)

    c = CHUNK
    hd = RWKV_HEAD_DIM
    ng = N_GROUPS
    nch = r_ref.shape[0] // c
    sls = [slice(g * GROUP, (g + 1) * GROUP) for g in range(ng)]

    rows = lax.broadcasted_iota(jnp.int32, (c, c), 0)
    cols = lax.broadcasted_iota(jnp.int32, (c, c), 1)
    tri = jnp.where(cols <= rows, 1.0, 0.0).astype(BF16)
    t_c = lax.broadcasted_iota(jnp.int32, (c, GROUP), 0)
    s_c = lax.broadcasted_iota(jnp.int32, (c, GROUP), 1) % hd
    strict = s_c < t_c
    incl = s_c <= t_c
    eye_c = jnp.where(s_c == t_c, 1.0, 0.0)
    bd = (lax.broadcasted_iota(jnp.int32, (GROUP, GROUP), 0) // hd
          == lax.broadcasted_iota(jnp.int32, (GROUP, GROUP), 1) // hd)
    ones_bd = _block_ones(GROUP, hd)

    def expand(x_c):
        return jnp.where(bd, jnp.concatenate([x_c] * (GROUP // hd), axis=0), 0.0).astype(BF16)

    def seg_sums(xs):
        hi, lo = _split2(jnp.concatenate(xs, axis=0))
        s = _dot(jnp.concatenate([hi, lo], axis=0), ones_bd)
        n = c * len(xs)
        s = s[:n] + s[n:]
        return [s[i * c:(i + 1) * c] for i in range(len(xs))]

    state = [s_ref[g] for g in range(ng)]
    prep = [None] * nch

    def phase_a(ci):
        rs_ = slice(ci * c, (ci + 1) * c)
        lw = lw_ref[rs_, :]
        hi, mid, lo = _split3(lw)
        cum = _dot(tri, hi) + _dot(tri, mid) + _dot(tri, lo)
        tot = cum[c - 1:c, :]
        yield
        r = r_ref[rs_, :]
        kf = kf_ref[rs_, :]
        v = v_ref[rs_, :]
        bb = b_ref[rs_, :]
        p_inv = jnp.exp(-cum)
        p_end = jnp.exp(tot - cum)
        rq = r * jnp.exp(cum)
        kap = kn_ref[rs_, :] * jnp.exp(cum - lw)
        bet = bb * p_inv
        kt = kf * p_inv
        lhs2 = [jnp.concatenate([kap[:, sl], rq[:, sl]], axis=0).astype(BF16) for sl in sls]
        amat = [_dot_nt(lhs2[g], jnp.concatenate([expand(bet[:, sls[g]]), expand(kt[:, sls[g]])], axis=0))
                for g in range(ng)]
        yield
        l_c = [jnp.where(strict, a[:c, :GROUP], 0.0) for a in amat]
        a_lo = [jnp.concatenate([jnp.where(strict, a[:c, GROUP:], 0.0),
                                 jnp.where(incl, a[c:, GROUP:], 0.0)], axis=0).astype(BF16) for a in amat]
        arb = [jnp.where(incl, a[c:, :GROUP], 0.0).astype(BF16) for a in amat]
        av = [_dot(a_lo[g], expand(v[:, sls[g]])) for g in range(ng)]
        x_c = [eye_c - jnp.where((t_c >> 1) == (s_c >> 1), l, 0.0) for l in l_c]
        for lvl in range(2, 7):
            lmask = ((t_c >> lvl) == (s_c >> lvl)) & ((t_c >> (lvl - 1)) != (s_c >> (lvl - 1)))
            y_c = [_dot(x_c[g].astype(BF16), expand(jnp.where(lmask, l_c[g], 0.0))) for g in range(ng)]
            yield
            x_c = [x_c[g] - _dot(y_c[g].astype(BF16), expand(x_c[g])) for g in range(ng)]
            yield
        upd_rhs = [jnp.concatenate([-(bb * p_end)[:, sl], (kf * p_end)[:, sl]], axis=0).astype(BF16)
                   for sl in sls]
        prep[ci] = dict(lhs2=lhs2, av=av, arb=arb, x=[x.astype(BF16) for x in x_c], v=v,
                        upd_rhs=upd_rhs, p_tot=jnp.exp(tot), rkk=r * kf * rk_ref[...])

    def phase_b(ci):
        p = prep[ci]
        rs_ = slice(ci * c, (ci + 1) * c)
        v = p["v"]
        rs = [_dot_nt(p["lhs2"][g], state[g].astype(BF16)) for g in range(ng)]
        yield
        u_c = [_dot(p["x"][g], expand(rs[g][:c] + p["av"][g][:c])) for g in range(ng)]
        yield
        o_c = [rs[g][c:] + p["av"][g][c:] - _dot(p["arb"][g], expand(u_c[g])) for g in range(ng)]
        for g in range(ng):
            upd = _dot_tn(jnp.concatenate([u_c[g], v[:, sls[g]]], axis=0).astype(BF16), p["upd_rhs"][g])
            state[g] = state[g] * p["p_tot"][:, sls[g]] + jnp.where(bd, upd, 0.0)
        yield
        sums = seg_sums(o_c + [p["rkk"][:, sl] for sl in sls])
        dev = [o_c[g] - sums[g] * (1.0 / hd) for g in range(ng)]
        yield
        var = seg_sums([dv * dv for dv in dev])
        o_parts = []
        for g in range(ng):
            o_n = dev[g] * lax.rsqrt(var[g] * (1.0 / hd) + GN_EPS) * gng_ref[:, sls[g]] + gnb_ref[:, sls[g]]
            o_parts.append(o_n + sums[ng + g] * v[:, sls[g]])
        o = jnp.concatenate(o_parts, axis=1)
        out_ref[rs_, :] = zgb_ref[rs_, :] * (o * g_ref[rs_, :])

    pairs = [list(range(i, min(i + 2, nch))) for i in range(0, nch, 2)]
    _interleave(*[phase_a(ci) for ci in pairs[0]])
    for k in range(len(pairs)):
        b_gen = _chain(*[phase_b(ci) for ci in pairs[k]])
        a_gens = [phase_a(ci) for ci in pairs[k + 1]] if k + 1 < len(pairs) else []
        _interleave(b_gen, *a_gens)

    for g in range(ng):
        s_ref[g] = state[g]


def _rwkv_rec(zin, lw, kf, kn, b, g, rk, gng, gnb):
    t, d = lw.shape
    br = min(REC_ROWS, t)
    blk = pl.BlockSpec((br, d), lambda i: (i, 0))
    zcol = lambda c: pl.BlockSpec((br, d), lambda i: (i, c))
    row = pl.BlockSpec((1, d), lambda i: (0, 0))
    return pl.pallas_call(
        _rwkv_rec_kernel,
        grid=(t // br,),
        in_specs=[zcol(4), blk, blk, zcol(6), blk, blk, blk, zcol(3), row, row, row],
        out_specs=blk,
        out_shape=jax.ShapeDtypeStruct((t, d), F32),
        scratch_shapes=[pltpu.VMEM((N_GROUPS, GROUP, GROUP), F32)],
        compiler_params=_params(("arbitrary",), VMEM_LIMIT_BIG_TILES),
        name="rwkv_rec",
    )(zin, lw, kf, zin, kn, b, g, zin, rk, gng, gnb)


def _out_proj_kernel(ya_ref, yb_ref, h_ref, w_ref, g_ref, b_ref, o_ref):
    y = (ya_ref[...] + yb_ref[...]).astype(BF16)
    mix = _dot(y, w_ref[...])
    o_ref[...] = _layer_norm(ALPHA * h_ref[...] + mix, g_ref[...], b_ref[...])


def _resident(shape):
    return pl.BlockSpec(shape, lambda *_: (0,) * len(shape), pipeline_mode=pl.Buffered(1))


def _out_proj(ya, yb, h, w, g, b, tm=512):
    t, d = h.shape
    blk = pl.BlockSpec((tm, d), lambda i: (i, 0))
    row = pl.BlockSpec((1, d), lambda i: (0, 0))
    return pl.pallas_call(
        _out_proj_kernel,
        grid=(t // tm,),
        in_specs=[blk, blk, blk, _resident((d, d)), row, row],
        out_specs=blk,
        out_shape=jax.ShapeDtypeStruct((t, d), F32),
        compiler_params=_params(("arbitrary",)),
        name="out_proj",
    )(ya, yb, h, w, g, b)


def _mem_kv_kernel(mem_ref, wk_ref, wv_ref, k_ref, v_ref):
    m = mem_ref[...].astype(BF16)
    k_ref[...] = _dot(m, wk_ref[...]).astype(BF16)
    v_ref[...] = _dot(m, wv_ref[...]).astype(BF16)


def _mem_kv(mem, wk, wv, tn=512):
    n, d = mem.shape
    return pl.pallas_call(
        _mem_kv_kernel,
        grid=(d // tn,),
        in_specs=[pl.BlockSpec((n, d), lambda j: (0, 0)),
                  pl.BlockSpec((d, tn), lambda j: (0, j)),
                  pl.BlockSpec((d, tn), lambda j: (0, j))],
        out_specs=[pl.BlockSpec((n, tn), lambda j: (0, j))] * 2,
        out_shape=[jax.ShapeDtypeStruct((n, d), BF16)] * 2,
        compiler_params=_params(("arbitrary",)),
        name="mem_kv",
    )(mem, wk, wv)


def _xattn_kernel(h_ref, wq_ref, k_ref, v_ref, wo_ref, g_ref, b_ref, o_ref):
    h = h_ref[...]
    q = _dot(h.astype(BF16), wq_ref[...]).astype(BF16)
    scale = XATTN_HEAD_DIM ** -0.5
    outs = []
    for hh in range(XATTN_HEADS):
        sl = slice(hh * XATTN_HEAD_DIM, (hh + 1) * XATTN_HEAD_DIM)
        s = _dot_nt(q[:, sl], k_ref[:, sl]) * scale
        e = jnp.exp(s - jnp.max(s, axis=-1, keepdims=True))
        p = e / jnp.sum(e, axis=-1, keepdims=True)
        outs.append(_dot(p.astype(BF16), v_ref[:, sl]))
    o = jnp.concatenate(outs, axis=1).astype(BF16)
    xa = _dot(o, wo_ref[...])
    o_ref[...] = _layer_norm(ALPHA * h + xa, g_ref[...], b_ref[...])


def _xattn(h, wq, kmem, vmem, wo, g, b, tm=512):
    t, d = h.shape
    n = kmem.shape[0]
    blk = pl.BlockSpec((tm, d), lambda i: (i, 0))
    row = pl.BlockSpec((1, d), lambda i: (0, 0))
    return pl.pallas_call(
        _xattn_kernel,
        grid=(t // tm,),
        in_specs=[blk, _resident((d, d)), _resident((n, d)), _resident((n, d)), _resident((d, d)),
                  row, row],
        out_specs=blk,
        out_shape=jax.ShapeDtypeStruct((t, d), F32),
        compiler_params=_params(("arbitrary",)),
        name="xattn",
    )(h, wq, kmem, vmem, wo, g, b)


def _ffn_kernel(h_ref, wg_ref, wu_ref, wd_ref, g_ref, b_ref, o_ref):
    j = pl.program_id(1)

    @pl.when(j == 0)
    def _():
        o_ref[...] = jnp.zeros_like(o_ref)

    hb = h_ref[...].astype(BF16)
    gate = _dot(hb, wg_ref[...])
    up = _dot(hb, wu_ref[...])
    act = (gate * _sigmoid(gate) * up).astype(BF16)
    o_ref[...] += _dot(act, wd_ref[...])

    @pl.when(j == pl.num_programs(1) - 1)
    def _():
        o_ref[...] = _layer_norm(ALPHA * h_ref[...] + o_ref[...], g_ref[...], b_ref[...])


def _ffn(h, wg, wu, wd, g, b, tm=1024, tf=512):
    t, d = h.shape
    f = wg.shape[1]
    tm = min(tm, t)
    blk = pl.BlockSpec((tm, d), lambda i, j: (i, 0))
    row = pl.BlockSpec((1, d), lambda i, j: (0, 0))
    return pl.pallas_call(
        _ffn_kernel,
        grid=(t // tm, f // tf),
        in_specs=[blk,
                  pl.BlockSpec((d, tf), lambda i, j: (0, j)),
                  pl.BlockSpec((d, tf), lambda i, j: (0, j)),
                  pl.BlockSpec((tf, d), lambda i, j: (j, 0)),
                  row, row],
        out_specs=blk,
        out_shape=jax.ShapeDtypeStruct((t, d), F32),
        compiler_params=_params(("arbitrary", "arbitrary"), VMEM_LIMIT_BIG_TILES),
        name="ffn",
    )(h, wg, wu, wd, g, b)


def _pad_cols(w, n):
    return jnp.pad(w, ((0, 0), (0, n - w.shape[1])))


def _pad_rows(w, n):
    return jnp.pad(w, ((0, n - w.shape[0]), (0, 0)))


def _split_in_cols(w):
    d = D_MODEL
    o = 5 * d
    lo = w.shape[1] - 7 * d - D_GATE_LORA
    w_lo, a_lo = w[:, o:o + lo // 2], w[:, o + lo // 2:o + lo]
    g_lo = w[:, o + lo:o + lo + D_GATE_LORA]
    gates = w[:, o + lo + D_GATE_LORA:]
    tail = jnp.concatenate(
        [gates, _pad_cols(w_lo, D_LORA_PAD), _pad_cols(a_lo, D_LORA_PAD), g_lo], axis=1)
    return w[:, :o], tail


def kernel(x, mem, ln_in_g, ln_in_b, w_in, conv_w, conv_b, lru_wa, lru_ba, lru_wx, lru_bx, lru_lambda, rw_mu, rw_w0, rw_wB, rw_a0, rw_aB, rw_gB, rw_kk, rw_ka, rw_rk, rw_gn_g, rw_gn_b, w_out, ln1_g, ln1_b, xa_wq, xa_wk, xa_wv, xa_wo, ln2_g, ln2_b, ffn_wg, ffn_wu, ffn_wd, ln3_g, ln3_b):
    bsz, t, d = x.shape
    depth = w_in.shape[0]
    assert bsz == 1 and d == D_MODEL and t % 512 == 0
    row = lambda p: p.reshape(1, -1)

    h = None
    for l in range(depth):
        w_all = w_in[l].astype(BF16)
        _, w_tail = _split_in_cols(w_all)
        mu_main, mu_tail = _split_in_cols(jnp.pad(row(rw_mu[l]), ((0, 0), (2 * d, 2 * d))))
        n_cols = mu_main.shape[1] + mu_tail.shape[1]
        col_params = jnp.concatenate([
            jnp.concatenate([mu_main, mu_tail], axis=1),
            _pad_cols(conv_w[l], n_cols),
            _pad_cols(row(conv_b[l]), n_cols),
            jnp.zeros((CP_ROWS - CP_CONV_B - 1, n_cols), F32)], axis=0)
        if l == 0:
            h, zin = _in_proj(x[0], row(ln_in_g), row(ln_in_b), w_all, w_tail, col_params)
        else:
            raise NotImplementedError("DEPTH > 1 is not part of this problem")

        ya = _lru(zin, lru_wa[l].astype(BF16), row(lru_ba[l]),
                  lru_wx[l].astype(BF16), row(lru_bx[l]), row(lru_lambda[l]))

        lw, kf, kn, b, g = _rwkv_prep(
            zin, row(rw_w0[l]),
            _pad_rows(rw_wB[l], D_LORA_PAD).astype(BF16), row(rw_a0[l]),
            _pad_rows(rw_aB[l], D_LORA_PAD).astype(BF16), rw_gB[l].astype(BF16),
            row(rw_kk[l]), row(rw_ka[l]))
        yb = _rwkv_rec(zin, lw, kf, kn, b, g, row(rw_rk[l]), row(rw_gn_g[l]), row(rw_gn_b[l]))

        h = _out_proj(ya, yb, h, w_out[l].astype(BF16), row(ln1_g[l]), row(ln1_b[l]))
        kmem, vmem = _mem_kv(mem[0], xa_wk[l].astype(BF16), xa_wv[l].astype(BF16))
        h = _xattn(h, xa_wq[l].astype(BF16), kmem, vmem, xa_wo[l].astype(BF16),
                   row(ln2_g[l]), row(ln2_b[l]))
        h = _ffn(h, ffn_wg[l].astype(BF16), ffn_wu[l].astype(BF16), ffn_wd[l].astype(BF16),
                 row(ln3_g[l]), row(ln3_b[l]))
    return h[None]
```

```python
import functools

import jax
import jax.numpy as jnp
from jax import lax
from jax.experimental import pallas as pl
from jax.experimental.pallas import tpu as pltpu

F32 = jnp.float32
BF16 = jnp.bfloat16

D_MODEL = 2048
LN_EPS = 1e-5
ALPHA = 2.0 ** 0.25

LRU_HEADS = 16
LRU_HEAD_DIM = 128
CONV_WIDTH = 4
LRU_C = 8.0

RWKV_HEAD_DIM = 64
D_LORA_PAD = 128
D_GATE_LORA = 256
GN_EPS = 64e-5
CHUNK = 64
GROUP = 256
N_GROUPS = D_MODEL // GROUP

XATTN_HEADS = 4
XATTN_HEAD_DIM = 512
N_MEM = 256
D_FF = 5632

VMEM_LIMIT = 56 * 1024 * 1024
VMEM_LIMIT_BIG_TILES = 60 * 1024 * 1024


def _dot(a, b):
    return jnp.dot(a, b, preferred_element_type=F32)


def _dot_nt(a, b):
    return lax.dot_general(a, b, (((1,), (1,)), ((), ())), preferred_element_type=F32)


def _dot_tn(a, b):
    return lax.dot_general(a, b, (((0,), (0,)), ((), ())), preferred_element_type=F32)


def _split2(x):
    hi = x.astype(BF16)
    lo = (x - hi.astype(F32)).astype(BF16)
    return hi, lo


def _split3(x):
    hi = x.astype(BF16)
    r1 = x - hi.astype(F32)
    mid = r1.astype(BF16)
    lo = (r1 - mid.astype(F32)).astype(BF16)
    return hi, mid, lo


def _sigmoid(x):
    return 1.0 / (1.0 + jnp.exp(-x))


def _softplus(x):
    return jnp.maximum(x, 0.0) + jnp.log1p(jnp.exp(-jnp.abs(x)))


def _gelu_tanh(x):
    c = 0.7978845608028654
    return 0.5 * x * (1.0 + jnp.tanh(c * (x + 0.044715 * (x * x * x))))


def _layer_norm(x, g, b):
    mu = jnp.mean(x, axis=-1, keepdims=True)
    xc = x - mu
    var = jnp.mean(xc * xc, axis=-1, keepdims=True)
    return xc * lax.rsqrt(var + LN_EPS) * g + b


def _shift_rows(z, prev8, s):
    rolled = pltpu.roll(z, s, 0)
    row8 = lax.broadcasted_iota(jnp.int32, prev8.shape, 0)
    head = jnp.where(row8 < s, pltpu.roll(prev8, s, 0), rolled[:8])
    return jnp.concatenate([head, rolled[8:]], axis=0)


def _params(sem, vmem_limit=VMEM_LIMIT):
    return pltpu.CompilerParams(dimension_semantics=sem, vmem_limit_bytes=vmem_limit)


IN_TN = 512
IN_MAIN_BLOCKS = 5 * D_MODEL // IN_TN
IN_TAIL_BLOCKS = (2 * D_MODEL + 2 * D_LORA_PAD + D_GATE_LORA) // IN_TN


def _in_out_block(j):
    b2 = 2 * D_MODEL // IN_TN
    return jnp.where(j < b2, j,
                     jnp.where(j < IN_MAIN_BLOCKS, j + b2,
                               jnp.where(j < IN_MAIN_BLOCKS + b2, j - (IN_MAIN_BLOCKS - b2), j)))


IN_D_BLOCKS = D_MODEL // IN_TN
CP_MU, CP_CONV_W, CP_CONV_B, CP_ROWS = 0, 1, 1 + CONV_WIDTH, 8


def _in_proj_kernel(x_ref, g_ref, b_ref, wm_ref, wt_ref, cp_ref, z_ref, hb_ref, carry_ref):
    i = pl.program_id(0)
    j = pl.program_id(1)
    nd = IN_D_BLOCKS

    @pl.when(j == 0)
    def _():
        hb_ref[...] = _layer_norm(x_ref[...], g_ref[...], b_ref[...]).astype(BF16)

    @pl.when((i == 0) & (j == 0))
    def _():
        carry_ref[...] = jnp.zeros_like(carry_ref)

    def keep_tail(z):
        carry_ref[j] = z[z.shape[0] - 8:, :]

    @pl.when(j < nd)
    def _():
        z = _dot(hb_ref[...], wm_ref[...].astype(BF16))
        prev8 = carry_ref[j]
        cw = CP_CONV_W + CONV_WIDTH - 1
        conv = cp_ref[CP_CONV_B:CP_CONV_B + 1, :] + z * cp_ref[cw:cw + 1, :]
        for s in range(1, CONV_WIDTH):
            conv = conv + _shift_rows(z, prev8, s) * cp_ref[cw - s:cw - s + 1, :]
        keep_tail(z)
        z_ref[...] = conv

    @pl.when((j >= nd) & (j < 2 * nd))
    def _():
        z_ref[...] = _gelu_tanh(_dot(hb_ref[...], wm_ref[...].astype(BF16)))

    def token_shift(z):
        zp = _shift_rows(z, carry_ref[j], 1)
        keep_tail(z)
        return z + (zp - z) * cp_ref[CP_MU:CP_MU + 1, :]

    @pl.when((j >= 2 * nd) & (j < IN_MAIN_BLOCKS))
    def _():
        z_ref[...] = token_shift(_dot(hb_ref[...], wm_ref[...].astype(BF16)))

    @pl.when((j >= IN_MAIN_BLOCKS) & (j < IN_MAIN_BLOCKS + 2 * nd))
    def _():
        z_ref[...] = _sigmoid(_dot(hb_ref[...], wt_ref[...]))

    @pl.when(j >= IN_MAIN_BLOCKS + 2 * nd)
    def _():
        z_ref[...] = token_shift(_dot(hb_ref[...], wt_ref[...]))


def _in_proj(x, g, b, w_main, w_tail, col_params, tm=1024):
    t, d = x.shape
    tn = IN_TN
    nb = IN_MAIN_BLOCKS + IN_TAIL_BLOCKS
    tm = min(tm, t)
    return pl.pallas_call(
        _in_proj_kernel,
        grid=(t // tm, nb),
        in_specs=[
            pl.BlockSpec((tm, d), lambda i, j: (i, 0)),
            pl.BlockSpec((1, d), lambda i, j: (0, 0)),
            pl.BlockSpec((1, d), lambda i, j: (0, 0)),
            pl.BlockSpec((d, tn), lambda i, j: (0, jnp.minimum(j, IN_MAIN_BLOCKS - 1))),
            pl.BlockSpec((d, tn), lambda i, j: (0, jnp.maximum(j - IN_MAIN_BLOCKS, 0))),
            pl.BlockSpec((CP_ROWS, tn), lambda i, j: (0, j)),
        ],
        out_specs=pl.BlockSpec((tm, tn), lambda i, j: (i, _in_out_block(j))),
        out_shape=jax.ShapeDtypeStruct((t, nb * tn), F32),
        scratch_shapes=[pltpu.VMEM((tm, d), BF16), pltpu.VMEM((nb, 8, tn), F32)],
        compiler_params=_params(("arbitrary", "arbitrary")),
        name="in_proj",
    )(x, g, b, w_main, w_tail, col_params)


def _lru_kernel(u_ref, gate_ref, ga_ref, wa_ref, ba_ref, wx_ref, bx_ref,
                lam_ref, out_ref, hcarry):
    @pl.when(pl.program_id(0) == 0)
    def _():
        hcarry[...] = jnp.zeros_like(hcarry)

    conv = u_ref[...]
    tm = conv.shape[0]
    cb16 = conv.astype(BF16)
    r_parts, i_parts = [], []
    for g in range(LRU_HEADS):
        ug = cb16[:, g * LRU_HEAD_DIM:(g + 1) * LRU_HEAD_DIM]
        r_parts.append(_dot(ug, wa_ref[g]))
        i_parts.append(_dot(ug, wx_ref[g]))
    r = _sigmoid(jnp.concatenate(r_parts, axis=1) + ba_ref[...])
    ig = _sigmoid(jnp.concatenate(i_parts, axis=1) + bx_ref[...])

    log_a = (-LRU_C) * r * _softplus(-lam_ref[...])
    a_c = jnp.exp(log_a)
    b_c = jnp.sqrt(-jnp.tanh(log_a) * (a_c * a_c + 1.0)) * (ig * conv)

    n8 = tm // 8
    a3 = a_c.reshape(n8, 8, a_c.shape[1])
    b3 = b_c.reshape(n8, 8, b_c.shape[1])
    sub = lax.broadcasted_iota(jnp.int32, a3.shape, 1)
    for d in (1, 2, 4):
        m = sub >= d
        a_sh = jnp.where(m, pltpu.roll(a3, d, 1), 1.0)
        b_sh = jnp.where(m, pltpu.roll(b3, d, 1), 0.0)
        b3 = a3 * b_sh + b3
        a3 = a3 * a_sh
    carry = hcarry[...]
    hs = []
    for i in range(n8):
        h_i = b3[i] + a3[i] * carry
        hs.append(h_i)
        carry = h_i[7:8, :]
    hcarry[...] = carry
    h = jnp.concatenate(hs, axis=0)

    out_ref[...] = ga_ref[...] * (gate_ref[...] * h)


def _lru(zin, wa, ba, wx, bx, lam, tm=256):
    t = zin.shape[0]
    d = D_MODEL
    row = lambda i: (0, 0)
    return pl.pallas_call(
        _lru_kernel,
        grid=(t // tm,),
        in_specs=[
            pl.BlockSpec((tm, d), lambda i: (i, 0)),
            pl.BlockSpec((tm, d), lambda i: (i, 1)),
            pl.BlockSpec((tm, d), lambda i: (i, 2)),
            pl.BlockSpec((LRU_HEADS, LRU_HEAD_DIM, LRU_HEAD_DIM), lambda i: (0, 0, 0)),
            pl.BlockSpec((1, d), row),
            pl.BlockSpec((LRU_HEADS, LRU_HEAD_DIM, LRU_HEAD_DIM), lambda i: (0, 0, 0)),
            pl.BlockSpec((1, d), row),
            pl.BlockSpec((1, d), row),
        ],
        out_specs=pl.BlockSpec((tm, d), lambda i: (i, 0)),
        out_shape=jax.ShapeDtypeStruct((t, d), F32),
        scratch_shapes=[pltpu.VMEM((1, d), F32)],
        compiler_params=_params(("arbitrary",)),
        name="lru",
    )(zin, zin, zin, wa, ba, wx, bx, lam)


def _block_ones(n, seg):
    r = lax.broadcasted_iota(jnp.int32, (n, n), 0) // seg
    c = lax.broadcasted_iota(jnp.int32, (n, n), 1) // seg
    return jnp.where(r == c, 1.0, 0.0).astype(BF16)


def _seg_sum(x, ones_bd):
    parts = []
    for g in range(x.shape[1] // GROUP):
        hi, lo = _split2(x[:, g * GROUP:(g + 1) * GROUP])
        parts.append(_dot(hi, ones_bd) + _dot(lo, ones_bd))
    return jnp.concatenate(parts, axis=1)


def _rwkv_prep_kernel(zk_ref, zl_ref, w0_ref, wb_ref, a0_ref, ab_ref, gb_ref, kk_ref, ka_ref,
                      lw_out, kf_out, kn_out, b_out, g_out):
    k = zk_ref[...]
    zl = zl_ref[...]
    w_lo = zl[:, 0:D_LORA_PAD]
    a_lo = zl[:, D_LORA_PAD:2 * D_LORA_PAD]
    g_lo = zl[:, 2 * D_LORA_PAD:]

    w_log = -_softplus(-(w0_ref[...] + _dot(jnp.tanh(w_lo).astype(BF16), wb_ref[...]))) - 0.5
    lw_out[...] = -jnp.exp(w_log)
    a = _sigmoid(a0_ref[...] + _dot(a_lo.astype(BF16), ab_ref[...]))
    g_out[...] = _dot(_sigmoid(g_lo).astype(BF16), gb_ref[...])

    ones_bd = _block_ones(GROUP, RWKV_HEAD_DIM)
    kk = k * kk_ref[...]
    norm = jnp.sqrt(_seg_sum(kk * kk, ones_bd))
    kn = kk / jnp.maximum(norm, 1e-12)
    kn_out[...] = kn
    b_out[...] = kn * a
    kf_out[...] = k * (1.0 + (a - 1.0) * ka_ref[...])


def _rwkv_prep(zin, w0, wb, a0, ab, gb, k_k, k_a, tm=256):
    t = zin.shape[0]
    d = D_MODEL
    dl = 2 * D_LORA_PAD + D_GATE_LORA
    row = lambda i: (0, 0)
    out = jax.ShapeDtypeStruct((t, d), F32)
    blk = pl.BlockSpec((tm, d), lambda i: (i, 0))
    return pl.pallas_call(
        _rwkv_prep_kernel,
        grid=(t // tm,),
        in_specs=[
            pl.BlockSpec((tm, d), lambda i: (i, 5)),
            pl.BlockSpec((tm, dl), lambda i: (i, 7 * d // dl)),
            pl.BlockSpec((1, d), row),
            pl.BlockSpec((D_LORA_PAD, d), row),
            pl.BlockSpec((1, d), row),
            pl.BlockSpec((D_LORA_PAD, d), row),
            pl.BlockSpec((D_GATE_LORA, d), row),
            pl.BlockSpec((1, d), row),
            pl.BlockSpec((1, d), row),
        ],
        out_specs=[blk] * 5,
        out_shape=[out] * 5,
        compiler_params=_params(("arbitrary",)),
        name="rwkv_prep",
    )(zin, zin, w0, wb, a0, ab, gb, k_k, k_a)


REC_ROWS = 256


def _interleave(*gens):
    live = list(gens)
    while live:
        for gen in list(live):
            try:
                next(gen)
            except StopIteration:
                live.remove(gen)


def _chain(*gens):
    for gen in gens:
        yield from gen


def _rwkv_rec_kernel(r_ref, lw_ref, kf_ref, v_ref, kn_ref, b_ref, g_ref, zgb_ref,
                     rk_ref, gng_ref, gnb_ref, out_ref, s_ref):
    @pl.when(pl.program_id(0) == 0)
    def _():
        s_ref[...] = jnp.zeros_like(s_ref)

    c = CHUNK
    hd = RWKV_HEAD_DIM
    ng = N_GROUPS
    nch = r_ref.shape[0] // c
    sls = [slice(g * GROUP, (g + 1) * GROUP) for g in range(ng)]

    rows = lax.broadcasted_iota(jnp.int32, (c, c), 0)
    cols = lax.broadcasted_iota(jnp.int32, (c, c), 1)
    tri = jnp.where(cols <= rows, 1.0, 0.0).astype(BF16)
    t_c = lax.broadcasted_iota(jnp.int32, (c, GROUP), 0)
    s_c = lax.broadcasted_iota(jnp.int32, (c, GROUP), 1) % hd
    strict = s_c < t_c
    incl = s_c <= t_c
    eye_c = jnp.where(s_c == t_c, 1.0, 0.0)
    bd = (lax.broadcasted_iota(jnp.int32, (GROUP, GROUP), 0) // hd
          == lax.broadcasted_iota(jnp.int32, (GROUP, GROUP), 1) // hd)
    ones_bd = _block_ones(GROUP, hd)

    def expand(x_c):
        return jnp.where(bd, jnp.concatenate([x_c] * (GROUP // hd), axis=0), 0.0).astype(BF16)

    def seg_sums(xs):
        hi, lo = _split2(jnp.concatenate(xs, axis=0))
        s = _dot(jnp.concatenate([hi, lo], axis=0), ones_bd)
        n = c * len(xs)
        s = s[:n] + s[n:]
        return [s[i * c:(i + 1) * c] for i in range(len(xs))]

    state = [s_ref[g] for g in range(ng)]
    prep = [None] * nch

    def phase_a(ci):
        rs_ = slice(ci * c, (ci + 1) * c)
        lw = lw_ref[rs_, :]
        hi, mid, lo = _split3(lw)
        cum = _dot(tri, hi) + _dot(tri, mid) + _dot(tri, lo)
        tot = cum[c - 1:c, :]
        yield
        r = r_ref[rs_, :]
        kf = kf_ref[rs_, :]
        v = v_ref[rs_, :]
        bb = b_ref[rs_, :]
        p_inv = jnp.exp(-cum)
        p_end = jnp.exp(tot - cum)
        rq = r * jnp.exp(cum)
        kap = kn_ref[rs_, :] * jnp.exp(cum - lw)
        bet = bb * p_inv
        kt = kf * p_inv
        lhs2 = [jnp.concatenate([kap[:, sl], rq[:, sl]], axis=0).astype(BF16) for sl in sls]
        amat = [_dot_nt(lhs2[g], jnp.concatenate([expand(bet[:, sls[g]]), expand(kt[:, sls[g]])], axis=0))
                for g in range(ng)]
        yield
        l_c = [jnp.where(strict, a[:c, :GROUP], 0.0) for a in amat]
        a_lo = [jnp.concatenate([jnp.where(strict, a[:c, GROUP:], 0.0),
                                 jnp.where(incl, a[c:, GROUP:], 0.0)], axis=0).astype(BF16) for a in amat]
        arb = [jnp.where(incl, a[c:, :GROUP], 0.0).astype(BF16) for a in amat]
        av = [_dot(a_lo[g], expand(v[:, sls[g]])) for g in range(ng)]
        x_c = [eye_c - jnp.where((t_c >> 1) == (s_c >> 1), l, 0.0) for l in l_c]
        for lvl in range(2, 7):
            lmask = ((t_c >> lvl) == (s_c >> lvl)) & ((t_c >> (lvl - 1)) != (s_c >> (lvl - 1)))
            y_c = [_dot(x_c[g].astype(BF16), expand(jnp.where(lmask, l_c[g], 0.0))) for g in range(ng)]
            yield
            x_c = [x_c[g] - _dot(y_c[g].astype(BF16), expand(x_c[g])) for g in range(ng)]
            yield
        upd_rhs = [jnp.concatenate([-(bb * p_end)[:, sl], (kf * p_end)[:, sl]], axis=0).astype(BF16)
                   for sl in sls]
        prep[ci] = dict(lhs2=lhs2, av=av, arb=arb, x=[x.astype(BF16) for x in x_c], v=v,
                        upd_rhs=upd_rhs, p_tot=jnp.exp(tot), rkk=r * kf * rk_ref[...])

    def phase_b(ci):
        p = prep[ci]
        rs_ = slice(ci * c, (ci + 1) * c)
        v = p["v"]
        rs = [_dot_nt(p["lhs2"][g], state[g].astype(BF16)) for g in range(ng)]
        yield
        u_c = [_dot(p["x"][g], expand(rs[g][:c] + p["av"][g][:c])) for g in range(ng)]
        yield
        o_c = [rs[g][c:] + p["av"][g][c:] - _dot(p["arb"][g], expand(u_c[g])) for g in range(ng)]
        for g in range(ng):
            upd = _dot_tn(jnp.concatenate([u_c[g], v[:, sls[g]]], axis=0).astype(BF16), p["upd_rhs"][g])
            state[g] = state[g] * p["p_tot"][:, sls[g]] + jnp.where(bd, upd, 0.0)
        yield
        sums = seg_sums(o_c + [p["rkk"][:, sl] for sl in sls])
        dev = [o_c[g] - sums[g] * (1.0 / hd) for g in range(ng)]
        yield
        var = seg_sums([dv * dv for dv in dev])
        o_parts = []
        for g in range(ng):
            o_n = dev[g] * lax.rsqrt(var[g] * (1.0 / hd) + GN_EPS) * gng_ref[:, sls[g]] + gnb_ref[:, sls[g]]
            o_parts.append(o_n + sums[ng + g] * v[:, sls[g]])
        o = jnp.concatenate(o_parts, axis=1)
        out_ref[rs_, :] = zgb_ref[rs_, :] * (o * g_ref[rs_, :])

    pairs = [list(range(i, min(i + 2, nch))) for i in range(0, nch, 2)]
    _interleave(*[phase_a(ci) for ci in pairs[0]])
    for k in range(len(pairs)):
        b_gen = _chain(*[phase_b(ci) for ci in pairs[k]])
        a_gens = [phase_a(ci) for ci in pairs[k + 1]] if k + 1 < len(pairs) else []
        _interleave(b_gen, *a_gens)

    for g in range(ng):
        s_ref[g] = state[g]


def _rwkv_rec(zin, lw, kf, kn, b, g, rk, gng, gnb):
    t, d = lw.shape
    br = min(REC_ROWS, t)
    blk = pl.BlockSpec((br, d), lambda i: (i, 0))
    zcol = lambda c: pl.BlockSpec((br, d), lambda i: (i, c))
    row = pl.BlockSpec((1, d), lambda i: (0, 0))
    return pl.pallas_call(
        _rwkv_rec_kernel,
        grid=(t // br,),
        in_specs=[zcol(4), blk, blk, zcol(6), blk, blk, blk, zcol(3), row, row, row],
        out_specs=blk,
        out_shape=jax.ShapeDtypeStruct((t, d), F32),
        scratch_shapes=[pltpu.VMEM((N_GROUPS, GROUP, GROUP), F32)],
        compiler_params=_params(("arbitrary",), VMEM_LIMIT_BIG_TILES),
        name="rwkv_rec",
    )(zin, lw, kf, zin, kn, b, g, zin, rk, gng, gnb)


def _out_proj_kernel(ya_ref, yb_ref, x_ref, gin_ref, bin_ref, w_ref, g_ref, b_ref, o_ref):
    y = (ya_ref[...] + yb_ref[...]).astype(BF16)
    mix = _dot(y, w_ref[...])
    h = _layer_norm(x_ref[...], gin_ref[...], bin_ref[...])
    o_ref[...] = _layer_norm(ALPHA * h + mix, g_ref[...], b_ref[...])


def _resident(shape):
    return pl.BlockSpec(shape, lambda *_: (0,) * len(shape), pipeline_mode=pl.Buffered(1))


def _out_proj(ya, yb, x, g_in, b_in, w, g, b, tm=512):
    t, d = x.shape
    blk = pl.BlockSpec((tm, d), lambda i: (i, 0))
    row = pl.BlockSpec((1, d), lambda i: (0, 0))
    return pl.pallas_call(
        _out_proj_kernel,
        grid=(t // tm,),
        in_specs=[blk, blk, blk, row, row, _resident((d, d)), row, row],
        out_specs=blk,
        out_shape=jax.ShapeDtypeStruct((t, d), F32),
        compiler_params=_params(("arbitrary",)),
        name="out_proj",
    )(ya, yb, x, g_in, b_in, w, g, b)


def _mem_kv_kernel(mem_ref, wk_ref, wv_ref, k_ref, v_ref):
    m = mem_ref[...].astype(BF16)
    k_ref[...] = _dot(m, wk_ref[...]).astype(BF16)
    v_ref[...] = _dot(m, wv_ref[...]).astype(BF16)


def _mem_kv(mem, wk, wv, tn=512):
    n, d = mem.shape
    return pl.pallas_call(
        _mem_kv_kernel,
        grid=(d // tn,),
        in_specs=[pl.BlockSpec((n, d), lambda j: (0, 0)),
                  pl.BlockSpec((d, tn), lambda j: (0, j)),
                  pl.BlockSpec((d, tn), lambda j: (0, j))],
        out_specs=[pl.BlockSpec((n, tn), lambda j: (0, j))] * 2,
        out_shape=[jax.ShapeDtypeStruct((n, d), BF16)] * 2,
        compiler_params=_params(("arbitrary",)),
        name="mem_kv",
    )(mem, wk, wv)


def _xattn_kernel(h_ref, wq_ref, k_ref, v_ref, wo_ref, g_ref, b_ref, o_ref):
    h = h_ref[...]
    q = _dot(h.astype(BF16), wq_ref[...]).astype(BF16)
    scale = XATTN_HEAD_DIM ** -0.5
    outs = []
    for hh in range(XATTN_HEADS):
        sl = slice(hh * XATTN_HEAD_DIM, (hh + 1) * XATTN_HEAD_DIM)
        s = _dot_nt(q[:, sl], k_ref[:, sl]) * scale
        e = jnp.exp(s - jnp.max(s, axis=-1, keepdims=True))
        p = e / jnp.sum(e, axis=-1, keepdims=True)
        outs.append(_dot(p.astype(BF16), v_ref[:, sl]))
    o = jnp.concatenate(outs, axis=1).astype(BF16)
    xa = _dot(o, wo_ref[...])
    o_ref[...] = _layer_norm(ALPHA * h + xa, g_ref[...], b_ref[...])


def _xattn(h, wq, kmem, vmem, wo, g, b, tm=512):
    t, d = h.shape
    n = kmem.shape[0]
    blk = pl.BlockSpec((tm, d), lambda i: (i, 0))
    row = pl.BlockSpec((1, d), lambda i: (0, 0))
    return pl.pallas_call(
        _xattn_kernel,
        grid=(t // tm,),
        in_specs=[blk, _resident((d, d)), _resident((n, d)), _resident((n, d)), _resident((d, d)),
                  row, row],
        out_specs=blk,
        out_shape=jax.ShapeDtypeStruct((t, d), F32),
        compiler_params=_params(("arbitrary",)),
        name="xattn",
    )(h, wq, kmem, vmem, wo, g, b)


def _ffn_kernel(h_ref, wg_ref, wu_ref, wd_ref, g_ref, b_ref, o_ref):
    j = pl.program_id(1)

    @pl.when(j == 0)
    def _():
        o_ref[...] = jnp.zeros_like(o_ref)

    hb = h_ref[...].astype(BF16)
    gate = _dot(hb, wg_ref[...].astype(BF16))
    up = _dot(hb, wu_ref[...].astype(BF16))
    act = (gate * _sigmoid(gate) * up).astype(BF16)
    o_ref[...] += _dot(act, wd_ref[...].astype(BF16))

    @pl.when(j == pl.num_programs(1) - 1)
    def _():
        o_ref[...] = _layer_norm(ALPHA * h_ref[...] + o_ref[...], g_ref[...], b_ref[...])


def _ffn(h, wg, wu, wd, g, b, tm=1024, tf=256):
    t, d = h.shape
    f = wg.shape[1]
    tm = min(tm, t)
    blk = pl.BlockSpec((tm, d), lambda i, j: (i, 0))
    row = pl.BlockSpec((1, d), lambda i, j: (0, 0))
    return pl.pallas_call(
        _ffn_kernel,
        grid=(t // tm, f // tf),
        in_specs=[blk,
                  pl.BlockSpec((d, tf), lambda i, j: (0, j)),
                  pl.BlockSpec((d, tf), lambda i, j: (0, j)),
                  pl.BlockSpec((tf, d), lambda i, j: (j, 0)),
                  row, row],
        out_specs=blk,
        out_shape=jax.ShapeDtypeStruct((t, d), F32),
        compiler_params=_params(("arbitrary", "arbitrary"), VMEM_LIMIT_BIG_TILES),
        name="ffn",
    )(h, wg, wu, wd, g, b)


def _pad_cols(w, n):
    return jnp.pad(w, ((0, 0), (0, n - w.shape[1])))


def _pad_rows(w, n):
    return jnp.pad(w, ((0, n - w.shape[0]), (0, 0)))


def _split_in_cols(w):
    d = D_MODEL
    o = 5 * d
    lo = w.shape[1] - 7 * d - D_GATE_LORA
    w_lo, a_lo = w[:, o:o + lo // 2], w[:, o + lo // 2:o + lo]
    g_lo = w[:, o + lo:o + lo + D_GATE_LORA]
    gates = w[:, o + lo + D_GATE_LORA:]
    tail = jnp.concatenate(
        [gates, _pad_cols(w_lo, D_LORA_PAD), _pad_cols(a_lo, D_LORA_PAD), g_lo], axis=1)
    return w[:, :o], tail


def kernel(x, mem, ln_in_g, ln_in_b, w_in, conv_w, conv_b, lru_wa, lru_ba, lru_wx, lru_bx, lru_lambda, rw_mu, rw_w0, rw_wB, rw_a0, rw_aB, rw_gB, rw_kk, rw_ka, rw_rk, rw_gn_g, rw_gn_b, w_out, ln1_g, ln1_b, xa_wq, xa_wk, xa_wv, xa_wo, ln2_g, ln2_b, ffn_wg, ffn_wu, ffn_wd, ln3_g, ln3_b):
    bsz, t, d = x.shape
    depth = w_in.shape[0]
    assert bsz == 1 and d == D_MODEL and t % 512 == 0
    row = lambda p: p.reshape(1, -1)

    h = None
    for l in range(depth):
        _, w_tail = _split_in_cols(w_in[l])
        mu_main, mu_tail = _split_in_cols(jnp.pad(row(rw_mu[l]), ((0, 0), (2 * d, 2 * d))))
        n_cols = mu_main.shape[1] + mu_tail.shape[1]
        col_params = jnp.concatenate([
            jnp.concatenate([mu_main, mu_tail], axis=1),
            _pad_cols(conv_w[l], n_cols),
            _pad_cols(row(conv_b[l]), n_cols),
            jnp.zeros((CP_ROWS - CP_CONV_B - 1, n_cols), F32)], axis=0)
        if l == 0:
            zin = _in_proj(x[0], row(ln_in_g), row(ln_in_b), w_in[l], w_tail.astype(BF16), col_params)
        else:
            raise NotImplementedError("DEPTH > 1 is not part of this problem")

        ya = _lru(zin, lru_wa[l].astype(BF16), row(lru_ba[l]),
                  lru_wx[l].astype(BF16), row(lru_bx[l]), row(lru_lambda[l]))

        lw, kf, kn, b, g = _rwkv_prep(
            zin, row(rw_w0[l]),
            _pad_rows(rw_wB[l], D_LORA_PAD).astype(BF16), row(rw_a0[l]),
            _pad_rows(rw_aB[l], D_LORA_PAD).astype(BF16), rw_gB[l].astype(BF16),
            row(rw_kk[l]), row(rw_ka[l]))
        yb = _rwkv_rec(zin, lw, kf, kn, b, g, row(rw_rk[l]), row(rw_gn_g[l]), row(rw_gn_b[l]))

        h = _out_proj(ya, yb, x[0], row(ln_in_g), row(ln_in_b), w_out[l].astype(BF16),
                      row(ln1_g[l]), row(ln1_b[l]))
        kmem, vmem = _mem_kv(mem[0], xa_wk[l].astype(BF16), xa_wv[l].astype(BF16))
        h = _xattn(h, xa_wq[l].astype(BF16), kmem, vmem, xa_wo[l].astype(BF16),
                   row(ln2_g[l]), row(ln2_b[l]))
        h = _ffn(h, ffn_wg[l], ffn_wu[l], ffn_wd[l], row(ln3_g[l]), row(ln3_b[l]))
    return h[None]
```

```python
import functools

import jax
import jax.numpy as jnp
from jax import lax
from jax.experimental import pallas as pl
from jax.experimental.pallas import tpu as pltpu

F32 = jnp.float32
BF16 = jnp.bfloat16

D_MODEL = 2048
LN_EPS = 1e-5
ALPHA = 2.0 ** 0.25

LRU_HEADS = 16
LRU_HEAD_DIM = 128
CONV_WIDTH = 4
LRU_C = 8.0

RWKV_HEAD_DIM = 64
D_LORA_PAD = 128
D_GATE_LORA = 256
GN_EPS = 64e-5
CHUNK = 64
GROUP = 256
N_GROUPS = D_MODEL // GROUP

XATTN_HEADS = 4
XATTN_HEAD_DIM = 512
N_MEM = 256
D_FF = 5632

VMEM_LIMIT = 56 * 1024 * 1024
VMEM_LIMIT_BIG_TILES = 60 * 1024 * 1024


def _dot(a, b):
    return jnp.dot(a, b, preferred_element_type=F32)


def _dot_nt(a, b):
    return lax.dot_general(a, b, (((1,), (1,)), ((), ())), preferred_element_type=F32)


def _dot_tn(a, b):
    return lax.dot_general(a, b, (((0,), (0,)), ((), ())), preferred_element_type=F32)


def _split2(x):
    hi = x.astype(BF16)
    lo = (x - hi.astype(F32)).astype(BF16)
    return hi, lo


def _split3(x):
    hi = x.astype(BF16)
    r1 = x - hi.astype(F32)
    mid = r1.astype(BF16)
    lo = (r1 - mid.astype(F32)).astype(BF16)
    return hi, mid, lo


def _sigmoid(x):
    return 1.0 / (1.0 + jnp.exp(-x))


def _softplus(x):
    return jnp.maximum(x, 0.0) + jnp.log1p(jnp.exp(-jnp.abs(x)))


def _gelu_tanh(x):
    c = 0.7978845608028654
    return 0.5 * x * (1.0 + jnp.tanh(c * (x + 0.044715 * (x * x * x))))


def _layer_norm(x, g, b):
    mu = jnp.mean(x, axis=-1, keepdims=True)
    xc = x - mu
    var = jnp.mean(xc * xc, axis=-1, keepdims=True)
    return xc * lax.rsqrt(var + LN_EPS) * g + b


def _shift_rows(z, prev8, s):
    rolled = pltpu.roll(z, s, 0)
    row8 = lax.broadcasted_iota(jnp.int32, prev8.shape, 0)
    head = jnp.where(row8 < s, pltpu.roll(prev8, s, 0), rolled[:8])
    return jnp.concatenate([head, rolled[8:]], axis=0)


def _params(sem, vmem_limit=VMEM_LIMIT):
    return pltpu.CompilerParams(dimension_semantics=sem, vmem_limit_bytes=vmem_limit)


IN_TN = 512
IN_MAIN_BLOCKS = 5 * D_MODEL // IN_TN
IN_TAIL_BLOCKS = (2 * D_MODEL + 2 * D_LORA_PAD + D_GATE_LORA) // IN_TN


def _in_out_block(j):
    b2 = 2 * D_MODEL // IN_TN
    return jnp.where(j < b2, j,
                     jnp.where(j < IN_MAIN_BLOCKS, j + b2,
                               jnp.where(j < IN_MAIN_BLOCKS + b2, j - (IN_MAIN_BLOCKS - b2), j)))


IN_D_BLOCKS = D_MODEL // IN_TN
CP_MU, CP_CONV_W, CP_CONV_B, CP_ROWS = 0, 1, 1 + CONV_WIDTH, 8


def _in_proj_kernel(x_ref, g_ref, b_ref, wm_ref, wt_ref, cp_ref, z_ref, hb_ref, carry_ref):
    i = pl.program_id(0)
    j = pl.program_id(1)
    nd = IN_D_BLOCKS

    @pl.when(j == 0)
    def _():
        hb_ref[...] = _layer_norm(x_ref[...], g_ref[...], b_ref[...]).astype(BF16)

    @pl.when((i == 0) & (j == 0))
    def _():
        carry_ref[...] = jnp.zeros_like(carry_ref)

    def keep_tail(z):
        carry_ref[j] = z[z.shape[0] - 8:, :]

    @pl.when(j < nd)
    def _():
        z = _dot_nt(hb_ref[...], wm_ref[...].astype(BF16))
        prev8 = carry_ref[j]
        cw = CP_CONV_W + CONV_WIDTH - 1
        conv = cp_ref[CP_CONV_B:CP_CONV_B + 1, :] + z * cp_ref[cw:cw + 1, :]
        for s in range(1, CONV_WIDTH):
            conv = conv + _shift_rows(z, prev8, s) * cp_ref[cw - s:cw - s + 1, :]
        keep_tail(z)
        z_ref[...] = conv

    @pl.when((j >= nd) & (j < 2 * nd))
    def _():
        z_ref[...] = _gelu_tanh(_dot_nt(hb_ref[...], wm_ref[...].astype(BF16)))

    def token_shift(z):
        zp = _shift_rows(z, carry_ref[j], 1)
        keep_tail(z)
        return z + (zp - z) * cp_ref[CP_MU:CP_MU + 1, :]

    @pl.when((j >= 2 * nd) & (j < IN_MAIN_BLOCKS))
    def _():
        z_ref[...] = token_shift(_dot_nt(hb_ref[...], wm_ref[...].astype(BF16)))

    @pl.when((j >= IN_MAIN_BLOCKS) & (j < IN_MAIN_BLOCKS + 2 * nd))
    def _():
        z_ref[...] = _sigmoid(_dot_nt(hb_ref[...], wt_ref[...].astype(BF16)))

    @pl.when(j >= IN_MAIN_BLOCKS + 2 * nd)
    def _():
        z_ref[...] = token_shift(_dot_nt(hb_ref[...], wt_ref[...].astype(BF16)))


def _in_proj(x, g, b, w_main, w_tail, col_params, tm=1024):
    t, d = x.shape
    tn = IN_TN
    nb = IN_MAIN_BLOCKS + IN_TAIL_BLOCKS
    tm = min(tm, t)
    return pl.pallas_call(
        _in_proj_kernel,
        grid=(t // tm, nb),
        in_specs=[
            pl.BlockSpec((tm, d), lambda i, j: (i, 0)),
            pl.BlockSpec((1, d), lambda i, j: (0, 0)),
            pl.BlockSpec((1, d), lambda i, j: (0, 0)),
            pl.BlockSpec((tn, d), lambda i, j: (jnp.minimum(j, IN_MAIN_BLOCKS - 1), 0)),
            pl.BlockSpec((tn, d), lambda i, j: (jnp.maximum(j - IN_MAIN_BLOCKS, 0), 0)),
            pl.BlockSpec((CP_ROWS, tn), lambda i, j: (0, j)),
        ],
        out_specs=pl.BlockSpec((tm, tn), lambda i, j: (i, _in_out_block(j))),
        out_shape=jax.ShapeDtypeStruct((t, nb * tn), F32),
        scratch_shapes=[pltpu.VMEM((tm, d), BF16), pltpu.VMEM((nb, 8, tn), F32)],
        compiler_params=_params(("arbitrary", "arbitrary")),
        name="in_proj",
    )(x, g, b, w_main, w_tail, col_params)


def _lru_kernel(u_ref, gate_ref, ga_ref, wa_ref, ba_ref, wx_ref, bx_ref,
                lam_ref, out_ref, hcarry):
    @pl.when(pl.program_id(0) == 0)
    def _():
        hcarry[...] = jnp.zeros_like(hcarry)

    conv = u_ref[...]
    tm = conv.shape[0]
    cb16 = conv.astype(BF16)
    r_parts, i_parts = [], []
    for g in range(LRU_HEADS):
        ug = cb16[:, g * LRU_HEAD_DIM:(g + 1) * LRU_HEAD_DIM]
        r_parts.append(_dot(ug, wa_ref[g]))
        i_parts.append(_dot(ug, wx_ref[g]))
    r = _sigmoid(jnp.concatenate(r_parts, axis=1) + ba_ref[...])
    ig = _sigmoid(jnp.concatenate(i_parts, axis=1) + bx_ref[...])

    log_a = (-LRU_C) * r * _softplus(-lam_ref[...])
    a_c = jnp.exp(log_a)
    b_c = jnp.sqrt(-jnp.tanh(log_a) * (a_c * a_c + 1.0)) * (ig * conv)

    n8 = tm // 8
    a3 = a_c.reshape(n8, 8, a_c.shape[1])
    b3 = b_c.reshape(n8, 8, b_c.shape[1])
    sub = lax.broadcasted_iota(jnp.int32, a3.shape, 1)
    for d in (1, 2, 4):
        m = sub >= d
        a_sh = jnp.where(m, pltpu.roll(a3, d, 1), 1.0)
        b_sh = jnp.where(m, pltpu.roll(b3, d, 1), 0.0)
        b3 = a3 * b_sh + b3
        a3 = a3 * a_sh
    carry = hcarry[...]
    hs = []
    for i in range(n8):
        h_i = b3[i] + a3[i] * carry
        hs.append(h_i)
        carry = h_i[7:8, :]
    hcarry[...] = carry
    h = jnp.concatenate(hs, axis=0)

    out_ref[...] = ga_ref[...] * (gate_ref[...] * h)


def _lru(zin, wa, ba, wx, bx, lam, tm=256):
    t = zin.shape[0]
    d = D_MODEL
    row = lambda i: (0, 0)
    return pl.pallas_call(
        _lru_kernel,
        grid=(t // tm,),
        in_specs=[
            pl.BlockSpec((tm, d), lambda i: (i, 0)),
            pl.BlockSpec((tm, d), lambda i: (i, 1)),
            pl.BlockSpec((tm, d), lambda i: (i, 2)),
            pl.BlockSpec((LRU_HEADS, LRU_HEAD_DIM, LRU_HEAD_DIM), lambda i: (0, 0, 0)),
            pl.BlockSpec((1, d), row),
            pl.BlockSpec((LRU_HEADS, LRU_HEAD_DIM, LRU_HEAD_DIM), lambda i: (0, 0, 0)),
            pl.BlockSpec((1, d), row),
            pl.BlockSpec((1, d), row),
        ],
        out_specs=pl.BlockSpec((tm, d), lambda i: (i, 0)),
        out_shape=jax.ShapeDtypeStruct((t, d), F32),
        scratch_shapes=[pltpu.VMEM((1, d), F32)],
        compiler_params=_params(("arbitrary",)),
        name="lru",
    )(zin, zin, zin, wa, ba, wx, bx, lam)


def _block_ones(n, seg):
    r = lax.broadcasted_iota(jnp.int32, (n, n), 0) // seg
    c = lax.broadcasted_iota(jnp.int32, (n, n), 1) // seg
    return jnp.where(r == c, 1.0, 0.0).astype(BF16)


def _seg_sum(x, ones_bd):
    parts = []
    for g in range(x.shape[1] // GROUP):
        hi, lo = _split2(x[:, g * GROUP:(g + 1) * GROUP])
        parts.append(_dot(hi, ones_bd) + _dot(lo, ones_bd))
    return jnp.concatenate(parts, axis=1)


def _rwkv_prep_kernel(zk_ref, zl_ref, w0_ref, wb_ref, a0_ref, ab_ref, gb_ref, kk_ref, ka_ref,
                      lw_out, kf_out, kn_out, b_out, g_out):
    k = zk_ref[...]
    zl = zl_ref[...]
    w_lo = zl[:, 0:D_LORA_PAD]
    a_lo = zl[:, D_LORA_PAD:2 * D_LORA_PAD]
    g_lo = zl[:, 2 * D_LORA_PAD:]

    w_log = -_softplus(-(w0_ref[...] + _dot(jnp.tanh(w_lo).astype(BF16), wb_ref[...]))) - 0.5
    lw_out[...] = -jnp.exp(w_log)
    a = _sigmoid(a0_ref[...] + _dot(a_lo.astype(BF16), ab_ref[...]))
    g_out[...] = _dot(_sigmoid(g_lo).astype(BF16), gb_ref[...])

    ones_bd = _block_ones(GROUP, RWKV_HEAD_DIM)
    kk = k * kk_ref[...]
    norm = jnp.sqrt(_seg_sum(kk * kk, ones_bd))
    kn = kk / jnp.maximum(norm, 1e-12)
    kn_out[...] = kn
    b_out[...] = kn * a
    kf_out[...] = k * (1.0 + (a - 1.0) * ka_ref[...])


def _rwkv_prep(zin, w0, wb, a0, ab, gb, k_k, k_a, tm=256):
    t = zin.shape[0]
    d = D_MODEL
    dl = 2 * D_LORA_PAD + D_GATE_LORA
    row = lambda i: (0, 0)
    out = jax.ShapeDtypeStruct((t, d), F32)
    blk = pl.BlockSpec((tm, d), lambda i: (i, 0))
    return pl.pallas_call(
        _rwkv_prep_kernel,
        grid=(t // tm,),
        in_specs=[
            pl.BlockSpec((tm, d), lambda i: (i, 5)),
            pl.BlockSpec((tm, dl), lambda i: (i, 7 * d // dl)),
            pl.BlockSpec((1, d), row),
            pl.BlockSpec((D_LORA_PAD, d), row),
            pl.BlockSpec((1, d), row),
            pl.BlockSpec((D_LORA_PAD, d), row),
            pl.BlockSpec((D_GATE_LORA, d), row),
            pl.BlockSpec((1, d), row),
            pl.BlockSpec((1, d), row),
        ],
        out_specs=[blk] * 5,
        out_shape=[out] * 5,
        compiler_params=_params(("arbitrary",)),
        name="rwkv_prep",
    )(zin, zin, w0, wb, a0, ab, gb, k_k, k_a)


REC_ROWS = 256


def _interleave(*gens):
    live = list(gens)
    while live:
        for gen in list(live):
            try:
                next(gen)
            except StopIteration:
                live.remove(gen)


def _chain(*gens):
    for gen in gens:
        yield from gen


def _rwkv_rec_kernel(r_ref, lw_ref, kf_ref, v_ref, kn_ref, b_ref, g_ref, zgb_ref,
                     rk_ref, gng_ref, gnb_ref, out_ref, s_ref):
    @pl.when(pl.program_id(0) == 0)
    def _():
        s_ref[...] = jnp.zeros_like(s_ref)

    c = CHUNK
    hd = RWKV_HEAD_DIM
    ng = N_GROUPS
    nch = r_ref.shape[0] // c
    sls = [slice(g * GROUP, (g + 1) * GROUP) for g in range(ng)]

    rows = lax.broadcasted_iota(jnp.int32, (c, c), 0)
    cols = lax.broadcasted_iota(jnp.int32, (c, c), 1)
    tri = jnp.where(cols <= rows, 1.0, 0.0).astype(BF16)
    t_c = lax.broadcasted_iota(jnp.int32, (c, GROUP), 0)
    s_c = lax.broadcasted_iota(jnp.int32, (c, GROUP), 1) % hd
    strict = s_c < t_c
    incl = s_c <= t_c
    eye_c = jnp.where(s_c == t_c, 1.0, 0.0)
    bd = (lax.broadcasted_iota(jnp.int32, (GROUP, GROUP), 0) // hd
          == lax.broadcasted_iota(jnp.int32, (GROUP, GROUP), 1) // hd)
    ones_bd = _block_ones(GROUP, hd)

    def expand(x_c):
        return jnp.where(bd, jnp.concatenate([x_c] * (GROUP // hd), axis=0), 0.0).astype(BF16)

    def seg_sums(xs):
        hi, lo = _split2(jnp.concatenate(xs, axis=0))
        s = _dot(jnp.concatenate([hi, lo], axis=0), ones_bd)
        n = c * len(xs)
        s = s[:n] + s[n:]
        return [s[i * c:(i + 1) * c] for i in range(len(xs))]

    state = [s_ref[g] for g in range(ng)]
    prep = [None] * nch

    def phase_a(ci):
        rs_ = slice(ci * c, (ci + 1) * c)
        lw = lw_ref[rs_, :]
        hi, mid, lo = _split3(lw)
        cum = _dot(tri, hi) + _dot(tri, mid) + _dot(tri, lo)
        tot = cum[c - 1:c, :]
        yield
        r = r_ref[rs_, :]
        kf = kf_ref[rs_, :]
        v = v_ref[rs_, :]
        bb = b_ref[rs_, :]
        p_inv = jnp.exp(-cum)
        p_end = jnp.exp(tot - cum)
        rq = r * jnp.exp(cum)
        kap = kn_ref[rs_, :] * jnp.exp(cum - lw)
        bet = bb * p_inv
        kt = kf * p_inv
        lhs2 = [jnp.concatenate([kap[:, sl], rq[:, sl]], axis=0).astype(BF16) for sl in sls]
        amat = [_dot_nt(lhs2[g], jnp.concatenate([expand(bet[:, sls[g]]), expand(kt[:, sls[g]])], axis=0))
                for g in range(ng)]
        yield
        l_c = [jnp.where(strict, a[:c, :GROUP], 0.0) for a in amat]
        a_lo = [jnp.concatenate([jnp.where(strict, a[:c, GROUP:], 0.0),
                                 jnp.where(incl, a[c:, GROUP:], 0.0)], axis=0).astype(BF16) for a in amat]
        arb = [jnp.where(incl, a[c:, :GROUP], 0.0).astype(BF16) for a in amat]
        av = [_dot(a_lo[g], expand(v[:, sls[g]])) for g in range(ng)]
        x_c = [eye_c - jnp.where((t_c >> 1) == (s_c >> 1), l, 0.0) for l in l_c]
        for lvl in range(2, 7):
            lmask = ((t_c >> lvl) == (s_c >> lvl)) & ((t_c >> (lvl - 1)) != (s_c >> (lvl - 1)))
            y_c = [_dot(x_c[g].astype(BF16), expand(jnp.where(lmask, l_c[g], 0.0))) for g in range(ng)]
            yield
            x_c = [x_c[g] - _dot(y_c[g].astype(BF16), expand(x_c[g])) for g in range(ng)]
            yield
        upd_rhs = [jnp.concatenate([-(bb * p_end)[:, sl], (kf * p_end)[:, sl]], axis=0).astype(BF16)
                   for sl in sls]
        prep[ci] = dict(lhs2=lhs2, av=av, arb=arb, x=[x.astype(BF16) for x in x_c], v=v,
                        upd_rhs=upd_rhs, p_tot=jnp.exp(tot), rkk=r * kf * rk_ref[...])

    def phase_b(ci):
        p = prep[ci]
        rs_ = slice(ci * c, (ci + 1) * c)
        v = p["v"]
        rs = [_dot_nt(p["lhs2"][g], state[g].astype(BF16)) for g in range(ng)]
        yield
        u_c = [_dot(p["x"][g], expand(rs[g][:c] + p["av"][g][:c])) for g in range(ng)]
        yield
        o_c = [rs[g][c:] + p["av"][g][c:] - _dot(p["arb"][g], expand(u_c[g])) for g in range(ng)]
        for g in range(ng):
            upd = _dot_tn(jnp.concatenate([u_c[g], v[:, sls[g]]], axis=0).astype(BF16), p["upd_rhs"][g])
            state[g] = state[g] * p["p_tot"][:, sls[g]] + jnp.where(bd, upd, 0.0)
        yield
        sums = seg_sums(o_c + [p["rkk"][:, sl] for sl in sls])
        dev = [o_c[g] - sums[g] * (1.0 / hd) for g in range(ng)]
        yield
        var = seg_sums([dv * dv for dv in dev])
        o_parts = []
        for g in range(ng):
            o_n = dev[g] * lax.rsqrt(var[g] * (1.0 / hd) + GN_EPS) * gng_ref[:, sls[g]] + gnb_ref[:, sls[g]]
            o_parts.append(o_n + sums[ng + g] * v[:, sls[g]])
        o = jnp.concatenate(o_parts, axis=1)
        out_ref[rs_, :] = zgb_ref[rs_, :] * (o * g_ref[rs_, :])

    pairs = [list(range(i, min(i + 2, nch))) for i in range(0, nch, 2)]
    _interleave(*[phase_a(ci) for ci in pairs[0]])
    for k in range(len(pairs)):
        b_gen = _chain(*[phase_b(ci) for ci in pairs[k]])
        a_gens = [phase_a(ci) for ci in pairs[k + 1]] if k + 1 < len(pairs) else []
        _interleave(b_gen, *a_gens)

    for g in range(ng):
        s_ref[g] = state[g]


def _rwkv_rec(zin, lw, kf, kn, b, g, rk, gng, gnb):
    t, d = lw.shape
    br = min(REC_ROWS, t)
    blk = pl.BlockSpec((br, d), lambda i: (i, 0))
    zcol = lambda c: pl.BlockSpec((br, d), lambda i: (i, c))
    row = pl.BlockSpec((1, d), lambda i: (0, 0))
    return pl.pallas_call(
        _rwkv_rec_kernel,
        grid=(t // br,),
        in_specs=[zcol(4), blk, blk, zcol(6), blk, blk, blk, zcol(3), row, row, row],
        out_specs=blk,
        out_shape=jax.ShapeDtypeStruct((t, d), F32),
        scratch_shapes=[pltpu.VMEM((N_GROUPS, GROUP, GROUP), F32)],
        compiler_params=_params(("arbitrary",), VMEM_LIMIT_BIG_TILES),
        name="rwkv_rec",
    )(zin, lw, kf, zin, kn, b, g, zin, rk, gng, gnb)


def _out_proj_kernel(ya_ref, yb_ref, x_ref, gin_ref, bin_ref, w_ref, g_ref, b_ref, o_ref):
    y = (ya_ref[...] + yb_ref[...]).astype(BF16)
    mix = _dot(y, w_ref[...])
    h = _layer_norm(x_ref[...], gin_ref[...], bin_ref[...])
    o_ref[...] = _layer_norm(ALPHA * h + mix, g_ref[...], b_ref[...])


def _resident(shape):
    return pl.BlockSpec(shape, lambda *_: (0,) * len(shape), pipeline_mode=pl.Buffered(1))


def _out_proj(ya, yb, x, g_in, b_in, w, g, b, tm=512):
    t, d = x.shape
    blk = pl.BlockSpec((tm, d), lambda i: (i, 0))
    row = pl.BlockSpec((1, d), lambda i: (0, 0))
    return pl.pallas_call(
        _out_proj_kernel,
        grid=(t // tm,),
        in_specs=[blk, blk, blk, row, row, _resident((d, d)), row, row],
        out_specs=blk,
        out_shape=jax.ShapeDtypeStruct((t, d), F32),
        compiler_params=_params(("arbitrary",)),
        name="out_proj",
    )(ya, yb, x, g_in, b_in, w, g, b)


def _mem_kv_kernel(mem_ref, wk_ref, wv_ref, k_ref, v_ref):
    m = mem_ref[...].astype(BF16)
    k_ref[...] = _dot(m, wk_ref[...]).astype(BF16)
    v_ref[...] = _dot(m, wv_ref[...]).astype(BF16)


def _mem_kv(mem, wk, wv, tn=512):
    n, d = mem.shape
    return pl.pallas_call(
        _mem_kv_kernel,
        grid=(d // tn,),
        in_specs=[pl.BlockSpec((n, d), lambda j: (0, 0)),
                  pl.BlockSpec((d, tn), lambda j: (0, j)),
                  pl.BlockSpec((d, tn), lambda j: (0, j))],
        out_specs=[pl.BlockSpec((n, tn), lambda j: (0, j))] * 2,
        out_shape=[jax.ShapeDtypeStruct((n, d), BF16)] * 2,
        compiler_params=_params(("arbitrary",)),
        name="mem_kv",
    )(mem, wk, wv)


def _xattn_kernel(h_ref, wq_ref, k_ref, v_ref, wo_ref, g_ref, b_ref, o_ref):
    h = h_ref[...]
    q = _dot(h.astype(BF16), wq_ref[...]).astype(BF16)
    scale = XATTN_HEAD_DIM ** -0.5
    outs = []
    for hh in range(XATTN_HEADS):
        sl = slice(hh * XATTN_HEAD_DIM, (hh + 1) * XATTN_HEAD_DIM)
        s = _dot_nt(q[:, sl], k_ref[:, sl]) * scale
        e = jnp.exp(s - jnp.max(s, axis=-1, keepdims=True))
        p = e / jnp.sum(e, axis=-1, keepdims=True)
        outs.append(_dot(p.astype(BF16), v_ref[:, sl]))
    o = jnp.concatenate(outs, axis=1).astype(BF16)
    xa = _dot(o, wo_ref[...])
    o_ref[...] = _layer_norm(ALPHA * h + xa, g_ref[...], b_ref[...])


def _xattn(h, wq, kmem, vmem, wo, g, b, tm=512):
    t, d = h.shape
    n = kmem.shape[0]
    blk = pl.BlockSpec((tm, d), lambda i: (i, 0))
    row = pl.BlockSpec((1, d), lambda i: (0, 0))
    return pl.pallas_call(
        _xattn_kernel,
        grid=(t // tm,),
        in_specs=[blk, _resident((d, d)), _resident((n, d)), _resident((n, d)), _resident((d, d)),
                  row, row],
        out_specs=blk,
        out_shape=jax.ShapeDtypeStruct((t, d), F32),
        compiler_params=_params(("arbitrary",)),
        name="xattn",
    )(h, wq, kmem, vmem, wo, g, b)


def _ffn_kernel(h_ref, wg_ref, wu_ref, wd_ref, g_ref, b_ref, o_ref):
    j = pl.program_id(1)

    @pl.when(j == 0)
    def _():
        o_ref[...] = jnp.zeros_like(o_ref)

    hb = h_ref[...].astype(BF16)
    gate = _dot(hb, wg_ref[...].astype(BF16))
    up = _dot(hb, wu_ref[...].astype(BF16))
    act = (gate * _sigmoid(gate) * up).astype(BF16)
    o_ref[...] += _dot(act, wd_ref[...].astype(BF16))

    @pl.when(j == pl.num_programs(1) - 1)
    def _():
        o_ref[...] = _layer_norm(ALPHA * h_ref[...] + o_ref[...], g_ref[...], b_ref[...])


def _ffn(h, wg, wu, wd, g, b, tm=1024, tf=256):
    t, d = h.shape
    f = wg.shape[1]
    tm = min(tm, t)
    blk = pl.BlockSpec((tm, d), lambda i, j: (i, 0))
    row = pl.BlockSpec((1, d), lambda i, j: (0, 0))
    return pl.pallas_call(
        _ffn_kernel,
        grid=(t // tm, f // tf),
        in_specs=[blk,
                  pl.BlockSpec((d, tf), lambda i, j: (0, j)),
                  pl.BlockSpec((d, tf), lambda i, j: (0, j)),
                  pl.BlockSpec((tf, d), lambda i, j: (j, 0)),
                  row, row],
        out_specs=blk,
        out_shape=jax.ShapeDtypeStruct((t, d), F32),
        compiler_params=_params(("arbitrary", "arbitrary"), VMEM_LIMIT_BIG_TILES),
        name="ffn",
    )(h, wg, wu, wd, g, b)


def _pad_cols(w, n):
    return jnp.pad(w, ((0, 0), (0, n - w.shape[1])))


def _pad_rows(w, n):
    return jnp.pad(w, ((0, n - w.shape[0]), (0, 0)))


def _split_in(w, axis):
    d = D_MODEL
    o = 5 * d
    n = w.shape[axis]
    lo = n - 7 * d - D_GATE_LORA
    cut = lambda a, b: lax.slice_in_dim(w, a, b, axis=axis)

    def padded(p):
        widths = [(0, 0)] * w.ndim
        widths[axis] = (0, D_LORA_PAD - p.shape[axis])
        return jnp.pad(p, widths)

    tail = jnp.concatenate(
        [cut(o + lo + D_GATE_LORA, n), padded(cut(o, o + lo // 2)), padded(cut(o + lo // 2, o + lo)),
         cut(o + lo, o + lo + D_GATE_LORA)], axis=axis)
    return cut(0, o), tail


def kernel(x, mem, ln_in_g, ln_in_b, w_in, conv_w, conv_b, lru_wa, lru_ba, lru_wx, lru_bx, lru_lambda, rw_mu, rw_w0, rw_wB, rw_a0, rw_aB, rw_gB, rw_kk, rw_ka, rw_rk, rw_gn_g, rw_gn_b, w_out, ln1_g, ln1_b, xa_wq, xa_wk, xa_wv, xa_wo, ln2_g, ln2_b, ffn_wg, ffn_wu, ffn_wd, ln3_g, ln3_b):
    bsz, t, d = x.shape
    depth = w_in.shape[0]
    assert bsz == 1 and d == D_MODEL and t % 512 == 0
    row = lambda p: p.reshape(1, -1)

    h = None
    for l in range(depth):
        w_t = jnp.swapaxes(w_in[l], 0, 1)
        _, w_tail = _split_in(w_t, 0)
        mu_main, mu_tail = _split_in(jnp.pad(row(rw_mu[l]), ((0, 0), (2 * d, 2 * d))), 1)
        n_cols = mu_main.shape[1] + mu_tail.shape[1]
        col_params = jnp.concatenate([
            jnp.concatenate([mu_main, mu_tail], axis=1),
            _pad_cols(conv_w[l], n_cols),
            _pad_cols(row(conv_b[l]), n_cols),
            jnp.zeros((CP_ROWS - CP_CONV_B - 1, n_cols), F32)], axis=0)
        if l == 0:
            zin = _in_proj(x[0], row(ln_in_g), row(ln_in_b), w_t, w_tail, col_params)
        else:
            raise NotImplementedError("DEPTH > 1 is not part of this problem")

        ya = _lru(zin, lru_wa[l].astype(BF16), row(lru_ba[l]),
                  lru_wx[l].astype(BF16), row(lru_bx[l]), row(lru_lambda[l]))

        lw, kf, kn, b, g = _rwkv_prep(
            zin, row(rw_w0[l]),
            _pad_rows(rw_wB[l], D_LORA_PAD).astype(BF16), row(rw_a0[l]),
            _pad_rows(rw_aB[l], D_LORA_PAD).astype(BF16), rw_gB[l].astype(BF16),
            row(rw_kk[l]), row(rw_ka[l]))
        yb = _rwkv_rec(zin, lw, kf, kn, b, g, row(rw_rk[l]), row(rw_gn_g[l]), row(rw_gn_b[l]))

        h = _out_proj(ya, yb, x[0], row(ln_in_g), row(ln_in_b), w_out[l].astype(BF16),
                      row(ln1_g[l]), row(ln1_b[l]))
        kmem, vmem = _mem_kv(mem[0], xa_wk[l].astype(BF16), xa_wv[l].astype(BF16))
        h = _xattn(h, xa_wq[l].astype(BF16), kmem, vmem, xa_wo[l].astype(BF16),
                   row(ln2_g[l]), row(ln2_b[l]))
        h = _ffn(h, ffn_wg[l], ffn_wu[l], ffn_wd[l], row(ln3_g[l]), row(ln3_b[l]))
    return h[None]
```

```python
import jax
import jax.numpy as jnp
from jax import lax
from jax.experimental import pallas as pl
from jax.experimental.pallas import tpu as pltpu

F32 = jnp.float32
BF16 = jnp.bfloat16

D_MODEL = 2048
LN_EPS = 1e-5
ALPHA = 2.0 ** 0.25

LRU_HEADS = 16
LRU_HEAD_DIM = 128
CONV_WIDTH = 4
LRU_C = 8.0

RWKV_HEAD_DIM = 64
D_LORA_PAD = 128
D_GATE_LORA = 256
GN_EPS = 64e-5
CHUNK = 64
GROUP = 256
N_GROUPS = D_MODEL // GROUP

XATTN_HEADS = 4
XATTN_HEAD_DIM = 512
N_MEM = 256
D_FF = 5632

VMEM_LIMIT = 56 * 1024 * 1024
VMEM_LIMIT_BIG_TILES = 60 * 1024 * 1024


def _dot(a, b):
    return jnp.dot(a, b, preferred_element_type=F32)


def _dot_nt(a, b):
    return lax.dot_general(a, b, (((1,), (1,)), ((), ())), preferred_element_type=F32)


def _dot_tn(a, b):
    return lax.dot_general(a, b, (((0,), (0,)), ((), ())), preferred_element_type=F32)


def _split2(x):
    hi = x.astype(BF16)
    lo = (x - hi.astype(F32)).astype(BF16)
    return hi, lo


def _split3(x):
    hi = x.astype(BF16)
    r1 = x - hi.astype(F32)
    mid = r1.astype(BF16)
    lo = (r1 - mid.astype(F32)).astype(BF16)
    return hi, mid, lo


def _sigmoid(x):
    return 1.0 / (1.0 + jnp.exp(-x))


def _softplus(x):
    return jnp.maximum(x, 0.0) + jnp.log1p(jnp.exp(-jnp.abs(x)))


def _gelu_tanh(x):
    c = 0.7978845608028654
    return 0.5 * x * (1.0 + jnp.tanh(c * (x + 0.044715 * (x * x * x))))


def _layer_norm(x, g, b):
    mu = jnp.mean(x, axis=-1, keepdims=True)
    xc = x - mu
    var = jnp.mean(xc * xc, axis=-1, keepdims=True)
    return xc * lax.rsqrt(var + LN_EPS) * g + b


def _shift_rows(z, prev8, s):
    rolled = pltpu.roll(z, s, 0)
    row8 = lax.broadcasted_iota(jnp.int32, prev8.shape, 0)
    head = jnp.where(row8 < s, pltpu.roll(prev8, s, 0), rolled[:8])
    return jnp.concatenate([head, rolled[8:]], axis=0)


def _params(sem, vmem_limit=VMEM_LIMIT):
    return pltpu.CompilerParams(dimension_semantics=sem, vmem_limit_bytes=vmem_limit)


def _resident(shape):
    return pl.BlockSpec(shape, lambda *_: (0,) * len(shape), pipeline_mode=pl.Buffered(1))


IN_TN = 512
IN_MAIN_BLOCKS = 5 * D_MODEL // IN_TN
IN_TAIL_BLOCKS = (2 * D_MODEL + 2 * D_LORA_PAD + D_GATE_LORA) // IN_TN


def _in_out_block(j):
    b2 = 2 * D_MODEL // IN_TN
    return jnp.where(j < b2, j,
                     jnp.where(j < IN_MAIN_BLOCKS, j + b2,
                               jnp.where(j < IN_MAIN_BLOCKS + b2, j - (IN_MAIN_BLOCKS - b2), j)))


IN_D_BLOCKS = D_MODEL // IN_TN
CP_MU, CP_CONV_W, CP_CONV_B, CP_ROWS = 0, 1, 1 + CONV_WIDTH, 8


def _in_proj_kernel(x_ref, g_ref, b_ref, wm_ref, wt_ref, cp_ref, z_ref, hb_ref, carry_ref):
    i = pl.program_id(0)
    j = pl.program_id(1)
    nd = IN_D_BLOCKS

    @pl.when(j == 0)
    def _():
        hb_ref[...] = _layer_norm(x_ref[...], g_ref[...], b_ref[...]).astype(BF16)

    @pl.when((i == 0) & (j == 0))
    def _():
        carry_ref[...] = jnp.zeros_like(carry_ref)

    def keep_tail(z):
        carry_ref[j] = z[z.shape[0] - 8:, :]

    @pl.when(j < nd)
    def _():
        z = _dot_nt(hb_ref[...], wm_ref[...].astype(BF16))
        prev8 = carry_ref[j]
        cw = CP_CONV_W + CONV_WIDTH - 1
        conv = cp_ref[CP_CONV_B:CP_CONV_B + 1, :] + z * cp_ref[cw:cw + 1, :]
        for s in range(1, CONV_WIDTH):
            conv = conv + _shift_rows(z, prev8, s) * cp_ref[cw - s:cw - s + 1, :]
        keep_tail(z)
        z_ref[...] = conv

    @pl.when((j >= nd) & (j < 2 * nd))
    def _():
        z_ref[...] = _gelu_tanh(_dot_nt(hb_ref[...], wm_ref[...].astype(BF16)))

    def token_shift(z):
        zp = _shift_rows(z, carry_ref[j], 1)
        keep_tail(z)
        return z + (zp - z) * cp_ref[CP_MU:CP_MU + 1, :]

    @pl.when((j >= 2 * nd) & (j < IN_MAIN_BLOCKS))
    def _():
        z_ref[...] = token_shift(_dot_nt(hb_ref[...], wm_ref[...].astype(BF16)))

    @pl.when((j >= IN_MAIN_BLOCKS) & (j < IN_MAIN_BLOCKS + 2 * nd))
    def _():
        z_ref[...] = _sigmoid(_dot_nt(hb_ref[...], wt_ref[...].astype(BF16)))

    @pl.when(j >= IN_MAIN_BLOCKS + 2 * nd)
    def _():
        z_ref[...] = token_shift(_dot_nt(hb_ref[...], wt_ref[...].astype(BF16)))


def _in_proj(x, g, b, w_main, w_tail, col_params, tm=1024):
    t, d = x.shape
    tn = IN_TN
    nb = IN_MAIN_BLOCKS + IN_TAIL_BLOCKS
    tm = min(tm, t)
    return pl.pallas_call(
        _in_proj_kernel,
        grid=(t // tm, nb),
        in_specs=[
            pl.BlockSpec((tm, d), lambda i, j: (i, 0)),
            pl.BlockSpec((1, d), lambda i, j: (0, 0)),
            pl.BlockSpec((1, d), lambda i, j: (0, 0)),
            pl.BlockSpec((tn, d), lambda i, j: (jnp.minimum(j, IN_MAIN_BLOCKS - 1), 0)),
            pl.BlockSpec((tn, d), lambda i, j: (jnp.maximum(j - IN_MAIN_BLOCKS, 0), 0)),
            pl.BlockSpec((CP_ROWS, tn), lambda i, j: (0, j)),
        ],
        out_specs=pl.BlockSpec((tm, tn), lambda i, j: (i, _in_out_block(j))),
        out_shape=jax.ShapeDtypeStruct((t, nb * tn), F32),
        scratch_shapes=[pltpu.VMEM((tm, d), BF16), pltpu.VMEM((nb, 8, tn), F32)],
        compiler_params=_params(("arbitrary", "arbitrary")),
        name="in_proj",
    )(x, g, b, w_main, w_tail, col_params)


def _lru_kernel(u_ref, gate_ref, ga_ref, wa_ref, ba_ref, wx_ref, bx_ref,
                lam_ref, out_ref, hcarry):
    @pl.when(pl.program_id(0) == 0)
    def _():
        hcarry[...] = jnp.zeros_like(hcarry)

    conv = u_ref[...]
    tm = conv.shape[0]
    cb16 = conv.astype(BF16)
    r_parts, i_parts = [], []
    for g in range(LRU_HEADS):
        ug = cb16[:, g * LRU_HEAD_DIM:(g + 1) * LRU_HEAD_DIM]
        r_parts.append(_dot(ug, wa_ref[g]))
        i_parts.append(_dot(ug, wx_ref[g]))
    r = _sigmoid(jnp.concatenate(r_parts, axis=1) + ba_ref[...])
    ig = _sigmoid(jnp.concatenate(i_parts, axis=1) + bx_ref[...])

    log_a = (-LRU_C) * r * _softplus(-lam_ref[...])
    a_c = jnp.exp(log_a)
    b_c = jnp.sqrt(-jnp.tanh(log_a) * (a_c * a_c + 1.0)) * (ig * conv)

    n8 = tm // 8
    a3 = a_c.reshape(n8, 8, a_c.shape[1])
    b3 = b_c.reshape(n8, 8, b_c.shape[1])
    sub = lax.broadcasted_iota(jnp.int32, a3.shape, 1)
    for d in (1, 2, 4):
        m = sub >= d
        a_sh = jnp.where(m, pltpu.roll(a3, d, 1), 1.0)
        b_sh = jnp.where(m, pltpu.roll(b3, d, 1), 0.0)
        b3 = a3 * b_sh + b3
        a3 = a3 * a_sh
    carry = hcarry[...]
    hs = []
    for i in range(n8):
        h_i = b3[i] + a3[i] * carry
        hs.append(h_i)
        carry = h_i[7:8, :]
    hcarry[...] = carry
    h = jnp.concatenate(hs, axis=0)

    out_ref[...] = ga_ref[...] * (gate_ref[...] * h)


def _lru(zin, wa, ba, wx, bx, lam, tm=256):
    t = zin.shape[0]
    d = D_MODEL
    row = lambda i: (0, 0)
    return pl.pallas_call(
        _lru_kernel,
        grid=(t // tm,),
        in_specs=[
            pl.BlockSpec((tm, d), lambda i: (i, 0)),
            pl.BlockSpec((tm, d), lambda i: (i, 1)),
            pl.BlockSpec((tm, d), lambda i: (i, 2)),
            pl.BlockSpec((LRU_HEADS, LRU_HEAD_DIM, LRU_HEAD_DIM), lambda i: (0, 0, 0)),
            pl.BlockSpec((1, d), row),
            pl.BlockSpec((LRU_HEADS, LRU_HEAD_DIM, LRU_HEAD_DIM), lambda i: (0, 0, 0)),
            pl.BlockSpec((1, d), row),
            pl.BlockSpec((1, d), row),
        ],
        out_specs=pl.BlockSpec((tm, d), lambda i: (i, 0)),
        out_shape=jax.ShapeDtypeStruct((t, d), F32),
        scratch_shapes=[pltpu.VMEM((1, d), F32)],
        compiler_params=_params(("arbitrary",)),
        name="lru",
    )(zin, zin, zin, wa, ba, wx, bx, lam)


def _block_ones(n, seg):
    r = lax.broadcasted_iota(jnp.int32, (n, n), 0) // seg
    c = lax.broadcasted_iota(jnp.int32, (n, n), 1) // seg
    return jnp.where(r == c, 1.0, 0.0).astype(BF16)


REC_ROWS = 256


def _interleave(*gens):
    live = list(gens)
    while live:
        for gen in list(live):
            try:
                next(gen)
            except StopIteration:
                live.remove(gen)


def _chain(*gens):
    for gen in gens:
        yield from gen


def _rwkv_rec_kernel(r_ref, k_ref, v_ref, zl_ref, zgb_ref, w0_ref, wb_ref, a0_ref, ab_ref, gb_ref,
                     kk_ref, ka_ref, rk_ref, gng_ref, gnb_ref, out_ref, s_ref):
    @pl.when(pl.program_id(0) == 0)
    def _():
        s_ref[...] = jnp.zeros_like(s_ref)

    c = CHUNK
    hd = RWKV_HEAD_DIM
    ng = N_GROUPS
    nch = r_ref.shape[0] // c
    sls = [slice(g * GROUP, (g + 1) * GROUP) for g in range(ng)]

    rows = lax.broadcasted_iota(jnp.int32, (c, c), 0)
    cols = lax.broadcasted_iota(jnp.int32, (c, c), 1)
    tri = jnp.where(cols <= rows, 1.0, 0.0).astype(BF16)
    t_c = lax.broadcasted_iota(jnp.int32, (c, GROUP), 0)
    s_c = lax.broadcasted_iota(jnp.int32, (c, GROUP), 1) % hd
    strict = s_c < t_c
    incl = s_c <= t_c
    eye_c = jnp.where(s_c == t_c, 1.0, 0.0)
    bd = (lax.broadcasted_iota(jnp.int32, (GROUP, GROUP), 0) // hd
          == lax.broadcasted_iota(jnp.int32, (GROUP, GROUP), 1) // hd)
    ones_bd = _block_ones(GROUP, hd)

    def expand(x_c):
        return jnp.where(bd, jnp.concatenate([x_c] * (GROUP // hd), axis=0), 0.0).astype(BF16)

    def seg_sums(xs):
        hi, lo = _split2(jnp.concatenate(xs, axis=0))
        s = _dot(jnp.concatenate([hi, lo], axis=0), ones_bd)
        n = c * len(xs)
        s = s[:n] + s[n:]
        return [s[i * c:(i + 1) * c] for i in range(len(xs))]

    state = [s_ref[g] for g in range(ng)]
    prep = [None] * nch

    def phase_a(ci):
        rs_ = slice(ci * c, (ci + 1) * c)
        zl = zl_ref[rs_, :]
        w_lo = zl[:, 0:D_LORA_PAD]
        a_lo = zl[:, D_LORA_PAD:2 * D_LORA_PAD]
        g_lo = zl[:, 2 * D_LORA_PAD:]
        w_log = -_softplus(-(w0_ref[...] + _dot(jnp.tanh(w_lo).astype(BF16), wb_ref[...]))) - 0.5
        lw = -jnp.exp(w_log)
        a = _sigmoid(a0_ref[...] + _dot(a_lo.astype(BF16), ab_ref[...]))
        gate = _dot(_sigmoid(g_lo).astype(BF16), gb_ref[...])
        hi, mid, lo = _split3(lw)
        cum = _dot(tri, hi) + _dot(tri, mid) + _dot(tri, lo)
        tot = cum[c - 1:c, :]
        yield
        k = k_ref[rs_, :]
        kk = k * kk_ref[...]
        n2 = jnp.concatenate(seg_sums([(kk * kk)[:, sl] for sl in sls]), axis=1)
        kn = kk / jnp.maximum(jnp.sqrt(n2), 1e-12)
        bb = kn * a
        kf = k * (1.0 + (a - 1.0) * ka_ref[...])
        yield
        r = r_ref[rs_, :]
        v = v_ref[rs_, :]
        p_inv = jnp.exp(-cum)
        p_end = jnp.exp(tot - cum)
        rq = r * jnp.exp(cum)
        kap = kn * jnp.exp(cum - lw)
        bet = bb * p_inv
        kt = kf * p_inv
        lhs2 = [jnp.concatenate([kap[:, sl], rq[:, sl]], axis=0).astype(BF16) for sl in sls]
        amat = [_dot_nt(lhs2[g], jnp.concatenate([expand(bet[:, sls[g]]), expand(kt[:, sls[g]])], axis=0))
                for g in range(ng)]
        yield
        l_c = [jnp.where(strict, a[:c, :GROUP], 0.0) for a in amat]
        a_lo = [jnp.concatenate([jnp.where(strict, a[:c, GROUP:], 0.0),
                                 jnp.where(incl, a[c:, GROUP:], 0.0)], axis=0).astype(BF16) for a in amat]
        arb = [jnp.where(incl, a[c:, :GROUP], 0.0).astype(BF16) for a in amat]
        av = [_dot(a_lo[g], expand(v[:, sls[g]])) for g in range(ng)]
        x_c = [eye_c - jnp.where((t_c >> 1) == (s_c >> 1), l, 0.0) for l in l_c]
        for lvl in range(2, 7):
            lmask = ((t_c >> lvl) == (s_c >> lvl)) & ((t_c >> (lvl - 1)) != (s_c >> (lvl - 1)))
            y_c = [_dot(x_c[g].astype(BF16), expand(jnp.where(lmask, l_c[g], 0.0))) for g in range(ng)]
            yield
            x_c = [x_c[g] - _dot(y_c[g].astype(BF16), expand(x_c[g])) for g in range(ng)]
            yield
        upd_rhs = [jnp.concatenate([-(bb * p_end)[:, sl], (kf * p_end)[:, sl]], axis=0).astype(BF16)
                   for sl in sls]
        prep[ci] = dict(lhs2=lhs2, av=av, arb=arb, x=[x.astype(BF16) for x in x_c], v=v,
                        upd_rhs=upd_rhs, p_tot=jnp.exp(tot), rkk=r * kf * rk_ref[...], gate=gate)

    def phase_b(ci):
        p = prep[ci]
        rs_ = slice(ci * c, (ci + 1) * c)
        v = p["v"]
        rs = [_dot_nt(p["lhs2"][g], state[g].astype(BF16)) for g in range(ng)]
        yield
        u_c = [_dot(p["x"][g], expand(rs[g][:c] + p["av"][g][:c])) for g in range(ng)]
        yield
        o_c = [rs[g][c:] + p["av"][g][c:] - _dot(p["arb"][g], expand(u_c[g])) for g in range(ng)]
        for g in range(ng):
            upd = _dot_tn(jnp.concatenate([u_c[g], v[:, sls[g]]], axis=0).astype(BF16), p["upd_rhs"][g])
            state[g] = state[g] * p["p_tot"][:, sls[g]] + jnp.where(bd, upd, 0.0)
        yield
        sums = seg_sums(o_c + [p["rkk"][:, sl] for sl in sls])
        dev = [o_c[g] - sums[g] * (1.0 / hd) for g in range(ng)]
        yield
        var = seg_sums([dv * dv for dv in dev])
        o_parts = []
        for g in range(ng):
            o_n = dev[g] * lax.rsqrt(var[g] * (1.0 / hd) + GN_EPS) * gng_ref[:, sls[g]] + gnb_ref[:, sls[g]]
            o_parts.append(o_n + sums[ng + g] * v[:, sls[g]])
        o = jnp.concatenate(o_parts, axis=1)
        out_ref[rs_, :] = zgb_ref[rs_, :] * (o * p["gate"])

    pairs = [list(range(i, min(i + 2, nch))) for i in range(0, nch, 2)]
    _interleave(*[phase_a(ci) for ci in pairs[0]])
    for k in range(len(pairs)):
        b_gen = _chain(*[phase_b(ci) for ci in pairs[k]])
        a_gens = [phase_a(ci) for ci in pairs[k + 1]] if k + 1 < len(pairs) else []
        _interleave(b_gen, *a_gens)

    for g in range(ng):
        s_ref[g] = state[g]


def _rwkv_rec(zin, w0, wb, a0, ab, gb, k_k, k_a, rk, gng, gnb):
    t = zin.shape[0]
    d = D_MODEL
    dl = 2 * D_LORA_PAD + D_GATE_LORA
    br = min(REC_ROWS, t)
    zcol = lambda c: pl.BlockSpec((br, d), lambda i: (i, c))
    row = pl.BlockSpec((1, d), lambda i: (0, 0))
    return pl.pallas_call(
        _rwkv_rec_kernel,
        grid=(t // br,),
        in_specs=[zcol(4), zcol(5), zcol(6), pl.BlockSpec((br, dl), lambda i: (i, 7 * d // dl)), zcol(3),
                  row, _resident((D_LORA_PAD, d)), row, _resident((D_LORA_PAD, d)),
                  _resident((D_GATE_LORA, d)), row, row, row, row, row],
        out_specs=pl.BlockSpec((br, d), lambda i: (i, 0)),
        out_shape=jax.ShapeDtypeStruct((t, d), F32),
        scratch_shapes=[pltpu.VMEM((N_GROUPS, GROUP, GROUP), F32)],
        compiler_params=_params(("arbitrary",), VMEM_LIMIT_BIG_TILES),
        name="rwkv_rec",
    )(zin, zin, zin, zin, zin, w0, wb, a0, ab, gb, k_k, k_a, rk, gng, gnb)


def _out_proj_kernel(ya_ref, yb_ref, x_ref, gin_ref, bin_ref, w_ref, g_ref, b_ref, o_ref):
    y = (ya_ref[...] + yb_ref[...]).astype(BF16)
    mix = _dot(y, w_ref[...])
    h = _layer_norm(x_ref[...], gin_ref[...], bin_ref[...])
    o_ref[...] = _layer_norm(ALPHA * h + mix, g_ref[...], b_ref[...])


def _out_proj(ya, yb, x, g_in, b_in, w, g, b, tm=512):
    t, d = x.shape
    blk = pl.BlockSpec((tm, d), lambda i: (i, 0))
    row = pl.BlockSpec((1, d), lambda i: (0, 0))
    return pl.pallas_call(
        _out_proj_kernel,
        grid=(t // tm,),
        in_specs=[blk, blk, blk, row, row, _resident((d, d)), row, row],
        out_specs=blk,
        out_shape=jax.ShapeDtypeStruct((t, d), F32),
        compiler_params=_params(("arbitrary",)),
        name="out_proj",
    )(ya, yb, x, g_in, b_in, w, g, b)


def _mem_kv_kernel(mem_ref, wk_ref, wv_ref, k_ref, v_ref):
    m = mem_ref[...].astype(BF16)
    k_ref[...] = _dot(m, wk_ref[...]).astype(BF16)
    v_ref[...] = _dot(m, wv_ref[...]).astype(BF16)


def _mem_kv(mem, wk, wv, tn=512):
    n, d = mem.shape
    return pl.pallas_call(
        _mem_kv_kernel,
        grid=(d // tn,),
        in_specs=[pl.BlockSpec((n, d), lambda j: (0, 0)),
                  pl.BlockSpec((d, tn), lambda j: (0, j)),
                  pl.BlockSpec((d, tn), lambda j: (0, j))],
        out_specs=[pl.BlockSpec((n, tn), lambda j: (0, j))] * 2,
        out_shape=[jax.ShapeDtypeStruct((n, d), BF16)] * 2,
        compiler_params=_params(("arbitrary",)),
        name="mem_kv",
    )(mem, wk, wv)


def _xattn_kernel(h_ref, wq_ref, k_ref, v_ref, wo_ref, g_ref, b_ref, o_ref):
    h = h_ref[...]
    q = _dot(h.astype(BF16), wq_ref[...]).astype(BF16)
    scale = XATTN_HEAD_DIM ** -0.5
    outs = []
    for hh in range(XATTN_HEADS):
        sl = slice(hh * XATTN_HEAD_DIM, (hh + 1) * XATTN_HEAD_DIM)
        s = _dot_nt(q[:, sl], k_ref[:, sl]) * scale
        e = jnp.exp(s - jnp.max(s, axis=-1, keepdims=True))
        p = e / jnp.sum(e, axis=-1, keepdims=True)
        outs.append(_dot(p.astype(BF16), v_ref[:, sl]))
    o = jnp.concatenate(outs, axis=1).astype(BF16)
    xa = _dot(o, wo_ref[...])
    o_ref[...] = _layer_norm(ALPHA * h + xa, g_ref[...], b_ref[...])


def _xattn(h, wq, kmem, vmem, wo, g, b, tm=512):
    t, d = h.shape
    n = kmem.shape[0]
    blk = pl.BlockSpec((tm, d), lambda i: (i, 0))
    row = pl.BlockSpec((1, d), lambda i: (0, 0))
    return pl.pallas_call(
        _xattn_kernel,
        grid=(t // tm,),
        in_specs=[blk, _resident((d, d)), _resident((n, d)), _resident((n, d)), _resident((d, d)),
                  row, row],
        out_specs=blk,
        out_shape=jax.ShapeDtypeStruct((t, d), F32),
        compiler_params=_params(("arbitrary",)),
        name="xattn",
    )(h, wq, kmem, vmem, wo, g, b)


def _ffn_kernel(h_ref, wg_ref, wu_ref, wd_ref, g_ref, b_ref, o_ref):
    j = pl.program_id(1)

    @pl.when(j == 0)
    def _():
        o_ref[...] = jnp.zeros_like(o_ref)

    hb = h_ref[...].astype(BF16)
    gate = _dot(hb, wg_ref[...].astype(BF16))
    up = _dot(hb, wu_ref[...].astype(BF16))
    act = (gate * _sigmoid(gate) * up).astype(BF16)
    o_ref[...] += _dot(act, wd_ref[...].astype(BF16))

    @pl.when(j == pl.num_programs(1) - 1)
    def _():
        o_ref[...] = _layer_norm(ALPHA * h_ref[...] + o_ref[...], g_ref[...], b_ref[...])


def _ffn(h, wg, wu, wd, g, b, tm=1024, tf=256):
    t, d = h.shape
    f = wg.shape[1]
    tm = min(tm, t)
    blk = pl.BlockSpec((tm, d), lambda i, j: (i, 0))
    row = pl.BlockSpec((1, d), lambda i, j: (0, 0))
    return pl.pallas_call(
        _ffn_kernel,
        grid=(t // tm, f // tf),
        in_specs=[blk,
                  pl.BlockSpec((d, tf), lambda i, j: (0, j)),
                  pl.BlockSpec((d, tf), lambda i, j: (0, j)),
                  pl.BlockSpec((tf, d), lambda i, j: (j, 0)),
                  row, row],
        out_specs=blk,
        out_shape=jax.ShapeDtypeStruct((t, d), F32),
        compiler_params=_params(("arbitrary", "arbitrary"), VMEM_LIMIT_BIG_TILES),
        name="ffn",
    )(h, wg, wu, wd, g, b)


def _pad_cols(w, n):
    return jnp.pad(w, ((0, 0), (0, n - w.shape[1])))


def _pad_rows(w, n):
    return jnp.pad(w, ((0, n - w.shape[0]), (0, 0)))


def _split_in(w, axis):
    d = D_MODEL
    o = 5 * d
    n = w.shape[axis]
    lo = n - 7 * d - D_GATE_LORA
    cut = lambda a, b: lax.slice_in_dim(w, a, b, axis=axis)

    def padded(p):
        widths = [(0, 0)] * w.ndim
        widths[axis] = (0, D_LORA_PAD - p.shape[axis])
        return jnp.pad(p, widths)

    tail = jnp.concatenate(
        [cut(o + lo + D_GATE_LORA, n), padded(cut(o, o + lo // 2)), padded(cut(o + lo // 2, o + lo)),
         cut(o + lo, o + lo + D_GATE_LORA)], axis=axis)
    return cut(0, o), tail


def kernel(x, mem, ln_in_g, ln_in_b, w_in, conv_w, conv_b, lru_wa, lru_ba, lru_wx, lru_bx, lru_lambda, rw_mu, rw_w0, rw_wB, rw_a0, rw_aB, rw_gB, rw_kk, rw_ka, rw_rk, rw_gn_g, rw_gn_b, w_out, ln1_g, ln1_b, xa_wq, xa_wk, xa_wv, xa_wo, ln2_g, ln2_b, ffn_wg, ffn_wu, ffn_wd, ln3_g, ln3_b):
    bsz, t, d = x.shape
    depth = w_in.shape[0]
    assert bsz == 1 and d == D_MODEL and t % 512 == 0
    row = lambda p: p.reshape(1, -1)

    h = None
    for l in range(depth):
        w_t = jnp.swapaxes(w_in[l], 0, 1)
        _, w_tail = _split_in(w_t, 0)
        mu_main, mu_tail = _split_in(jnp.pad(row(rw_mu[l]), ((0, 0), (2 * d, 2 * d))), 1)
        n_cols = mu_main.shape[1] + mu_tail.shape[1]
        col_params = jnp.concatenate([
            jnp.concatenate([mu_main, mu_tail], axis=1),
            _pad_cols(conv_w[l], n_cols),
            _pad_cols(row(conv_b[l]), n_cols),
            jnp.zeros((CP_ROWS - CP_CONV_B - 1, n_cols), F32)], axis=0)
        if l == 0:
            zin = _in_proj(x[0], row(ln_in_g), row(ln_in_b), w_t, w_tail, col_params)
        else:
            raise NotImplementedError("DEPTH > 1 is not part of this problem")

        ya = _lru(zin, lru_wa[l].astype(BF16), row(lru_ba[l]),
                  lru_wx[l].astype(BF16), row(lru_bx[l]), row(lru_lambda[l]))

        yb = _rwkv_rec(
            zin, row(rw_w0[l]),
            _pad_rows(rw_wB[l], D_LORA_PAD).astype(BF16), row(rw_a0[l]),
            _pad_rows(rw_aB[l], D_LORA_PAD).astype(BF16), rw_gB[l].astype(BF16),
            row(rw_kk[l]), row(rw_ka[l]), row(rw_rk[l]), row(rw_gn_g[l]), row(rw_gn_b[l]))

        h = _out_proj(ya, yb, x[0], row(ln_in_g), row(ln_in_b), w_out[l].astype(BF16),
                      row(ln1_g[l]), row(ln1_b[l]))
        kmem, vmem = _mem_kv(mem[0], xa_wk[l].astype(BF16), xa_wv[l].astype(BF16))
        h = _xattn(h, xa_wq[l].astype(BF16), kmem, vmem, xa_wo[l].astype(BF16),
                   row(ln2_g[l]), row(ln2_b[l]))
        h = _ffn(h, ffn_wg[l], ffn_wu[l], ffn_wd[l], row(ln3_g[l]), row(ln3_b[l]))
    return h[None]
```

```python
import jax
import jax.numpy as jnp
from jax import lax
from jax.experimental import pallas as pl
from jax.experimental.pallas import tpu as pltpu

F32 = jnp.float32
BF16 = jnp.bfloat16

D_MODEL = 2048
LN_EPS = 1e-5
ALPHA = 2.0 ** 0.25

LRU_HEADS = 16
LRU_HEAD_DIM = 128
CONV_WIDTH = 4
LRU_C = 8.0

RWKV_HEAD_DIM = 64
D_LORA = 96
D_LORA_PAD = 128
D_GATE_LORA = 256
GN_EPS = 64e-5
CHUNK = 64
GROUP = 256
N_GROUPS = D_MODEL // GROUP

XATTN_HEADS = 4
XATTN_HEAD_DIM = 512
N_MEM = 256
D_FF = 5632

VMEM_LIMIT = 56 * 1024 * 1024
VMEM_LIMIT_BIG_TILES = 60 * 1024 * 1024


def _dot(a, b):
    return jnp.dot(a, b, preferred_element_type=F32)


def _dot_nt(a, b):
    return lax.dot_general(a, b, (((1,), (1,)), ((), ())), preferred_element_type=F32)


def _dot_tn(a, b):
    return lax.dot_general(a, b, (((0,), (0,)), ((), ())), preferred_element_type=F32)


def _split2(x):
    hi = x.astype(BF16)
    lo = (x - hi.astype(F32)).astype(BF16)
    return hi, lo


def _split3(x):
    hi = x.astype(BF16)
    r1 = x - hi.astype(F32)
    mid = r1.astype(BF16)
    lo = (r1 - mid.astype(F32)).astype(BF16)
    return hi, mid, lo


def _sigmoid(x):
    return 1.0 / (1.0 + jnp.exp(-x))


def _softplus(x):
    return jnp.maximum(x, 0.0) + jnp.log1p(jnp.exp(-jnp.abs(x)))


def _gelu_tanh(x):
    c = 0.7978845608028654
    return 0.5 * x * (1.0 + jnp.tanh(c * (x + 0.044715 * (x * x * x))))


def _layer_norm(x, g, b):
    mu = jnp.mean(x, axis=-1, keepdims=True)
    xc = x - mu
    var = jnp.mean(xc * xc, axis=-1, keepdims=True)
    return xc * lax.rsqrt(var + LN_EPS) * g + b


def _shift_rows(z, prev8, s):
    rolled = pltpu.roll(z, s, 0)
    row8 = lax.broadcasted_iota(jnp.int32, prev8.shape, 0)
    head = jnp.where(row8 < s, pltpu.roll(prev8, s, 0), rolled[:8])
    return jnp.concatenate([head, rolled[8:]], axis=0)


def _params(sem, vmem_limit=VMEM_LIMIT):
    return pltpu.CompilerParams(dimension_semantics=sem, vmem_limit_bytes=vmem_limit)


def _resident(shape):
    return pl.BlockSpec(shape, lambda *_: (0,) * len(shape), pipeline_mode=pl.Buffered(1))


IN_TN = 512
IN_D_BLOCKS = D_MODEL // IN_TN
IN_MAIN_BLOCKS = 5 * IN_D_BLOCKS
IN_GATE_BLOCKS = 2 * IN_D_BLOCKS
IN_BLOCKS = IN_MAIN_BLOCKS + IN_GATE_BLOCKS + 1


def _in_weight_row(j, n_out):
    gates = n_out - 2 * D_MODEL + (j - IN_MAIN_BLOCKS) * IN_TN
    return jnp.where(j < IN_MAIN_BLOCKS, j * IN_TN,
                     jnp.where(j < IN_MAIN_BLOCKS + IN_GATE_BLOCKS, gates, 5 * D_MODEL))


def _in_out_block(j):
    b2 = 2 * D_MODEL // IN_TN
    return jnp.where(j < b2, j,
                     jnp.where(j < IN_MAIN_BLOCKS, j + b2,
                               jnp.where(j < IN_MAIN_BLOCKS + b2, j - (IN_MAIN_BLOCKS - b2), j)))


CP_MU, CP_CONV_W, CP_CONV_B, CP_ROWS = 0, 1, 1 + CONV_WIDTH, 8


def _in_proj_kernel(x_ref, g_ref, b_ref, w_ref, cp_ref, z_ref, hb_ref, carry_ref):
    i = pl.program_id(0)
    j = pl.program_id(1)
    nd = IN_D_BLOCKS

    @pl.when(j == 0)
    def _():
        hb_ref[...] = _layer_norm(x_ref[...], g_ref[...], b_ref[...]).astype(BF16)

    @pl.when((i == 0) & (j == 0))
    def _():
        carry_ref[...] = jnp.zeros_like(carry_ref)

    def keep_tail(z):
        carry_ref[j] = z[z.shape[0] - 8:, :]

    @pl.when(j < nd)
    def _():
        z = _dot_nt(hb_ref[...], w_ref[...].astype(BF16))
        prev8 = carry_ref[j]
        cw = CP_CONV_W + CONV_WIDTH - 1
        conv = cp_ref[CP_CONV_B:CP_CONV_B + 1, :] + z * cp_ref[cw:cw + 1, :]
        for s in range(1, CONV_WIDTH):
            conv = conv + _shift_rows(z, prev8, s) * cp_ref[cw - s:cw - s + 1, :]
        keep_tail(z)
        z_ref[...] = conv

    @pl.when((j >= nd) & (j < 2 * nd))
    def _():
        z_ref[...] = _gelu_tanh(_dot_nt(hb_ref[...], w_ref[...].astype(BF16)))

    def token_shift(z):
        zp = _shift_rows(z, carry_ref[j], 1)
        keep_tail(z)
        return z + (zp - z) * cp_ref[CP_MU:CP_MU + 1, :]

    @pl.when((j >= 2 * nd) & (j < IN_MAIN_BLOCKS))
    def _():
        z_ref[...] = token_shift(_dot_nt(hb_ref[...], w_ref[...].astype(BF16)))

    @pl.when((j >= IN_MAIN_BLOCKS) & (j < IN_MAIN_BLOCKS + IN_GATE_BLOCKS))
    def _():
        z_ref[...] = _sigmoid(_dot_nt(hb_ref[...], w_ref[...].astype(BF16)))

    @pl.when(j >= IN_MAIN_BLOCKS + IN_GATE_BLOCKS)
    def _():
        z_ref[...] = token_shift(_dot_nt(hb_ref[...], w_ref[...].astype(BF16)))


def _in_proj(x, g, b, w_t, col_params, tm=1024):
    t, d = x.shape
    tn = IN_TN
    nb = IN_BLOCKS
    n_out = w_t.shape[0]
    tm = min(tm, t)
    return pl.pallas_call(
        _in_proj_kernel,
        grid=(t // tm, nb),
        in_specs=[
            pl.BlockSpec((tm, d), lambda i, j: (i, 0)),
            pl.BlockSpec((1, d), lambda i, j: (0, 0)),
            pl.BlockSpec((1, d), lambda i, j: (0, 0)),
            pl.BlockSpec((pl.Element(tn), pl.Element(d)),
                         lambda i, j: (pl.multiple_of(_in_weight_row(j, n_out), 8), 0)),
            pl.BlockSpec((CP_ROWS, tn), lambda i, j: (0, j)),
        ],
        out_specs=pl.BlockSpec((tm, tn), lambda i, j: (i, _in_out_block(j))),
        out_shape=jax.ShapeDtypeStruct((t, nb * tn), F32),
        scratch_shapes=[pltpu.VMEM((tm, d), BF16), pltpu.VMEM((nb, 8, tn), F32)],
        compiler_params=_params(("arbitrary", "arbitrary")),
        name="in_proj",
    )(x, g, b, w_t, col_params)


def _lru_kernel(u_ref, gate_ref, ga_ref, wa_ref, ba_ref, wx_ref, bx_ref,
                lam_ref, out_ref, hcarry):
    @pl.when(pl.program_id(0) == 0)
    def _():
        hcarry[...] = jnp.zeros_like(hcarry)

    conv = u_ref[...]
    tm = conv.shape[0]
    cb16 = conv.astype(BF16)
    r_parts, i_parts = [], []
    for g in range(LRU_HEADS):
        ug = cb16[:, g * LRU_HEAD_DIM:(g + 1) * LRU_HEAD_DIM]
        r_parts.append(_dot(ug, wa_ref[g]))
        i_parts.append(_dot(ug, wx_ref[g]))
    r = _sigmoid(jnp.concatenate(r_parts, axis=1) + ba_ref[...])
    ig = _sigmoid(jnp.concatenate(i_parts, axis=1) + bx_ref[...])

    log_a = (-LRU_C) * r * _softplus(-lam_ref[...])
    a_c = jnp.exp(log_a)
    b_c = jnp.sqrt(-jnp.tanh(log_a) * (a_c * a_c + 1.0)) * (ig * conv)

    n8 = tm // 8
    a3 = a_c.reshape(n8, 8, a_c.shape[1])
    b3 = b_c.reshape(n8, 8, b_c.shape[1])
    sub = lax.broadcasted_iota(jnp.int32, a3.shape, 1)
    for d in (1, 2, 4):
        m = sub >= d
        a_sh = jnp.where(m, pltpu.roll(a3, d, 1), 1.0)
        b_sh = jnp.where(m, pltpu.roll(b3, d, 1), 0.0)
        b3 = a3 * b_sh + b3
        a3 = a3 * a_sh
    carry = hcarry[...]
    hs = []
    for i in range(n8):
        h_i = b3[i] + a3[i] * carry
        hs.append(h_i)
        carry = h_i[7:8, :]
    hcarry[...] = carry
    h = jnp.concatenate(hs, axis=0)

    out_ref[...] = ga_ref[...] * (gate_ref[...] * h)


def _lru(zin, wa, ba, wx, bx, lam, tm=256):
    t = zin.shape[0]
    d = D_MODEL
    row = lambda i: (0, 0)
    return pl.pallas_call(
        _lru_kernel,
        grid=(t // tm,),
        in_specs=[
            pl.BlockSpec((tm, d), lambda i: (i, 0)),
            pl.BlockSpec((tm, d), lambda i: (i, 1)),
            pl.BlockSpec((tm, d), lambda i: (i, 2)),
            pl.BlockSpec((LRU_HEADS, LRU_HEAD_DIM, LRU_HEAD_DIM), lambda i: (0, 0, 0)),
            pl.BlockSpec((1, d), row),
            pl.BlockSpec((LRU_HEADS, LRU_HEAD_DIM, LRU_HEAD_DIM), lambda i: (0, 0, 0)),
            pl.BlockSpec((1, d), row),
            pl.BlockSpec((1, d), row),
        ],
        out_specs=pl.BlockSpec((tm, d), lambda i: (i, 0)),
        out_shape=jax.ShapeDtypeStruct((t, d), F32),
        scratch_shapes=[pltpu.VMEM((1, d), F32)],
        compiler_params=_params(("arbitrary",)),
        name="lru",
    )(zin, zin, zin, wa, ba, wx, bx, lam)


def _block_ones(n, seg):
    r = lax.broadcasted_iota(jnp.int32, (n, n), 0) // seg
    c = lax.broadcasted_iota(jnp.int32, (n, n), 1) // seg
    return jnp.where(r == c, 1.0, 0.0).astype(BF16)


REC_ROWS = 256


def _interleave(*gens):
    live = list(gens)
    while live:
        for gen in list(live):
            try:
                next(gen)
            except StopIteration:
                live.remove(gen)


def _chain(*gens):
    for gen in gens:
        yield from gen


def _rwkv_rec_kernel(r_ref, k_ref, v_ref, zl_ref, zgb_ref, w0_ref, wb_ref, a0_ref, ab_ref, gb_ref,
                     kk_ref, ka_ref, rk_ref, gng_ref, gnb_ref, out_ref, s_ref):
    @pl.when(pl.program_id(0) == 0)
    def _():
        s_ref[...] = jnp.zeros_like(s_ref)

    c = CHUNK
    hd = RWKV_HEAD_DIM
    ng = N_GROUPS
    nch = r_ref.shape[0] // c
    sls = [slice(g * GROUP, (g + 1) * GROUP) for g in range(ng)]

    rows = lax.broadcasted_iota(jnp.int32, (c, c), 0)
    cols = lax.broadcasted_iota(jnp.int32, (c, c), 1)
    tri = jnp.where(cols <= rows, 1.0, 0.0).astype(BF16)
    t_c = lax.broadcasted_iota(jnp.int32, (c, GROUP), 0)
    s_c = lax.broadcasted_iota(jnp.int32, (c, GROUP), 1) % hd
    strict = s_c < t_c
    incl = s_c <= t_c
    eye_c = jnp.where(s_c == t_c, 1.0, 0.0)
    bd = (lax.broadcasted_iota(jnp.int32, (GROUP, GROUP), 0) // hd
          == lax.broadcasted_iota(jnp.int32, (GROUP, GROUP), 1) // hd)
    ones_bd = _block_ones(GROUP, hd)

    def expand(x_c):
        return jnp.where(bd, jnp.concatenate([x_c] * (GROUP // hd), axis=0), 0.0).astype(BF16)

    def seg_sums(xs):
        hi, lo = _split2(jnp.concatenate(xs, axis=0))
        s = _dot(jnp.concatenate([hi, lo], axis=0), ones_bd)
        n = c * len(xs)
        s = s[:n] + s[n:]
        return [s[i * c:(i + 1) * c] for i in range(len(xs))]

    state = [s_ref[g] for g in range(ng)]
    prep = [None] * nch

    def phase_a(ci):
        rs_ = slice(ci * c, (ci + 1) * c)
        zl = zl_ref[rs_, :]
        nl = zl.shape[1]
        w_lo = zl[:, 0:D_LORA_PAD]
        a_lo = pltpu.roll(zl, nl - D_LORA, 1)[:, 0:D_LORA_PAD]
        g_lo = pltpu.roll(zl, nl - 2 * D_LORA, 1)[:, 0:D_GATE_LORA]
        w_log = -_softplus(-(w0_ref[...] + _dot(jnp.tanh(w_lo).astype(BF16), wb_ref[...]))) - 0.5
        lw = -jnp.exp(w_log)
        a = _sigmoid(a0_ref[...] + _dot(a_lo.astype(BF16), ab_ref[...]))
        gate = _dot(_sigmoid(g_lo).astype(BF16), gb_ref[...])
        hi, mid, lo = _split3(lw)
        cum = _dot(tri, hi) + _dot(tri, mid) + _dot(tri, lo)
        tot = cum[c - 1:c, :]
        yield
        k = k_ref[rs_, :]
        kk = k * kk_ref[...]
        n2 = jnp.concatenate(seg_sums([(kk * kk)[:, sl] for sl in sls]), axis=1)
        kn = kk / jnp.maximum(jnp.sqrt(n2), 1e-12)
        bb = kn * a
        kf = k * (1.0 + (a - 1.0) * ka_ref[...])
        yield
        r = r_ref[rs_, :]
        v = v_ref[rs_, :]
        p_inv = jnp.exp(-cum)
        p_end = jnp.exp(tot - cum)
        rq = r * jnp.exp(cum)
        kap = kn * jnp.exp(cum - lw)
        bet = bb * p_inv
        kt = kf * p_inv
        lhs2 = [jnp.concatenate([kap[:, sl], rq[:, sl]], axis=0).astype(BF16) for sl in sls]
        amat = [_dot_nt(lhs2[g], jnp.concatenate([expand(bet[:, sls[g]]), expand(kt[:, sls[g]])], axis=0))
                for g in range(ng)]
        yield
        l_c = [jnp.where(strict, a[:c, :GROUP], 0.0) for a in amat]
        a_lo = [jnp.concatenate([jnp.where(strict, a[:c, GROUP:], 0.0),
                                 jnp.where(incl, a[c:, GROUP:], 0.0)], axis=0).astype(BF16) for a in amat]
        arb = [jnp.where(incl, a[c:, :GROUP], 0.0).astype(BF16) for a in amat]
        av = [_dot(a_lo[g], expand(v[:, sls[g]])) for g in range(ng)]
        x_c = [eye_c - jnp.where((t_c >> 1) == (s_c >> 1), l, 0.0) for l in l_c]
        for lvl in range(2, 7):
            lmask = ((t_c >> lvl) == (s_c >> lvl)) & ((t_c >> (lvl - 1)) != (s_c >> (lvl - 1)))
            y_c = [_dot(x_c[g].astype(BF16), expand(jnp.where(lmask, l_c[g], 0.0))) for g in range(ng)]
            yield
            x_c = [x_c[g] - _dot(y_c[g].astype(BF16), expand(x_c[g])) for g in range(ng)]
            yield
        upd_rhs = [jnp.concatenate([-(bb * p_end)[:, sl], (kf * p_end)[:, sl]], axis=0).astype(BF16)
                   for sl in sls]
        prep[ci] = dict(lhs2=lhs2, av=av, arb=arb, x=[x.astype(BF16) for x in x_c], v=v,
                        upd_rhs=upd_rhs, p_tot=jnp.exp(tot), rkk=r * kf * rk_ref[...], gate=gate)

    def phase_b(ci):
        p = prep[ci]
        rs_ = slice(ci * c, (ci + 1) * c)
        v = p["v"]
        rs = [_dot_nt(p["lhs2"][g], state[g].astype(BF16)) for g in range(ng)]
        yield
        u_c = [_dot(p["x"][g], expand(rs[g][:c] + p["av"][g][:c])) for g in range(ng)]
        yield
        o_c = [rs[g][c:] + p["av"][g][c:] - _dot(p["arb"][g], expand(u_c[g])) for g in range(ng)]
        for g in range(ng):
            upd = _dot_tn(jnp.concatenate([u_c[g], v[:, sls[g]]], axis=0).astype(BF16), p["upd_rhs"][g])
            state[g] = state[g] * p["p_tot"][:, sls[g]] + jnp.where(bd, upd, 0.0)
        yield
        sums = seg_sums(o_c + [p["rkk"][:, sl] for sl in sls])
        dev = [o_c[g] - sums[g] * (1.0 / hd) for g in range(ng)]
        yield
        var = seg_sums([dv * dv for dv in dev])
        o_parts = []
        for g in range(ng):
            o_n = dev[g] * lax.rsqrt(var[g] * (1.0 / hd) + GN_EPS) * gng_ref[:, sls[g]] + gnb_ref[:, sls[g]]
            o_parts.append(o_n + sums[ng + g] * v[:, sls[g]])
        o = jnp.concatenate(o_parts, axis=1)
        out_ref[rs_, :] = zgb_ref[rs_, :] * (o * p["gate"])

    pairs = [list(range(i, min(i + 2, nch))) for i in range(0, nch, 2)]
    _interleave(*[phase_a(ci) for ci in pairs[0]])
    for k in range(len(pairs)):
        b_gen = _chain(*[phase_b(ci) for ci in pairs[k]])
        a_gens = [phase_a(ci) for ci in pairs[k + 1]] if k + 1 < len(pairs) else []
        _interleave(b_gen, *a_gens)

    for g in range(ng):
        s_ref[g] = state[g]


def _rwkv_rec(zin, w0, wb, a0, ab, gb, k_k, k_a, rk, gng, gnb):
    t = zin.shape[0]
    d = D_MODEL
    dl = IN_TN
    br = min(REC_ROWS, t)
    zcol = lambda c: pl.BlockSpec((br, d), lambda i: (i, c))
    row = pl.BlockSpec((1, d), lambda i: (0, 0))
    return pl.pallas_call(
        _rwkv_rec_kernel,
        grid=(t // br,),
        in_specs=[zcol(4), zcol(5), zcol(6), pl.BlockSpec((br, dl), lambda i: (i, 7 * d // dl)), zcol(3),
                  row, _resident((D_LORA_PAD, d)), row, _resident((D_LORA_PAD, d)),
                  _resident((D_GATE_LORA, d)), row, row, row, row, row],
        out_specs=pl.BlockSpec((br, d), lambda i: (i, 0)),
        out_shape=jax.ShapeDtypeStruct((t, d), F32),
        scratch_shapes=[pltpu.VMEM((N_GROUPS, GROUP, GROUP), F32)],
        compiler_params=_params(("arbitrary",), VMEM_LIMIT_BIG_TILES),
        name="rwkv_rec",
    )(zin, zin, zin, zin, zin, w0, wb, a0, ab, gb, k_k, k_a, rk, gng, gnb)


def _out_proj_kernel(ya_ref, yb_ref, x_ref, gin_ref, bin_ref, w_ref, g_ref, b_ref, o_ref):
    y = (ya_ref[...] + yb_ref[...]).astype(BF16)
    mix = _dot(y, w_ref[...])
    h = _layer_norm(x_ref[...], gin_ref[...], bin_ref[...])
    o_ref[...] = _layer_norm(ALPHA * h + mix, g_ref[...], b_ref[...])


def _out_proj(ya, yb, x, g_in, b_in, w, g, b, tm=512):
    t, d = x.shape
    blk = pl.BlockSpec((tm, d), lambda i: (i, 0))
    row = pl.BlockSpec((1, d), lambda i: (0, 0))
    return pl.pallas_call(
        _out_proj_kernel,
        grid=(t // tm,),
        in_specs=[blk, blk, blk, row, row, _resident((d, d)), row, row],
        out_specs=blk,
        out_shape=jax.ShapeDtypeStruct((t, d), F32),
        compiler_params=_params(("arbitrary",)),
        name="out_proj",
    )(ya, yb, x, g_in, b_in, w, g, b)


def _mem_kv_kernel(mem_ref, wk_ref, wv_ref, k_ref, v_ref):
    m = mem_ref[...].astype(BF16)
    k_ref[...] = _dot(m, wk_ref[...]).astype(BF16)
    v_ref[...] = _dot(m, wv_ref[...]).astype(BF16)


def _mem_kv(mem, wk, wv, tn=512):
    n, d = mem.shape
    return pl.pallas_call(
        _mem_kv_kernel,
        grid=(d // tn,),
        in_specs=[pl.BlockSpec((n, d), lambda j: (0, 0)),
                  pl.BlockSpec((d, tn), lambda j: (0, j)),
                  pl.BlockSpec((d, tn), lambda j: (0, j))],
        out_specs=[pl.BlockSpec((n, tn), lambda j: (0, j))] * 2,
        out_shape=[jax.ShapeDtypeStruct((n, d), BF16)] * 2,
        compiler_params=_params(("arbitrary",)),
        name="mem_kv",
    )(mem, wk, wv)


def _xattn_kernel(h_ref, wq_ref, k_ref, v_ref, wo_ref, g_ref, b_ref, o_ref):
    h = h_ref[...]
    q = _dot(h.astype(BF16), wq_ref[...]).astype(BF16)
    scale = XATTN_HEAD_DIM ** -0.5
    outs = []
    for hh in range(XATTN_HEADS):
        sl = slice(hh * XATTN_HEAD_DIM, (hh + 1) * XATTN_HEAD_DIM)
        s = _dot_nt(q[:, sl], k_ref[:, sl]) * scale
        e = jnp.exp(s - jnp.max(s, axis=-1, keepdims=True))
        p = e / jnp.sum(e, axis=-1, keepdims=True)
        outs.append(_dot(p.astype(BF16), v_ref[:, sl]))
    o = jnp.concatenate(outs, axis=1).astype(BF16)
    xa = _dot(o, wo_ref[...])
    o_ref[...] = _layer_norm(ALPHA * h + xa, g_ref[...], b_ref[...])


def _xattn(h, wq, kmem, vmem, wo, g, b, tm=512):
    t, d = h.shape
    n = kmem.shape[0]
    blk = pl.BlockSpec((tm, d), lambda i: (i, 0))
    row = pl.BlockSpec((1, d), lambda i: (0, 0))
    return pl.pallas_call(
        _xattn_kernel,
        grid=(t // tm,),
        in_specs=[blk, _resident((d, d)), _resident((n, d)), _resident((n, d)), _resident((d, d)),
                  row, row],
        out_specs=blk,
        out_shape=jax.ShapeDtypeStruct((t, d), F32),
        compiler_params=_params(("arbitrary",)),
        name="xattn",
    )(h, wq, kmem, vmem, wo, g, b)


def _ffn_kernel(h_ref, wg_ref, wu_ref, wd_ref, g_ref, b_ref, o_ref):
    j = pl.program_id(1)

    @pl.when(j == 0)
    def _():
        o_ref[...] = jnp.zeros_like(o_ref)

    hb = h_ref[...].astype(BF16)
    gate = _dot(hb, wg_ref[...].astype(BF16))
    up = _dot(hb, wu_ref[...].astype(BF16))
    act = (gate * _sigmoid(gate) * up).astype(BF16)
    o_ref[...] += _dot(act, wd_ref[...].astype(BF16))

    @pl.when(j == pl.num_programs(1) - 1)
    def _():
        o_ref[...] = _layer_norm(ALPHA * h_ref[...] + o_ref[...], g_ref[...], b_ref[...])


def _ffn(h, wg, wu, wd, g, b, tm=1024, tf=256):
    t, d = h.shape
    f = wg.shape[1]
    tm = min(tm, t)
    blk = pl.BlockSpec((tm, d), lambda i, j: (i, 0))
    row = pl.BlockSpec((1, d), lambda i, j: (0, 0))
    return pl.pallas_call(
        _ffn_kernel,
        grid=(t // tm, f // tf),
        in_specs=[blk,
                  pl.BlockSpec((d, tf), lambda i, j: (0, j)),
                  pl.BlockSpec((d, tf), lambda i, j: (0, j)),
                  pl.BlockSpec((tf, d), lambda i, j: (j, 0)),
                  row, row],
        out_specs=blk,
        out_shape=jax.ShapeDtypeStruct((t, d), F32),
        compiler_params=_params(("arbitrary", "arbitrary"), VMEM_LIMIT_BIG_TILES),
        name="ffn",
    )(h, wg, wu, wd, g, b)


def _pad_cols(w, n):
    return jnp.pad(w, ((0, 0), (0, n - w.shape[1])))


def _pad_rows(w, n):
    return jnp.pad(w, ((0, n - w.shape[0]), (0, 0)))


def kernel(x, mem, ln_in_g, ln_in_b, w_in, conv_w, conv_b, lru_wa, lru_ba, lru_wx, lru_bx, lru_lambda, rw_mu, rw_w0, rw_wB, rw_a0, rw_aB, rw_gB, rw_kk, rw_ka, rw_rk, rw_gn_g, rw_gn_b, w_out, ln1_g, ln1_b, xa_wq, xa_wk, xa_wv, xa_wo, ln2_g, ln2_b, ffn_wg, ffn_wu, ffn_wd, ln3_g, ln3_b):
    bsz, t, d = x.shape
    depth = w_in.shape[0]
    assert bsz == 1 and d == D_MODEL and t % 512 == 0
    row = lambda p: p.reshape(1, -1)

    h = None
    for l in range(depth):
        w_t = jnp.swapaxes(w_in[l], 0, 1)
        n_cols = IN_BLOCKS * IN_TN
        mu = row(rw_mu[l])
        mu_steps = jnp.concatenate(
            [jnp.zeros((1, 2 * d), F32), mu[:, :3 * d], jnp.zeros((1, 2 * d), F32), mu[:, 3 * d:]], axis=1)
        col_params = jnp.concatenate([
            _pad_cols(mu_steps, n_cols),
            _pad_cols(conv_w[l], n_cols),
            _pad_cols(row(conv_b[l]), n_cols),
            jnp.zeros((CP_ROWS - CP_CONV_B - 1, n_cols), F32)], axis=0)
        if l == 0:
            zin = _in_proj(x[0], row(ln_in_g), row(ln_in_b), w_t, col_params)
        else:
            raise NotImplementedError("DEPTH > 1 is not part of this problem")

        ya = _lru(zin, lru_wa[l].astype(BF16), row(lru_ba[l]),
                  lru_wx[l].astype(BF16), row(lru_bx[l]), row(lru_lambda[l]))

        yb = _rwkv_rec(
            zin, row(rw_w0[l]),
            _pad_rows(rw_wB[l], D_LORA_PAD).astype(BF16), row(rw_a0[l]),
            _pad_rows(rw_aB[l], D_LORA_PAD).astype(BF16), rw_gB[l].astype(BF16),
            row(rw_kk[l]), row(rw_ka[l]), row(rw_rk[l]), row(rw_gn_g[l]), row(rw_gn_b[l]))

        h = _out_proj(ya, yb, x[0], row(ln_in_g), row(ln_in_b), w_out[l].astype(BF16),
                      row(ln1_g[l]), row(ln1_b[l]))
        kmem, vmem = _mem_kv(mem[0], xa_wk[l].astype(BF16), xa_wv[l].astype(BF16))
        h = _xattn(h, xa_wq[l].astype(BF16), kmem, vmem, xa_wo[l].astype(BF16),
                   row(ln2_g[l]), row(ln2_b[l]))
        h = _ffn(h, ffn_wg[l], ffn_wu[l], ffn_wd[l], row(ln3_g[l]), row(ln3_b[l]))
    return h[None]
```

```python
import jax
import jax.numpy as jnp
from jax import lax
from jax.experimental import pallas as pl
from jax.experimental.pallas import tpu as pltpu

F32 = jnp.float32
BF16 = jnp.bfloat16

D_MODEL = 2048
LN_EPS = 1e-5
ALPHA = 2.0 ** 0.25

LRU_HEADS = 16
LRU_HEAD_DIM = 128
CONV_WIDTH = 4
LRU_C = 8.0

RWKV_HEAD_DIM = 64
D_LORA = 96
D_LORA_PAD = 128
D_GATE_LORA = 256
ZL_WIDTH = 512
GN_EPS = 64e-5
CHUNK = 64
GROUP = 256
N_GROUPS = D_MODEL // GROUP

XATTN_HEADS = 4
XATTN_HEAD_DIM = 512
N_MEM = 256
D_FF = 5632

VMEM_LIMIT = 56 * 1024 * 1024
VMEM_LIMIT_BIG_TILES = 60 * 1024 * 1024


def _dot(a, b):
    return jnp.dot(a, b, preferred_element_type=F32)


def _dot_nt(a, b):
    return lax.dot_general(a, b, (((1,), (1,)), ((), ())), preferred_element_type=F32)


def _dot_tn(a, b):
    return lax.dot_general(a, b, (((0,), (0,)), ((), ())), preferred_element_type=F32)


def _split2(x):
    hi = x.astype(BF16)
    lo = (x - hi.astype(F32)).astype(BF16)
    return hi, lo


def _split3(x):
    hi = x.astype(BF16)
    r1 = x - hi.astype(F32)
    mid = r1.astype(BF16)
    lo = (r1 - mid.astype(F32)).astype(BF16)
    return hi, mid, lo


def _sigmoid(x):
    return 1.0 / (1.0 + jnp.exp(-x))


def _softplus(x):
    return jnp.maximum(x, 0.0) + jnp.log1p(jnp.exp(-jnp.abs(x)))


def _gelu_tanh(x):
    c = 0.7978845608028654
    return 0.5 * x * (1.0 + jnp.tanh(c * (x + 0.044715 * (x * x * x))))


def _layer_norm(x, g, b):
    mu = jnp.mean(x, axis=-1, keepdims=True)
    xc = x - mu
    var = jnp.mean(xc * xc, axis=-1, keepdims=True)
    return xc * lax.rsqrt(var + LN_EPS) * g + b


def _shift_rows(z, prev8, s):
    rolled = pltpu.roll(z, s, 0)
    row8 = lax.broadcasted_iota(jnp.int32, prev8.shape, 0)
    head = jnp.where(row8 < s, pltpu.roll(prev8, s, 0), rolled[:8])
    return jnp.concatenate([head, rolled[8:]], axis=0)


def _params(sem, vmem_limit=VMEM_LIMIT):
    return pltpu.CompilerParams(dimension_semantics=sem, vmem_limit_bytes=vmem_limit)


def _resident(shape):
    return pl.BlockSpec(shape, lambda *_: (0,) * len(shape), pipeline_mode=pl.Buffered(1))


IN_TN = 1024
IN_D_BLOCKS = D_MODEL // IN_TN
IN_MAIN_BLOCKS = 5 * IN_D_BLOCKS
IN_GATE_BLOCKS = 2 * IN_D_BLOCKS
IN_BLOCKS = IN_MAIN_BLOCKS + IN_GATE_BLOCKS + 1


def _in_weight_row(j, n_out):
    gates = n_out - 2 * D_MODEL + (j - IN_MAIN_BLOCKS) * IN_TN
    return jnp.where(j < IN_MAIN_BLOCKS, j * IN_TN,
                     jnp.where(j < IN_MAIN_BLOCKS + IN_GATE_BLOCKS, gates, 5 * D_MODEL))


def _in_out_block(j):
    b2 = 2 * D_MODEL // IN_TN
    return jnp.where(j < b2, j,
                     jnp.where(j < IN_MAIN_BLOCKS, j + b2,
                               jnp.where(j < IN_MAIN_BLOCKS + b2, j - (IN_MAIN_BLOCKS - b2), j)))


CP_MU, CP_CONV_W, CP_CONV_B, CP_ROWS = 0, 1, 1 + CONV_WIDTH, 8


def _in_proj_kernel(x_ref, g_ref, b_ref, w_ref, cp_ref, z_ref, hb_ref, carry_ref):
    i = pl.program_id(0)
    j = pl.program_id(1)
    nd = IN_D_BLOCKS

    @pl.when(j == 0)
    def _():
        hb_ref[...] = _layer_norm(x_ref[...], g_ref[...], b_ref[...]).astype(BF16)

    @pl.when((i == 0) & (j == 0))
    def _():
        carry_ref[...] = jnp.zeros_like(carry_ref)

    def keep_tail(z):
        carry_ref[j] = z[z.shape[0] - 8:, :]

    @pl.when(j < nd)
    def _():
        z = _dot_nt(hb_ref[...], w_ref[...].astype(BF16))
        prev8 = carry_ref[j]
        cw = CP_CONV_W + CONV_WIDTH - 1
        conv = cp_ref[CP_CONV_B:CP_CONV_B + 1, :] + z * cp_ref[cw:cw + 1, :]
        for s in range(1, CONV_WIDTH):
            conv = conv + _shift_rows(z, prev8, s) * cp_ref[cw - s:cw - s + 1, :]
        keep_tail(z)
        z_ref[...] = conv

    @pl.when((j >= nd) & (j < 2 * nd))
    def _():
        z_ref[...] = _gelu_tanh(_dot_nt(hb_ref[...], w_ref[...].astype(BF16)))

    def token_shift(z):
        zp = _shift_rows(z, carry_ref[j], 1)
        keep_tail(z)
        return z + (zp - z) * cp_ref[CP_MU:CP_MU + 1, :]

    @pl.when((j >= 2 * nd) & (j < IN_MAIN_BLOCKS))
    def _():
        z_ref[...] = token_shift(_dot_nt(hb_ref[...], w_ref[...].astype(BF16)))

    @pl.when((j >= IN_MAIN_BLOCKS) & (j < IN_MAIN_BLOCKS + IN_GATE_BLOCKS))
    def _():
        z_ref[...] = _sigmoid(_dot_nt(hb_ref[...], w_ref[...].astype(BF16)))

    @pl.when(j >= IN_MAIN_BLOCKS + IN_GATE_BLOCKS)
    def _():
        z_ref[...] = token_shift(_dot_nt(hb_ref[...], w_ref[...].astype(BF16)))


def _in_proj(x, g, b, w_t, col_params, tm=1024):
    t, d = x.shape
    tn = IN_TN
    nb = IN_BLOCKS
    n_out = w_t.shape[0]
    tm = min(tm, t)
    return pl.pallas_call(
        _in_proj_kernel,
        grid=(t // tm, nb),
        in_specs=[
            pl.BlockSpec((tm, d), lambda i, j: (i, 0)),
            pl.BlockSpec((1, d), lambda i, j: (0, 0)),
            pl.BlockSpec((1, d), lambda i, j: (0, 0)),
            pl.BlockSpec((pl.Element(tn), pl.Element(d)),
                         lambda i, j: (pl.multiple_of(_in_weight_row(j, n_out), 8), 0)),
            pl.BlockSpec((CP_ROWS, tn), lambda i, j: (0, j)),
        ],
        out_specs=pl.BlockSpec((tm, tn), lambda i, j: (i, _in_out_block(j))),
        out_shape=jax.ShapeDtypeStruct((t, nb * tn), F32),
        scratch_shapes=[pltpu.VMEM((tm, d), BF16), pltpu.VMEM((nb, 8, tn), F32)],
        compiler_params=_params(("arbitrary", "arbitrary")),
        name="in_proj",
    )(x, g, b, w_t, col_params)


def _lru_kernel(u_ref, gate_ref, ga_ref, wa_ref, ba_ref, wx_ref, bx_ref,
                lam_ref, out_ref, hcarry):
    @pl.when(pl.program_id(0) == 0)
    def _():
        hcarry[...] = jnp.zeros_like(hcarry)

    conv = u_ref[...]
    tm = conv.shape[0]
    cb16 = conv.astype(BF16)
    r_parts, i_parts = [], []
    for g in range(LRU_HEADS):
        ug = cb16[:, g * LRU_HEAD_DIM:(g + 1) * LRU_HEAD_DIM]
        r_parts.append(_dot(ug, wa_ref[g]))
        i_parts.append(_dot(ug, wx_ref[g]))
    r = _sigmoid(jnp.concatenate(r_parts, axis=1) + ba_ref[...])
    ig = _sigmoid(jnp.concatenate(i_parts, axis=1) + bx_ref[...])

    log_a = (-LRU_C) * r * _softplus(-lam_ref[...])
    a_c = jnp.exp(log_a)
    b_c = jnp.sqrt(-jnp.tanh(log_a) * (a_c * a_c + 1.0)) * (ig * conv)

    n8 = tm // 8
    a3 = a_c.reshape(n8, 8, a_c.shape[1])
    b3 = b_c.reshape(n8, 8, b_c.shape[1])
    sub = lax.broadcasted_iota(jnp.int32, a3.shape, 1)
    for d in (1, 2, 4):
        m = sub >= d
        a_sh = jnp.where(m, pltpu.roll(a3, d, 1), 1.0)
        b_sh = jnp.where(m, pltpu.roll(b3, d, 1), 0.0)
        b3 = a3 * b_sh + b3
        a3 = a3 * a_sh
    carry = hcarry[...]
    hs = []
    for i in range(n8):
        h_i = b3[i] + a3[i] * carry
        hs.append(h_i)
        carry = h_i[7:8, :]
    hcarry[...] = carry
    h = jnp.concatenate(hs, axis=0)

    out_ref[...] = ga_ref[...] * (gate_ref[...] * h)


def _lru(zin, wa, ba, wx, bx, lam, tm=256):
    t = zin.shape[0]
    d = D_MODEL
    row = lambda i: (0, 0)
    return pl.pallas_call(
        _lru_kernel,
        grid=(t // tm,),
        in_specs=[
            pl.BlockSpec((tm, d), lambda i: (i, 0)),
            pl.BlockSpec((tm, d), lambda i: (i, 1)),
            pl.BlockSpec((tm, d), lambda i: (i, 2)),
            pl.BlockSpec((LRU_HEADS, LRU_HEAD_DIM, LRU_HEAD_DIM), lambda i: (0, 0, 0)),
            pl.BlockSpec((1, d), row),
            pl.BlockSpec((LRU_HEADS, LRU_HEAD_DIM, LRU_HEAD_DIM), lambda i: (0, 0, 0)),
            pl.BlockSpec((1, d), row),
            pl.BlockSpec((1, d), row),
        ],
        out_specs=pl.BlockSpec((tm, d), lambda i: (i, 0)),
        out_shape=jax.ShapeDtypeStruct((t, d), F32),
        scratch_shapes=[pltpu.VMEM((1, d), F32)],
        compiler_params=_params(("arbitrary",)),
        name="lru",
    )(zin, zin, zin, wa, ba, wx, bx, lam)


def _block_ones(n, seg):
    r = lax.broadcasted_iota(jnp.int32, (n, n), 0) // seg
    c = lax.broadcasted_iota(jnp.int32, (n, n), 1) // seg
    return jnp.where(r == c, 1.0, 0.0).astype(BF16)


REC_ROWS = 256


def _interleave(*gens):
    live = list(gens)
    while live:
        for gen in list(live):
            try:
                next(gen)
            except StopIteration:
                live.remove(gen)


def _chain(*gens):
    for gen in gens:
        yield from gen


def _rwkv_rec_kernel(r_ref, k_ref, v_ref, zl_ref, zgb_ref, w0_ref, wb_ref, a0_ref, ab_ref, gb_ref,
                     kk_ref, ka_ref, rk_ref, gng_ref, gnb_ref, out_ref, s_ref):
    @pl.when(pl.program_id(0) == 0)
    def _():
        s_ref[...] = jnp.zeros_like(s_ref)

    c = CHUNK
    hd = RWKV_HEAD_DIM
    ng = N_GROUPS
    nch = r_ref.shape[0] // c
    sls = [slice(g * GROUP, (g + 1) * GROUP) for g in range(ng)]

    rows = lax.broadcasted_iota(jnp.int32, (c, c), 0)
    cols = lax.broadcasted_iota(jnp.int32, (c, c), 1)
    tri = jnp.where(cols <= rows, 1.0, 0.0).astype(BF16)
    t_c = lax.broadcasted_iota(jnp.int32, (c, GROUP), 0)
    s_c = lax.broadcasted_iota(jnp.int32, (c, GROUP), 1) % hd
    strict = s_c < t_c
    incl = s_c <= t_c
    eye_c = jnp.where(s_c == t_c, 1.0, 0.0)
    bd = (lax.broadcasted_iota(jnp.int32, (GROUP, GROUP), 0) // hd
          == lax.broadcasted_iota(jnp.int32, (GROUP, GROUP), 1) // hd)
    ones_bd = _block_ones(GROUP, hd)

    def expand(x_c):
        return jnp.where(bd, jnp.concatenate([x_c] * (GROUP // hd), axis=0), 0.0).astype(BF16)

    def seg_sums(xs):
        hi, lo = _split2(jnp.concatenate(xs, axis=0))
        s = _dot(jnp.concatenate([hi, lo], axis=0), ones_bd)
        n = c * len(xs)
        s = s[:n] + s[n:]
        return [s[i * c:(i + 1) * c] for i in range(len(xs))]

    state = [s_ref[g] for g in range(ng)]
    prep = [None] * nch

    def phase_a(ci):
        rs_ = slice(ci * c, (ci + 1) * c)
        zl = zl_ref[rs_, :]
        nl = zl.shape[1]
        w_lo = zl[:, 0:D_LORA_PAD]
        a_lo = pltpu.roll(zl, nl - D_LORA, 1)[:, 0:D_LORA_PAD]
        g_lo = pltpu.roll(zl, nl - 2 * D_LORA, 1)[:, 0:D_GATE_LORA]
        w_log = -_softplus(-(w0_ref[...] + _dot(jnp.tanh(w_lo).astype(BF16), wb_ref[...]))) - 0.5
        lw = -jnp.exp(w_log)
        a = _sigmoid(a0_ref[...] + _dot(a_lo.astype(BF16), ab_ref[...]))
        gate = _dot(_sigmoid(g_lo).astype(BF16), gb_ref[...])
        hi, mid, lo = _split3(lw)
        cum = _dot(tri, hi) + _dot(tri, mid) + _dot(tri, lo)
        tot = cum[c - 1:c, :]
        yield
        k = k_ref[rs_, :]
        kk = k * kk_ref[...]
        n2 = jnp.concatenate(seg_sums([(kk * kk)[:, sl] for sl in sls]), axis=1)
        kn = kk / jnp.maximum(jnp.sqrt(n2), 1e-12)
        bb = kn * a
        kf = k * (1.0 + (a - 1.0) * ka_ref[...])
        yield
        r = r_ref[rs_, :]
        v = v_ref[rs_, :]
        p_inv = jnp.exp(-cum)
        p_end = jnp.exp(tot - cum)
        rq = r * jnp.exp(cum)
        kap = kn * jnp.exp(cum - lw)
        bet = bb * p_inv
        kt = kf * p_inv
        lhs2 = [jnp.concatenate([kap[:, sl], rq[:, sl]], axis=0).astype(BF16) for sl in sls]
        amat = [_dot_nt(lhs2[g], jnp.concatenate([expand(bet[:, sls[g]]), expand(kt[:, sls[g]])], axis=0))
                for g in range(ng)]
        yield
        l_c = [jnp.where(strict, a[:c, :GROUP], 0.0) for a in amat]
        a_lo = [jnp.concatenate([jnp.where(strict, a[:c, GROUP:], 0.0),
                                 jnp.where(incl, a[c:, GROUP:], 0.0)], axis=0).astype(BF16) for a in amat]
        arb = [jnp.where(incl, a[c:, :GROUP], 0.0).astype(BF16) for a in amat]
        av = [_dot(a_lo[g], expand(v[:, sls[g]])) for g in range(ng)]
        x_c = [eye_c - jnp.where((t_c >> 1) == (s_c >> 1), l, 0.0) for l in l_c]
        for lvl in range(2, 7):
            lmask = ((t_c >> lvl) == (s_c >> lvl)) & ((t_c >> (lvl - 1)) != (s_c >> (lvl - 1)))
            y_c = [_dot(x_c[g].astype(BF16), expand(jnp.where(lmask, l_c[g], 0.0))) for g in range(ng)]
            yield
            x_c = [x_c[g] - _dot(y_c[g].astype(BF16), expand(x_c[g])) for g in range(ng)]
            yield
        upd_rhs = [jnp.concatenate([-(bb * p_end)[:, sl], (kf * p_end)[:, sl]], axis=0).astype(BF16)
                   for sl in sls]
        prep[ci] = dict(lhs2=lhs2, av=av, arb=arb, x=[x.astype(BF16) for x in x_c], v=v,
                        upd_rhs=upd_rhs, p_tot=jnp.exp(tot), rkk=r * kf * rk_ref[...], gate=gate)

    def phase_b(ci):
        p = prep[ci]
        rs_ = slice(ci * c, (ci + 1) * c)
        v = p["v"]
        rs = [_dot_nt(p["lhs2"][g], state[g].astype(BF16)) for g in range(ng)]
        yield
        u_c = [_dot(p["x"][g], expand(rs[g][:c] + p["av"][g][:c])) for g in range(ng)]
        yield
        o_c = [rs[g][c:] + p["av"][g][c:] - _dot(p["arb"][g], expand(u_c[g])) for g in range(ng)]
        for g in range(ng):
            upd = _dot_tn(jnp.concatenate([u_c[g], v[:, sls[g]]], axis=0).astype(BF16), p["upd_rhs"][g])
            state[g] = state[g] * p["p_tot"][:, sls[g]] + jnp.where(bd, upd, 0.0)
        yield
        sums = seg_sums(o_c + [p["rkk"][:, sl] for sl in sls])
        dev = [o_c[g] - sums[g] * (1.0 / hd) for g in range(ng)]
        yield
        var = seg_sums([dv * dv for dv in dev])
        o_parts = []
        for g in range(ng):
            o_n = dev[g] * lax.rsqrt(var[g] * (1.0 / hd) + GN_EPS) * gng_ref[:, sls[g]] + gnb_ref[:, sls[g]]
            o_parts.append(o_n + sums[ng + g] * v[:, sls[g]])
        o = jnp.concatenate(o_parts, axis=1)
        out_ref[rs_, :] = zgb_ref[rs_, :] * (o * p["gate"])

    pairs = [list(range(i, min(i + 2, nch))) for i in range(0, nch, 2)]
    _interleave(*[phase_a(ci) for ci in pairs[0]])
    for k in range(len(pairs)):
        b_gen = _chain(*[phase_b(ci) for ci in pairs[k]])
        a_gens = [phase_a(ci) for ci in pairs[k + 1]] if k + 1 < len(pairs) else []
        _interleave(b_gen, *a_gens)

    for g in range(ng):
        s_ref[g] = state[g]


def _rwkv_rec(zin, w0, wb, a0, ab, gb, k_k, k_a, rk, gng, gnb):
    t = zin.shape[0]
    d = D_MODEL
    dl = ZL_WIDTH
    br = min(REC_ROWS, t)
    zcol = lambda c: pl.BlockSpec((br, d), lambda i: (i, c))
    row = pl.BlockSpec((1, d), lambda i: (0, 0))
    return pl.pallas_call(
        _rwkv_rec_kernel,
        grid=(t // br,),
        in_specs=[zcol(4), zcol(5), zcol(6), pl.BlockSpec((br, dl), lambda i: (i, 7 * d // dl)), zcol(3),
                  row, _resident((D_LORA_PAD, d)), row, _resident((D_LORA_PAD, d)),
                  _resident((D_GATE_LORA, d)), row, row, row, row, row],
        out_specs=pl.BlockSpec((br, d), lambda i: (i, 0)),
        out_shape=jax.ShapeDtypeStruct((t, d), F32),
        scratch_shapes=[pltpu.VMEM((N_GROUPS, GROUP, GROUP), F32)],
        compiler_params=_params(("arbitrary",), VMEM_LIMIT_BIG_TILES),
        name="rwkv_rec",
    )(zin, zin, zin, zin, zin, w0, wb, a0, ab, gb, k_k, k_a, rk, gng, gnb)


def _out_proj_kernel(ya_ref, yb_ref, x_ref, gin_ref, bin_ref, w_ref, g_ref, b_ref, o_ref):
    y = (ya_ref[...] + yb_ref[...]).astype(BF16)
    mix = _dot(y, w_ref[...])
    h = _layer_norm(x_ref[...], gin_ref[...], bin_ref[...])
    o_ref[...] = _layer_norm(ALPHA * h + mix, g_ref[...], b_ref[...])


def _out_proj(ya, yb, x, g_in, b_in, w, g, b, tm=512):
    t, d = x.shape
    blk = pl.BlockSpec((tm, d), lambda i: (i, 0))
    row = pl.BlockSpec((1, d), lambda i: (0, 0))
    return pl.pallas_call(
        _out_proj_kernel,
        grid=(t // tm,),
        in_specs=[blk, blk, blk, row, row, _resident((d, d)), row, row],
        out_specs=blk,
        out_shape=jax.ShapeDtypeStruct((t, d), F32),
        compiler_params=_params(("arbitrary",)),
        name="out_proj",
    )(ya, yb, x, g_in, b_in, w, g, b)


def _mem_kv_kernel(mem_ref, wk_ref, wv_ref, k_ref, v_ref):
    m = mem_ref[...].astype(BF16)
    k_ref[...] = _dot(m, wk_ref[...]).astype(BF16)
    v_ref[...] = _dot(m, wv_ref[...]).astype(BF16)


def _mem_kv(mem, wk, wv, tn=512):
    n, d = mem.shape
    return pl.pallas_call(
        _mem_kv_kernel,
        grid=(d // tn,),
        in_specs=[pl.BlockSpec((n, d), lambda j: (0, 0)),
                  pl.BlockSpec((d, tn), lambda j: (0, j)),
                  pl.BlockSpec((d, tn), lambda j: (0, j))],
        out_specs=[pl.BlockSpec((n, tn), lambda j: (0, j))] * 2,
        out_shape=[jax.ShapeDtypeStruct((n, d), BF16)] * 2,
        compiler_params=_params(("arbitrary",)),
        name="mem_kv",
    )(mem, wk, wv)


def _xattn_kernel(h_ref, wq_ref, k_ref, v_ref, wo_ref, g_ref, b_ref, o_ref):
    h = h_ref[...]
    q = _dot(h.astype(BF16), wq_ref[...]).astype(BF16)
    scale = XATTN_HEAD_DIM ** -0.5
    outs = []
    for hh in range(XATTN_HEADS):
        sl = slice(hh * XATTN_HEAD_DIM, (hh + 1) * XATTN_HEAD_DIM)
        s = _dot_nt(q[:, sl], k_ref[:, sl]) * scale
        e = jnp.exp(s - jnp.max(s, axis=-1, keepdims=True))
        p = e / jnp.sum(e, axis=-1, keepdims=True)
        outs.append(_dot(p.astype(BF16), v_ref[:, sl]))
    o = jnp.concatenate(outs, axis=1).astype(BF16)
    xa = _dot(o, wo_ref[...])
    o_ref[...] = _layer_norm(ALPHA * h + xa, g_ref[...], b_ref[...])


def _xattn(h, wq, kmem, vmem, wo, g, b, tm=512):
    t, d = h.shape
    n = kmem.shape[0]
    blk = pl.BlockSpec((tm, d), lambda i: (i, 0))
    row = pl.BlockSpec((1, d), lambda i: (0, 0))
    return pl.pallas_call(
        _xattn_kernel,
        grid=(t // tm,),
        in_specs=[blk, _resident((d, d)), _resident((n, d)), _resident((n, d)), _resident((d, d)),
                  row, row],
        out_specs=blk,
        out_shape=jax.ShapeDtypeStruct((t, d), F32),
        compiler_params=_params(("arbitrary",)),
        name="xattn",
    )(h, wq, kmem, vmem, wo, g, b)


def _ffn_kernel(h_ref, wg_ref, wu_ref, wd_ref, g_ref, b_ref, o_ref):
    j = pl.program_id(1)

    @pl.when(j == 0)
    def _():
        o_ref[...] = jnp.zeros_like(o_ref)

    hb = h_ref[...].astype(BF16)
    gate = _dot(hb, wg_ref[...].astype(BF16))
    up = _dot(hb, wu_ref[...].astype(BF16))
    act = (gate * _sigmoid(gate) * up).astype(BF16)
    o_ref[...] += _dot(act, wd_ref[...].astype(BF16))

    @pl.when(j == pl.num_programs(1) - 1)
    def _():
        o_ref[...] = _layer_norm(ALPHA * h_ref[...] + o_ref[...], g_ref[...], b_ref[...])


def _ffn(h, wg, wu, wd, g, b, tm=1024, tf=256):
    t, d = h.shape
    f = wg.shape[1]
    tm = min(tm, t)
    blk = pl.BlockSpec((tm, d), lambda i, j: (i, 0))
    row = pl.BlockSpec((1, d), lambda i, j: (0, 0))
    return pl.pallas_call(
        _ffn_kernel,
        grid=(t // tm, f // tf),
        in_specs=[blk,
                  pl.BlockSpec((d, tf), lambda i, j: (0, j)),
                  pl.BlockSpec((d, tf), lambda i, j: (0, j)),
                  pl.BlockSpec((tf, d), lambda i, j: (j, 0)),
                  row, row],
        out_specs=blk,
        out_shape=jax.ShapeDtypeStruct((t, d), F32),
        compiler_params=_params(("arbitrary", "arbitrary"), VMEM_LIMIT_BIG_TILES),
        name="ffn",
    )(h, wg, wu, wd, g, b)


def _pad_cols(w, n):
    return jnp.pad(w, ((0, 0), (0, n - w.shape[1])))


def _pad_rows(w, n):
    return jnp.pad(w, ((0, n - w.shape[0]), (0, 0)))


def kernel(x, mem, ln_in_g, ln_in_b, w_in, conv_w, conv_b, lru_wa, lru_ba, lru_wx, lru_bx, lru_lambda, rw_mu, rw_w0, rw_wB, rw_a0, rw_aB, rw_gB, rw_kk, rw_ka, rw_rk, rw_gn_g, rw_gn_b, w_out, ln1_g, ln1_b, xa_wq, xa_wk, xa_wv, xa_wo, ln2_g, ln2_b, ffn_wg, ffn_wu, ffn_wd, ln3_g, ln3_b):
    bsz, t, d = x.shape
    depth = w_in.shape[0]
    assert bsz == 1 and d == D_MODEL and t % 512 == 0
    row = lambda p: p.reshape(1, -1)

    h = None
    for l in range(depth):
        w_t = jnp.swapaxes(w_in[l], 0, 1)
        n_cols = IN_BLOCKS * IN_TN
        mu = row(rw_mu[l])
        mu_steps = jnp.concatenate(
            [jnp.zeros((1, 2 * d), F32), mu[:, :3 * d], jnp.zeros((1, 2 * d), F32), mu[:, 3 * d:]], axis=1)
        col_params = jnp.concatenate([
            _pad_cols(mu_steps, n_cols),
            _pad_cols(conv_w[l], n_cols),
            _pad_cols(row(conv_b[l]), n_cols),
            jnp.zeros((CP_ROWS - CP_CONV_B - 1, n_cols), F32)], axis=0)
        if l == 0:
            zin = _in_proj(x[0], row(ln_in_g), row(ln_in_b), w_t, col_params)
        else:
            raise NotImplementedError("DEPTH > 1 is not part of this problem")

        ya = _lru(zin, lru_wa[l].astype(BF16), row(lru_ba[l]),
                  lru_wx[l].astype(BF16), row(lru_bx[l]), row(lru_lambda[l]))

        yb = _rwkv_rec(
            zin, row(rw_w0[l]),
            _pad_rows(rw_wB[l], D_LORA_PAD).astype(BF16), row(rw_a0[l]),
            _pad_rows(rw_aB[l], D_LORA_PAD).astype(BF16), rw_gB[l].astype(BF16),
            row(rw_kk[l]), row(rw_ka[l]), row(rw_rk[l]), row(rw_gn_g[l]), row(rw_gn_b[l]))

        h = _out_proj(ya, yb, x[0], row(ln_in_g), row(ln_in_b), w_out[l].astype(BF16),
                      row(ln1_g[l]), row(ln1_b[l]))
        kmem, vmem = _mem_kv(mem[0], xa_wk[l].astype(BF16), xa_wv[l].astype(BF16))
        h = _xattn(h, xa_wq[l].astype(BF16), kmem, vmem, xa_wo[l].astype(BF16),
                   row(ln2_g[l]), row(ln2_b[l]))
        h = _ffn(h, ffn_wg[l], ffn_wu[l], ffn_wd[l], row(ln3_g[l]), row(ln3_b[l]))
    return h[None]
```

```python
import jax
import jax.numpy as jnp
from jax import lax
from jax.experimental import pallas as pl
from jax.experimental.pallas import tpu as pltpu

F32 = jnp.float32
BF16 = jnp.bfloat16

D_MODEL = 2048
LN_EPS = 1e-5
ALPHA = 2.0 ** 0.25

LRU_HEADS = 16
LRU_HEAD_DIM = 128
CONV_WIDTH = 4
LRU_C = 8.0

RWKV_HEAD_DIM = 64
D_LORA = 96
D_LORA_PAD = 128
D_GATE_LORA = 256
ZL_WIDTH = 512
GN_EPS = 64e-5
CHUNK = 64
GROUP = 256
N_GROUPS = D_MODEL // GROUP

XATTN_HEADS = 4
XATTN_HEAD_DIM = 512
N_MEM = 256
D_FF = 5632

VMEM_LIMIT = 56 * 1024 * 1024
VMEM_LIMIT_BIG_TILES = 60 * 1024 * 1024


def _dot(a, b):
    return jnp.dot(a, b, preferred_element_type=F32)


def _dot_nt(a, b):
    return lax.dot_general(a, b, (((1,), (1,)), ((), ())), preferred_element_type=F32)


def _dot_tn(a, b):
    return lax.dot_general(a, b, (((0,), (0,)), ((), ())), preferred_element_type=F32)


def _split2(x):
    hi = x.astype(BF16)
    lo = (x - hi.astype(F32)).astype(BF16)
    return hi, lo


def _split3(x):
    hi = x.astype(BF16)
    r1 = x - hi.astype(F32)
    mid = r1.astype(BF16)
    lo = (r1 - mid.astype(F32)).astype(BF16)
    return hi, mid, lo


def _sigmoid(x):
    return 1.0 / (1.0 + jnp.exp(-x))


def _softplus(x):
    return jnp.maximum(x, 0.0) + jnp.log1p(jnp.exp(-jnp.abs(x)))


def _softplus_log(x):
    return jnp.maximum(x, 0.0) + jnp.log(1.0 + jnp.exp(-jnp.abs(x)))


def _gelu_tanh(x):
    c = 0.7978845608028654
    return 0.5 * x * (1.0 + jnp.tanh(c * (x + 0.044715 * (x * x * x))))


def _layer_norm(x, g, b):
    mu = jnp.mean(x, axis=-1, keepdims=True)
    xc = x - mu
    var = jnp.mean(xc * xc, axis=-1, keepdims=True)
    return xc * lax.rsqrt(var + LN_EPS) * g + b


def _shift_rows(z, prev8, s):
    rolled = pltpu.roll(z, s, 0)
    row8 = lax.broadcasted_iota(jnp.int32, prev8.shape, 0)
    head = jnp.where(row8 < s, pltpu.roll(prev8, s, 0), rolled[:8])
    return jnp.concatenate([head, rolled[8:]], axis=0)


def _params(sem, vmem_limit=VMEM_LIMIT):
    return pltpu.CompilerParams(dimension_semantics=sem, vmem_limit_bytes=vmem_limit)


def _resident(shape):
    return pl.BlockSpec(shape, lambda *_: (0,) * len(shape), pipeline_mode=pl.Buffered(1))


IN_TN = 1024
IN_D_BLOCKS = D_MODEL // IN_TN
IN_MAIN_BLOCKS = 5 * IN_D_BLOCKS
IN_GATE_BLOCKS = 2 * IN_D_BLOCKS
IN_BLOCKS = IN_MAIN_BLOCKS + IN_GATE_BLOCKS + 1


def _in_weight_row(j, n_out):
    gates = n_out - 2 * D_MODEL + (j - IN_MAIN_BLOCKS) * IN_TN
    return jnp.where(j < IN_MAIN_BLOCKS, j * IN_TN,
                     jnp.where(j < IN_MAIN_BLOCKS + IN_GATE_BLOCKS, gates, 5 * D_MODEL))


def _in_out_block(j):
    b2 = 2 * D_MODEL // IN_TN
    return jnp.where(j < b2, j,
                     jnp.where(j < IN_MAIN_BLOCKS, j + b2,
                               jnp.where(j < IN_MAIN_BLOCKS + b2, j - (IN_MAIN_BLOCKS - b2), j)))


CP_MU, CP_CONV_W, CP_CONV_B, CP_ROWS = 0, 1, 1 + CONV_WIDTH, 8


def _in_proj_kernel(x_ref, g_ref, b_ref, w_ref, cp_ref, z_ref, hb_ref, carry_ref):
    i = pl.program_id(0)
    j = pl.program_id(1)
    nd = IN_D_BLOCKS

    @pl.when(j == 0)
    def _():
        hb_ref[...] = _layer_norm(x_ref[...], g_ref[...], b_ref[...]).astype(BF16)

    @pl.when((i == 0) & (j == 0))
    def _():
        carry_ref[...] = jnp.zeros_like(carry_ref)

    def keep_tail(z):
        carry_ref[j] = z[z.shape[0] - 8:, :]

    @pl.when(j < nd)
    def _():
        z = _dot_nt(hb_ref[...], w_ref[...].astype(BF16))
        prev8 = carry_ref[j]
        cw = CP_CONV_W + CONV_WIDTH - 1
        conv = cp_ref[CP_CONV_B:CP_CONV_B + 1, :] + z * cp_ref[cw:cw + 1, :]
        for s in range(1, CONV_WIDTH):
            conv = conv + _shift_rows(z, prev8, s) * cp_ref[cw - s:cw - s + 1, :]
        keep_tail(z)
        z_ref[...] = conv

    @pl.when((j >= nd) & (j < 2 * nd))
    def _():
        z_ref[...] = _gelu_tanh(_dot_nt(hb_ref[...], w_ref[...].astype(BF16)))

    def token_shift(z):
        zp = _shift_rows(z, carry_ref[j], 1)
        keep_tail(z)
        return z + (zp - z) * cp_ref[CP_MU:CP_MU + 1, :]

    @pl.when((j >= 2 * nd) & (j < IN_MAIN_BLOCKS))
    def _():
        z_ref[...] = token_shift(_dot_nt(hb_ref[...], w_ref[...].astype(BF16)))

    @pl.when((j >= IN_MAIN_BLOCKS) & (j < IN_MAIN_BLOCKS + IN_GATE_BLOCKS))
    def _():
        z_ref[...] = _sigmoid(_dot_nt(hb_ref[...], w_ref[...].astype(BF16)))

    @pl.when(j >= IN_MAIN_BLOCKS + IN_GATE_BLOCKS)
    def _():
        z_ref[...] = token_shift(_dot_nt(hb_ref[...], w_ref[...].astype(BF16)))


def _in_proj(x, g, b, w_t, col_params, tm=1024):
    t, d = x.shape
    tn = IN_TN
    nb = IN_BLOCKS
    n_out = w_t.shape[0]
    tm = min(tm, t)
    return pl.pallas_call(
        _in_proj_kernel,
        grid=(t // tm, nb),
        in_specs=[
            pl.BlockSpec((tm, d), lambda i, j: (i, 0)),
            pl.BlockSpec((1, d), lambda i, j: (0, 0)),
            pl.BlockSpec((1, d), lambda i, j: (0, 0)),
            pl.BlockSpec((pl.Element(tn), pl.Element(d)),
                         lambda i, j: (pl.multiple_of(_in_weight_row(j, n_out), 8), 0)),
            pl.BlockSpec((CP_ROWS, tn), lambda i, j: (0, j)),
        ],
        out_specs=pl.BlockSpec((tm, tn), lambda i, j: (i, _in_out_block(j))),
        out_shape=jax.ShapeDtypeStruct((t, nb * tn), F32),
        scratch_shapes=[pltpu.VMEM((tm, d), BF16), pltpu.VMEM((nb, 8, tn), F32)],
        compiler_params=_params(("arbitrary", "arbitrary")),
        name="in_proj",
    )(x, g, b, w_t, col_params)


def _lru_kernel(u_ref, gate_ref, ga_ref, wa_ref, ba_ref, wx_ref, bx_ref,
                lam_ref, out_ref, hcarry):
    @pl.when(pl.program_id(0) == 0)
    def _():
        hcarry[...] = jnp.zeros_like(hcarry)

    conv = u_ref[...]
    tm = conv.shape[0]
    cb16 = conv.astype(BF16)
    r_parts, i_parts = [], []
    for g in range(LRU_HEADS):
        ug = cb16[:, g * LRU_HEAD_DIM:(g + 1) * LRU_HEAD_DIM]
        r_parts.append(_dot(ug, wa_ref[g]))
        i_parts.append(_dot(ug, wx_ref[g]))
    r = _sigmoid(jnp.concatenate(r_parts, axis=1) + ba_ref[...])
    ig = _sigmoid(jnp.concatenate(i_parts, axis=1) + bx_ref[...])

    log_a = (-LRU_C) * r * _softplus(-lam_ref[...])
    a_c = jnp.exp(log_a)
    b_c = jnp.sqrt(-jnp.tanh(log_a) * (a_c * a_c + 1.0)) * (ig * conv)

    n8 = tm // 8
    a3 = a_c.reshape(n8, 8, a_c.shape[1])
    b3 = b_c.reshape(n8, 8, b_c.shape[1])
    sub = lax.broadcasted_iota(jnp.int32, a3.shape, 1)
    for d in (1, 2, 4):
        m = sub >= d
        a_sh = jnp.where(m, pltpu.roll(a3, d, 1), 1.0)
        b_sh = jnp.where(m, pltpu.roll(b3, d, 1), 0.0)
        b3 = a3 * b_sh + b3
        a3 = a3 * a_sh
    carry = hcarry[...]
    hs = []
    for i in range(n8):
        h_i = b3[i] + a3[i] * carry
        hs.append(h_i)
        carry = h_i[7:8, :]
    hcarry[...] = carry
    h = jnp.concatenate(hs, axis=0)

    out_ref[...] = ga_ref[...] * (gate_ref[...] * h)


def _lru(zin, wa, ba, wx, bx, lam, tm=256):
    t = zin.shape[0]
    d = D_MODEL
    row = lambda i: (0, 0)
    return pl.pallas_call(
        _lru_kernel,
        grid=(t // tm,),
        in_specs=[
            pl.BlockSpec((tm, d), lambda i: (i, 0)),
            pl.BlockSpec((tm, d), lambda i: (i, 1)),
            pl.BlockSpec((tm, d), lambda i: (i, 2)),
            pl.BlockSpec((LRU_HEADS, LRU_HEAD_DIM, LRU_HEAD_DIM), lambda i: (0, 0, 0)),
            pl.BlockSpec((1, d), row),
            pl.BlockSpec((LRU_HEADS, LRU_HEAD_DIM, LRU_HEAD_DIM), lambda i: (0, 0, 0)),
            pl.BlockSpec((1, d), row),
            pl.BlockSpec((1, d), row),
        ],
        out_specs=pl.BlockSpec((tm, d), lambda i: (i, 0)),
        out_shape=jax.ShapeDtypeStruct((t, d), F32),
        scratch_shapes=[pltpu.VMEM((1, d), F32)],
        compiler_params=_params(("arbitrary",)),
        name="lru",
    )(zin, zin, zin, wa, ba, wx, bx, lam)


def _block_ones(n, seg):
    r = lax.broadcasted_iota(jnp.int32, (n, n), 0) // seg
    c = lax.broadcasted_iota(jnp.int32, (n, n), 1) // seg
    return jnp.where(r == c, 1.0, 0.0).astype(BF16)


REC_ROWS = 256


def _round_robin(*gens):
    live = list(gens)
    while live:
        for gen in list(live):
            try:
                next(gen)
            except StopIteration:
                live.remove(gen)
        yield


def _interleave(*gens):
    for _ in _round_robin(*gens):
        pass


def _chain(*gens):
    for gen in gens:
        yield from gen


def _rwkv_rec_kernel(r_ref, k_ref, v_ref, zl_ref, zgb_ref, w0_ref, wb_ref, a0_ref, ab_ref, gb_ref,
                     kk_ref, ka_ref, rk_ref, gng_ref, gnb_ref, out_ref, s_ref):
    @pl.when(pl.program_id(0) == 0)
    def _():
        s_ref[...] = jnp.zeros_like(s_ref)

    c = CHUNK
    hd = RWKV_HEAD_DIM
    ng = N_GROUPS
    nch = r_ref.shape[0] // c
    sls = [slice(g * GROUP, (g + 1) * GROUP) for g in range(ng)]

    assert nch % 2 == 0
    rows = lax.broadcasted_iota(jnp.int32, (2 * c, 2 * c), 0)
    cols = lax.broadcasted_iota(jnp.int32, (2 * c, 2 * c), 1)
    tri2 = jnp.where((cols <= rows) & (cols // c == rows // c), 1.0, 0.0).astype(BF16)
    t_c = lax.broadcasted_iota(jnp.int32, (c, GROUP), 0)
    s_c = lax.broadcasted_iota(jnp.int32, (c, GROUP), 1) % hd
    strict = s_c < t_c
    incl = s_c <= t_c
    eye_c = jnp.where(s_c == t_c, 1.0, 0.0)
    bd = (lax.broadcasted_iota(jnp.int32, (GROUP, GROUP), 0) // hd
          == lax.broadcasted_iota(jnp.int32, (GROUP, GROUP), 1) // hd)
    ones_bd = _block_ones(GROUP, hd)

    def expand(x_c):
        return jnp.where(bd, jnp.concatenate([x_c] * (GROUP // hd), axis=0), 0.0).astype(BF16)

    def seg_sums(xs):
        hi, lo = _split2(jnp.concatenate(xs, axis=0))
        s = _dot(jnp.concatenate([hi, lo], axis=0), ones_bd)
        n = c * len(xs)
        s = s[:n] + s[n:]
        return [s[i * c:(i + 1) * c] for i in range(len(xs))]

    state = [s_ref[g] for g in range(ng)]
    prep = [None] * nch

    pair_prep = {}

    def lora_stage(pi):
        rs_ = slice(2 * pi * c, (2 * pi + 2) * c)
        zl = zl_ref[rs_, :]
        nl = zl.shape[1]
        w_lo = zl[:, 0:D_LORA_PAD]
        a_lo = pltpu.roll(zl, nl - D_LORA, 1)[:, 0:D_LORA_PAD]
        g_lo = pltpu.roll(zl, nl - 2 * D_LORA, 1)[:, 0:D_GATE_LORA]
        w_log = -_softplus_log(-(w0_ref[...] + _dot(jnp.tanh(w_lo).astype(BF16), wb_ref[...]))) - 0.5
        lw = -jnp.exp(w_log)
        a = _sigmoid(a0_ref[...] + _dot(a_lo.astype(BF16), ab_ref[...]))
        gate = _dot(_sigmoid(g_lo).astype(BF16), gb_ref[...])
        hi, mid, lo = _split3(lw)
        cum = _dot(tri2, hi) + _dot(tri2, mid) + _dot(tri2, lo)
        pair_prep[pi] = dict(lw=lw, a=a, gate=gate, cum=cum)
        yield

    def phase_a(ci):
        rs_ = slice(ci * c, (ci + 1) * c)
        half = slice((ci % 2) * c, (ci % 2 + 1) * c)
        pp = pair_prep[ci // 2]
        lw, a, gate, cum = pp["lw"][half], pp["a"][half], pp["gate"][half], pp["cum"][half]
        tot = cum[c - 1:c, :]
        k = k_ref[rs_, :]
        kk = k * kk_ref[...]
        kk2 = kk * kk
        n2 = jnp.concatenate(seg_sums([kk2[:, sl] for sl in sls]), axis=1)
        kn = kk * lax.rsqrt(jnp.maximum(n2, 1e-24))
        bb = kn * a
        kf = k * (1.0 + (a - 1.0) * ka_ref[...])
        yield
        r = r_ref[rs_, :]
        v = v_ref[rs_, :]
        p_inv = jnp.exp(-cum)
        p_end = jnp.exp(tot - cum)
        rq = r * jnp.exp(cum)
        kap = kn * jnp.exp(cum - lw)
        bet = bb * p_inv
        kt = kf * p_inv
        lhs2 = [jnp.concatenate([kap[:, sl], rq[:, sl]], axis=0).astype(BF16) for sl in sls]
        amat = [_dot_nt(lhs2[g], jnp.concatenate([expand(bet[:, sls[g]]), expand(kt[:, sls[g]])], axis=0))
                for g in range(ng)]
        yield
        l_c = [jnp.where(strict, a[:c, :GROUP], 0.0) for a in amat]
        a_lo = [jnp.concatenate([jnp.where(strict, a[:c, GROUP:], 0.0),
                                 jnp.where(incl, a[c:, GROUP:], 0.0)], axis=0).astype(BF16) for a in amat]
        arb = [jnp.where(incl, a[c:, :GROUP], 0.0).astype(BF16) for a in amat]
        av = [_dot(a_lo[g], expand(v[:, sls[g]])) for g in range(ng)]
        x_c = [eye_c - jnp.where((t_c >> 1) == (s_c >> 1), l, 0.0) for l in l_c]
        for lvl in range(2, 7):
            lmask = ((t_c >> lvl) == (s_c >> lvl)) & ((t_c >> (lvl - 1)) != (s_c >> (lvl - 1)))
            y_c = [_dot(x_c[g].astype(BF16), expand(jnp.where(lmask, l_c[g], 0.0))) for g in range(ng)]
            yield
            x_c = [x_c[g] - _dot(y_c[g].astype(BF16), expand(x_c[g])) for g in range(ng)]
            yield
        bhat_neg = -(bb * p_end)
        khat = kf * p_end
        upd_rhs = [jnp.concatenate([bhat_neg[:, sl], khat[:, sl]], axis=0).astype(BF16) for sl in sls]
        prep[ci] = dict(lhs2=lhs2, av=av, arb=arb, x=[x.astype(BF16) for x in x_c], v=v,
                        upd_rhs=upd_rhs, p_tot=jnp.exp(tot), rkk=r * kf * rk_ref[...], gate=gate)

    def phase_b(ci):
        p = prep[ci]
        rs_ = slice(ci * c, (ci + 1) * c)
        v = p["v"]
        rs = [_dot_nt(p["lhs2"][g], state[g].astype(BF16)) for g in range(ng)]
        yield
        u_c = [_dot(p["x"][g], expand(rs[g][:c] + p["av"][g][:c])) for g in range(ng)]
        yield
        o_c = [rs[g][c:] + p["av"][g][c:] - _dot(p["arb"][g], expand(u_c[g])) for g in range(ng)]
        for g in range(ng):
            upd = _dot_tn(jnp.concatenate([u_c[g], v[:, sls[g]]], axis=0).astype(BF16), p["upd_rhs"][g])
            state[g] = state[g] * p["p_tot"][:, sls[g]] + jnp.where(bd, upd, 0.0)
        yield
        sums = seg_sums(o_c + [p["rkk"][:, sl] for sl in sls])
        dev = [o_c[g] - sums[g] * (1.0 / hd) for g in range(ng)]
        yield
        var = seg_sums([dv * dv for dv in dev])
        o_parts = []
        for g in range(ng):
            o_n = dev[g] * lax.rsqrt(var[g] * (1.0 / hd) + GN_EPS) * gng_ref[:, sls[g]] + gnb_ref[:, sls[g]]
            o_parts.append(o_n + sums[ng + g] * v[:, sls[g]])
        o = jnp.concatenate(o_parts, axis=1)
        out_ref[rs_, :] = zgb_ref[rs_, :] * (o * p["gate"])

    def pair_a(pi):
        yield from lora_stage(pi)
        yield from _round_robin(phase_a(2 * pi), phase_a(2 * pi + 1))

    n_pairs = nch // 2
    _interleave(pair_a(0))
    for k in range(n_pairs):
        b_gen = _chain(phase_b(2 * k), phase_b(2 * k + 1))
        _interleave(b_gen, *([pair_a(k + 1)] if k + 1 < n_pairs else []))

    for g in range(ng):
        s_ref[g] = state[g]


def _rwkv_rec(zin, w0, wb, a0, ab, gb, k_k, k_a, rk, gng, gnb):
    t = zin.shape[0]
    d = D_MODEL
    dl = ZL_WIDTH
    br = min(REC_ROWS, t)
    zcol = lambda c: pl.BlockSpec((br, d), lambda i: (i, c))
    row = pl.BlockSpec((1, d), lambda i: (0, 0))
    return pl.pallas_call(
        _rwkv_rec_kernel,
        grid=(t // br,),
        in_specs=[zcol(4), zcol(5), zcol(6), pl.BlockSpec((br, dl), lambda i: (i, 7 * d // dl)), zcol(3),
                  row, _resident((D_LORA_PAD, d)), row, _resident((D_LORA_PAD, d)),
                  _resident((D_GATE_LORA, d)), row, row, row, row, row],
        out_specs=pl.BlockSpec((br, d), lambda i: (i, 0)),
        out_shape=jax.ShapeDtypeStruct((t, d), F32),
        scratch_shapes=[pltpu.VMEM((N_GROUPS, GROUP, GROUP), F32)],
        compiler_params=_params(("arbitrary",), VMEM_LIMIT_BIG_TILES),
        name="rwkv_rec",
    )(zin, zin, zin, zin, zin, w0, wb, a0, ab, gb, k_k, k_a, rk, gng, gnb)


def _out_proj_kernel(ya_ref, yb_ref, x_ref, gin_ref, bin_ref, w_ref, g_ref, b_ref, o_ref):
    y = (ya_ref[...] + yb_ref[...]).astype(BF16)
    mix = _dot(y, w_ref[...])
    h = _layer_norm(x_ref[...], gin_ref[...], bin_ref[...])
    o_ref[...] = _layer_norm(ALPHA * h + mix, g_ref[...], b_ref[...])


def _out_proj(ya, yb, x, g_in, b_in, w, g, b, tm=512):
    t, d = x.shape
    blk = pl.BlockSpec((tm, d), lambda i: (i, 0))
    row = pl.BlockSpec((1, d), lambda i: (0, 0))
    return pl.pallas_call(
        _out_proj_kernel,
        grid=(t // tm,),
        in_specs=[blk, blk, blk, row, row, _resident((d, d)), row, row],
        out_specs=blk,
        out_shape=jax.ShapeDtypeStruct((t, d), F32),
        compiler_params=_params(("arbitrary",)),
        name="out_proj",
    )(ya, yb, x, g_in, b_in, w, g, b)


def _mem_kv_kernel(mem_ref, wk_ref, wv_ref, k_ref, v_ref):
    m = mem_ref[...].astype(BF16)
    k_ref[...] = _dot(m, wk_ref[...].astype(BF16)).astype(BF16)
    v_ref[...] = _dot(m, wv_ref[...].astype(BF16)).astype(BF16)


def _mem_kv(mem, wk, wv, tn=512):
    n, d = mem.shape
    return pl.pallas_call(
        _mem_kv_kernel,
        grid=(d // tn,),
        in_specs=[pl.BlockSpec((n, d), lambda j: (0, 0)),
                  pl.BlockSpec((d, tn), lambda j: (0, j)),
                  pl.BlockSpec((d, tn), lambda j: (0, j))],
        out_specs=[pl.BlockSpec((n, tn), lambda j: (0, j))] * 2,
        out_shape=[jax.ShapeDtypeStruct((n, d), BF16)] * 2,
        compiler_params=_params(("arbitrary",)),
        name="mem_kv",
    )(mem, wk, wv)


def _xattn_kernel(h_ref, wq_ref, k_ref, v_ref, wo_ref, g_ref, b_ref, o_ref):
    h = h_ref[...]
    q = _dot(h.astype(BF16), wq_ref[...]).astype(BF16)
    scale = XATTN_HEAD_DIM ** -0.5
    outs = []
    for hh in range(XATTN_HEADS):
        sl = slice(hh * XATTN_HEAD_DIM, (hh + 1) * XATTN_HEAD_DIM)
        s = _dot_nt(q[:, sl], k_ref[:, sl]) * scale
        e = jnp.exp(s - jnp.max(s, axis=-1, keepdims=True))
        p = e / jnp.sum(e, axis=-1, keepdims=True)
        outs.append(_dot(p.astype(BF16), v_ref[:, sl]))
    o = jnp.concatenate(outs, axis=1).astype(BF16)
    xa = _dot(o, wo_ref[...])
    o_ref[...] = _layer_norm(ALPHA * h + xa, g_ref[...], b_ref[...])


def _xattn(h, wq, kmem, vmem, wo, g, b, tm=512):
    t, d = h.shape
    n = kmem.shape[0]
    blk = pl.BlockSpec((tm, d), lambda i: (i, 0))
    row = pl.BlockSpec((1, d), lambda i: (0, 0))
    return pl.pallas_call(
        _xattn_kernel,
        grid=(t // tm,),
        in_specs=[blk, _resident((d, d)), _resident((n, d)), _resident((n, d)), _resident((d, d)),
                  row, row],
        out_specs=blk,
        out_shape=jax.ShapeDtypeStruct((t, d), F32),
        compiler_params=_params(("arbitrary",)),
        name="xattn",
    )(h, wq, kmem, vmem, wo, g, b)


def _ffn_kernel(h_ref, wg_ref, wu_ref, wd_ref, g_ref, b_ref, o_ref):
    j = pl.program_id(1)

    @pl.when(j == 0)
    def _():
        o_ref[...] = jnp.zeros_like(o_ref)

    hb = h_ref[...].astype(BF16)
    gate = _dot(hb, wg_ref[...].astype(BF16))
    up = _dot(hb, wu_ref[...].astype(BF16))
    act = (gate * _sigmoid(gate) * up).astype(BF16)
    o_ref[...] += _dot(act, wd_ref[...].astype(BF16))

    @pl.when(j == pl.num_programs(1) - 1)
    def _():
        o_ref[...] = _layer_norm(ALPHA * h_ref[...] + o_ref[...], g_ref[...], b_ref[...])


def _ffn(h, wg, wu, wd, g, b, tm=1024, tf=256):
    t, d = h.shape
    f = wg.shape[1]
    tm = min(tm, t)
    blk = pl.BlockSpec((tm, d), lambda i, j: (i, 0))
    row = pl.BlockSpec((1, d), lambda i, j: (0, 0))
    return pl.pallas_call(
        _ffn_kernel,
        grid=(t // tm, f // tf),
        in_specs=[blk,
                  pl.BlockSpec((d, tf), lambda i, j: (0, j)),
                  pl.BlockSpec((d, tf), lambda i, j: (0, j)),
                  pl.BlockSpec((tf, d), lambda i, j: (j, 0)),
                  row, row],
        out_specs=blk,
        out_shape=jax.ShapeDtypeStruct((t, d), F32),
        compiler_params=_params(("arbitrary", "arbitrary"), VMEM_LIMIT_BIG_TILES),
        name="ffn",
    )(h, wg, wu, wd, g, b)


def _pad_cols(w, n):
    return jnp.pad(w, ((0, 0), (0, n - w.shape[1])))


def _pad_rows(w, n):
    return jnp.pad(w, ((0, n - w.shape[0]), (0, 0)))


def kernel(x, mem, ln_in_g, ln_in_b, w_in, conv_w, conv_b, lru_wa, lru_ba, lru_wx, lru_bx, lru_lambda, rw_mu, rw_w0, rw_wB, rw_a0, rw_aB, rw_gB, rw_kk, rw_ka, rw_rk, rw_gn_g, rw_gn_b, w_out, ln1_g, ln1_b, xa_wq, xa_wk, xa_wv, xa_wo, ln2_g, ln2_b, ffn_wg, ffn_wu, ffn_wd, ln3_g, ln3_b):
    bsz, t, d = x.shape
    depth = w_in.shape[0]
    assert bsz == 1 and d == D_MODEL and t % 512 == 0
    row = lambda p: p.reshape(1, -1)

    h = None
    for l in range(depth):
        w_t = jnp.swapaxes(w_in[l], 0, 1)
        n_cols = IN_BLOCKS * IN_TN
        mu = row(rw_mu[l])
        mu_steps = jnp.concatenate(
            [jnp.zeros((1, 2 * d), F32), mu[:, :3 * d], jnp.zeros((1, 2 * d), F32), mu[:, 3 * d:]], axis=1)
        col_params = jnp.concatenate([
            _pad_cols(mu_steps, n_cols),
            _pad_cols(conv_w[l], n_cols),
            _pad_cols(row(conv_b[l]), n_cols),
            jnp.zeros((CP_ROWS - CP_CONV_B - 1, n_cols), F32)], axis=0)
        if l == 0:
            zin = _in_proj(x[0], row(ln_in_g), row(ln_in_b), w_t, col_params)
        else:
            raise NotImplementedError("DEPTH > 1 is not part of this problem")

        ya = _lru(zin, lru_wa[l].astype(BF16), row(lru_ba[l]),
                  lru_wx[l].astype(BF16), row(lru_bx[l]), row(lru_lambda[l]))

        yb = _rwkv_rec(
            zin, row(rw_w0[l]),
            _pad_rows(rw_wB[l], D_LORA_PAD).astype(BF16), row(rw_a0[l]),
            _pad_rows(rw_aB[l], D_LORA_PAD).astype(BF16), rw_gB[l].astype(BF16),
            row(rw_kk[l]), row(rw_ka[l]), row(rw_rk[l]), row(rw_gn_g[l]), row(rw_gn_b[l]))

        h = _out_proj(ya, yb, x[0], row(ln_in_g), row(ln_in_b), w_out[l].astype(BF16),
                      row(ln1_g[l]), row(ln1_b[l]))
        kmem, vmem = _mem_kv(mem[0], xa_wk[l], xa_wv[l])
        h = _xattn(h, xa_wq[l].astype(BF16), kmem, vmem, xa_wo[l].astype(BF16),
                   row(ln2_g[l]), row(ln2_b[l]))
        h = _ffn(h, ffn_wg[l], ffn_wu[l], ffn_wd[l], row(ln3_g[l]), row(ln3_b[l]))
    return h[None]
```

```python
import jax
import jax.numpy as jnp
from jax import lax
from jax.experimental import pallas as pl
from jax.experimental.pallas import tpu as pltpu

F32 = jnp.float32
BF16 = jnp.bfloat16

D_MODEL = 2048
LN_EPS = 1e-5
ALPHA = 2.0 ** 0.25

LRU_HEADS = 16
LRU_HEAD_DIM = 128
CONV_WIDTH = 4
LRU_C = 8.0

RWKV_HEAD_DIM = 64
D_LORA = 96
D_LORA_PAD = 128
D_GATE_LORA = 256
ZL_WIDTH = 512
GN_EPS = 64e-5
CHUNK = 64
GROUP = 256
N_GROUPS = D_MODEL // GROUP

XATTN_HEADS = 4
XATTN_HEAD_DIM = 512
N_MEM = 256
D_FF = 5632

VMEM_LIMIT = 56 * 1024 * 1024
VMEM_LIMIT_BIG_TILES = 60 * 1024 * 1024


def _dot(a, b):
    return jnp.dot(a, b, preferred_element_type=F32)


def _dot_nt(a, b):
    return lax.dot_general(a, b, (((1,), (1,)), ((), ())), preferred_element_type=F32)


def _dot_tn(a, b):
    return lax.dot_general(a, b, (((0,), (0,)), ((), ())), preferred_element_type=F32)


def _split2(x):
    hi = x.astype(BF16)
    lo = (x - hi.astype(F32)).astype(BF16)
    return hi, lo


def _split3(x):
    hi = x.astype(BF16)
    r1 = x - hi.astype(F32)
    mid = r1.astype(BF16)
    lo = (r1 - mid.astype(F32)).astype(BF16)
    return hi, mid, lo


def _sigmoid(x):
    return 1.0 / (1.0 + jnp.exp(-x))


def _softplus(x):
    return jnp.maximum(x, 0.0) + jnp.log1p(jnp.exp(-jnp.abs(x)))


def _softplus_log(x):
    return jnp.maximum(x, 0.0) + jnp.log(1.0 + jnp.exp(-jnp.abs(x)))


def _gelu_tanh(x):
    c = 0.7978845608028654
    return 0.5 * x * (1.0 + jnp.tanh(c * (x + 0.044715 * (x * x * x))))


def _layer_norm(x, g, b):
    mu = jnp.mean(x, axis=-1, keepdims=True)
    xc = x - mu
    var = jnp.mean(xc * xc, axis=-1, keepdims=True)
    return xc * lax.rsqrt(var + LN_EPS) * g + b


def _shift_rows(z, prev8, s):
    rolled = pltpu.roll(z, s, 0)
    row8 = lax.broadcasted_iota(jnp.int32, prev8.shape, 0)
    head = jnp.where(row8 < s, pltpu.roll(prev8, s, 0), rolled[:8])
    return jnp.concatenate([head, rolled[8:]], axis=0)


def _params(sem, vmem_limit=VMEM_LIMIT):
    return pltpu.CompilerParams(dimension_semantics=sem, vmem_limit_bytes=vmem_limit)


def _resident(shape):
    return pl.BlockSpec(shape, lambda *_: (0,) * len(shape), pipeline_mode=pl.Buffered(1))


IN_TN = 1024
IN_D_BLOCKS = D_MODEL // IN_TN
IN_MAIN_BLOCKS = 5 * IN_D_BLOCKS
IN_GATE_BLOCKS = 2 * IN_D_BLOCKS
IN_BLOCKS = IN_MAIN_BLOCKS + IN_GATE_BLOCKS + 1


def _in_weight_row(j, n_out):
    gates = n_out - 2 * D_MODEL + (j - IN_MAIN_BLOCKS) * IN_TN
    return jnp.where(j < IN_MAIN_BLOCKS, j * IN_TN,
                     jnp.where(j < IN_MAIN_BLOCKS + IN_GATE_BLOCKS, gates, 5 * D_MODEL))


def _in_out_block(j):
    b2 = 2 * D_MODEL // IN_TN
    return jnp.where(j < b2, j,
                     jnp.where(j < IN_MAIN_BLOCKS, j + b2,
                               jnp.where(j < IN_MAIN_BLOCKS + b2, j - (IN_MAIN_BLOCKS - b2), j)))


CP_MU, CP_CONV_W, CP_CONV_B, CP_ROWS = 0, 1, 1 + CONV_WIDTH, 8


def _in_proj_kernel(x_ref, g_ref, b_ref, w_ref, cp_ref, z_ref, hb_ref, carry_ref):
    i = pl.program_id(0)
    j = pl.program_id(1)
    nd = IN_D_BLOCKS

    @pl.when(j == 0)
    def _():
        hb_ref[...] = _layer_norm(x_ref[...], g_ref[...], b_ref[...]).astype(BF16)

    @pl.when((i == 0) & (j == 0))
    def _():
        carry_ref[...] = jnp.zeros_like(carry_ref)

    def keep_tail(z):
        carry_ref[j] = z[z.shape[0] - 8:, :]

    @pl.when(j < nd)
    def _():
        z = _dot_nt(hb_ref[...], w_ref[...].astype(BF16))
        prev8 = carry_ref[j]
        cw = CP_CONV_W + CONV_WIDTH - 1
        conv = cp_ref[CP_CONV_B:CP_CONV_B + 1, :] + z * cp_ref[cw:cw + 1, :]
        for s in range(1, CONV_WIDTH):
            conv = conv + _shift_rows(z, prev8, s) * cp_ref[cw - s:cw - s + 1, :]
        keep_tail(z)
        z_ref[...] = conv

    @pl.when((j >= nd) & (j < 2 * nd))
    def _():
        z_ref[...] = _gelu_tanh(_dot_nt(hb_ref[...], w_ref[...].astype(BF16)))

    def token_shift(z):
        zp = _shift_rows(z, carry_ref[j], 1)
        keep_tail(z)
        return z + (zp - z) * cp_ref[CP_MU:CP_MU + 1, :]

    @pl.when((j >= 2 * nd) & (j < IN_MAIN_BLOCKS))
    def _():
        z_ref[...] = token_shift(_dot_nt(hb_ref[...], w_ref[...].astype(BF16)))

    @pl.when((j >= IN_MAIN_BLOCKS) & (j < IN_MAIN_BLOCKS + IN_GATE_BLOCKS))
    def _():
        z_ref[...] = _sigmoid(_dot_nt(hb_ref[...], w_ref[...].astype(BF16)))

    @pl.when(j >= IN_MAIN_BLOCKS + IN_GATE_BLOCKS)
    def _():
        z_ref[...] = token_shift(_dot_nt(hb_ref[...], w_ref[...].astype(BF16)))


def _in_proj(x, g, b, w_t, col_params, tm=1024):
    t, d = x.shape
    tn = IN_TN
    nb = IN_BLOCKS
    n_out = w_t.shape[0]
    tm = min(tm, t)
    return pl.pallas_call(
        _in_proj_kernel,
        grid=(t // tm, nb),
        in_specs=[
            pl.BlockSpec((tm, d), lambda i, j: (i, 0)),
            pl.BlockSpec((1, d), lambda i, j: (0, 0)),
            pl.BlockSpec((1, d), lambda i, j: (0, 0)),
            pl.BlockSpec((pl.Element(tn), pl.Element(d)),
                         lambda i, j: (pl.multiple_of(_in_weight_row(j, n_out), 8), 0)),
            pl.BlockSpec((CP_ROWS, tn), lambda i, j: (0, j)),
        ],
        out_specs=pl.BlockSpec((tm, tn), lambda i, j: (i, _in_out_block(j))),
        out_shape=jax.ShapeDtypeStruct((t, nb * tn), F32),
        scratch_shapes=[pltpu.VMEM((tm, d), BF16), pltpu.VMEM((nb, 8, tn), F32)],
        compiler_params=_params(("arbitrary", "arbitrary")),
        name="in_proj",
    )(x, g, b, w_t, col_params)


def _lru_kernel(u_ref, gate_ref, ga_ref, wa_ref, ba_ref, wx_ref, bx_ref,
                lam_ref, out_ref, hcarry):
    @pl.when(pl.program_id(0) == 0)
    def _():
        hcarry[...] = jnp.zeros_like(hcarry)

    conv = u_ref[...]
    tm = conv.shape[0]
    cb16 = conv.astype(BF16)
    r_parts, i_parts = [], []
    for g in range(LRU_HEADS):
        ug = cb16[:, g * LRU_HEAD_DIM:(g + 1) * LRU_HEAD_DIM]
        r_parts.append(_dot(ug, wa_ref[g]))
        i_parts.append(_dot(ug, wx_ref[g]))
    r = _sigmoid(jnp.concatenate(r_parts, axis=1) + ba_ref[...])
    ig = _sigmoid(jnp.concatenate(i_parts, axis=1) + bx_ref[...])

    log_a = (-LRU_C) * r * _softplus(-lam_ref[...])
    a_c = jnp.exp(log_a)
    om = -jnp.tanh(log_a) * (a_c * a_c + 1.0)
    b_c = jnp.where(om > 0.0, om * lax.rsqrt(om), 0.0) * (ig * conv)

    n8 = tm // 8
    a3 = a_c.reshape(n8, 8, a_c.shape[1])
    b3 = b_c.reshape(n8, 8, b_c.shape[1])
    sub = lax.broadcasted_iota(jnp.int32, a3.shape, 1)
    for d in (1, 2, 4):
        m = sub >= d
        a_sh = jnp.where(m, pltpu.roll(a3, d, 1), 1.0)
        b_sh = jnp.where(m, pltpu.roll(b3, d, 1), 0.0)
        b3 = a3 * b_sh + b3
        a3 = a3 * a_sh
    carry = hcarry[...]
    hs = []
    for i in range(n8):
        h_i = b3[i] + a3[i] * carry
        hs.append(h_i)
        carry = h_i[7:8, :]
    hcarry[...] = carry
    h = jnp.concatenate(hs, axis=0)

    out_ref[...] = ga_ref[...] * (gate_ref[...] * h)


def _lru(zin, wa, ba, wx, bx, lam, tm=256):
    t = zin.shape[0]
    d = D_MODEL
    row = lambda i: (0, 0)
    return pl.pallas_call(
        _lru_kernel,
        grid=(t // tm,),
        in_specs=[
            pl.BlockSpec((tm, d), lambda i: (i, 0)),
            pl.BlockSpec((tm, d), lambda i: (i, 1)),
            pl.BlockSpec((tm, d), lambda i: (i, 2)),
            pl.BlockSpec((LRU_HEADS, LRU_HEAD_DIM, LRU_HEAD_DIM), lambda i: (0, 0, 0)),
            pl.BlockSpec((1, d), row),
            pl.BlockSpec((LRU_HEADS, LRU_HEAD_DIM, LRU_HEAD_DIM), lambda i: (0, 0, 0)),
            pl.BlockSpec((1, d), row),
            pl.BlockSpec((1, d), row),
        ],
        out_specs=pl.BlockSpec((tm, d), lambda i: (i, 0)),
        out_shape=jax.ShapeDtypeStruct((t, d), F32),
        scratch_shapes=[pltpu.VMEM((1, d), F32)],
        compiler_params=_params(("arbitrary",)),
        name="lru",
    )(zin, zin, zin, wa, ba, wx, bx, lam)


def _block_ones(n, seg):
    r = lax.broadcasted_iota(jnp.int32, (n, n), 0) // seg
    c = lax.broadcasted_iota(jnp.int32, (n, n), 1) // seg
    return jnp.where(r == c, 1.0, 0.0).astype(BF16)


REC_ROWS = 256


def _round_robin(*gens):
    live = list(gens)
    while live:
        for gen in list(live):
            try:
                next(gen)
            except StopIteration:
                live.remove(gen)
        yield


def _interleave(*gens):
    for _ in _round_robin(*gens):
        pass


def _chain(*gens):
    for gen in gens:
        yield from gen


def _rwkv_rec_kernel(r_ref, k_ref, v_ref, zl_ref, zgb_ref, w0_ref, wb_ref, a0_ref, ab_ref, gb_ref,
                     kk_ref, ka_ref, rk_ref, gng_ref, gnb_ref, out_ref, s_ref):
    @pl.when(pl.program_id(0) == 0)
    def _():
        s_ref[...] = jnp.zeros_like(s_ref)

    c = CHUNK
    hd = RWKV_HEAD_DIM
    ng = N_GROUPS
    nch = r_ref.shape[0] // c
    sls = [slice(g * GROUP, (g + 1) * GROUP) for g in range(ng)]

    assert nch % 2 == 0
    rows = lax.broadcasted_iota(jnp.int32, (2 * c, 2 * c), 0)
    cols = lax.broadcasted_iota(jnp.int32, (2 * c, 2 * c), 1)
    tri2 = jnp.where((cols <= rows) & (cols // c == rows // c), 1.0, 0.0).astype(BF16)
    t_c = lax.broadcasted_iota(jnp.int32, (c, GROUP), 0)
    s_c = lax.broadcasted_iota(jnp.int32, (c, GROUP), 1) % hd
    strict = s_c < t_c
    incl = s_c <= t_c
    eye_c = jnp.where(s_c == t_c, 1.0, 0.0)
    bd = (lax.broadcasted_iota(jnp.int32, (GROUP, GROUP), 0) // hd
          == lax.broadcasted_iota(jnp.int32, (GROUP, GROUP), 1) // hd)
    ones_bd = _block_ones(GROUP, hd)

    def expand(x_c):
        return jnp.where(bd, jnp.concatenate([x_c] * (GROUP // hd), axis=0), 0.0).astype(BF16)

    def seg_sums(xs):
        hi, lo = _split2(jnp.concatenate(xs, axis=0))
        s = _dot(jnp.concatenate([hi, lo], axis=0), ones_bd)
        n = c * len(xs)
        s = s[:n] + s[n:]
        return [s[i * c:(i + 1) * c] for i in range(len(xs))]

    state = [s_ref[g] for g in range(ng)]
    prep = [None] * nch

    pair_prep = {}

    def lora_stage(pi):
        rs_ = slice(2 * pi * c, (2 * pi + 2) * c)
        zl = zl_ref[rs_, :]
        nl = zl.shape[1]
        w_lo = zl[:, 0:D_LORA_PAD]
        a_lo = pltpu.roll(zl, nl - D_LORA, 1)[:, 0:D_LORA_PAD]
        g_lo = pltpu.roll(zl, nl - 2 * D_LORA, 1)[:, 0:D_GATE_LORA]
        w_log = -_softplus_log(-(w0_ref[...] + _dot(jnp.tanh(w_lo).astype(BF16), wb_ref[...]))) - 0.5
        lw = -jnp.exp(w_log)
        a = _sigmoid(a0_ref[...] + _dot(a_lo.astype(BF16), ab_ref[...]))
        gate = _dot(_sigmoid(g_lo).astype(BF16), gb_ref[...])
        hi, mid, lo = _split3(lw)
        cum = _dot(tri2, hi) + _dot(tri2, mid) + _dot(tri2, lo)
        pair_prep[pi] = dict(lw=lw, a=a, gate=gate, cum=cum)
        yield

    def phase_a(ci):
        rs_ = slice(ci * c, (ci + 1) * c)
        half = slice((ci % 2) * c, (ci % 2 + 1) * c)
        pp = pair_prep[ci // 2]
        lw, a, gate, cum = pp["lw"][half], pp["a"][half], pp["gate"][half], pp["cum"][half]
        tot = cum[c - 1:c, :]
        k = k_ref[rs_, :]
        kk = k * kk_ref[...]
        kk2 = kk * kk
        n2 = jnp.concatenate(seg_sums([kk2[:, sl] for sl in sls]), axis=1)
        kn = kk * lax.rsqrt(jnp.maximum(n2, 1e-24))
        bb = kn * a
        kf = k * (1.0 + (a - 1.0) * ka_ref[...])
        yield
        r = r_ref[rs_, :]
        v = v_ref[rs_, :]
        p_inv = jnp.exp(-cum)
        p_end = jnp.exp(tot - cum)
        rq = r * jnp.exp(cum)
        kap = kn * jnp.exp(cum - lw)
        bet = bb * p_inv
        kt = kf * p_inv
        lhs2 = [jnp.concatenate([kap[:, sl], rq[:, sl]], axis=0).astype(BF16) for sl in sls]
        amat = [_dot_nt(lhs2[g], jnp.concatenate([expand(bet[:, sls[g]]), expand(kt[:, sls[g]])], axis=0))
                for g in range(ng)]
        yield
        l_c = [jnp.where(strict, a[:c, :GROUP], 0.0) for a in amat]
        a_lo = [jnp.concatenate([jnp.where(strict, a[:c, GROUP:], 0.0),
                                 jnp.where(incl, a[c:, GROUP:], 0.0)], axis=0).astype(BF16) for a in amat]
        arb = [jnp.where(incl, a[c:, :GROUP], 0.0).astype(BF16) for a in amat]
        av = [_dot(a_lo[g], expand(v[:, sls[g]])) for g in range(ng)]
        x_c = [eye_c - jnp.where((t_c >> 1) == (s_c >> 1), l, 0.0) for l in l_c]
        for lvl in range(2, 7):
            lmask = ((t_c >> lvl) == (s_c >> lvl)) & ((t_c >> (lvl - 1)) != (s_c >> (lvl - 1)))
            y_c = [_dot(x_c[g].astype(BF16), expand(jnp.where(lmask, l_c[g], 0.0))) for g in range(ng)]
            yield
            x_c = [x_c[g] - _dot(y_c[g].astype(BF16), expand(x_c[g])) for g in range(ng)]
            yield
        bhat_neg = -(bb * p_end)
        khat = kf * p_end
        upd_rhs = [jnp.concatenate([bhat_neg[:, sl], khat[:, sl]], axis=0).astype(BF16) for sl in sls]
        prep[ci] = dict(lhs2=lhs2, av=av, arb=arb, x=[x.astype(BF16) for x in x_c], v=v,
                        upd_rhs=upd_rhs, p_tot=jnp.exp(tot), rkk=r * kf * rk_ref[...], gate=gate)

    def phase_b(ci):
        p = prep[ci]
        rs_ = slice(ci * c, (ci + 1) * c)
        v = p["v"]
        rs = [_dot_nt(p["lhs2"][g], state[g].astype(BF16)) for g in range(ng)]
        yield
        u_c = [_dot(p["x"][g], expand(rs[g][:c] + p["av"][g][:c])) for g in range(ng)]
        yield
        o_c = [rs[g][c:] + p["av"][g][c:] - _dot(p["arb"][g], expand(u_c[g])) for g in range(ng)]
        for g in range(ng):
            upd = _dot_tn(jnp.concatenate([u_c[g], v[:, sls[g]]], axis=0).astype(BF16), p["upd_rhs"][g])
            state[g] = state[g] * p["p_tot"][:, sls[g]] + jnp.where(bd, upd, 0.0)
        yield
        sums = seg_sums(o_c + [p["rkk"][:, sl] for sl in sls])
        dev = [o_c[g] - sums[g] * (1.0 / hd) for g in range(ng)]
        yield
        var = seg_sums([dv * dv for dv in dev])
        o_parts = []
        for g in range(ng):
            o_n = dev[g] * lax.rsqrt(var[g] * (1.0 / hd) + GN_EPS) * gng_ref[:, sls[g]] + gnb_ref[:, sls[g]]
            o_parts.append(o_n + sums[ng + g] * v[:, sls[g]])
        o = jnp.concatenate(o_parts, axis=1)
        out_ref[rs_, :] = zgb_ref[rs_, :] * (o * p["gate"])

    def pair_a(pi):
        yield from lora_stage(pi)
        yield from _round_robin(phase_a(2 * pi), phase_a(2 * pi + 1))

    n_pairs = nch // 2
    _interleave(pair_a(0))
    for k in range(n_pairs):
        b_gen = _chain(phase_b(2 * k), phase_b(2 * k + 1))
        _interleave(b_gen, *([pair_a(k + 1)] if k + 1 < n_pairs else []))

    for g in range(ng):
        s_ref[g] = state[g]


def _rwkv_rec(zin, w0, wb, a0, ab, gb, k_k, k_a, rk, gng, gnb):
    t = zin.shape[0]
    d = D_MODEL
    dl = ZL_WIDTH
    br = min(REC_ROWS, t)
    zcol = lambda c: pl.BlockSpec((br, d), lambda i: (i, c))
    row = pl.BlockSpec((1, d), lambda i: (0, 0))
    return pl.pallas_call(
        _rwkv_rec_kernel,
        grid=(t // br,),
        in_specs=[zcol(4), zcol(5), zcol(6), pl.BlockSpec((br, dl), lambda i: (i, 7 * d // dl)), zcol(3),
                  row, _resident((D_LORA_PAD, d)), row, _resident((D_LORA_PAD, d)),
                  _resident((D_GATE_LORA, d)), row, row, row, row, row],
        out_specs=pl.BlockSpec((br, d), lambda i: (i, 0)),
        out_shape=jax.ShapeDtypeStruct((t, d), F32),
        scratch_shapes=[pltpu.VMEM((N_GROUPS, GROUP, GROUP), F32)],
        compiler_params=_params(("arbitrary",), VMEM_LIMIT_BIG_TILES),
        name="rwkv_rec",
    )(zin, zin, zin, zin, zin, w0, wb, a0, ab, gb, k_k, k_a, rk, gng, gnb)


def _mem_kv_kernel(mem_ref, wk_ref, wv_ref, k_ref, v_ref):
    m = mem_ref[...].astype(BF16)
    k_ref[...] = _dot(m, wk_ref[...].astype(BF16)).astype(BF16)
    v_ref[...] = _dot(m, wv_ref[...].astype(BF16)).astype(BF16)


def _mem_kv(mem, wk, wv, tn=512):
    n, d = mem.shape
    return pl.pallas_call(
        _mem_kv_kernel,
        grid=(d // tn,),
        in_specs=[pl.BlockSpec((n, d), lambda j: (0, 0)),
                  pl.BlockSpec((d, tn), lambda j: (0, j)),
                  pl.BlockSpec((d, tn), lambda j: (0, j))],
        out_specs=[pl.BlockSpec((n, tn), lambda j: (0, j))] * 2,
        out_shape=[jax.ShapeDtypeStruct((n, d), BF16)] * 2,
        compiler_params=_params(("arbitrary",)),
        name="mem_kv",
    )(mem, wk, wv)


def _mix_xattn_kernel(ya_ref, yb_ref, x_ref, gin_ref, bin_ref, wout_ref, g1_ref, b1_ref,
                      wq_ref, k_ref, v_ref, wo_ref, g_ref, b_ref, o_ref):
    y = (ya_ref[...] + yb_ref[...]).astype(BF16)
    mix = _dot(y, wout_ref[...])
    h0 = _layer_norm(x_ref[...], gin_ref[...], bin_ref[...])
    h = _layer_norm(ALPHA * h0 + mix, g1_ref[...], b1_ref[...])
    q = _dot(h.astype(BF16), wq_ref[...]).astype(BF16)
    scale = XATTN_HEAD_DIM ** -0.5
    outs = []
    for hh in range(XATTN_HEADS):
        sl = slice(hh * XATTN_HEAD_DIM, (hh + 1) * XATTN_HEAD_DIM)
        s = _dot_nt(q[:, sl], k_ref[:, sl]) * scale
        e = jnp.exp(s - jnp.max(s, axis=-1, keepdims=True))
        p = e / jnp.sum(e, axis=-1, keepdims=True)
        outs.append(_dot(p.astype(BF16), v_ref[:, sl]))
    o = jnp.concatenate(outs, axis=1).astype(BF16)
    xa = _dot(o, wo_ref[...])
    o_ref[...] = _layer_norm(ALPHA * h + xa, g_ref[...], b_ref[...])


def _mix_xattn(ya, yb, x, g_in, b_in, w_out, g1, b1, wq, kmem, vmem, wo, g2, b2, tm=256):
    t, d = x.shape
    n = kmem.shape[0]
    blk = pl.BlockSpec((tm, d), lambda i: (i, 0))
    row = pl.BlockSpec((1, d), lambda i: (0, 0))
    return pl.pallas_call(
        _mix_xattn_kernel,
        grid=(t // tm,),
        in_specs=[blk, blk, blk, row, row, _resident((d, d)), row, row,
                  _resident((d, d)), _resident((n, d)), _resident((n, d)), _resident((d, d)), row, row],
        out_specs=blk,
        out_shape=jax.ShapeDtypeStruct((t, d), F32),
        compiler_params=_params(("arbitrary",)),
        name="mix_xattn",
    )(ya, yb, x, g_in, b_in, w_out, g1, b1, wq, kmem, vmem, wo, g2, b2)


def _ffn_kernel(h_ref, wg_ref, wu_ref, wd_ref, g_ref, b_ref, o_ref):
    j = pl.program_id(1)

    @pl.when(j == 0)
    def _():
        o_ref[...] = jnp.zeros_like(o_ref)

    hb = h_ref[...].astype(BF16)
    gate = _dot(hb, wg_ref[...].astype(BF16))
    up = _dot(hb, wu_ref[...].astype(BF16))
    act = (gate * _sigmoid(gate) * up).astype(BF16)
    o_ref[...] += _dot(act, wd_ref[...].astype(BF16))

    @pl.when(j == pl.num_programs(1) - 1)
    def _():
        o_ref[...] = _layer_norm(ALPHA * h_ref[...] + o_ref[...], g_ref[...], b_ref[...])


def _ffn(h, wg, wu, wd, g, b, tm=1024, tf=256):
    t, d = h.shape
    f = wg.shape[1]
    tm = min(tm, t)
    blk = pl.BlockSpec((tm, d), lambda i, j: (i, 0))
    row = pl.BlockSpec((1, d), lambda i, j: (0, 0))
    return pl.pallas_call(
        _ffn_kernel,
        grid=(t // tm, f // tf),
        in_specs=[blk,
                  pl.BlockSpec((d, tf), lambda i, j: (0, j)),
                  pl.BlockSpec((d, tf), lambda i, j: (0, j)),
                  pl.BlockSpec((tf, d), lambda i, j: (j, 0)),
                  row, row],
        out_specs=blk,
        out_shape=jax.ShapeDtypeStruct((t, d), F32),
        compiler_params=_params(("arbitrary", "arbitrary"), VMEM_LIMIT_BIG_TILES),
        name="ffn",
    )(h, wg, wu, wd, g, b)


def _pad_cols(w, n):
    return jnp.pad(w, ((0, 0), (0, n - w.shape[1])))


def _pad_rows(w, n):
    return jnp.pad(w, ((0, n - w.shape[0]), (0, 0)))


def kernel(x, mem, ln_in_g, ln_in_b, w_in, conv_w, conv_b, lru_wa, lru_ba, lru_wx, lru_bx, lru_lambda, rw_mu, rw_w0, rw_wB, rw_a0, rw_aB, rw_gB, rw_kk, rw_ka, rw_rk, rw_gn_g, rw_gn_b, w_out, ln1_g, ln1_b, xa_wq, xa_wk, xa_wv, xa_wo, ln2_g, ln2_b, ffn_wg, ffn_wu, ffn_wd, ln3_g, ln3_b):
    bsz, t, d = x.shape
    depth = w_in.shape[0]
    assert bsz == 1 and d == D_MODEL and t % 512 == 0
    row = lambda p: p.reshape(1, -1)

    h = None
    for l in range(depth):
        w_t = jnp.swapaxes(w_in[l], 0, 1)
        n_cols = IN_BLOCKS * IN_TN
        mu = row(rw_mu[l])
        mu_steps = jnp.concatenate(
            [jnp.zeros((1, 2 * d), F32), mu[:, :3 * d], jnp.zeros((1, 2 * d), F32), mu[:, 3 * d:]], axis=1)
        col_params = jnp.concatenate([
            _pad_cols(mu_steps, n_cols),
            _pad_cols(conv_w[l], n_cols),
            _pad_cols(row(conv_b[l]), n_cols),
            jnp.zeros((CP_ROWS - CP_CONV_B - 1, n_cols), F32)], axis=0)
        if l == 0:
            zin = _in_proj(x[0], row(ln_in_g), row(ln_in_b), w_t, col_params)
        else:
            raise NotImplementedError("DEPTH > 1 is not part of this problem")

        ya = _lru(zin, lru_wa[l].astype(BF16), row(lru_ba[l]),
                  lru_wx[l].astype(BF16), row(lru_bx[l]), row(lru_lambda[l]))

        yb = _rwkv_rec(
            zin, row(rw_w0[l]),
            _pad_rows(rw_wB[l], D_LORA_PAD).astype(BF16), row(rw_a0[l]),
            _pad_rows(rw_aB[l], D_LORA_PAD).astype(BF16), rw_gB[l].astype(BF16),
            row(rw_kk[l]), row(rw_ka[l]), row(rw_rk[l]), row(rw_gn_g[l]), row(rw_gn_b[l]))

        kmem, vmem = _mem_kv(mem[0], xa_wk[l], xa_wv[l])
        h = _mix_xattn(ya, yb, x[0], row(ln_in_g), row(ln_in_b), w_out[l].astype(BF16),
                       row(ln1_g[l]), row(ln1_b[l]), xa_wq[l].astype(BF16), kmem, vmem,
                       xa_wo[l].astype(BF16), row(ln2_g[l]), row(ln2_b[l]))
        h = _ffn(h, ffn_wg[l], ffn_wu[l], ffn_wd[l], row(ln3_g[l]), row(ln3_b[l]))
    return h[None]
```

```python
import jax
import jax.numpy as jnp
from jax import lax
from jax.experimental import pallas as pl
from jax.experimental.pallas import tpu as pltpu

F32 = jnp.float32
BF16 = jnp.bfloat16

D_MODEL = 2048
LN_EPS = 1e-5
ALPHA = 2.0 ** 0.25

LRU_HEADS = 16
LRU_HEAD_DIM = 128
CONV_WIDTH = 4
LRU_C = 8.0

RWKV_HEAD_DIM = 64
D_LORA = 96
D_LORA_PAD = 128
D_GATE_LORA = 256
ZL_WIDTH = 512
GN_EPS = 64e-5
CHUNK = 64
GROUP = 256
N_GROUPS = D_MODEL // GROUP

XATTN_HEADS = 4
XATTN_HEAD_DIM = 512
N_MEM = 256
D_FF = 5632

VMEM_LIMIT = 56 * 1024 * 1024
VMEM_LIMIT_BIG_TILES = 60 * 1024 * 1024


def _dot(a, b):
    return jnp.dot(a, b, preferred_element_type=F32)


def _dot_nt(a, b):
    return lax.dot_general(a, b, (((1,), (1,)), ((), ())), preferred_element_type=F32)


def _dot_tn(a, b):
    return lax.dot_general(a, b, (((0,), (0,)), ((), ())), preferred_element_type=F32)


def _split2(x):
    hi = x.astype(BF16)
    lo = (x - hi.astype(F32)).astype(BF16)
    return hi, lo


def _split3(x):
    hi = x.astype(BF16)
    r1 = x - hi.astype(F32)
    mid = r1.astype(BF16)
    lo = (r1 - mid.astype(F32)).astype(BF16)
    return hi, mid, lo


def _sigmoid(x):
    return 1.0 / (1.0 + jnp.exp(-x))


def _softplus(x):
    return jnp.maximum(x, 0.0) + jnp.log1p(jnp.exp(-jnp.abs(x)))


def _softplus_log(x):
    return jnp.maximum(x, 0.0) + jnp.log(1.0 + jnp.exp(-jnp.abs(x)))


def _gelu_tanh(x):
    c = 0.7978845608028654
    return 0.5 * x * (1.0 + jnp.tanh(c * (x + 0.044715 * (x * x * x))))


def _layer_norm(x, g, b):
    mu = jnp.mean(x, axis=-1, keepdims=True)
    xc = x - mu
    var = jnp.mean(xc * xc, axis=-1, keepdims=True)
    return xc * lax.rsqrt(var + LN_EPS) * g + b


def _shift_rows(z, prev8, s):
    rolled = pltpu.roll(z, s, 0)
    row8 = lax.broadcasted_iota(jnp.int32, prev8.shape, 0)
    head = jnp.where(row8 < s, pltpu.roll(prev8, s, 0), rolled[:8])
    return jnp.concatenate([head, rolled[8:]], axis=0)


def _params(sem, vmem_limit=VMEM_LIMIT):
    return pltpu.CompilerParams(dimension_semantics=sem, vmem_limit_bytes=vmem_limit)


def _resident(shape):
    return pl.BlockSpec(shape, lambda *_: (0,) * len(shape), pipeline_mode=pl.Buffered(1))


IN_TN = 1024
IN_D_BLOCKS = D_MODEL // IN_TN
IN_MAIN_BLOCKS = 5 * IN_D_BLOCKS
IN_GATE_BLOCKS = 2 * IN_D_BLOCKS
IN_BLOCKS = IN_MAIN_BLOCKS + IN_GATE_BLOCKS + 1


def _in_weight_row(j, n_out):
    gates = n_out - 2 * D_MODEL + (j - IN_MAIN_BLOCKS) * IN_TN
    return jnp.where(j < IN_MAIN_BLOCKS, j * IN_TN,
                     jnp.where(j < IN_MAIN_BLOCKS + IN_GATE_BLOCKS, gates, 5 * D_MODEL))


def _in_out_block(j):
    b2 = 2 * D_MODEL // IN_TN
    return jnp.where(j < b2, j,
                     jnp.where(j < IN_MAIN_BLOCKS, j + b2,
                               jnp.where(j < IN_MAIN_BLOCKS + b2, j - (IN_MAIN_BLOCKS - b2), j)))


CP_MU, CP_CONV_W, CP_CONV_B, CP_ROWS = 0, 1, 1 + CONV_WIDTH, 8


def _in_proj_kernel(x_ref, g_ref, b_ref, w_ref, cp_ref, z_ref, hb_ref, carry_ref):
    i = pl.program_id(0)
    j = pl.program_id(1)
    nd = IN_D_BLOCKS

    @pl.when(j == 0)
    def _():
        hb_ref[...] = _layer_norm(x_ref[...], g_ref[...], b_ref[...]).astype(BF16)

    @pl.when((i == 0) & (j == 0))
    def _():
        carry_ref[...] = jnp.zeros_like(carry_ref)

    def keep_tail(z):
        carry_ref[j] = z[z.shape[0] - 8:, :]

    @pl.when(j < nd)
    def _():
        z = _dot_nt(hb_ref[...], w_ref[...].astype(BF16))
        prev8 = carry_ref[j]
        cw = CP_CONV_W + CONV_WIDTH - 1
        conv = cp_ref[CP_CONV_B:CP_CONV_B + 1, :] + z * cp_ref[cw:cw + 1, :]
        for s in range(1, CONV_WIDTH):
            conv = conv + _shift_rows(z, prev8, s) * cp_ref[cw - s:cw - s + 1, :]
        keep_tail(z)
        z_ref[...] = conv

    @pl.when((j >= nd) & (j < 2 * nd))
    def _():
        z_ref[...] = _gelu_tanh(_dot_nt(hb_ref[...], w_ref[...].astype(BF16)))

    def token_shift(z):
        zp = _shift_rows(z, carry_ref[j], 1)
        keep_tail(z)
        return z + (zp - z) * cp_ref[CP_MU:CP_MU + 1, :]

    @pl.when((j >= 2 * nd) & (j < IN_MAIN_BLOCKS))
    def _():
        z_ref[...] = token_shift(_dot_nt(hb_ref[...], w_ref[...].astype(BF16)))

    @pl.when((j >= IN_MAIN_BLOCKS) & (j < IN_MAIN_BLOCKS + IN_GATE_BLOCKS))
    def _():
        z_ref[...] = _sigmoid(_dot_nt(hb_ref[...], w_ref[...].astype(BF16)))

    @pl.when(j >= IN_MAIN_BLOCKS + IN_GATE_BLOCKS)
    def _():
        z_ref[...] = token_shift(_dot_nt(hb_ref[...], w_ref[...].astype(BF16)))


def _in_proj(x, g, b, w_t, col_params, tm=1024):
    t, d = x.shape
    tn = IN_TN
    nb = IN_BLOCKS
    n_out = w_t.shape[0]
    tm = min(tm, t)
    return pl.pallas_call(
        _in_proj_kernel,
        grid=(t // tm, nb),
        in_specs=[
            pl.BlockSpec((tm, d), lambda i, j: (i, 0)),
            pl.BlockSpec((1, d), lambda i, j: (0, 0)),
            pl.BlockSpec((1, d), lambda i, j: (0, 0)),
            pl.BlockSpec((pl.Element(tn), pl.Element(d)),
                         lambda i, j: (pl.multiple_of(_in_weight_row(j, n_out), 8), 0)),
            pl.BlockSpec((CP_ROWS, tn), lambda i, j: (0, j)),
        ],
        out_specs=pl.BlockSpec((tm, tn), lambda i, j: (i, _in_out_block(j))),
        out_shape=jax.ShapeDtypeStruct((t, nb * tn), F32),
        scratch_shapes=[pltpu.VMEM((tm, d), BF16), pltpu.VMEM((nb, 8, tn), F32)],
        compiler_params=_params(("arbitrary", "arbitrary")),
        name="in_proj",
    )(x, g, b, w_t, col_params)


def _lru_kernel(u_ref, gate_ref, ga_ref, wa_ref, ba_ref, wx_ref, bx_ref,
                lam_ref, out_ref, hcarry):
    @pl.when(pl.program_id(0) == 0)
    def _():
        hcarry[...] = jnp.zeros_like(hcarry)

    conv = u_ref[...]
    tm = conv.shape[0]
    cb16 = conv.astype(BF16)
    r_parts, i_parts = [], []
    for g in range(LRU_HEADS):
        ug = cb16[:, g * LRU_HEAD_DIM:(g + 1) * LRU_HEAD_DIM]
        r_parts.append(_dot(ug, wa_ref[g]))
        i_parts.append(_dot(ug, wx_ref[g]))
    r = _sigmoid(jnp.concatenate(r_parts, axis=1) + ba_ref[...])
    ig = _sigmoid(jnp.concatenate(i_parts, axis=1) + bx_ref[...])

    log_a = (-LRU_C) * r * _softplus(-lam_ref[...])
    a_c = jnp.exp(log_a)
    om = -jnp.tanh(log_a) * (a_c * a_c + 1.0)
    b_c = jnp.where(om > 0.0, om * lax.rsqrt(om), 0.0) * (ig * conv)

    n8 = tm // 8
    a3 = a_c.reshape(n8, 8, a_c.shape[1])
    b3 = b_c.reshape(n8, 8, b_c.shape[1])
    sub = lax.broadcasted_iota(jnp.int32, a3.shape, 1)
    for d in (1, 2, 4):
        m = sub >= d
        a_sh = jnp.where(m, pltpu.roll(a3, d, 1), 1.0)
        b_sh = jnp.where(m, pltpu.roll(b3, d, 1), 0.0)
        b3 = a3 * b_sh + b3
        a3 = a3 * a_sh
    carry = hcarry[...]
    hs = []
    for i in range(n8):
        h_i = b3[i] + a3[i] * carry
        hs.append(h_i)
        carry = h_i[7:8, :]
    hcarry[...] = carry
    h = jnp.concatenate(hs, axis=0)

    out_ref[...] = ga_ref[...] * (gate_ref[...] * h)


def _lru(zin, wa, ba, wx, bx, lam, tm=256):
    t = zin.shape[0]
    d = D_MODEL
    row = lambda i: (0, 0)
    return pl.pallas_call(
        _lru_kernel,
        grid=(t // tm,),
        in_specs=[
            pl.BlockSpec((tm, d), lambda i: (i, 0)),
            pl.BlockSpec((tm, d), lambda i: (i, 1)),
            pl.BlockSpec((tm, d), lambda i: (i, 2)),
            pl.BlockSpec((LRU_HEADS, LRU_HEAD_DIM, LRU_HEAD_DIM), lambda i: (0, 0, 0)),
            pl.BlockSpec((1, d), row),
            pl.BlockSpec((LRU_HEADS, LRU_HEAD_DIM, LRU_HEAD_DIM), lambda i: (0, 0, 0)),
            pl.BlockSpec((1, d), row),
            pl.BlockSpec((1, d), row),
        ],
        out_specs=pl.BlockSpec((tm, d), lambda i: (i, 0)),
        out_shape=jax.ShapeDtypeStruct((t, d), F32),
        scratch_shapes=[pltpu.VMEM((1, d), F32)],
        compiler_params=_params(("arbitrary",)),
        name="lru",
    )(zin, zin, zin, wa, ba, wx, bx, lam)


def _block_ones(n, seg):
    r = lax.broadcasted_iota(jnp.int32, (n, n), 0) // seg
    c = lax.broadcasted_iota(jnp.int32, (n, n), 1) // seg
    return jnp.where(r == c, 1.0, 0.0).astype(BF16)


REC_ROWS = 256


def _round_robin(*gens):
    live = list(gens)
    while live:
        for gen in list(live):
            try:
                next(gen)
            except StopIteration:
                live.remove(gen)
        yield


def _interleave(*gens):
    for _ in _round_robin(*gens):
        pass


def _staggered(gens, lag):
    live = dict(enumerate(gens))
    rnd = 0
    while live:
        for i in sorted(live):
            if rnd >= i * lag:
                try:
                    next(live[i])
                except StopIteration:
                    del live[i]
        rnd += 1
        yield


def _rwkv_rec_kernel(r_ref, k_ref, v_ref, zl_ref, zgb_ref, w0_ref, wb_ref, a0_ref, ab_ref, gb_ref,
                     kk_ref, ka_ref, rk_ref, gng_ref, gnb_ref, out_ref, s_ref):
    @pl.when(pl.program_id(0) == 0)
    def _():
        s_ref[...] = jnp.zeros_like(s_ref)

    c = CHUNK
    hd = RWKV_HEAD_DIM
    ng = N_GROUPS
    nch = r_ref.shape[0] // c
    sls = [slice(g * GROUP, (g + 1) * GROUP) for g in range(ng)]

    assert nch % 2 == 0
    rows = lax.broadcasted_iota(jnp.int32, (2 * c, 2 * c), 0)
    cols = lax.broadcasted_iota(jnp.int32, (2 * c, 2 * c), 1)
    tri2 = jnp.where((cols <= rows) & (cols // c == rows // c), 1.0, 0.0).astype(BF16)
    t_c = lax.broadcasted_iota(jnp.int32, (c, GROUP), 0)
    s_c = lax.broadcasted_iota(jnp.int32, (c, GROUP), 1) % hd
    strict = s_c < t_c
    incl = s_c <= t_c
    eye_c = jnp.where(s_c == t_c, 1.0, 0.0)
    bd = (lax.broadcasted_iota(jnp.int32, (GROUP, GROUP), 0) // hd
          == lax.broadcasted_iota(jnp.int32, (GROUP, GROUP), 1) // hd)
    ones_bd = _block_ones(GROUP, hd)

    def expand(x_c):
        return jnp.where(bd, jnp.concatenate([x_c] * (GROUP // hd), axis=0), 0.0).astype(BF16)

    def seg_sums(xs):
        hi, lo = _split2(jnp.concatenate(xs, axis=0))
        s = _dot(jnp.concatenate([hi, lo], axis=0), ones_bd)
        n = c * len(xs)
        s = s[:n] + s[n:]
        return [s[i * c:(i + 1) * c] for i in range(len(xs))]

    state = [s_ref[g] for g in range(ng)]
    prep = [None] * nch

    pair_prep = {}

    def lora_stage(pi):
        rs_ = slice(2 * pi * c, (2 * pi + 2) * c)
        zl = zl_ref[rs_, :]
        nl = zl.shape[1]
        w_lo = zl[:, 0:D_LORA_PAD]
        a_lo = pltpu.roll(zl, nl - D_LORA, 1)[:, 0:D_LORA_PAD]
        g_lo = pltpu.roll(zl, nl - 2 * D_LORA, 1)[:, 0:D_GATE_LORA]
        w_log = -_softplus_log(-(w0_ref[...] + _dot(jnp.tanh(w_lo).astype(BF16), wb_ref[...]))) - 0.5
        lw = -jnp.exp(w_log)
        a = _sigmoid(a0_ref[...] + _dot(a_lo.astype(BF16), ab_ref[...]))
        gate = _dot(_sigmoid(g_lo).astype(BF16), gb_ref[...])
        hi, mid, lo = _split3(lw)
        cum = _dot(tri2, hi) + _dot(tri2, mid) + _dot(tri2, lo)
        pair_prep[pi] = dict(lw=lw, a=a, gate=gate, cum=cum)
        yield

    def phase_a(ci):
        rs_ = slice(ci * c, (ci + 1) * c)
        half = slice((ci % 2) * c, (ci % 2 + 1) * c)
        pp = pair_prep[ci // 2]
        lw, a, gate, cum = pp["lw"][half], pp["a"][half], pp["gate"][half], pp["cum"][half]
        tot = cum[c - 1:c, :]
        k = k_ref[rs_, :]
        kk = k * kk_ref[...]
        kk2 = kk * kk
        n2 = jnp.concatenate(seg_sums([kk2[:, sl] for sl in sls]), axis=1)
        kn = kk * lax.rsqrt(jnp.maximum(n2, 1e-24))
        bb = kn * a
        kf = k * (1.0 + (a - 1.0) * ka_ref[...])
        yield
        r = r_ref[rs_, :]
        v = v_ref[rs_, :]
        p_inv = jnp.exp(-cum)
        p_end = jnp.exp(tot - cum)
        rq = r * jnp.exp(cum)
        kap = kn * jnp.exp(cum - lw)
        bet = bb * p_inv
        kt = kf * p_inv
        lhs2 = [jnp.concatenate([kap[:, sl], rq[:, sl]], axis=0).astype(BF16) for sl in sls]
        amat = [_dot_nt(lhs2[g], jnp.concatenate([expand(bet[:, sls[g]]), expand(kt[:, sls[g]])], axis=0))
                for g in range(ng)]
        yield
        l_c = [jnp.where(strict, a[:c, :GROUP], 0.0) for a in amat]
        a_lo = [jnp.concatenate([jnp.where(strict, a[:c, GROUP:], 0.0),
                                 jnp.where(incl, a[c:, GROUP:], 0.0)], axis=0).astype(BF16) for a in amat]
        arb = [jnp.where(incl, a[c:, :GROUP], 0.0).astype(BF16) for a in amat]
        av = [_dot(a_lo[g], expand(v[:, sls[g]])) for g in range(ng)]
        x_c = [eye_c - jnp.where((t_c >> 1) == (s_c >> 1), l, 0.0) for l in l_c]
        for lvl in range(2, 7):
            lmask = ((t_c >> lvl) == (s_c >> lvl)) & ((t_c >> (lvl - 1)) != (s_c >> (lvl - 1)))
            y_c = [_dot(x_c[g].astype(BF16), expand(jnp.where(lmask, l_c[g], 0.0))) for g in range(ng)]
            yield
            x_c = [x_c[g] - _dot(y_c[g].astype(BF16), expand(x_c[g])) for g in range(ng)]
            yield
        bhat_neg = -(bb * p_end)
        khat = kf * p_end
        upd_rhs = [jnp.concatenate([bhat_neg[:, sl], khat[:, sl]], axis=0).astype(BF16) for sl in sls]
        prep[ci] = dict(lhs2=lhs2, av=av, arb=arb, x=[x.astype(BF16) for x in x_c], v=v,
                        upd_rhs=upd_rhs, p_tot=jnp.exp(tot), rkk=r * kf * rk_ref[...], gate=gate)

    def phase_b(ci):
        p = prep[ci]
        rs_ = slice(ci * c, (ci + 1) * c)
        v = p["v"]
        rs = [_dot_nt(p["lhs2"][g], state[g].astype(BF16)) for g in range(ng)]
        yield
        u_c = [_dot(p["x"][g], expand(rs[g][:c] + p["av"][g][:c])) for g in range(ng)]
        yield
        o_c = [rs[g][c:] + p["av"][g][c:] - _dot(p["arb"][g], expand(u_c[g])) for g in range(ng)]
        for g in range(ng):
            upd = _dot_tn(jnp.concatenate([u_c[g], v[:, sls[g]]], axis=0).astype(BF16), p["upd_rhs"][g])
            state[g] = state[g] * p["p_tot"][:, sls[g]] + jnp.where(bd, upd, 0.0)
        yield
        sums = seg_sums(o_c + [p["rkk"][:, sl] for sl in sls])
        dev = [o_c[g] - sums[g] * (1.0 / hd) for g in range(ng)]
        yield
        var = seg_sums([dv * dv for dv in dev])
        o_parts = []
        for g in range(ng):
            o_n = dev[g] * lax.rsqrt(var[g] * (1.0 / hd) + GN_EPS) * gng_ref[:, sls[g]] + gnb_ref[:, sls[g]]
            o_parts.append(o_n + sums[ng + g] * v[:, sls[g]])
        o = jnp.concatenate(o_parts, axis=1)
        out_ref[rs_, :] = zgb_ref[rs_, :] * (o * p["gate"])

    def pair_a(pi):
        yield from lora_stage(pi)
        yield from _round_robin(phase_a(2 * pi), phase_a(2 * pi + 1))

    n_pairs = nch // 2
    _interleave(pair_a(0))
    for k in range(n_pairs):
        b_gen = _staggered([phase_b(2 * k), phase_b(2 * k + 1)], lag=3)
        _interleave(b_gen, *([pair_a(k + 1)] if k + 1 < n_pairs else []))

    for g in range(ng):
        s_ref[g] = state[g]


def _rwkv_rec(zin, w0, wb, a0, ab, gb, k_k, k_a, rk, gng, gnb):
    t = zin.shape[0]
    d = D_MODEL
    dl = ZL_WIDTH
    br = min(REC_ROWS, t)
    zcol = lambda c: pl.BlockSpec((br, d), lambda i: (i, c))
    row = pl.BlockSpec((1, d), lambda i: (0, 0))
    return pl.pallas_call(
        _rwkv_rec_kernel,
        grid=(t // br,),
        in_specs=[zcol(4), zcol(5), zcol(6), pl.BlockSpec((br, dl), lambda i: (i, 7 * d // dl)), zcol(3),
                  row, _resident((D_LORA_PAD, d)), row, _resident((D_LORA_PAD, d)),
                  _resident((D_GATE_LORA, d)), row, row, row, row, row],
        out_specs=pl.BlockSpec((br, d), lambda i: (i, 0)),
        out_shape=jax.ShapeDtypeStruct((t, d), F32),
        scratch_shapes=[pltpu.VMEM((N_GROUPS, GROUP, GROUP), F32)],
        compiler_params=_params(("arbitrary",), VMEM_LIMIT_BIG_TILES),
        name="rwkv_rec",
    )(zin, zin, zin, zin, zin, w0, wb, a0, ab, gb, k_k, k_a, rk, gng, gnb)


def _mem_kv_kernel(mem_ref, wk_ref, wv_ref, k_ref, v_ref):
    m = mem_ref[...].astype(BF16)
    k_ref[...] = _dot(m, wk_ref[...].astype(BF16)).astype(BF16)
    v_ref[...] = _dot(m, wv_ref[...].astype(BF16)).astype(BF16)


def _mem_kv(mem, wk, wv, tn=512):
    n, d = mem.shape
    return pl.pallas_call(
        _mem_kv_kernel,
        grid=(d // tn,),
        in_specs=[pl.BlockSpec((n, d), lambda j: (0, 0)),
                  pl.BlockSpec((d, tn), lambda j: (0, j)),
                  pl.BlockSpec((d, tn), lambda j: (0, j))],
        out_specs=[pl.BlockSpec((n, tn), lambda j: (0, j))] * 2,
        out_shape=[jax.ShapeDtypeStruct((n, d), BF16)] * 2,
        compiler_params=_params(("arbitrary",)),
        name="mem_kv",
    )(mem, wk, wv)


def _mix_xattn_kernel(ya_ref, yb_ref, x_ref, gin_ref, bin_ref, wout_ref, g1_ref, b1_ref,
                      wq_ref, k_ref, v_ref, wo_ref, g_ref, b_ref, o_ref):
    y = (ya_ref[...] + yb_ref[...]).astype(BF16)
    mix = _dot(y, wout_ref[...])
    h0 = _layer_norm(x_ref[...], gin_ref[...], bin_ref[...])
    h = _layer_norm(ALPHA * h0 + mix, g1_ref[...], b1_ref[...])
    q = _dot(h.astype(BF16), wq_ref[...]).astype(BF16)
    scale = XATTN_HEAD_DIM ** -0.5
    outs = []
    for hh in range(XATTN_HEADS):
        sl = slice(hh * XATTN_HEAD_DIM, (hh + 1) * XATTN_HEAD_DIM)
        s = _dot_nt(q[:, sl], k_ref[:, sl]) * scale
        e = jnp.exp(s - jnp.max(s, axis=-1, keepdims=True))
        p = e / jnp.sum(e, axis=-1, keepdims=True)
        outs.append(_dot(p.astype(BF16), v_ref[:, sl]))
    o = jnp.concatenate(outs, axis=1).astype(BF16)
    xa = _dot(o, wo_ref[...])
    o_ref[...] = _layer_norm(ALPHA * h + xa, g_ref[...], b_ref[...])


def _mix_xattn(ya, yb, x, g_in, b_in, w_out, g1, b1, wq, kmem, vmem, wo, g2, b2, tm=256):
    t, d = x.shape
    n = kmem.shape[0]
    blk = pl.BlockSpec((tm, d), lambda i: (i, 0))
    row = pl.BlockSpec((1, d), lambda i: (0, 0))
    return pl.pallas_call(
        _mix_xattn_kernel,
        grid=(t // tm,),
        in_specs=[blk, blk, blk, row, row, _resident((d, d)), row, row,
                  _resident((d, d)), _resident((n, d)), _resident((n, d)), _resident((d, d)), row, row],
        out_specs=blk,
        out_shape=jax.ShapeDtypeStruct((t, d), F32),
        compiler_params=_params(("arbitrary",)),
        name="mix_xattn",
    )(ya, yb, x, g_in, b_in, w_out, g1, b1, wq, kmem, vmem, wo, g2, b2)


def _ffn_kernel(h_ref, wg_ref, wu_ref, wd_ref, g_ref, b_ref, o_ref):
    j = pl.program_id(1)

    @pl.when(j == 0)
    def _():
        o_ref[...] = jnp.zeros_like(o_ref)

    hb = h_ref[...].astype(BF16)
    gate = _dot(hb, wg_ref[...].astype(BF16))
    up = _dot(hb, wu_ref[...].astype(BF16))
    act = (gate * _sigmoid(gate) * up).astype(BF16)
    o_ref[...] += _dot(act, wd_ref[...].astype(BF16))

    @pl.when(j == pl.num_programs(1) - 1)
    def _():
        o_ref[...] = _layer_norm(ALPHA * h_ref[...] + o_ref[...], g_ref[...], b_ref[...])


def _ffn(h, wg, wu, wd, g, b, tm=1024, tf=256):
    t, d = h.shape
    f = wg.shape[1]
    tm = min(tm, t)
    blk = pl.BlockSpec((tm, d), lambda i, j: (i, 0))
    row = pl.BlockSpec((1, d), lambda i, j: (0, 0))
    return pl.pallas_call(
        _ffn_kernel,
        grid=(t // tm, f // tf),
        in_specs=[blk,
                  pl.BlockSpec((d, tf), lambda i, j: (0, j)),
                  pl.BlockSpec((d, tf), lambda i, j: (0, j)),
                  pl.BlockSpec((tf, d), lambda i, j: (j, 0)),
                  row, row],
        out_specs=blk,
        out_shape=jax.ShapeDtypeStruct((t, d), F32),
        compiler_params=_params(("arbitrary", "arbitrary"), VMEM_LIMIT_BIG_TILES),
        name="ffn",
    )(h, wg, wu, wd, g, b)


def _pad_cols(w, n):
    return jnp.pad(w, ((0, 0), (0, n - w.shape[1])))


def _pad_rows(w, n):
    return jnp.pad(w, ((0, n - w.shape[0]), (0, 0)))


def kernel(x, mem, ln_in_g, ln_in_b, w_in, conv_w, conv_b, lru_wa, lru_ba, lru_wx, lru_bx, lru_lambda, rw_mu, rw_w0, rw_wB, rw_a0, rw_aB, rw_gB, rw_kk, rw_ka, rw_rk, rw_gn_g, rw_gn_b, w_out, ln1_g, ln1_b, xa_wq, xa_wk, xa_wv, xa_wo, ln2_g, ln2_b, ffn_wg, ffn_wu, ffn_wd, ln3_g, ln3_b):
    bsz, t, d = x.shape
    depth = w_in.shape[0]
    assert bsz == 1 and d == D_MODEL and t % 512 == 0
    row = lambda p: p.reshape(1, -1)

    h = None
    for l in range(depth):
        w_t = jnp.swapaxes(w_in[l], 0, 1)
        n_cols = IN_BLOCKS * IN_TN
        mu = row(rw_mu[l])
        mu_steps = jnp.concatenate(
            [jnp.zeros((1, 2 * d), F32), mu[:, :3 * d], jnp.zeros((1, 2 * d), F32), mu[:, 3 * d:]], axis=1)
        col_params = jnp.concatenate([
            _pad_cols(mu_steps, n_cols),
            _pad_cols(conv_w[l], n_cols),
            _pad_cols(row(conv_b[l]), n_cols),
            jnp.zeros((CP_ROWS - CP_CONV_B - 1, n_cols), F32)], axis=0)
        if l == 0:
            zin = _in_proj(x[0], row(ln_in_g), row(ln_in_b), w_t, col_params)
        else:
            raise NotImplementedError("DEPTH > 1 is not part of this problem")

        ya = _lru(zin, lru_wa[l].astype(BF16), row(lru_ba[l]),
                  lru_wx[l].astype(BF16), row(lru_bx[l]), row(lru_lambda[l]))

        yb = _rwkv_rec(
            zin, row(rw_w0[l]),
            _pad_rows(rw_wB[l], D_LORA_PAD).astype(BF16), row(rw_a0[l]),
            _pad_rows(rw_aB[l], D_LORA_PAD).astype(BF16), rw_gB[l].astype(BF16),
            row(rw_kk[l]), row(rw_ka[l]), row(rw_rk[l]), row(rw_gn_g[l]), row(rw_gn_b[l]))

        kmem, vmem = _mem_kv(mem[0], xa_wk[l], xa_wv[l])
        h = _mix_xattn(ya, yb, x[0], row(ln_in_g), row(ln_in_b), w_out[l].astype(BF16),
                       row(ln1_g[l]), row(ln1_b[l]), xa_wq[l].astype(BF16), kmem, vmem,
                       xa_wo[l].astype(BF16), row(ln2_g[l]), row(ln2_b[l]))
        h = _ffn(h, ffn_wg[l], ffn_wu[l], ffn_wd[l], row(ln3_g[l]), row(ln3_b[l]))
    return h[None]
```

```python
import jax
import jax.numpy as jnp
from jax import lax
from jax.experimental import pallas as pl
from jax.experimental.pallas import tpu as pltpu

F32 = jnp.float32
BF16 = jnp.bfloat16

D_MODEL = 2048
LN_EPS = 1e-5
ALPHA = 2.0 ** 0.25

LRU_HEADS = 16
LRU_HEAD_DIM = 128
CONV_WIDTH = 4
LRU_C = 8.0

RWKV_HEAD_DIM = 64
D_LORA = 96
D_LORA_PAD = 128
D_GATE_LORA = 256
ZL_WIDTH = 512
GN_EPS = 64e-5
CHUNK = 64
GROUP = 256
N_GROUPS = D_MODEL // GROUP

XATTN_HEADS = 4
XATTN_HEAD_DIM = 512

VMEM_LIMIT = 56 * 1024 * 1024
VMEM_LIMIT_BIG_TILES = 60 * 1024 * 1024


def _dot(a, b):
    return jnp.dot(a, b, preferred_element_type=F32)


def _dot_nt(a, b):
    return lax.dot_general(a, b, (((1,), (1,)), ((), ())), preferred_element_type=F32)


def _dot_tn(a, b):
    return lax.dot_general(a, b, (((0,), (0,)), ((), ())), preferred_element_type=F32)


def _split2(x):
    hi = x.astype(BF16)
    lo = (x - hi.astype(F32)).astype(BF16)
    return hi, lo


def _split3(x):
    hi = x.astype(BF16)
    r1 = x - hi.astype(F32)
    mid = r1.astype(BF16)
    lo = (r1 - mid.astype(F32)).astype(BF16)
    return hi, mid, lo


def _sigmoid(x):
    return 1.0 / (1.0 + jnp.exp(-x))


def _softplus(x):
    return jnp.maximum(x, 0.0) + jnp.log1p(jnp.exp(-jnp.abs(x)))


def _softplus_log(x):
    return jnp.maximum(x, 0.0) + jnp.log(1.0 + jnp.exp(-jnp.abs(x)))


def _gelu_tanh(x):
    c = 0.7978845608028654
    return 0.5 * x * (1.0 + jnp.tanh(c * (x + 0.044715 * (x * x * x))))


def _layer_norm(x, g, b):
    mu = jnp.mean(x, axis=-1, keepdims=True)
    xc = x - mu
    var = jnp.mean(xc * xc, axis=-1, keepdims=True)
    return xc * lax.rsqrt(var + LN_EPS) * g + b


def _shift_rows(z, prev8, s):
    rolled = pltpu.roll(z, s, 0)
    row8 = lax.broadcasted_iota(jnp.int32, prev8.shape, 0)
    head = jnp.where(row8 < s, pltpu.roll(prev8, s, 0), rolled[:8])
    return jnp.concatenate([head, rolled[8:]], axis=0)


def _params(sem, vmem_limit=VMEM_LIMIT):
    return pltpu.CompilerParams(dimension_semantics=sem, vmem_limit_bytes=vmem_limit)


def _resident(shape):
    return pl.BlockSpec(shape, lambda *_: (0,) * len(shape), pipeline_mode=pl.Buffered(1))


IN_TN = 1024
IN_D_BLOCKS = D_MODEL // IN_TN
IN_MAIN_BLOCKS = 5 * IN_D_BLOCKS
IN_GATE_BLOCKS = 2 * IN_D_BLOCKS
IN_BLOCKS = IN_MAIN_BLOCKS + IN_GATE_BLOCKS + 1


def _in_weight_row(j, n_out):
    gates = n_out - 2 * D_MODEL + (j - IN_MAIN_BLOCKS) * IN_TN
    return jnp.where(j < IN_MAIN_BLOCKS, j * IN_TN,
                     jnp.where(j < IN_MAIN_BLOCKS + IN_GATE_BLOCKS, gates, 5 * D_MODEL))


def _in_out_block(j):
    b2 = 2 * D_MODEL // IN_TN
    return jnp.where(j < b2, j,
                     jnp.where(j < IN_MAIN_BLOCKS, j + b2,
                               jnp.where(j < IN_MAIN_BLOCKS + b2, j - (IN_MAIN_BLOCKS - b2), j)))


CP_MU, CP_CONV_W, CP_CONV_B, CP_ROWS = 0, 1, 1 + CONV_WIDTH, 8


def _in_proj_kernel(x_ref, g_ref, b_ref, w_ref, cp_ref, z_ref, hb_ref, carry_ref):
    i = pl.program_id(0)
    j = pl.program_id(1)
    nd = IN_D_BLOCKS

    @pl.when(j == 0)
    def _():
        hb_ref[...] = _layer_norm(x_ref[...], g_ref[...], b_ref[...]).astype(BF16)

    @pl.when((i == 0) & (j == 0))
    def _():
        carry_ref[...] = jnp.zeros_like(carry_ref)

    def keep_tail(z):
        carry_ref[j] = z[z.shape[0] - 8:, :]

    @pl.when(j < nd)
    def _():
        z = _dot_nt(hb_ref[...], w_ref[...].astype(BF16))
        prev8 = carry_ref[j]
        cw = CP_CONV_W + CONV_WIDTH - 1
        conv = cp_ref[CP_CONV_B:CP_CONV_B + 1, :] + z * cp_ref[cw:cw + 1, :]
        for s in range(1, CONV_WIDTH):
            conv = conv + _shift_rows(z, prev8, s) * cp_ref[cw - s:cw - s + 1, :]
        keep_tail(z)
        z_ref[...] = conv

    @pl.when((j >= nd) & (j < 2 * nd))
    def _():
        z_ref[...] = _gelu_tanh(_dot_nt(hb_ref[...], w_ref[...].astype(BF16)))

    def token_shift(z):
        n = z.shape[1]
        zp = _shift_rows(z, carry_ref[j][:, 0:n], 1)
        carry_ref[j, :, 0:n] = z[z.shape[0] - 8:, :]
        return z + (zp - z) * cp_ref[CP_MU:CP_MU + 1, 0:n]

    @pl.when((j >= 2 * nd) & (j < IN_MAIN_BLOCKS))
    def _():
        z_ref[...] = token_shift(_dot_nt(hb_ref[...], w_ref[...].astype(BF16)))

    @pl.when((j >= IN_MAIN_BLOCKS) & (j < IN_MAIN_BLOCKS + IN_GATE_BLOCKS))
    def _():
        z_ref[...] = _sigmoid(_dot_nt(hb_ref[...], w_ref[...].astype(BF16)))

    @pl.when(j >= IN_MAIN_BLOCKS + IN_GATE_BLOCKS)
    def _():
        z_ref[:, 0:ZL_WIDTH] = token_shift(_dot_nt(hb_ref[...], w_ref[0:ZL_WIDTH, :].astype(BF16)))
        z_ref[:, ZL_WIDTH:] = jnp.zeros((z_ref.shape[0], z_ref.shape[1] - ZL_WIDTH), F32)


def _in_proj(x, g, b, w_t, col_params, tm=1024):
    t, d = x.shape
    tn = IN_TN
    nb = IN_BLOCKS
    n_out = w_t.shape[0]
    tm = min(tm, t)
    return pl.pallas_call(
        _in_proj_kernel,
        grid=(t // tm, nb),
        in_specs=[
            pl.BlockSpec((tm, d), lambda i, j: (i, 0)),
            pl.BlockSpec((1, d), lambda i, j: (0, 0)),
            pl.BlockSpec((1, d), lambda i, j: (0, 0)),
            pl.BlockSpec((pl.Element(tn), pl.Element(d)),
                         lambda i, j: (pl.multiple_of(_in_weight_row(j, n_out), 8), 0)),
            pl.BlockSpec((CP_ROWS, tn), lambda i, j: (0, j)),
        ],
        out_specs=pl.BlockSpec((tm, tn), lambda i, j: (i, _in_out_block(j))),
        out_shape=jax.ShapeDtypeStruct((t, nb * tn), F32),
        scratch_shapes=[pltpu.VMEM((tm, d), BF16), pltpu.VMEM((nb, 8, tn), F32)],
        compiler_params=_params(("arbitrary", "arbitrary")),
        name="in_proj",
    )(x, g, b, w_t, col_params)


def _lru_kernel(u_ref, gate_ref, ga_ref, wa_ref, ba_ref, wx_ref, bx_ref,
                lam_ref, out_ref, hcarry):
    @pl.when(pl.program_id(0) == 0)
    def _():
        hcarry[...] = jnp.zeros_like(hcarry)

    conv = u_ref[...]
    tm = conv.shape[0]
    cb16 = conv.astype(BF16)
    r_parts, i_parts = [], []
    for g in range(LRU_HEADS):
        ug = cb16[:, g * LRU_HEAD_DIM:(g + 1) * LRU_HEAD_DIM]
        r_parts.append(_dot(ug, wa_ref[g]))
        i_parts.append(_dot(ug, wx_ref[g]))
    r = _sigmoid(jnp.concatenate(r_parts, axis=1) + ba_ref[...])
    ig = _sigmoid(jnp.concatenate(i_parts, axis=1) + bx_ref[...])

    log_a = (-LRU_C) * r * _softplus(-lam_ref[...])
    a_c = jnp.exp(log_a)
    om = -jnp.tanh(log_a) * (a_c * a_c + 1.0)
    b_c = jnp.where(om > 0.0, om * lax.rsqrt(om), 0.0) * (ig * conv)

    n8 = tm // 8
    a3 = a_c.reshape(n8, 8, a_c.shape[1])
    b3 = b_c.reshape(n8, 8, b_c.shape[1])
    sub = lax.broadcasted_iota(jnp.int32, a3.shape, 1)
    for d in (1, 2, 4):
        m = sub >= d
        a_sh = jnp.where(m, pltpu.roll(a3, d, 1), 1.0)
        b_sh = jnp.where(m, pltpu.roll(b3, d, 1), 0.0)
        b3 = a3 * b_sh + b3
        a3 = a3 * a_sh
    carry = hcarry[...]
    hs = []
    for i in range(n8):
        h_i = b3[i] + a3[i] * carry
        hs.append(h_i)
        carry = h_i[7:8, :]
    hcarry[...] = carry
    h = jnp.concatenate(hs, axis=0)

    out_ref[...] = ga_ref[...] * (gate_ref[...] * h)


def _lru(zin, wa, ba, wx, bx, lam, tm=256):
    t = zin.shape[0]
    d = D_MODEL
    row = lambda i: (0, 0)
    return pl.pallas_call(
        _lru_kernel,
        grid=(t // tm,),
        in_specs=[
            pl.BlockSpec((tm, d), lambda i: (i, 0)),
            pl.BlockSpec((tm, d), lambda i: (i, 1)),
            pl.BlockSpec((tm, d), lambda i: (i, 2)),
            pl.BlockSpec((LRU_HEADS, LRU_HEAD_DIM, LRU_HEAD_DIM), lambda i: (0, 0, 0)),
            pl.BlockSpec((1, d), row),
            pl.BlockSpec((LRU_HEADS, LRU_HEAD_DIM, LRU_HEAD_DIM), lambda i: (0, 0, 0)),
            pl.BlockSpec((1, d), row),
            pl.BlockSpec((1, d), row),
        ],
        out_specs=pl.BlockSpec((tm, d), lambda i: (i, 0)),
        out_shape=jax.ShapeDtypeStruct((t, d), F32),
        scratch_shapes=[pltpu.VMEM((1, d), F32)],
        compiler_params=_params(("arbitrary",)),
        name="lru",
    )(zin, zin, zin, wa, ba, wx, bx, lam)


def _block_ones(n, seg):
    r = lax.broadcasted_iota(jnp.int32, (n, n), 0) // seg
    c = lax.broadcasted_iota(jnp.int32, (n, n), 1) // seg
    return jnp.where(r == c, 1.0, 0.0).astype(BF16)


REC_ROWS = 256


def _round_robin(*gens):
    live = list(gens)
    while live:
        for gen in list(live):
            try:
                next(gen)
            except StopIteration:
                live.remove(gen)
        yield


def _interleave(*gens):
    for _ in _round_robin(*gens):
        pass


def _chain(*gens):
    for gen in gens:
        yield from gen


def _rwkv_rec_kernel(r_ref, k_ref, v_ref, zl_ref, zgb_ref, w0_ref, wb_ref, a0_ref, ab_ref, gb_ref,
                     kk_ref, ka_ref, rk_ref, gng_ref, gnb_ref, out_ref, s_ref):
    @pl.when(pl.program_id(0) == 0)
    def _():
        s_ref[...] = jnp.zeros_like(s_ref)

    c = CHUNK
    hd = RWKV_HEAD_DIM
    ng = N_GROUPS
    nch = r_ref.shape[0] // c
    sls = [slice(g * GROUP, (g + 1) * GROUP) for g in range(ng)]

    assert nch % 2 == 0
    rows = lax.broadcasted_iota(jnp.int32, (2 * c, 2 * c), 0)
    cols = lax.broadcasted_iota(jnp.int32, (2 * c, 2 * c), 1)
    tri2 = jnp.where((cols <= rows) & (cols // c == rows // c), 1.0, 0.0).astype(BF16)
    t_c = lax.broadcasted_iota(jnp.int32, (c, GROUP), 0)
    s_c = lax.broadcasted_iota(jnp.int32, (c, GROUP), 1) % hd
    strict = s_c < t_c
    incl = s_c <= t_c
    eye_c = jnp.where(s_c == t_c, 1.0, 0.0)
    bd = (lax.broadcasted_iota(jnp.int32, (GROUP, GROUP), 0) // hd
          == lax.broadcasted_iota(jnp.int32, (GROUP, GROUP), 1) // hd)
    ones_bd = _block_ones(GROUP, hd)

    def expand(x_c):
        return jnp.where(bd, jnp.concatenate([x_c] * (GROUP // hd), axis=0), 0.0).astype(BF16)

    def seg_sums(xs):
        hi, lo = _split2(jnp.concatenate(xs, axis=0))
        s = _dot(jnp.concatenate([hi, lo], axis=0), ones_bd)
        n = c * len(xs)
        s = s[:n] + s[n:]
        return [s[i * c:(i + 1) * c] for i in range(len(xs))]

    state = [s_ref[g] for g in range(ng)]
    prep = [None] * nch

    pair_prep = {}

    def lora_stage(pi):
        rs_ = slice(2 * pi * c, (2 * pi + 2) * c)
        zl = zl_ref[rs_, :]
        nl = zl.shape[1]
        w_lo = zl[:, 0:D_LORA_PAD]
        a_lo = pltpu.roll(zl, nl - D_LORA, 1)[:, 0:D_LORA_PAD]
        g_lo = pltpu.roll(zl, nl - 2 * D_LORA, 1)[:, 0:D_GATE_LORA]
        w_log = -_softplus_log(-(w0_ref[...] + _dot(jnp.tanh(w_lo).astype(BF16), wb_ref[...]))) - 0.5
        lw = -jnp.exp(w_log)
        a = _sigmoid(a0_ref[...] + _dot(a_lo.astype(BF16), ab_ref[...]))
        gate = _dot(_sigmoid(g_lo).astype(BF16), gb_ref[...])
        hi, mid, lo = _split3(lw)
        cum = _dot(tri2, hi) + _dot(tri2, mid) + _dot(tri2, lo)
        pair_prep[pi] = dict(lw=lw, a=a, gate=gate, cum=cum)
        yield

    def phase_a(ci):
        rs_ = slice(ci * c, (ci + 1) * c)
        half = slice((ci % 2) * c, (ci % 2 + 1) * c)
        pp = pair_prep[ci // 2]
        lw, a, gate, cum = pp["lw"][half], pp["a"][half], pp["gate"][half], pp["cum"][half]
        tot = cum[c - 1:c, :]
        k = k_ref[rs_, :]
        kk = k * kk_ref[...]
        kk2 = kk * kk
        n2 = jnp.concatenate(seg_sums([kk2[:, sl] for sl in sls]), axis=1)
        kn = kk * lax.rsqrt(jnp.maximum(n2, 1e-24))
        bb = kn * a
        kf = k * (1.0 + (a - 1.0) * ka_ref[...])
        yield
        r = r_ref[rs_, :]
        v = v_ref[rs_, :]
        p_inv = jnp.exp(-cum)
        p_end = jnp.exp(tot - cum)
        rq = r * jnp.exp(cum)
        kap = kn * jnp.exp(cum - lw)
        bet = bb * p_inv
        kt = kf * p_inv
        lhs2 = [jnp.concatenate([kap[:, sl], rq[:, sl]], axis=0).astype(BF16) for sl in sls]
        amat = [_dot_nt(lhs2[g], jnp.concatenate([expand(bet[:, sls[g]]), expand(kt[:, sls[g]])], axis=0))
                for g in range(ng)]
        yield
        l_c = [jnp.where(strict, a[:c, :GROUP], 0.0) for a in amat]
        a_lo = [jnp.concatenate([jnp.where(strict, a[:c, GROUP:], 0.0),
                                 jnp.where(incl, a[c:, GROUP:], 0.0)], axis=0).astype(BF16) for a in amat]
        arb = [jnp.where(incl, a[c:, :GROUP], 0.0).astype(BF16) for a in amat]
        av = [_dot(a_lo[g], expand(v[:, sls[g]])) for g in range(ng)]
        x_c = [eye_c - jnp.where((t_c >> 1) == (s_c >> 1), l, 0.0) for l in l_c]
        for lvl in range(2, 7):
            lmask = ((t_c >> lvl) == (s_c >> lvl)) & ((t_c >> (lvl - 1)) != (s_c >> (lvl - 1)))
            y_c = [_dot(x_c[g].astype(BF16), expand(jnp.where(lmask, l_c[g], 0.0))) for g in range(ng)]
            yield
            x_c = [x_c[g] - _dot(y_c[g].astype(BF16), expand(x_c[g])) for g in range(ng)]
            yield
        bhat_neg = -(bb * p_end)
        khat = kf * p_end
        upd_rhs = [jnp.concatenate([bhat_neg[:, sl], khat[:, sl]], axis=0).astype(BF16) for sl in sls]
        prep[ci] = dict(lhs2=lhs2, av=av, arb=arb, x=[x.astype(BF16) for x in x_c], v=v,
                        upd_rhs=upd_rhs, p_tot=jnp.exp(tot), rkk=r * kf * rk_ref[...], gate=gate)

    def phase_b(ci):
        p = prep[ci]
        rs_ = slice(ci * c, (ci + 1) * c)
        v = p["v"]
        rs = [_dot_nt(p["lhs2"][g], state[g].astype(BF16)) for g in range(ng)]
        yield
        u_c = [_dot(p["x"][g], expand(rs[g][:c] + p["av"][g][:c])) for g in range(ng)]
        yield
        o_c = [rs[g][c:] + p["av"][g][c:] - _dot(p["arb"][g], expand(u_c[g])) for g in range(ng)]
        for g in range(ng):
            upd = _dot_tn(jnp.concatenate([u_c[g], v[:, sls[g]]], axis=0).astype(BF16), p["upd_rhs"][g])
            state[g] = state[g] * p["p_tot"][:, sls[g]] + jnp.where(bd, upd, 0.0)
        yield
        sums = seg_sums(o_c + [p["rkk"][:, sl] for sl in sls])
        dev = [o_c[g] - sums[g] * (1.0 / hd) for g in range(ng)]
        yield
        var = seg_sums([dv * dv for dv in dev])
        o_parts = []
        for g in range(ng):
            o_n = dev[g] * lax.rsqrt(var[g] * (1.0 / hd) + GN_EPS) * gng_ref[:, sls[g]] + gnb_ref[:, sls[g]]
            o_parts.append(o_n + sums[ng + g] * v[:, sls[g]])
        o = jnp.concatenate(o_parts, axis=1)
        out_ref[rs_, :] = zgb_ref[rs_, :] * (o * p["gate"])

    def pair_a(pi):
        yield from lora_stage(pi)
        yield from _round_robin(phase_a(2 * pi), phase_a(2 * pi + 1))

    n_pairs = nch // 2
    _interleave(pair_a(0))
    for k in range(n_pairs):
        b_gen = _chain(phase_b(2 * k), phase_b(2 * k + 1))
        _interleave(b_gen, *([pair_a(k + 1)] if k + 1 < n_pairs else []))

    for g in range(ng):
        s_ref[g] = state[g]


def _rwkv_rec(zin, w0, wb, a0, ab, gb, k_k, k_a, rk, gng, gnb):
    t = zin.shape[0]
    d = D_MODEL
    dl = ZL_WIDTH
    br = min(REC_ROWS, t)
    zcol = lambda c: pl.BlockSpec((br, d), lambda i: (i, c))
    row = pl.BlockSpec((1, d), lambda i: (0, 0))
    return pl.pallas_call(
        _rwkv_rec_kernel,
        grid=(t // br,),
        in_specs=[zcol(4), zcol(5), zcol(6), pl.BlockSpec((br, dl), lambda i: (i, 7 * d // dl)), zcol(3),
                  row, _resident((D_LORA_PAD, d)), row, _resident((D_LORA_PAD, d)),
                  _resident((D_GATE_LORA, d)), row, row, row, row, row],
        out_specs=pl.BlockSpec((br, d), lambda i: (i, 0)),
        out_shape=jax.ShapeDtypeStruct((t, d), F32),
        scratch_shapes=[pltpu.VMEM((N_GROUPS, GROUP, GROUP), F32)],
        compiler_params=_params(("arbitrary",), VMEM_LIMIT_BIG_TILES),
        name="rwkv_rec",
    )(zin, zin, zin, zin, zin, w0, wb, a0, ab, gb, k_k, k_a, rk, gng, gnb)


def _mem_kv_kernel(mem_ref, wk_ref, wv_ref, k_ref, v_ref):
    m = mem_ref[...].astype(BF16)
    k_ref[...] = _dot(m, wk_ref[...].astype(BF16)).astype(BF16)
    v_ref[...] = _dot(m, wv_ref[...].astype(BF16)).astype(BF16)


def _mem_kv(mem, wk, wv, tn=512):
    n, d = mem.shape
    return pl.pallas_call(
        _mem_kv_kernel,
        grid=(d // tn,),
        in_specs=[pl.BlockSpec((n, d), lambda j: (0, 0)),
                  pl.BlockSpec((d, tn), lambda j: (0, j)),
                  pl.BlockSpec((d, tn), lambda j: (0, j))],
        out_specs=[pl.BlockSpec((n, tn), lambda j: (0, j))] * 2,
        out_shape=[jax.ShapeDtypeStruct((n, d), BF16)] * 2,
        compiler_params=_params(("arbitrary",)),
        name="mem_kv",
    )(mem, wk, wv)


def _mix_xattn_kernel(ya_ref, yb_ref, x_ref, gin_ref, bin_ref, wout_ref, g1_ref, b1_ref,
                      wq_ref, k_ref, v_ref, wo_ref, g_ref, b_ref, o_ref):
    y = (ya_ref[...] + yb_ref[...]).astype(BF16)
    mix = _dot(y, wout_ref[...])
    h0 = _layer_norm(x_ref[...], gin_ref[...], bin_ref[...])
    h = _layer_norm(ALPHA * h0 + mix, g1_ref[...], b1_ref[...])
    q = _dot(h.astype(BF16), wq_ref[...]).astype(BF16)
    scale = XATTN_HEAD_DIM ** -0.5
    outs = []
    for hh in range(XATTN_HEADS):
        sl = slice(hh * XATTN_HEAD_DIM, (hh + 1) * XATTN_HEAD_DIM)
        s = _dot_nt(q[:, sl], k_ref[:, sl]) * scale
        e = jnp.exp(s - jnp.max(s, axis=-1, keepdims=True))
        p = e / jnp.sum(e, axis=-1, keepdims=True)
        outs.append(_dot(p.astype(BF16), v_ref[:, sl]))
    o = jnp.concatenate(outs, axis=1).astype(BF16)
    xa = _dot(o, wo_ref[...])
    o_ref[...] = _layer_norm(ALPHA * h + xa, g_ref[...], b_ref[...])


def _mix_xattn(ya, yb, x, g_in, b_in, w_out, g1, b1, wq, kmem, vmem, wo, g2, b2, tm=256):
    t, d = x.shape
    n = kmem.shape[0]
    blk = pl.BlockSpec((tm, d), lambda i: (i, 0))
    row = pl.BlockSpec((1, d), lambda i: (0, 0))
    return pl.pallas_call(
        _mix_xattn_kernel,
        grid=(t // tm,),
        in_specs=[blk, blk, blk, row, row, _resident((d, d)), row, row,
                  _resident((d, d)), _resident((n, d)), _resident((n, d)), _resident((d, d)), row, row],
        out_specs=blk,
        out_shape=jax.ShapeDtypeStruct((t, d), F32),
        compiler_params=_params(("arbitrary",)),
        name="mix_xattn",
    )(ya, yb, x, g_in, b_in, w_out, g1, b1, wq, kmem, vmem, wo, g2, b2)


def _ffn_kernel(h_ref, wg_ref, wu_ref, wd_ref, g_ref, b_ref, o_ref):
    j = pl.program_id(1)

    @pl.when(j == 0)
    def _():
        o_ref[...] = jnp.zeros_like(o_ref)

    hb = h_ref[...].astype(BF16)
    gate = _dot(hb, wg_ref[...].astype(BF16))
    up = _dot(hb, wu_ref[...].astype(BF16))
    act = (gate * _sigmoid(gate) * up).astype(BF16)
    o_ref[...] += _dot(act, wd_ref[...].astype(BF16))

    @pl.when(j == pl.num_programs(1) - 1)
    def _():
        o_ref[...] = _layer_norm(ALPHA * h_ref[...] + o_ref[...], g_ref[...], b_ref[...])


def _ffn(h, wg, wu, wd, g, b, tm=1024, tf=256):
    t, d = h.shape
    f = wg.shape[1]
    tm = min(tm, t)
    blk = pl.BlockSpec((tm, d), lambda i, j: (i, 0))
    row = pl.BlockSpec((1, d), lambda i, j: (0, 0))
    return pl.pallas_call(
        _ffn_kernel,
        grid=(t // tm, f // tf),
        in_specs=[blk,
                  pl.BlockSpec((d, tf), lambda i, j: (0, j)),
                  pl.BlockSpec((d, tf), lambda i, j: (0, j)),
                  pl.BlockSpec((tf, d), lambda i, j: (j, 0)),
                  row, row],
        out_specs=blk,
        out_shape=jax.ShapeDtypeStruct((t, d), F32),
        compiler_params=_params(("arbitrary", "arbitrary"), VMEM_LIMIT_BIG_TILES),
        name="ffn",
    )(h, wg, wu, wd, g, b)


def _pad_cols(w, n):
    return jnp.pad(w, ((0, 0), (0, n - w.shape[1])))


def _pad_rows(w, n):
    return jnp.pad(w, ((0, n - w.shape[0]), (0, 0)))


def kernel(x, mem, ln_in_g, ln_in_b, w_in, conv_w, conv_b, lru_wa, lru_ba, lru_wx, lru_bx, lru_lambda, rw_mu, rw_w0, rw_wB, rw_a0, rw_aB, rw_gB, rw_kk, rw_ka, rw_rk, rw_gn_g, rw_gn_b, w_out, ln1_g, ln1_b, xa_wq, xa_wk, xa_wv, xa_wo, ln2_g, ln2_b, ffn_wg, ffn_wu, ffn_wd, ln3_g, ln3_b):
    bsz, t, d = x.shape
    depth = w_in.shape[0]
    assert bsz == 1 and d == D_MODEL and t % 512 == 0
    row = lambda p: p.reshape(1, -1)

    h = None
    for l in range(depth):
        w_t = jnp.swapaxes(w_in[l], 0, 1)
        n_cols = IN_BLOCKS * IN_TN
        mu = row(rw_mu[l])
        mu_steps = jnp.concatenate(
            [jnp.zeros((1, 2 * d), F32), mu[:, :3 * d], jnp.zeros((1, 2 * d), F32), mu[:, 3 * d:]], axis=1)
        col_params = jnp.concatenate([
            _pad_cols(mu_steps, n_cols),
            _pad_cols(conv_w[l], n_cols),
            _pad_cols(row(conv_b[l]), n_cols),
            jnp.zeros((CP_ROWS - CP_CONV_B - 1, n_cols), F32)], axis=0)
        if l == 0:
            zin = _in_proj(x[0], row(ln_in_g), row(ln_in_b), w_t, col_params)
        else:
            raise NotImplementedError("DEPTH > 1 is not part of this problem")

        ya = _lru(zin, lru_wa[l].astype(BF16), row(lru_ba[l]),
                  lru_wx[l].astype(BF16), row(lru_bx[l]), row(lru_lambda[l]))

        yb = _rwkv_rec(
            zin, row(rw_w0[l]),
            _pad_rows(rw_wB[l], D_LORA_PAD).astype(BF16), row(rw_a0[l]),
            _pad_rows(rw_aB[l], D_LORA_PAD).astype(BF16), rw_gB[l].astype(BF16),
            row(rw_kk[l]), row(rw_ka[l]), row(rw_rk[l]), row(rw_gn_g[l]), row(rw_gn_b[l]))

        kmem, vmem = _mem_kv(mem[0], xa_wk[l], xa_wv[l])
        h = _mix_xattn(ya, yb, x[0], row(ln_in_g), row(ln_in_b), w_out[l].astype(BF16),
                       row(ln1_g[l]), row(ln1_b[l]), xa_wq[l].astype(BF16), kmem, vmem,
                       xa_wo[l].astype(BF16), row(ln2_g[l]), row(ln2_b[l]))
        h = _ffn(h, ffn_wg[l], ffn_wu[l], ffn_wd[l], row(ln3_g[l]), row(ln3_b[l]))
    return h[None]
```

```python
import jax
import jax.numpy as jnp
from jax import lax
from jax.experimental import pallas as pl
from jax.experimental.pallas import tpu as pltpu

F32 = jnp.float32
BF16 = jnp.bfloat16

D_MODEL = 2048
LN_EPS = 1e-5
ALPHA = 2.0 ** 0.25

LRU_HEADS = 16
LRU_HEAD_DIM = 128
CONV_WIDTH = 4
LRU_C = 8.0

RWKV_HEAD_DIM = 64
D_LORA = 96
D_LORA_PAD = 128
D_GATE_LORA = 256
ZL_WIDTH = 512
GN_EPS = 64e-5
CHUNK = 64
GROUP = 256
N_GROUPS = D_MODEL // GROUP

XATTN_HEADS = 4
XATTN_HEAD_DIM = 512

VMEM_LIMIT = 56 * 1024 * 1024
VMEM_LIMIT_BIG_TILES = 60 * 1024 * 1024


def _dot(a, b):
    return jnp.dot(a, b, preferred_element_type=F32)


def _dot_nt(a, b):
    return lax.dot_general(a, b, (((1,), (1,)), ((), ())), preferred_element_type=F32)


def _dot_tn(a, b):
    return lax.dot_general(a, b, (((0,), (0,)), ((), ())), preferred_element_type=F32)


def _split2(x):
    hi = x.astype(BF16)
    lo = (x - hi.astype(F32)).astype(BF16)
    return hi, lo


def _split3(x):
    hi = x.astype(BF16)
    r1 = x - hi.astype(F32)
    mid = r1.astype(BF16)
    lo = (r1 - mid.astype(F32)).astype(BF16)
    return hi, mid, lo


def _sigmoid(x):
    return 1.0 / (1.0 + jnp.exp(-x))


def _softplus(x):
    return jnp.maximum(x, 0.0) + jnp.log1p(jnp.exp(-jnp.abs(x)))


def _softplus_log(x):
    return jnp.maximum(x, 0.0) + jnp.log(1.0 + jnp.exp(-jnp.abs(x)))


def _gelu_tanh(x):
    c = 0.7978845608028654
    return 0.5 * x * (1.0 + jnp.tanh(c * (x + 0.044715 * (x * x * x))))


def _layer_norm(x, g, b):
    mu = jnp.mean(x, axis=-1, keepdims=True)
    xc = x - mu
    var = jnp.mean(xc * xc, axis=-1, keepdims=True)
    return xc * lax.rsqrt(var + LN_EPS) * g + b


def _shift_rows(z, prev8, s):
    rolled = pltpu.roll(z, s, 0)
    row8 = lax.broadcasted_iota(jnp.int32, prev8.shape, 0)
    head = jnp.where(row8 < s, pltpu.roll(prev8, s, 0), rolled[:8])
    return jnp.concatenate([head, rolled[8:]], axis=0)


def _params(sem, vmem_limit=VMEM_LIMIT):
    return pltpu.CompilerParams(dimension_semantics=sem, vmem_limit_bytes=vmem_limit)


def _resident(shape):
    return pl.BlockSpec(shape, lambda *_: (0,) * len(shape), pipeline_mode=pl.Buffered(1))


IN_TN = 512
IN_D_BLOCKS = D_MODEL // IN_TN
IN_MAIN_BLOCKS = 5 * IN_D_BLOCKS
IN_GATE_BLOCKS = 2 * IN_D_BLOCKS
IN_BLOCKS = IN_MAIN_BLOCKS + IN_GATE_BLOCKS + 1


def _in_weight_row(j, n_out):
    gates = n_out - 2 * D_MODEL + (j - IN_MAIN_BLOCKS) * IN_TN
    return jnp.where(j < IN_MAIN_BLOCKS, j * IN_TN,
                     jnp.where(j < IN_MAIN_BLOCKS + IN_GATE_BLOCKS, gates, 5 * D_MODEL))


def _in_out_block(j):
    b2 = 2 * D_MODEL // IN_TN
    return jnp.where(j < b2, j,
                     jnp.where(j < IN_MAIN_BLOCKS, j + b2,
                               jnp.where(j < IN_MAIN_BLOCKS + b2, j - (IN_MAIN_BLOCKS - b2), j)))


CP_MU, CP_CONV_W, CP_CONV_B, CP_ROWS = 0, 1, 1 + CONV_WIDTH, 8


def _in_proj_kernel(x_ref, g_ref, b_ref, w_ref, cp_ref, z_ref, hb_ref, carry_ref):
    i = pl.program_id(0)
    j = pl.program_id(1)
    nd = IN_D_BLOCKS

    @pl.when(j == 0)
    def _():
        hb_ref[...] = _layer_norm(x_ref[...], g_ref[...], b_ref[...]).astype(BF16)

    @pl.when((i == 0) & (j == 0))
    def _():
        carry_ref[...] = jnp.zeros_like(carry_ref)

    def keep_tail(z):
        carry_ref[j] = z[z.shape[0] - 8:, :]

    @pl.when(j < nd)
    def _():
        z = _dot_nt(hb_ref[...], w_ref[...].astype(BF16))
        prev8 = carry_ref[j]
        cw = CP_CONV_W + CONV_WIDTH - 1
        conv = cp_ref[CP_CONV_B:CP_CONV_B + 1, :] + z * cp_ref[cw:cw + 1, :]
        for s in range(1, CONV_WIDTH):
            conv = conv + _shift_rows(z, prev8, s) * cp_ref[cw - s:cw - s + 1, :]
        keep_tail(z)
        z_ref[...] = conv

    @pl.when((j >= nd) & (j < 2 * nd))
    def _():
        z_ref[...] = _gelu_tanh(_dot_nt(hb_ref[...], w_ref[...].astype(BF16)))

    def token_shift(z):
        n = z.shape[1]
        zp = _shift_rows(z, carry_ref[j][:, 0:n], 1)
        carry_ref[j, :, 0:n] = z[z.shape[0] - 8:, :]
        return z + (zp - z) * cp_ref[CP_MU:CP_MU + 1, 0:n]

    @pl.when((j >= 2 * nd) & (j < IN_MAIN_BLOCKS))
    def _():
        z_ref[...] = token_shift(_dot_nt(hb_ref[...], w_ref[...].astype(BF16)))

    @pl.when((j >= IN_MAIN_BLOCKS) & (j < IN_MAIN_BLOCKS + IN_GATE_BLOCKS))
    def _():
        z_ref[...] = _sigmoid(_dot_nt(hb_ref[...], w_ref[...].astype(BF16)))

    @pl.when(j >= IN_MAIN_BLOCKS + IN_GATE_BLOCKS)
    def _():
        z_ref[:, 0:ZL_WIDTH] = token_shift(_dot_nt(hb_ref[...], w_ref[0:ZL_WIDTH, :].astype(BF16)))
        if z_ref.shape[1] > ZL_WIDTH:
            z_ref[:, ZL_WIDTH:] = jnp.zeros((z_ref.shape[0], z_ref.shape[1] - ZL_WIDTH), F32)


def _in_proj(x, g, b, w_t, col_params, tm=2048):
    t, d = x.shape
    tn = IN_TN
    nb = IN_BLOCKS
    n_out = w_t.shape[0]
    tm = min(tm, t)
    return pl.pallas_call(
        _in_proj_kernel,
        grid=(t // tm, nb),
        in_specs=[
            pl.BlockSpec((tm, d), lambda i, j: (i, 0), pipeline_mode=pl.Buffered(1)),
            pl.BlockSpec((1, d), lambda i, j: (0, 0)),
            pl.BlockSpec((1, d), lambda i, j: (0, 0)),
            pl.BlockSpec((pl.Element(tn), pl.Element(d)),
                         lambda i, j: (pl.multiple_of(_in_weight_row(j, n_out), 8), 0)),
            pl.BlockSpec((CP_ROWS, tn), lambda i, j: (0, j)),
        ],
        out_specs=pl.BlockSpec((tm, tn), lambda i, j: (i, _in_out_block(j))),
        out_shape=jax.ShapeDtypeStruct((t, nb * tn), F32),
        scratch_shapes=[pltpu.VMEM((tm, d), BF16), pltpu.VMEM((nb, 8, tn), F32)],
        compiler_params=_params(("arbitrary", "arbitrary")),
        name="in_proj",
    )(x, g, b, w_t, col_params)


def _lru_kernel(u_ref, gate_ref, ga_ref, wa_ref, ba_ref, wx_ref, bx_ref,
                lam_ref, out_ref, hcarry):
    @pl.when(pl.program_id(0) == 0)
    def _():
        hcarry[...] = jnp.zeros_like(hcarry)

    conv = u_ref[...]
    tm = conv.shape[0]
    cb16 = conv.astype(BF16)
    r_parts, i_parts = [], []
    for g in range(LRU_HEADS):
        ug = cb16[:, g * LRU_HEAD_DIM:(g + 1) * LRU_HEAD_DIM]
        r_parts.append(_dot(ug, wa_ref[g]))
        i_parts.append(_dot(ug, wx_ref[g]))
    r = _sigmoid(jnp.concatenate(r_parts, axis=1) + ba_ref[...])
    ig = _sigmoid(jnp.concatenate(i_parts, axis=1) + bx_ref[...])

    log_a = (-LRU_C) * r * _softplus(-lam_ref[...])
    a_c = jnp.exp(log_a)
    om = -jnp.tanh(log_a) * (a_c * a_c + 1.0)
    b_c = jnp.where(om > 0.0, om * lax.rsqrt(om), 0.0) * (ig * conv)

    n8 = tm // 8
    a3 = a_c.reshape(n8, 8, a_c.shape[1])
    b3 = b_c.reshape(n8, 8, b_c.shape[1])
    sub = lax.broadcasted_iota(jnp.int32, a3.shape, 1)
    for d in (1, 2, 4):
        m = sub >= d
        a_sh = jnp.where(m, pltpu.roll(a3, d, 1), 1.0)
        b_sh = jnp.where(m, pltpu.roll(b3, d, 1), 0.0)
        b3 = a3 * b_sh + b3
        a3 = a3 * a_sh
    carry = hcarry[...]
    hs = []
    for i in range(n8):
        h_i = b3[i] + a3[i] * carry
        hs.append(h_i)
        carry = h_i[7:8, :]
    hcarry[...] = carry
    h = jnp.concatenate(hs, axis=0)

    out_ref[...] = ga_ref[...] * (gate_ref[...] * h)


def _lru(zin, wa, ba, wx, bx, lam, tm=256):
    t = zin.shape[0]
    d = D_MODEL
    row = lambda i: (0, 0)
    return pl.pallas_call(
        _lru_kernel,
        grid=(t // tm,),
        in_specs=[
            pl.BlockSpec((tm, d), lambda i: (i, 0)),
            pl.BlockSpec((tm, d), lambda i: (i, 1)),
            pl.BlockSpec((tm, d), lambda i: (i, 2)),
            pl.BlockSpec((LRU_HEADS, LRU_HEAD_DIM, LRU_HEAD_DIM), lambda i: (0, 0, 0)),
            pl.BlockSpec((1, d), row),
            pl.BlockSpec((LRU_HEADS, LRU_HEAD_DIM, LRU_HEAD_DIM), lambda i: (0, 0, 0)),
            pl.BlockSpec((1, d), row),
            pl.BlockSpec((1, d), row),
        ],
        out_specs=pl.BlockSpec((tm, d), lambda i: (i, 0)),
        out_shape=jax.ShapeDtypeStruct((t, d), F32),
        scratch_shapes=[pltpu.VMEM((1, d), F32)],
        compiler_params=_params(("arbitrary",)),
        name="lru",
    )(zin, zin, zin, wa, ba, wx, bx, lam)


def _block_ones(n, seg):
    r = lax.broadcasted_iota(jnp.int32, (n, n), 0) // seg
    c = lax.broadcasted_iota(jnp.int32, (n, n), 1) // seg
    return jnp.where(r == c, 1.0, 0.0).astype(BF16)


REC_ROWS = 256


def _round_robin(*gens):
    live = list(gens)
    while live:
        for gen in list(live):
            try:
                next(gen)
            except StopIteration:
                live.remove(gen)
        yield


def _interleave(*gens):
    for _ in _round_robin(*gens):
        pass


def _chain(*gens):
    for gen in gens:
        yield from gen


def _rwkv_rec_kernel(r_ref, k_ref, v_ref, zl_ref, zgb_ref, w0_ref, wb_ref, a0_ref, ab_ref, gb_ref,
                     kk_ref, ka_ref, rk_ref, gng_ref, gnb_ref, out_ref, s_ref):
    @pl.when(pl.program_id(0) == 0)
    def _():
        s_ref[...] = jnp.zeros_like(s_ref)

    c = CHUNK
    hd = RWKV_HEAD_DIM
    ng = N_GROUPS
    nch = r_ref.shape[0] // c
    sls = [slice(g * GROUP, (g + 1) * GROUP) for g in range(ng)]

    assert nch % 2 == 0
    rows = lax.broadcasted_iota(jnp.int32, (2 * c, 2 * c), 0)
    cols = lax.broadcasted_iota(jnp.int32, (2 * c, 2 * c), 1)
    tri2 = jnp.where((cols <= rows) & (cols // c == rows // c), 1.0, 0.0).astype(BF16)
    t_c = lax.broadcasted_iota(jnp.int32, (c, GROUP), 0)
    s_c = lax.broadcasted_iota(jnp.int32, (c, GROUP), 1) % hd
    strict = s_c < t_c
    incl = s_c <= t_c
    eye_c = jnp.where(s_c == t_c, 1.0, 0.0)
    bd = (lax.broadcasted_iota(jnp.int32, (GROUP, GROUP), 0) // hd
          == lax.broadcasted_iota(jnp.int32, (GROUP, GROUP), 1) // hd)
    ones_bd = _block_ones(GROUP, hd)

    def expand(x_c):
        return jnp.where(bd, jnp.concatenate([x_c] * (GROUP // hd), axis=0), 0.0).astype(BF16)

    def seg_sums(xs):
        hi, lo = _split2(jnp.concatenate(xs, axis=0))
        s = _dot(jnp.concatenate([hi, lo], axis=0), ones_bd)
        n = c * len(xs)
        s = s[:n] + s[n:]
        return [s[i * c:(i + 1) * c] for i in range(len(xs))]

    state = [s_ref[g] for g in range(ng)]
    prep = [None] * nch

    pair_prep = {}

    def lora_stage(pi):
        rs_ = slice(2 * pi * c, (2 * pi + 2) * c)
        zl = zl_ref[rs_, :]
        nl = zl.shape[1]
        w_lo = zl[:, 0:D_LORA_PAD]
        a_lo = pltpu.roll(zl, nl - D_LORA, 1)[:, 0:D_LORA_PAD]
        g_lo = pltpu.roll(zl, nl - 2 * D_LORA, 1)[:, 0:D_GATE_LORA]
        w_log = -_softplus_log(-(w0_ref[...] + _dot(jnp.tanh(w_lo).astype(BF16), wb_ref[...]))) - 0.5
        lw = -jnp.exp(w_log)
        a = _sigmoid(a0_ref[...] + _dot(a_lo.astype(BF16), ab_ref[...]))
        gate = _dot(_sigmoid(g_lo).astype(BF16), gb_ref[...])
        hi, mid, lo = _split3(lw)
        cum = _dot(tri2, hi) + _dot(tri2, mid) + _dot(tri2, lo)
        pair_prep[pi] = dict(lw=lw, a=a, gate=gate, cum=cum)
        yield

    def phase_a(ci):
        rs_ = slice(ci * c, (ci + 1) * c)
        half = slice((ci % 2) * c, (ci % 2 + 1) * c)
        pp = pair_prep[ci // 2]
        lw, a, gate, cum = pp["lw"][half], pp["a"][half], pp["gate"][half], pp["cum"][half]
        tot = cum[c - 1:c, :]
        k = k_ref[rs_, :]
        kk = k * kk_ref[...]
        kk2 = kk * kk
        n2 = jnp.concatenate(seg_sums([kk2[:, sl] for sl in sls]), axis=1)
        kn = kk * lax.rsqrt(jnp.maximum(n2, 1e-24))
        bb = kn * a
        kf = k * (1.0 + (a - 1.0) * ka_ref[...])
        yield
        r = r_ref[rs_, :]
        v = v_ref[rs_, :]
        p_inv = jnp.exp(-cum)
        p_end = jnp.exp(tot - cum)
        rq = r * jnp.exp(cum)
        kap = kn * jnp.exp(cum - lw)
        bet = bb * p_inv
        kt = kf * p_inv
        lhs2 = [jnp.concatenate([kap[:, sl], rq[:, sl]], axis=0).astype(BF16) for sl in sls]
        amat = [_dot_nt(lhs2[g], jnp.concatenate([expand(bet[:, sls[g]]), expand(kt[:, sls[g]])], axis=0))
                for g in range(ng)]
        yield
        l_c = [jnp.where(strict, a[:c, :GROUP], 0.0) for a in amat]
        a_lo = [jnp.concatenate([jnp.where(strict, a[:c, GROUP:], 0.0),
                                 jnp.where(incl, a[c:, GROUP:], 0.0)], axis=0).astype(BF16) for a in amat]
        arb = [jnp.where(incl, a[c:, :GROUP], 0.0).astype(BF16) for a in amat]
        av = [_dot(a_lo[g], expand(v[:, sls[g]])) for g in range(ng)]
        x_c = [eye_c - jnp.where((t_c >> 1) == (s_c >> 1), l, 0.0) for l in l_c]
        for lvl in range(2, 7):
            lmask = ((t_c >> lvl) == (s_c >> lvl)) & ((t_c >> (lvl - 1)) != (s_c >> (lvl - 1)))
            y_c = [_dot(x_c[g].astype(BF16), expand(jnp.where(lmask, l_c[g], 0.0))) for g in range(ng)]
            yield
            x_c = [x_c[g] - _dot(y_c[g].astype(BF16), expand(x_c[g])) for g in range(ng)]
            yield
        bhat_neg = -(bb * p_end)
        khat = kf * p_end
        upd_rhs = [jnp.concatenate([bhat_neg[:, sl], khat[:, sl]], axis=0).astype(BF16) for sl in sls]
        prep[ci] = dict(lhs2=lhs2, av=av, arb=arb, x=[x.astype(BF16) for x in x_c], v=v,
                        upd_rhs=upd_rhs, p_tot=jnp.exp(tot), rkk=r * kf * rk_ref[...], gate=gate)

    def phase_b(ci):
        p = prep[ci]
        rs_ = slice(ci * c, (ci + 1) * c)
        v = p["v"]
        rs = [_dot_nt(p["lhs2"][g], state[g].astype(BF16)) for g in range(ng)]
        yield
        u_c = [_dot(p["x"][g], expand(rs[g][:c] + p["av"][g][:c])) for g in range(ng)]
        yield
        o_c = [rs[g][c:] + p["av"][g][c:] - _dot(p["arb"][g], expand(u_c[g])) for g in range(ng)]
        for g in range(ng):
            upd = _dot_tn(jnp.concatenate([u_c[g], v[:, sls[g]]], axis=0).astype(BF16), p["upd_rhs"][g])
            state[g] = state[g] * p["p_tot"][:, sls[g]] + jnp.where(bd, upd, 0.0)
        yield
        sums = seg_sums(o_c + [p["rkk"][:, sl] for sl in sls])
        dev = [o_c[g] - sums[g] * (1.0 / hd) for g in range(ng)]
        yield
        var = seg_sums([dv * dv for dv in dev])
        o_parts = []
        for g in range(ng):
            o_n = dev[g] * lax.rsqrt(var[g] * (1.0 / hd) + GN_EPS) * gng_ref[:, sls[g]] + gnb_ref[:, sls[g]]
            o_parts.append(o_n + sums[ng + g] * v[:, sls[g]])
        o = jnp.concatenate(o_parts, axis=1)
        out_ref[rs_, :] = zgb_ref[rs_, :] * (o * p["gate"])

    def pair_a(pi):
        yield from lora_stage(pi)
        yield from _round_robin(phase_a(2 * pi), phase_a(2 * pi + 1))

    n_pairs = nch // 2
    _interleave(pair_a(0))
    for k in range(n_pairs):
        b_gen = _chain(phase_b(2 * k), phase_b(2 * k + 1))
        _interleave(b_gen, *([pair_a(k + 1)] if k + 1 < n_pairs else []))

    for g in range(ng):
        s_ref[g] = state[g]


def _rwkv_rec(zin, w0, wb, a0, ab, gb, k_k, k_a, rk, gng, gnb):
    t = zin.shape[0]
    d = D_MODEL
    dl = ZL_WIDTH
    br = min(REC_ROWS, t)
    zcol = lambda c: pl.BlockSpec((br, d), lambda i: (i, c))
    row = pl.BlockSpec((1, d), lambda i: (0, 0))
    return pl.pallas_call(
        _rwkv_rec_kernel,
        grid=(t // br,),
        in_specs=[zcol(4), zcol(5), zcol(6), pl.BlockSpec((br, dl), lambda i: (i, 7 * d // dl)), zcol(3),
                  row, _resident((D_LORA_PAD, d)), row, _resident((D_LORA_PAD, d)),
                  _resident((D_GATE_LORA, d)), row, row, row, row, row],
        out_specs=pl.BlockSpec((br, d), lambda i: (i, 0)),
        out_shape=jax.ShapeDtypeStruct((t, d), F32),
        scratch_shapes=[pltpu.VMEM((N_GROUPS, GROUP, GROUP), F32)],
        compiler_params=_params(("arbitrary",), VMEM_LIMIT_BIG_TILES),
        name="rwkv_rec",
    )(zin, zin, zin, zin, zin, w0, wb, a0, ab, gb, k_k, k_a, rk, gng, gnb)


def _mem_kv_kernel(mem_ref, wk_ref, wv_ref, k_ref, v_ref):
    m = mem_ref[...].astype(BF16)
    k_ref[...] = _dot(m, wk_ref[...].astype(BF16)).astype(BF16)
    v_ref[...] = _dot(m, wv_ref[...].astype(BF16)).astype(BF16)


def _mem_kv(mem, wk, wv, tn=512):
    n, d = mem.shape
    return pl.pallas_call(
        _mem_kv_kernel,
        grid=(d // tn,),
        in_specs=[pl.BlockSpec((n, d), lambda j: (0, 0)),
                  pl.BlockSpec((d, tn), lambda j: (0, j)),
                  pl.BlockSpec((d, tn), lambda j: (0, j))],
        out_specs=[pl.BlockSpec((n, tn), lambda j: (0, j))] * 2,
        out_shape=[jax.ShapeDtypeStruct((n, d), BF16)] * 2,
        compiler_params=_params(("arbitrary",)),
        name="mem_kv",
    )(mem, wk, wv)


def _mix_xattn_kernel(ya_ref, yb_ref, x_ref, gin_ref, bin_ref, wout_ref, g1_ref, b1_ref,
                      wq_ref, k_ref, v_ref, wo_ref, g_ref, b_ref, o_ref):
    y = (ya_ref[...] + yb_ref[...]).astype(BF16)
    mix = _dot(y, wout_ref[...])
    h0 = _layer_norm(x_ref[...], gin_ref[...], bin_ref[...])
    h = _layer_norm(ALPHA * h0 + mix, g1_ref[...], b1_ref[...])
    q = _dot(h.astype(BF16), wq_ref[...]).astype(BF16)
    scale = XATTN_HEAD_DIM ** -0.5
    outs = []
    for hh in range(XATTN_HEADS):
        sl = slice(hh * XATTN_HEAD_DIM, (hh + 1) * XATTN_HEAD_DIM)
        s = _dot_nt(q[:, sl], k_ref[:, sl]) * scale
        e = jnp.exp(s - jnp.max(s, axis=-1, keepdims=True))
        p = e / jnp.sum(e, axis=-1, keepdims=True)
        outs.append(_dot(p.astype(BF16), v_ref[:, sl]))
    o = jnp.concatenate(outs, axis=1).astype(BF16)
    xa = _dot(o, wo_ref[...])
    o_ref[...] = _layer_norm(ALPHA * h + xa, g_ref[...], b_ref[...])


def _mix_xattn(ya, yb, x, g_in, b_in, w_out, g1, b1, wq, kmem, vmem, wo, g2, b2, tm=256):
    t, d = x.shape
    n = kmem.shape[0]
    blk = pl.BlockSpec((tm, d), lambda i: (i, 0))
    row = pl.BlockSpec((1, d), lambda i: (0, 0))
    return pl.pallas_call(
        _mix_xattn_kernel,
        grid=(t // tm,),
        in_specs=[blk, blk, blk, row, row, _resident((d, d)), row, row,
                  _resident((d, d)), _resident((n, d)), _resident((n, d)), _resident((d, d)), row, row],
        out_specs=blk,
        out_shape=jax.ShapeDtypeStruct((t, d), F32),
        compiler_params=_params(("arbitrary",)),
        name="mix_xattn",
    )(ya, yb, x, g_in, b_in, w_out, g1, b1, wq, kmem, vmem, wo, g2, b2)


def _ffn_kernel(h_ref, wg_ref, wu_ref, wd_ref, g_ref, b_ref, o_ref):
    j = pl.program_id(1)

    @pl.when(j == 0)
    def _():
        o_ref[...] = jnp.zeros_like(o_ref)

    hb = h_ref[...].astype(BF16)
    gate = _dot(hb, wg_ref[...].astype(BF16))
    up = _dot(hb, wu_ref[...].astype(BF16))
    act = (gate * _sigmoid(gate) * up).astype(BF16)
    o_ref[...] += _dot(act, wd_ref[...].astype(BF16))

    @pl.when(j == pl.num_programs(1) - 1)
    def _():
        o_ref[...] = _layer_norm(ALPHA * h_ref[...] + o_ref[...], g_ref[...], b_ref[...])


def _ffn(h, wg, wu, wd, g, b, tm=1024, tf=256):
    t, d = h.shape
    f = wg.shape[1]
    tm = min(tm, t)
    blk = pl.BlockSpec((tm, d), lambda i, j: (i, 0))
    row = pl.BlockSpec((1, d), lambda i, j: (0, 0))
    return pl.pallas_call(
        _ffn_kernel,
        grid=(t // tm, f // tf),
        in_specs=[blk,
                  pl.BlockSpec((d, tf), lambda i, j: (0, j)),
                  pl.BlockSpec((d, tf), lambda i, j: (0, j)),
                  pl.BlockSpec((tf, d), lambda i, j: (j, 0)),
                  row, row],
        out_specs=blk,
        out_shape=jax.ShapeDtypeStruct((t, d), F32),
        compiler_params=_params(("arbitrary", "arbitrary"), VMEM_LIMIT_BIG_TILES),
        name="ffn",
    )(h, wg, wu, wd, g, b)


def _pad_cols(w, n):
    return jnp.pad(w, ((0, 0), (0, n - w.shape[1])))


def _pad_rows(w, n):
    return jnp.pad(w, ((0, n - w.shape[0]), (0, 0)))


def kernel(x, mem, ln_in_g, ln_in_b, w_in, conv_w, conv_b, lru_wa, lru_ba, lru_wx, lru_bx, lru_lambda, rw_mu, rw_w0, rw_wB, rw_a0, rw_aB, rw_gB, rw_kk, rw_ka, rw_rk, rw_gn_g, rw_gn_b, w_out, ln1_g, ln1_b, xa_wq, xa_wk, xa_wv, xa_wo, ln2_g, ln2_b, ffn_wg, ffn_wu, ffn_wd, ln3_g, ln3_b):
    bsz, t, d = x.shape
    depth = w_in.shape[0]
    assert bsz == 1 and d == D_MODEL and t % 512 == 0
    row = lambda p: p.reshape(1, -1)

    h = None
    for l in range(depth):
        w_t = jnp.swapaxes(w_in[l], 0, 1)
        n_cols = IN_BLOCKS * IN_TN
        mu = row(rw_mu[l])
        mu_steps = jnp.concatenate(
            [jnp.zeros((1, 2 * d), F32), mu[:, :3 * d], jnp.zeros((1, 2 * d), F32), mu[:, 3 * d:]], axis=1)
        col_params = jnp.concatenate([
            _pad_cols(mu_steps, n_cols),
            _pad_cols(conv_w[l], n_cols),
            _pad_cols(row(conv_b[l]), n_cols),
            jnp.zeros((CP_ROWS - CP_CONV_B - 1, n_cols), F32)], axis=0)
        if l == 0:
            zin = _in_proj(x[0], row(ln_in_g), row(ln_in_b), w_t, col_params)
        else:
            raise NotImplementedError("DEPTH > 1 is not part of this problem")

        ya = _lru(zin, lru_wa[l].astype(BF16), row(lru_ba[l]),
                  lru_wx[l].astype(BF16), row(lru_bx[l]), row(lru_lambda[l]))

        yb = _rwkv_rec(
            zin, row(rw_w0[l]),
            _pad_rows(rw_wB[l], D_LORA_PAD).astype(BF16), row(rw_a0[l]),
            _pad_rows(rw_aB[l], D_LORA_PAD).astype(BF16), rw_gB[l].astype(BF16),
            row(rw_kk[l]), row(rw_ka[l]), row(rw_rk[l]), row(rw_gn_g[l]), row(rw_gn_b[l]))

        kmem, vmem = _mem_kv(mem[0], xa_wk[l], xa_wv[l])
        h = _mix_xattn(ya, yb, x[0], row(ln_in_g), row(ln_in_b), w_out[l].astype(BF16),
                       row(ln1_g[l]), row(ln1_b[l]), xa_wq[l].astype(BF16), kmem, vmem,
                       xa_wo[l].astype(BF16), row(ln2_g[l]), row(ln2_b[l]))
        h = _ffn(h, ffn_wg[l], ffn_wu[l], ffn_wd[l], row(ln3_g[l]), row(ln3_b[l]))
    return h[None]
```

```python
import jax
import jax.numpy as jnp
from jax import lax
from jax.experimental import pallas as pl
from jax.experimental.pallas import tpu as pltpu

F32 = jnp.float32
BF16 = jnp.bfloat16

D_MODEL = 2048
LN_EPS = 1e-5
ALPHA = 2.0 ** 0.25

LRU_HEADS = 16
LRU_HEAD_DIM = 128
CONV_WIDTH = 4
LRU_C = 8.0

RWKV_HEAD_DIM = 64
D_LORA = 96
D_LORA_PAD = 128
D_GATE_LORA = 256
ZL_WIDTH = 512
GN_EPS = 64e-5
CHUNK = 64
GROUP = 256
N_GROUPS = D_MODEL // GROUP

XATTN_HEADS = 4
XATTN_HEAD_DIM = 512

VMEM_LIMIT = 56 * 1024 * 1024
VMEM_LIMIT_BIG_TILES = 60 * 1024 * 1024


def _dot(a, b):
    return jnp.dot(a, b, preferred_element_type=F32)


def _dot_nt(a, b):
    return lax.dot_general(a, b, (((1,), (1,)), ((), ())), preferred_element_type=F32)


def _dot_tn(a, b):
    return lax.dot_general(a, b, (((0,), (0,)), ((), ())), preferred_element_type=F32)


def _split2(x):
    hi = x.astype(BF16)
    lo = (x - hi.astype(F32)).astype(BF16)
    return hi, lo


def _split3(x):
    hi = x.astype(BF16)
    r1 = x - hi.astype(F32)
    mid = r1.astype(BF16)
    lo = (r1 - mid.astype(F32)).astype(BF16)
    return hi, mid, lo


def _sigmoid(x):
    return 1.0 / (1.0 + jnp.exp(-x))


def _softplus(x):
    return jnp.maximum(x, 0.0) + jnp.log1p(jnp.exp(-jnp.abs(x)))


def _softplus_log(x):
    return jnp.maximum(x, 0.0) + jnp.log(1.0 + jnp.exp(-jnp.abs(x)))


def _gelu_tanh(x):
    c = 0.7978845608028654
    return 0.5 * x * (1.0 + jnp.tanh(c * (x + 0.044715 * (x * x * x))))


def _layer_norm(x, g, b):
    mu = jnp.mean(x, axis=-1, keepdims=True)
    xc = x - mu
    var = jnp.mean(xc * xc, axis=-1, keepdims=True)
    return xc * lax.rsqrt(var + LN_EPS) * g + b


def _shift_rows(z, prev8, s):
    rolled = pltpu.roll(z, s, 0)
    row8 = lax.broadcasted_iota(jnp.int32, prev8.shape, 0)
    head = jnp.where(row8 < s, pltpu.roll(prev8, s, 0), rolled[:8])
    return jnp.concatenate([head, rolled[8:]], axis=0)


def _params(sem, vmem_limit=VMEM_LIMIT):
    return pltpu.CompilerParams(dimension_semantics=sem, vmem_limit_bytes=vmem_limit)


def _resident(shape):
    return pl.BlockSpec(shape, lambda *_: (0,) * len(shape), pipeline_mode=pl.Buffered(1))


IN_TN = 1024
IN_D_BLOCKS = D_MODEL // IN_TN
IN_MAIN_BLOCKS = 5 * IN_D_BLOCKS
IN_GATE_BLOCKS = 2 * IN_D_BLOCKS
IN_BLOCKS = IN_MAIN_BLOCKS + IN_GATE_BLOCKS + 1


def _in_weight_row(j, n_out):
    gates = n_out - 2 * D_MODEL + (j - IN_MAIN_BLOCKS) * IN_TN
    return jnp.where(j < IN_MAIN_BLOCKS, j * IN_TN,
                     jnp.where(j < IN_MAIN_BLOCKS + IN_GATE_BLOCKS, gates, 5 * D_MODEL))


def _in_out_block(j):
    b2 = 2 * D_MODEL // IN_TN
    return jnp.where(j < b2, j,
                     jnp.where(j < IN_MAIN_BLOCKS, j + b2,
                               jnp.where(j < IN_MAIN_BLOCKS + b2, j - (IN_MAIN_BLOCKS - b2), j)))


CP_MU, CP_CONV_W, CP_CONV_B, CP_ROWS = 0, 1, 1 + CONV_WIDTH, 8


def _in_proj_kernel(x_ref, g_ref, b_ref, w_ref, cp_ref, z_ref, hb_ref, carry_ref):
    i = pl.program_id(0)
    j = pl.program_id(1)
    nd = IN_D_BLOCKS

    @pl.when(j == 0)
    def _():
        hb_ref[...] = _layer_norm(x_ref[...], g_ref[...], b_ref[...]).astype(BF16)

    @pl.when((i == 0) & (j == 0))
    def _():
        carry_ref[...] = jnp.zeros_like(carry_ref)

    def keep_tail(z):
        carry_ref[j] = z[z.shape[0] - 8:, :]

    @pl.when(j < nd)
    def _():
        z = _dot_nt(hb_ref[...], w_ref[...].astype(BF16))
        prev8 = carry_ref[j]
        cw = CP_CONV_W + CONV_WIDTH - 1
        conv = cp_ref[CP_CONV_B:CP_CONV_B + 1, :] + z * cp_ref[cw:cw + 1, :]
        for s in range(1, CONV_WIDTH):
            conv = conv + _shift_rows(z, prev8, s) * cp_ref[cw - s:cw - s + 1, :]
        keep_tail(z)
        z_ref[...] = conv

    @pl.when((j >= nd) & (j < 2 * nd))
    def _():
        z_ref[...] = _gelu_tanh(_dot_nt(hb_ref[...], w_ref[...].astype(BF16)))

    def token_shift(z):
        n = z.shape[1]
        zp = _shift_rows(z, carry_ref[j][:, 0:n], 1)
        carry_ref[j, :, 0:n] = z[z.shape[0] - 8:, :]
        return z + (zp - z) * cp_ref[CP_MU:CP_MU + 1, 0:n]

    @pl.when((j >= 2 * nd) & (j < IN_MAIN_BLOCKS))
    def _():
        z_ref[...] = token_shift(_dot_nt(hb_ref[...], w_ref[...].astype(BF16)))

    @pl.when((j >= IN_MAIN_BLOCKS) & (j < IN_MAIN_BLOCKS + IN_GATE_BLOCKS))
    def _():
        z_ref[...] = _sigmoid(_dot_nt(hb_ref[...], w_ref[...].astype(BF16)))

    @pl.when(j >= IN_MAIN_BLOCKS + IN_GATE_BLOCKS)
    def _():
        z_ref[:, 0:ZL_WIDTH] = token_shift(_dot_nt(hb_ref[...], w_ref[0:ZL_WIDTH, :].astype(BF16)))
        z_ref[:, ZL_WIDTH:] = jnp.zeros((z_ref.shape[0], z_ref.shape[1] - ZL_WIDTH), F32)


def _in_proj(x, g, b, w_t, col_params, tm=1024):
    t, d = x.shape
    tn = IN_TN
    nb = IN_BLOCKS
    n_out = w_t.shape[0]
    tm = min(tm, t)
    return pl.pallas_call(
        _in_proj_kernel,
        grid=(t // tm, nb),
        in_specs=[
            pl.BlockSpec((tm, d), lambda i, j: (i, 0)),
            pl.BlockSpec((1, d), lambda i, j: (0, 0)),
            pl.BlockSpec((1, d), lambda i, j: (0, 0)),
            pl.BlockSpec((pl.Element(tn), pl.Element(d)),
                         lambda i, j: (pl.multiple_of(_in_weight_row(j, n_out), 8), 0)),
            pl.BlockSpec((CP_ROWS, tn), lambda i, j: (0, j)),
        ],
        out_specs=pl.BlockSpec((tm, tn), lambda i, j: (i, _in_out_block(j))),
        out_shape=jax.ShapeDtypeStruct((t, nb * tn), F32),
        scratch_shapes=[pltpu.VMEM((tm, d), BF16), pltpu.VMEM((nb, 8, tn), F32)],
        compiler_params=_params(("arbitrary", "arbitrary")),
        name="in_proj",
    )(x, g, b, w_t, col_params)


def _lru_kernel(u_ref, gate_ref, ga_ref, wa_ref, ba_ref, wx_ref, bx_ref,
                lam_ref, out_ref, hcarry):
    @pl.when(pl.program_id(0) == 0)
    def _():
        hcarry[...] = jnp.zeros_like(hcarry)

    conv = u_ref[...]
    tm = conv.shape[0]
    cb16 = conv.astype(BF16)
    r_parts, i_parts = [], []
    for g in range(LRU_HEADS):
        ug = cb16[:, g * LRU_HEAD_DIM:(g + 1) * LRU_HEAD_DIM]
        r_parts.append(_dot(ug, wa_ref[g]))
        i_parts.append(_dot(ug, wx_ref[g]))
    r = _sigmoid(jnp.concatenate(r_parts, axis=1) + ba_ref[...])
    ig = _sigmoid(jnp.concatenate(i_parts, axis=1) + bx_ref[...])

    log_a = (-LRU_C) * r * _softplus(-lam_ref[...])
    a_c = jnp.exp(log_a)
    om = -jnp.tanh(log_a) * (a_c * a_c + 1.0)
    b_c = jnp.where(om > 0.0, om * lax.rsqrt(om), 0.0) * (ig * conv)

    n8 = tm // 8
    a3 = a_c.reshape(n8, 8, a_c.shape[1])
    b3 = b_c.reshape(n8, 8, b_c.shape[1])
    sub = lax.broadcasted_iota(jnp.int32, a3.shape, 1)
    for d in (1, 2, 4):
        m = sub >= d
        a_sh = jnp.where(m, pltpu.roll(a3, d, 1), 1.0)
        b_sh = jnp.where(m, pltpu.roll(b3, d, 1), 0.0)
        b3 = a3 * b_sh + b3
        a3 = a3 * a_sh
    carry = hcarry[...]
    hs = []
    for i in range(n8):
        h_i = b3[i] + a3[i] * carry
        hs.append(h_i)
        carry = h_i[7:8, :]
    hcarry[...] = carry
    h = jnp.concatenate(hs, axis=0)

    out_ref[...] = ga_ref[...] * (gate_ref[...] * h)


def _lru(zin, wa, ba, wx, bx, lam, tm=256):
    t = zin.shape[0]
    d = D_MODEL
    row = lambda i: (0, 0)
    return pl.pallas_call(
        _lru_kernel,
        grid=(t // tm,),
        in_specs=[
            pl.BlockSpec((tm, d), lambda i: (i, 0)),
            pl.BlockSpec((tm, d), lambda i: (i, 1)),
            pl.BlockSpec((tm, d), lambda i: (i, 2)),
            pl.BlockSpec((LRU_HEADS, LRU_HEAD_DIM, LRU_HEAD_DIM), lambda i: (0, 0, 0)),
            pl.BlockSpec((1, d), row),
            pl.BlockSpec((LRU_HEADS, LRU_HEAD_DIM, LRU_HEAD_DIM), lambda i: (0, 0, 0)),
            pl.BlockSpec((1, d), row),
            pl.BlockSpec((1, d), row),
        ],
        out_specs=pl.BlockSpec((tm, d), lambda i: (i, 0)),
        out_shape=jax.ShapeDtypeStruct((t, d), F32),
        scratch_shapes=[pltpu.VMEM((1, d), F32)],
        compiler_params=_params(("arbitrary",)),
        name="lru",
    )(zin, zin, zin, wa, ba, wx, bx, lam)


def _block_ones(n, seg):
    r = lax.broadcasted_iota(jnp.int32, (n, n), 0) // seg
    c = lax.broadcasted_iota(jnp.int32, (n, n), 1) // seg
    return jnp.where(r == c, 1.0, 0.0).astype(BF16)


REC_ROWS = 256


def _round_robin(*gens):
    live = list(gens)
    while live:
        for gen in list(live):
            try:
                next(gen)
            except StopIteration:
                live.remove(gen)
        yield


def _interleave(*gens):
    for _ in _round_robin(*gens):
        pass


def _chain(*gens):
    for gen in gens:
        yield from gen


def _rwkv_rec_kernel(r_ref, k_ref, v_ref, zl_ref, zgb_ref, w0_ref, wb_ref, a0_ref, ab_ref, gb_ref,
                     kk_ref, ka_ref, rk_ref, gng_ref, gnb_ref, out_ref, s_ref):
    @pl.when(pl.program_id(0) == 0)
    def _():
        s_ref[...] = jnp.zeros_like(s_ref)

    c = CHUNK
    hd = RWKV_HEAD_DIM
    ng = N_GROUPS
    nch = r_ref.shape[0] // c
    sls = [slice(g * GROUP, (g + 1) * GROUP) for g in range(ng)]

    assert nch % 2 == 0
    rows = lax.broadcasted_iota(jnp.int32, (2 * c, 2 * c), 0)
    cols = lax.broadcasted_iota(jnp.int32, (2 * c, 2 * c), 1)
    tri2 = jnp.where((cols <= rows) & (cols // c == rows // c), 1.0, 0.0).astype(BF16)
    t_c = lax.broadcasted_iota(jnp.int32, (c, GROUP), 0)
    s_c = lax.broadcasted_iota(jnp.int32, (c, GROUP), 1) % hd
    strict = s_c < t_c
    incl = s_c <= t_c
    eye_c = jnp.where(s_c == t_c, 1.0, 0.0)
    bd = (lax.broadcasted_iota(jnp.int32, (GROUP, GROUP), 0) // hd
          == lax.broadcasted_iota(jnp.int32, (GROUP, GROUP), 1) // hd)
    ones_bd = _block_ones(GROUP, hd)

    blk4 = (t_c >> 2) == (s_c >> 2)
    diag3 = (t_c - s_c) == 3

    def inverse_blocks4(l):
        m = jnp.where(blk4, l, 0.0)
        r1 = pltpu.roll(m, GROUP - 1, 1)
        r2 = pltpu.roll(m, GROUP - 2, 1)
        d1 = pltpu.roll(m, 1, 0)
        d2 = pltpu.roll(m, 2, 0)
        d1r1 = pltpu.roll(d1, GROUP - 1, 1)
        corr = jnp.where(diag3, r2 * d1 + r1 * d2 - r2 * d1r1 * d2, r1 * d1)
        return eye_c - jnp.where(blk4, m - corr, 0.0)

    def expand(x_c):
        return jnp.where(bd, jnp.concatenate([x_c] * (GROUP // hd), axis=0), 0.0).astype(BF16)

    def seg_sums(xs):
        hi, lo = _split2(jnp.concatenate(xs, axis=0))
        s = _dot(jnp.concatenate([hi, lo], axis=0), ones_bd)
        n = c * len(xs)
        s = s[:n] + s[n:]
        return [s[i * c:(i + 1) * c] for i in range(len(xs))]

    state = [s_ref[g] for g in range(ng)]
    prep = [None] * nch

    pair_prep = {}

    def lora_stage(pi):
        rs_ = slice(2 * pi * c, (2 * pi + 2) * c)
        zl = zl_ref[rs_, :]
        nl = zl.shape[1]
        w_lo = zl[:, 0:D_LORA_PAD]
        a_lo = pltpu.roll(zl, nl - D_LORA, 1)[:, 0:D_LORA_PAD]
        g_lo = pltpu.roll(zl, nl - 2 * D_LORA, 1)[:, 0:D_GATE_LORA]
        w_log = -_softplus_log(-(w0_ref[...] + _dot(jnp.tanh(w_lo).astype(BF16), wb_ref[...]))) - 0.5
        lw = -jnp.exp(w_log)
        a = _sigmoid(a0_ref[...] + _dot(a_lo.astype(BF16), ab_ref[...]))
        gate = _dot(_sigmoid(g_lo).astype(BF16), gb_ref[...])
        hi, mid, lo = _split3(lw)
        cum = _dot(tri2, hi) + _dot(tri2, mid) + _dot(tri2, lo)
        pair_prep[pi] = dict(lw=lw, a=a, gate=gate, cum=cum)
        yield

    def phase_a(ci):
        rs_ = slice(ci * c, (ci + 1) * c)
        half = slice((ci % 2) * c, (ci % 2 + 1) * c)
        pp = pair_prep[ci // 2]
        lw, a, gate, cum = pp["lw"][half], pp["a"][half], pp["gate"][half], pp["cum"][half]
        tot = cum[c - 1:c, :]
        k = k_ref[rs_, :]
        kk = k * kk_ref[...]
        kk2 = kk * kk
        n2 = jnp.concatenate(seg_sums([kk2[:, sl] for sl in sls]), axis=1)
        kn = kk * lax.rsqrt(jnp.maximum(n2, 1e-24))
        bb = kn * a
        kf = k * (1.0 + (a - 1.0) * ka_ref[...])
        yield
        r = r_ref[rs_, :]
        v = v_ref[rs_, :]
        p_inv = jnp.exp(-cum)
        p_end = jnp.exp(tot - cum)
        rq = r * jnp.exp(cum)
        kap = kn * jnp.exp(cum - lw)
        bet = bb * p_inv
        kt = kf * p_inv
        lhs2 = [jnp.concatenate([kap[:, sl], rq[:, sl]], axis=0).astype(BF16) for sl in sls]
        amat = [_dot_nt(lhs2[g], jnp.concatenate([expand(bet[:, sls[g]]), expand(kt[:, sls[g]])], axis=0))
                for g in range(ng)]
        yield
        l_c = [jnp.where(strict, a[:c, :GROUP], 0.0) for a in amat]
        a_lo = [jnp.concatenate([jnp.where(strict, a[:c, GROUP:], 0.0),
                                 jnp.where(incl, a[c:, GROUP:], 0.0)], axis=0).astype(BF16) for a in amat]
        arb = [jnp.where(incl, a[c:, :GROUP], 0.0).astype(BF16) for a in amat]
        av = [_dot(a_lo[g], expand(v[:, sls[g]])) for g in range(ng)]
        x_c = [inverse_blocks4(l) for l in l_c]
        yield
        for lvl in range(3, 7):
            lmask = ((t_c >> lvl) == (s_c >> lvl)) & ((t_c >> (lvl - 1)) != (s_c >> (lvl - 1)))
            y_c = [_dot(x_c[g].astype(BF16), expand(jnp.where(lmask, l_c[g], 0.0))) for g in range(ng)]
            yield
            x_c = [x_c[g] - _dot(y_c[g].astype(BF16), expand(x_c[g])) for g in range(ng)]
            yield
        bhat_neg = -(bb * p_end)
        khat = kf * p_end
        upd_rhs = [jnp.concatenate([bhat_neg[:, sl], khat[:, sl]], axis=0).astype(BF16) for sl in sls]
        prep[ci] = dict(lhs2=lhs2, av=av, arb=arb, x=[x.astype(BF16) for x in x_c], v=v,
                        upd_rhs=upd_rhs, p_tot=jnp.exp(tot), rkk=r * kf * rk_ref[...], gate=gate)

    def phase_b(ci):
        p = prep[ci]
        rs_ = slice(ci * c, (ci + 1) * c)
        v = p["v"]
        rs = [_dot_nt(p["lhs2"][g], state[g].astype(BF16)) for g in range(ng)]
        yield
        u_c = [_dot(p["x"][g], expand(rs[g][:c] + p["av"][g][:c])) for g in range(ng)]
        yield
        o_c = [rs[g][c:] + p["av"][g][c:] - _dot(p["arb"][g], expand(u_c[g])) for g in range(ng)]
        for g in range(ng):
            upd = _dot_tn(jnp.concatenate([u_c[g], v[:, sls[g]]], axis=0).astype(BF16), p["upd_rhs"][g])
            state[g] = state[g] * p["p_tot"][:, sls[g]] + jnp.where(bd, upd, 0.0)
        yield
        sums = seg_sums(o_c + [p["rkk"][:, sl] for sl in sls])
        dev = [o_c[g] - sums[g] * (1.0 / hd) for g in range(ng)]
        yield
        var = seg_sums([dv * dv for dv in dev])
        o_parts = []
        for g in range(ng):
            o_n = dev[g] * lax.rsqrt(var[g] * (1.0 / hd) + GN_EPS) * gng_ref[:, sls[g]] + gnb_ref[:, sls[g]]
            o_parts.append(o_n + sums[ng + g] * v[:, sls[g]])
        o = jnp.concatenate(o_parts, axis=1)
        out_ref[rs_, :] = zgb_ref[rs_, :] * (o * p["gate"])

    def pair_a(pi):
        yield from lora_stage(pi)
        yield from _round_robin(phase_a(2 * pi), phase_a(2 * pi + 1))

    n_pairs = nch // 2
    _interleave(pair_a(0))
    for k in range(n_pairs):
        b_gen = _chain(phase_b(2 * k), phase_b(2 * k + 1))
        _interleave(b_gen, *([pair_a(k + 1)] if k + 1 < n_pairs else []))

    for g in range(ng):
        s_ref[g] = state[g]


def _rwkv_rec(zin, w0, wb, a0, ab, gb, k_k, k_a, rk, gng, gnb):
    t = zin.shape[0]
    d = D_MODEL
    dl = ZL_WIDTH
    br = min(REC_ROWS, t)
    zcol = lambda c: pl.BlockSpec((br, d), lambda i: (i, c))
    row = pl.BlockSpec((1, d), lambda i: (0, 0))
    return pl.pallas_call(
        _rwkv_rec_kernel,
        grid=(t // br,),
        in_specs=[zcol(4), zcol(5), zcol(6), pl.BlockSpec((br, dl), lambda i: (i, 7 * d // dl)), zcol(3),
                  row, _resident((D_LORA_PAD, d)), row, _resident((D_LORA_PAD, d)),
                  _resident((D_GATE_LORA, d)), row, row, row, row, row],
        out_specs=pl.BlockSpec((br, d), lambda i: (i, 0)),
        out_shape=jax.ShapeDtypeStruct((t, d), F32),
        scratch_shapes=[pltpu.VMEM((N_GROUPS, GROUP, GROUP), F32)],
        compiler_params=_params(("arbitrary",), VMEM_LIMIT_BIG_TILES),
        name="rwkv_rec",
    )(zin, zin, zin, zin, zin, w0, wb, a0, ab, gb, k_k, k_a, rk, gng, gnb)


def _mem_kv_kernel(mem_ref, wk_ref, wv_ref, k_ref, v_ref):
    m = mem_ref[...].astype(BF16)
    k_ref[...] = _dot(m, wk_ref[...].astype(BF16)).astype(BF16)
    v_ref[...] = _dot(m, wv_ref[...].astype(BF16)).astype(BF16)


def _mem_kv(mem, wk, wv, tn=512):
    n, d = mem.shape
    return pl.pallas_call(
        _mem_kv_kernel,
        grid=(d // tn,),
        in_specs=[pl.BlockSpec((n, d), lambda j: (0, 0)),
                  pl.BlockSpec((d, tn), lambda j: (0, j)),
                  pl.BlockSpec((d, tn), lambda j: (0, j))],
        out_specs=[pl.BlockSpec((n, tn), lambda j: (0, j))] * 2,
        out_shape=[jax.ShapeDtypeStruct((n, d), BF16)] * 2,
        compiler_params=_params(("arbitrary",)),
        name="mem_kv",
    )(mem, wk, wv)


def _mix_xattn_kernel(ya_ref, yb_ref, x_ref, gin_ref, bin_ref, wout_ref, g1_ref, b1_ref,
                      wq_ref, k_ref, v_ref, wo_ref, g_ref, b_ref, o_ref):
    y = (ya_ref[...] + yb_ref[...]).astype(BF16)
    mix = _dot(y, wout_ref[...])
    h0 = _layer_norm(x_ref[...], gin_ref[...], bin_ref[...])
    h = _layer_norm(ALPHA * h0 + mix, g1_ref[...], b1_ref[...])
    q = _dot(h.astype(BF16), wq_ref[...]).astype(BF16)
    scale = XATTN_HEAD_DIM ** -0.5
    outs = []
    for hh in range(XATTN_HEADS):
        sl = slice(hh * XATTN_HEAD_DIM, (hh + 1) * XATTN_HEAD_DIM)
        s = _dot_nt(q[:, sl], k_ref[:, sl]) * scale
        e = jnp.exp(s - jnp.max(s, axis=-1, keepdims=True))
        p = e / jnp.sum(e, axis=-1, keepdims=True)
        outs.append(_dot(p.astype(BF16), v_ref[:, sl]))
    o = jnp.concatenate(outs, axis=1).astype(BF16)
    xa = _dot(o, wo_ref[...])
    o_ref[...] = _layer_norm(ALPHA * h + xa, g_ref[...], b_ref[...])


def _mix_xattn(ya, yb, x, g_in, b_in, w_out, g1, b1, wq, kmem, vmem, wo, g2, b2, tm=256):
    t, d = x.shape
    n = kmem.shape[0]
    blk = pl.BlockSpec((tm, d), lambda i: (i, 0))
    row = pl.BlockSpec((1, d), lambda i: (0, 0))
    return pl.pallas_call(
        _mix_xattn_kernel,
        grid=(t // tm,),
        in_specs=[blk, blk, blk, row, row, _resident((d, d)), row, row,
                  _resident((d, d)), _resident((n, d)), _resident((n, d)), _resident((d, d)), row, row],
        out_specs=blk,
        out_shape=jax.ShapeDtypeStruct((t, d), F32),
        compiler_params=_params(("arbitrary",)),
        name="mix_xattn",
    )(ya, yb, x, g_in, b_in, w_out, g1, b1, wq, kmem, vmem, wo, g2, b2)


def _ffn_kernel(h_ref, wg_ref, wu_ref, wd_ref, g_ref, b_ref, o_ref):
    j = pl.program_id(1)

    @pl.when(j == 0)
    def _():
        o_ref[...] = jnp.zeros_like(o_ref)

    hb = h_ref[...].astype(BF16)
    gate = _dot(hb, wg_ref[...].astype(BF16))
    up = _dot(hb, wu_ref[...].astype(BF16))
    act = (gate * _sigmoid(gate) * up).astype(BF16)
    o_ref[...] += _dot(act, wd_ref[...].astype(BF16))

    @pl.when(j == pl.num_programs(1) - 1)
    def _():
        o_ref[...] = _layer_norm(ALPHA * h_ref[...] + o_ref[...], g_ref[...], b_ref[...])


def _ffn(h, wg, wu, wd, g, b, tm=1024, tf=256):
    t, d = h.shape
    f = wg.shape[1]
    tm = min(tm, t)
    blk = pl.BlockSpec((tm, d), lambda i, j: (i, 0))
    row = pl.BlockSpec((1, d), lambda i, j: (0, 0))
    return pl.pallas_call(
        _ffn_kernel,
        grid=(t // tm, f // tf),
        in_specs=[blk,
                  pl.BlockSpec((d, tf), lambda i, j: (0, j)),
                  pl.BlockSpec((d, tf), lambda i, j: (0, j)),
                  pl.BlockSpec((tf, d), lambda i, j: (j, 0)),
                  row, row],
        out_specs=blk,
        out_shape=jax.ShapeDtypeStruct((t, d), F32),
        compiler_params=_params(("arbitrary", "arbitrary"), VMEM_LIMIT_BIG_TILES),
        name="ffn",
    )(h, wg, wu, wd, g, b)


def _pad_cols(w, n):
    return jnp.pad(w, ((0, 0), (0, n - w.shape[1])))


def _pad_rows(w, n):
    return jnp.pad(w, ((0, n - w.shape[0]), (0, 0)))


def kernel(x, mem, ln_in_g, ln_in_b, w_in, conv_w, conv_b, lru_wa, lru_ba, lru_wx, lru_bx, lru_lambda, rw_mu, rw_w0, rw_wB, rw_a0, rw_aB, rw_gB, rw_kk, rw_ka, rw_rk, rw_gn_g, rw_gn_b, w_out, ln1_g, ln1_b, xa_wq, xa_wk, xa_wv, xa_wo, ln2_g, ln2_b, ffn_wg, ffn_wu, ffn_wd, ln3_g, ln3_b):
    bsz, t, d = x.shape
    depth = w_in.shape[0]
    assert bsz == 1 and d == D_MODEL and t % 512 == 0
    row = lambda p: p.reshape(1, -1)

    h = None
    for l in range(depth):
        w_t = jnp.swapaxes(w_in[l], 0, 1)
        n_cols = IN_BLOCKS * IN_TN
        mu = row(rw_mu[l])
        mu_steps = jnp.concatenate(
            [jnp.zeros((1, 2 * d), F32), mu[:, :3 * d], jnp.zeros((1, 2 * d), F32), mu[:, 3 * d:]], axis=1)
        col_params = jnp.concatenate([
            _pad_cols(mu_steps, n_cols),
            _pad_cols(conv_w[l], n_cols),
            _pad_cols(row(conv_b[l]), n_cols),
            jnp.zeros((CP_ROWS - CP_CONV_B - 1, n_cols), F32)], axis=0)
        if l == 0:
            zin = _in_proj(x[0], row(ln_in_g), row(ln_in_b), w_t, col_params)
        else:
            raise NotImplementedError("DEPTH > 1 is not part of this problem")

        ya = _lru(zin, lru_wa[l].astype(BF16), row(lru_ba[l]),
                  lru_wx[l].astype(BF16), row(lru_bx[l]), row(lru_lambda[l]))

        yb = _rwkv_rec(
            zin, row(rw_w0[l]),
            _pad_rows(rw_wB[l], D_LORA_PAD).astype(BF16), row(rw_a0[l]),
            _pad_rows(rw_aB[l], D_LORA_PAD).astype(BF16), rw_gB[l].astype(BF16),
            row(rw_kk[l]), row(rw_ka[l]), row(rw_rk[l]), row(rw_gn_g[l]), row(rw_gn_b[l]))

        kmem, vmem = _mem_kv(mem[0], xa_wk[l], xa_wv[l])
        h = _mix_xattn(ya, yb, x[0], row(ln_in_g), row(ln_in_b), w_out[l].astype(BF16),
                       row(ln1_g[l]), row(ln1_b[l]), xa_wq[l].astype(BF16), kmem, vmem,
                       xa_wo[l].astype(BF16), row(ln2_g[l]), row(ln2_b[l]))
        h = _ffn(h, ffn_wg[l], ffn_wu[l], ffn_wd[l], row(ln3_g[l]), row(ln3_b[l]))
    return h[None]
```

```python
import jax
import jax.numpy as jnp
from jax import lax
from jax.experimental import pallas as pl
from jax.experimental.pallas import tpu as pltpu

F32 = jnp.float32
BF16 = jnp.bfloat16

D_MODEL = 2048
LN_EPS = 1e-5
ALPHA = 2.0 ** 0.25

LRU_HEADS = 16
LRU_HEAD_DIM = 128
CONV_WIDTH = 4
LRU_C = 8.0

RWKV_HEAD_DIM = 64
D_LORA = 96
D_LORA_PAD = 128
D_GATE_LORA = 256
ZL_WIDTH = 512
GN_EPS = 64e-5
CHUNK = 64
GROUP = 256
N_GROUPS = D_MODEL // GROUP

XATTN_HEADS = 4
XATTN_HEAD_DIM = 512

VMEM_LIMIT = 56 * 1024 * 1024
VMEM_LIMIT_BIG_TILES = 60 * 1024 * 1024


def _dot(a, b):
    return jnp.dot(a, b, preferred_element_type=F32)


def _dot_nt(a, b):
    return lax.dot_general(a, b, (((1,), (1,)), ((), ())), preferred_element_type=F32)


def _dot_tn(a, b):
    return lax.dot_general(a, b, (((0,), (0,)), ((), ())), preferred_element_type=F32)


def _split2(x):
    hi = x.astype(BF16)
    lo = (x - hi.astype(F32)).astype(BF16)
    return hi, lo


def _split3(x):
    hi = x.astype(BF16)
    r1 = x - hi.astype(F32)
    mid = r1.astype(BF16)
    lo = (r1 - mid.astype(F32)).astype(BF16)
    return hi, mid, lo


def _sigmoid(x):
    return 1.0 / (1.0 + jnp.exp(-x))


def _softplus(x):
    return jnp.maximum(x, 0.0) + jnp.log1p(jnp.exp(-jnp.abs(x)))


def _softplus_log(x):
    return jnp.maximum(x, 0.0) + jnp.log(1.0 + jnp.exp(-jnp.abs(x)))


def _gelu_tanh(x):
    c = 0.7978845608028654
    return 0.5 * x * (1.0 + jnp.tanh(c * (x + 0.044715 * (x * x * x))))


def _layer_norm(x, g, b):
    mu = jnp.mean(x, axis=-1, keepdims=True)
    xc = x - mu
    var = jnp.mean(xc * xc, axis=-1, keepdims=True)
    return xc * lax.rsqrt(var + LN_EPS) * g + b


def _shift_rows(z, prev8, s):
    rolled = pltpu.roll(z, s, 0)
    row8 = lax.broadcasted_iota(jnp.int32, prev8.shape, 0)
    head = jnp.where(row8 < s, pltpu.roll(prev8, s, 0), rolled[:8])
    return jnp.concatenate([head, rolled[8:]], axis=0)


def _params(sem, vmem_limit=VMEM_LIMIT):
    return pltpu.CompilerParams(dimension_semantics=sem, vmem_limit_bytes=vmem_limit)


def _resident(shape):
    return pl.BlockSpec(shape, lambda *_: (0,) * len(shape), pipeline_mode=pl.Buffered(1))


IN_TN = 1024
IN_D_BLOCKS = D_MODEL // IN_TN
IN_MAIN_BLOCKS = 5 * IN_D_BLOCKS
IN_GATE_BLOCKS = 2 * IN_D_BLOCKS
IN_BLOCKS = IN_MAIN_BLOCKS + IN_GATE_BLOCKS + 1


def _in_weight_row(j, n_out):
    gates = n_out - 2 * D_MODEL + (j - IN_MAIN_BLOCKS) * IN_TN
    return jnp.where(j < IN_MAIN_BLOCKS, j * IN_TN,
                     jnp.where(j < IN_MAIN_BLOCKS + IN_GATE_BLOCKS, gates, 5 * D_MODEL))


def _in_out_block(j):
    b2 = 2 * D_MODEL // IN_TN
    return jnp.where(j < b2, j,
                     jnp.where(j < IN_MAIN_BLOCKS, j + b2,
                               jnp.where(j < IN_MAIN_BLOCKS + b2, j - (IN_MAIN_BLOCKS - b2), j)))


CP_MU, CP_CONV_W, CP_CONV_B, CP_ROWS = 0, 1, 1 + CONV_WIDTH, 8


def _in_proj_kernel(x_ref, g_ref, b_ref, w_ref, cp_ref, z_ref, hb_ref, carry_ref):
    i = pl.program_id(0)
    j = pl.program_id(1)
    nd = IN_D_BLOCKS

    @pl.when(j == 0)
    def _():
        hb_ref[...] = _layer_norm(x_ref[...], g_ref[...], b_ref[...]).astype(BF16)

    @pl.when((i == 0) & (j == 0))
    def _():
        carry_ref[...] = jnp.zeros_like(carry_ref)

    def keep_tail(z):
        carry_ref[j] = z[z.shape[0] - 8:, :]

    @pl.when(j < nd)
    def _():
        z = _dot_nt(hb_ref[...], w_ref[...].astype(BF16))
        prev8 = carry_ref[j]
        cw = CP_CONV_W + CONV_WIDTH - 1
        conv = cp_ref[CP_CONV_B:CP_CONV_B + 1, :] + z * cp_ref[cw:cw + 1, :]
        for s in range(1, CONV_WIDTH):
            conv = conv + _shift_rows(z, prev8, s) * cp_ref[cw - s:cw - s + 1, :]
        keep_tail(z)
        z_ref[...] = conv

    @pl.when((j >= nd) & (j < 2 * nd))
    def _():
        z_ref[...] = _gelu_tanh(_dot_nt(hb_ref[...], w_ref[...].astype(BF16)))

    def token_shift(z):
        n = z.shape[1]
        zp = _shift_rows(z, carry_ref[j][:, 0:n], 1)
        carry_ref[j, :, 0:n] = z[z.shape[0] - 8:, :]
        return z + (zp - z) * cp_ref[CP_MU:CP_MU + 1, 0:n]

    @pl.when((j >= 2 * nd) & (j < IN_MAIN_BLOCKS))
    def _():
        z_ref[...] = token_shift(_dot_nt(hb_ref[...], w_ref[...].astype(BF16)))

    @pl.when((j >= IN_MAIN_BLOCKS) & (j < IN_MAIN_BLOCKS + IN_GATE_BLOCKS))
    def _():
        z_ref[...] = _sigmoid(_dot_nt(hb_ref[...], w_ref[...].astype(BF16)))

    @pl.when(j >= IN_MAIN_BLOCKS + IN_GATE_BLOCKS)
    def _():
        z_ref[:, 0:ZL_WIDTH] = token_shift(_dot_nt(hb_ref[...], w_ref[0:ZL_WIDTH, :].astype(BF16)))
        z_ref[:, ZL_WIDTH:] = jnp.zeros((z_ref.shape[0], z_ref.shape[1] - ZL_WIDTH), F32)


def _in_proj(x, g, b, w_t, col_params, tm=1024):
    t, d = x.shape
    tn = IN_TN
    nb = IN_BLOCKS
    n_out = w_t.shape[0]
    tm = min(tm, t)
    return pl.pallas_call(
        _in_proj_kernel,
        grid=(t // tm, nb),
        in_specs=[
            pl.BlockSpec((tm, d), lambda i, j: (i, 0)),
            pl.BlockSpec((1, d), lambda i, j: (0, 0)),
            pl.BlockSpec((1, d), lambda i, j: (0, 0)),
            pl.BlockSpec((pl.Element(tn), pl.Element(d)),
                         lambda i, j: (pl.multiple_of(_in_weight_row(j, n_out), 8), 0)),
            pl.BlockSpec((CP_ROWS, tn), lambda i, j: (0, j)),
        ],
        out_specs=pl.BlockSpec((tm, tn), lambda i, j: (i, _in_out_block(j))),
        out_shape=jax.ShapeDtypeStruct((t, nb * tn), F32),
        scratch_shapes=[pltpu.VMEM((tm, d), BF16), pltpu.VMEM((nb, 8, tn), F32)],
        compiler_params=_params(("arbitrary", "arbitrary")),
        name="in_proj",
    )(x, g, b, w_t, col_params)


def _lru_kernel(u_ref, gate_ref, ga_ref, wa_ref, ba_ref, wx_ref, bx_ref,
                lam_ref, out_ref, hcarry):
    @pl.when(pl.program_id(0) == 0)
    def _():
        hcarry[...] = jnp.zeros_like(hcarry)

    conv = u_ref[...]
    tm = conv.shape[0]
    cb16 = conv.astype(BF16)
    r_parts, i_parts = [], []
    for g in range(LRU_HEADS):
        ug = cb16[:, g * LRU_HEAD_DIM:(g + 1) * LRU_HEAD_DIM]
        r_parts.append(_dot(ug, wa_ref[g]))
        i_parts.append(_dot(ug, wx_ref[g]))
    r = _sigmoid(jnp.concatenate(r_parts, axis=1) + ba_ref[...])
    ig = _sigmoid(jnp.concatenate(i_parts, axis=1) + bx_ref[...])

    log_a = (-LRU_C) * r * _softplus(-lam_ref[...])
    a_c = jnp.exp(log_a)
    om = -jnp.tanh(log_a) * (a_c * a_c + 1.0)
    b_c = jnp.where(om > 0.0, om * lax.rsqrt(om), 0.0) * (ig * conv)

    n8 = tm // 8
    a3 = a_c.reshape(n8, 8, a_c.shape[1])
    b3 = b_c.reshape(n8, 8, b_c.shape[1])
    sub = lax.broadcasted_iota(jnp.int32, a3.shape, 1)
    for d in (1, 2, 4):
        m = sub >= d
        a_sh = jnp.where(m, pltpu.roll(a3, d, 1), 1.0)
        b_sh = jnp.where(m, pltpu.roll(b3, d, 1), 0.0)
        b3 = a3 * b_sh + b3
        a3 = a3 * a_sh
    carry = hcarry[...]
    hs = []
    for i in range(n8):
        h_i = b3[i] + a3[i] * carry
        hs.append(h_i)
        carry = h_i[7:8, :]
    hcarry[...] = carry
    h = jnp.concatenate(hs, axis=0)

    out_ref[...] = ga_ref[...] * (gate_ref[...] * h)


def _lru(zin, wa, ba, wx, bx, lam, tm=256):
    t = zin.shape[0]
    d = D_MODEL
    row = lambda i: (0, 0)
    return pl.pallas_call(
        _lru_kernel,
        grid=(t // tm,),
        in_specs=[
            pl.BlockSpec((tm, d), lambda i: (i, 0)),
            pl.BlockSpec((tm, d), lambda i: (i, 1)),
            pl.BlockSpec((tm, d), lambda i: (i, 2)),
            pl.BlockSpec((LRU_HEADS, LRU_HEAD_DIM, LRU_HEAD_DIM), lambda i: (0, 0, 0)),
            pl.BlockSpec((1, d), row),
            pl.BlockSpec((LRU_HEADS, LRU_HEAD_DIM, LRU_HEAD_DIM), lambda i: (0, 0, 0)),
            pl.BlockSpec((1, d), row),
            pl.BlockSpec((1, d), row),
        ],
        out_specs=pl.BlockSpec((tm, d), lambda i: (i, 0)),
        out_shape=jax.ShapeDtypeStruct((t, d), F32),
        scratch_shapes=[pltpu.VMEM((1, d), F32)],
        compiler_params=_params(("arbitrary",)),
        name="lru",
    )(zin, zin, zin, wa, ba, wx, bx, lam)


def _block_ones(n, seg):
    r = lax.broadcasted_iota(jnp.int32, (n, n), 0) // seg
    c = lax.broadcasted_iota(jnp.int32, (n, n), 1) // seg
    return jnp.where(r == c, 1.0, 0.0).astype(BF16)


REC_ROWS = 256


def _round_robin(*gens):
    live = list(gens)
    while live:
        for gen in list(live):
            try:
                next(gen)
            except StopIteration:
                live.remove(gen)
        yield


def _interleave(*gens):
    for _ in _round_robin(*gens):
        pass


def _chain(*gens):
    for gen in gens:
        yield from gen


def _rwkv_rec_kernel(r_ref, k_ref, v_ref, zl_ref, zgb_ref, w0_ref, wb_ref, a0_ref, ab_ref, gb_ref,
                     kk_ref, ka_ref, rk_ref, gng_ref, gnb_ref, out_ref, s_ref):
    @pl.when(pl.program_id(0) == 0)
    def _():
        s_ref[...] = jnp.zeros_like(s_ref)

    c = CHUNK
    hd = RWKV_HEAD_DIM
    ng = N_GROUPS
    nch = r_ref.shape[0] // c
    sls = [slice(g * GROUP, (g + 1) * GROUP) for g in range(ng)]

    assert nch % 2 == 0
    rows = lax.broadcasted_iota(jnp.int32, (2 * c, 2 * c), 0)
    cols = lax.broadcasted_iota(jnp.int32, (2 * c, 2 * c), 1)
    tri2 = jnp.where((cols <= rows) & (cols // c == rows // c), 1.0, 0.0).astype(BF16)
    t_c = lax.broadcasted_iota(jnp.int32, (c, GROUP), 0)
    s_c = lax.broadcasted_iota(jnp.int32, (c, GROUP), 1) % hd
    strict = s_c < t_c
    incl = s_c <= t_c
    eye_c = jnp.where(s_c == t_c, 1.0, 0.0)
    bd = (lax.broadcasted_iota(jnp.int32, (GROUP, GROUP), 0) // hd
          == lax.broadcasted_iota(jnp.int32, (GROUP, GROUP), 1) // hd)
    ones_bd = _block_ones(GROUP, hd)

    blk4 = (t_c >> 2) == (s_c >> 2)
    diag3 = (t_c - s_c) == 3

    def inverse_blocks4(l):
        m = jnp.where(blk4, l, 0.0)
        r1 = pltpu.roll(m, GROUP - 1, 1)
        r2 = pltpu.roll(m, GROUP - 2, 1)
        d1 = pltpu.roll(m, 1, 0)
        d2 = pltpu.roll(m, 2, 0)
        d1r1 = pltpu.roll(d1, GROUP - 1, 1)
        corr = jnp.where(diag3, r2 * d1 + r1 * d2 - r2 * d1r1 * d2, r1 * d1)
        return eye_c - jnp.where(blk4, m - corr, 0.0)

    def expand(x_c):
        return jnp.where(bd, jnp.concatenate([x_c] * (GROUP // hd), axis=0), 0.0).astype(BF16)

    def seg_sums(xs):
        hi, lo = _split2(jnp.concatenate(xs, axis=0))
        s = _dot(jnp.concatenate([hi, lo], axis=0), ones_bd)
        n = c * len(xs)
        s = s[:n] + s[n:]
        return [s[i * c:(i + 1) * c] for i in range(len(xs))]

    state = [s_ref[g] for g in range(ng)]
    prep = [None] * nch

    pair_prep = {}

    def lora_stage(pi):
        rs_ = slice(2 * pi * c, (2 * pi + 2) * c)
        zl = zl_ref[rs_, :]
        nl = zl.shape[1]
        w_lo = zl[:, 0:D_LORA_PAD]
        a_lo = pltpu.roll(zl, nl - D_LORA, 1)[:, 0:D_LORA_PAD]
        g_lo = pltpu.roll(zl, nl - 2 * D_LORA, 1)[:, 0:D_GATE_LORA]
        w_log = -_softplus_log(-(w0_ref[...] + _dot(jnp.tanh(w_lo).astype(BF16), wb_ref[...]))) - 0.5
        lw = -jnp.exp(w_log)
        a = _sigmoid(a0_ref[...] + _dot(a_lo.astype(BF16), ab_ref[...]))
        gate = _dot(_sigmoid(g_lo).astype(BF16), gb_ref[...])
        hi, mid, lo = _split3(lw)
        cum = _dot(tri2, hi) + _dot(tri2, mid) + _dot(tri2, lo)
        pair_prep[pi] = dict(lw=lw, a=a, gate=gate, cum=cum)
        yield

    def phase_a(ci):
        rs_ = slice(ci * c, (ci + 1) * c)
        half = slice((ci % 2) * c, (ci % 2 + 1) * c)
        pp = pair_prep[ci // 2]
        lw, a, gate, cum = pp["lw"][half], pp["a"][half], pp["gate"][half], pp["cum"][half]
        tot = cum[c - 1:c, :]
        k = k_ref[rs_, :]
        kk = k * kk_ref[...]
        kk2 = kk * kk
        n2 = jnp.concatenate(seg_sums([kk2[:, sl] for sl in sls]), axis=1)
        kn = kk * lax.rsqrt(jnp.maximum(n2, 1e-24))
        bb = kn * a
        kf = k * (1.0 + (a - 1.0) * ka_ref[...])
        yield
        r = r_ref[rs_, :]
        v = v_ref[rs_, :]
        p_inv = jnp.exp(-cum)
        p_end = jnp.exp(tot - cum)
        rq = r * jnp.exp(cum)
        kap = kn * jnp.exp(cum - lw)
        bet = bb * p_inv
        kt = kf * p_inv
        lhs2 = [jnp.concatenate([kap[:, sl], rq[:, sl]], axis=0).astype(BF16) for sl in sls]
        amat = [_dot_nt(lhs2[g], jnp.concatenate([expand(bet[:, sls[g]]), expand(kt[:, sls[g]])], axis=0))
                for g in range(ng)]
        yield
        l_c = [jnp.where(strict, a[:c, :GROUP], 0.0) for a in amat]
        a_lo = [jnp.concatenate([jnp.where(strict, a[:c, GROUP:], 0.0),
                                 jnp.where(incl, a[c:, GROUP:], 0.0)], axis=0).astype(BF16) for a in amat]
        arb = [jnp.where(incl, a[c:, :GROUP], 0.0).astype(BF16) for a in amat]
        av = [_dot(a_lo[g], expand(v[:, sls[g]])) for g in range(ng)]
        x_c = [inverse_blocks4(l) for l in l_c]
        yield
        for lvl in range(3, 7):
            lmask = ((t_c >> lvl) == (s_c >> lvl)) & ((t_c >> (lvl - 1)) != (s_c >> (lvl - 1)))
            y_c = [_dot(x_c[g].astype(BF16), expand(jnp.where(lmask, l_c[g], 0.0))) for g in range(ng)]
            yield
            x_c = [x_c[g] - _dot(y_c[g].astype(BF16), expand(x_c[g])) for g in range(ng)]
            yield
        bhat_neg = -(bb * p_end)
        khat = kf * p_end
        upd_rhs = [jnp.concatenate([bhat_neg[:, sl], khat[:, sl]], axis=0).astype(BF16) for sl in sls]
        prep[ci] = dict(lhs2=lhs2, av=av, arb=arb, x=[x.astype(BF16) for x in x_c], v=v,
                        upd_rhs=upd_rhs, p_tot=jnp.exp(tot), rkk=r * kf * rk_ref[...], gate=gate)

    def phase_b(ci):
        p = prep[ci]
        rs_ = slice(ci * c, (ci + 1) * c)
        v = p["v"]
        rs = [_dot_nt(p["lhs2"][g], state[g].astype(BF16)) for g in range(ng)]
        yield
        u_c = [_dot(p["x"][g], expand(rs[g][:c] + p["av"][g][:c])) for g in range(ng)]
        yield
        o_c = [rs[g][c:] + p["av"][g][c:] - _dot(p["arb"][g], expand(u_c[g])) for g in range(ng)]
        for g in range(ng):
            upd = _dot_tn(jnp.concatenate([u_c[g], v[:, sls[g]]], axis=0).astype(BF16), p["upd_rhs"][g])
            state[g] = state[g] * p["p_tot"][:, sls[g]] + jnp.where(bd, upd, 0.0)
        yield
        sums = seg_sums(o_c + [p["rkk"][:, sl] for sl in sls])
        dev = [o_c[g] - sums[g] * (1.0 / hd) for g in range(ng)]
        yield
        var = seg_sums([dv * dv for dv in dev])
        o_parts = []
        for g in range(ng):
            o_n = dev[g] * lax.rsqrt(var[g] * (1.0 / hd) + GN_EPS) * gng_ref[:, sls[g]] + gnb_ref[:, sls[g]]
            o_parts.append(o_n + sums[ng + g] * v[:, sls[g]])
        o = jnp.concatenate(o_parts, axis=1)
        out_ref[rs_, :] = zgb_ref[rs_, :] * (o * p["gate"])

    def pair_a(pi):
        yield from lora_stage(pi)
        yield from _round_robin(phase_a(2 * pi), phase_a(2 * pi + 1))

    n_pairs = nch // 2
    _interleave(pair_a(0))
    for k in range(n_pairs):
        b_gen = _chain(phase_b(2 * k), phase_b(2 * k + 1))
        _interleave(b_gen, *([pair_a(k + 1)] if k + 1 < n_pairs else []))

    for g in range(ng):
        s_ref[g] = state[g]


def _rwkv_rec(zin, w0, wb, a0, ab, gb, k_k, k_a, rk, gng, gnb):
    t = zin.shape[0]
    d = D_MODEL
    dl = ZL_WIDTH
    br = min(REC_ROWS, t)
    zcol = lambda c: pl.BlockSpec((br, d), lambda i: (i, c))
    row = pl.BlockSpec((1, d), lambda i: (0, 0))
    return pl.pallas_call(
        _rwkv_rec_kernel,
        grid=(t // br,),
        in_specs=[zcol(4), zcol(5), zcol(6), pl.BlockSpec((br, dl), lambda i: (i, 7 * d // dl)), zcol(3),
                  row, _resident((D_LORA_PAD, d)), row, _resident((D_LORA_PAD, d)),
                  _resident((D_GATE_LORA, d)), row, row, row, row, row],
        out_specs=pl.BlockSpec((br, d), lambda i: (i, 0)),
        out_shape=jax.ShapeDtypeStruct((t, d), F32),
        scratch_shapes=[pltpu.VMEM((N_GROUPS, GROUP, GROUP), F32)],
        compiler_params=_params(("arbitrary",), VMEM_LIMIT_BIG_TILES),
        name="rwkv_rec",
    )(zin, zin, zin, zin, zin, w0, wb, a0, ab, gb, k_k, k_a, rk, gng, gnb)


def _mem_kv_kernel(mem_ref, wk_ref, wv_ref, k_ref, v_ref):
    m = mem_ref[...].astype(BF16)
    k_ref[...] = _dot(m, wk_ref[...].astype(BF16)).astype(BF16)
    v_ref[...] = _dot(m, wv_ref[...].astype(BF16)).astype(BF16)


def _mem_kv(mem, wk, wv, tn=512):
    n, d = mem.shape
    return pl.pallas_call(
        _mem_kv_kernel,
        grid=(d // tn,),
        in_specs=[pl.BlockSpec((n, d), lambda j: (0, 0)),
                  pl.BlockSpec((d, tn), lambda j: (0, j)),
                  pl.BlockSpec((d, tn), lambda j: (0, j))],
        out_specs=[pl.BlockSpec((n, tn), lambda j: (0, j))] * 2,
        out_shape=[jax.ShapeDtypeStruct((n, d), BF16)] * 2,
        compiler_params=_params(("arbitrary",)),
        name="mem_kv",
    )(mem, wk, wv)


def _mix_xattn_kernel(ya_ref, yb_ref, x_ref, gin_ref, bin_ref, wout_ref, g1_ref, b1_ref,
                      wq_ref, k_ref, v_ref, wo_ref, g_ref, b_ref, o_ref):
    y = (ya_ref[...] + yb_ref[...]).astype(BF16)
    mix = _dot(y, wout_ref[...])
    h0 = _layer_norm(x_ref[...], gin_ref[...], bin_ref[...])
    h = _layer_norm(ALPHA * h0 + mix, g1_ref[...], b1_ref[...])
    q = _dot(h.astype(BF16), wq_ref[...]).astype(BF16)
    scale = XATTN_HEAD_DIM ** -0.5
    sls = [slice(hh * XATTN_HEAD_DIM, (hh + 1) * XATTN_HEAD_DIM) for hh in range(XATTN_HEADS)]
    ss = [_dot_nt(q[:, sl], k_ref[:, sl]) * scale for sl in sls]
    es = [jnp.exp(s - jnp.max(s, axis=-1, keepdims=True)) for s in ss]
    ps = [(e / jnp.sum(e, axis=-1, keepdims=True)).astype(BF16) for e in es]
    outs = [_dot(p, v_ref[:, sl]) for p, sl in zip(ps, sls)]
    o = jnp.concatenate(outs, axis=1).astype(BF16)
    xa = _dot(o, wo_ref[...])
    o_ref[...] = _layer_norm(ALPHA * h + xa, g_ref[...], b_ref[...])


def _mix_xattn(ya, yb, x, g_in, b_in, w_out, g1, b1, wq, kmem, vmem, wo, g2, b2, tm=256):
    t, d = x.shape
    n = kmem.shape[0]
    blk = pl.BlockSpec((tm, d), lambda i: (i, 0))
    row = pl.BlockSpec((1, d), lambda i: (0, 0))
    return pl.pallas_call(
        _mix_xattn_kernel,
        grid=(t // tm,),
        in_specs=[blk, blk, blk, row, row, _resident((d, d)), row, row,
                  _resident((d, d)), _resident((n, d)), _resident((n, d)), _resident((d, d)), row, row],
        out_specs=blk,
        out_shape=jax.ShapeDtypeStruct((t, d), F32),
        compiler_params=_params(("arbitrary",)),
        name="mix_xattn",
    )(ya, yb, x, g_in, b_in, w_out, g1, b1, wq, kmem, vmem, wo, g2, b2)


def _ffn_kernel(h_ref, wg_ref, wu_ref, wd_ref, g_ref, b_ref, o_ref):
    j = pl.program_id(1)

    @pl.when(j == 0)
    def _():
        o_ref[...] = jnp.zeros_like(o_ref)

    hb = h_ref[...].astype(BF16)
    gate = _dot(hb, wg_ref[...].astype(BF16))
    up = _dot(hb, wu_ref[...].astype(BF16))
    act = (gate * _sigmoid(gate) * up).astype(BF16)
    o_ref[...] += _dot(act, wd_ref[...].astype(BF16))

    @pl.when(j == pl.num_programs(1) - 1)
    def _():
        o_ref[...] = _layer_norm(ALPHA * h_ref[...] + o_ref[...], g_ref[...], b_ref[...])


def _ffn(h, wg, wu, wd, g, b, tm=1024, tf=256):
    t, d = h.shape
    f = wg.shape[1]
    tm = min(tm, t)
    blk = pl.BlockSpec((tm, d), lambda i, j: (i, 0))
    row = pl.BlockSpec((1, d), lambda i, j: (0, 0))
    return pl.pallas_call(
        _ffn_kernel,
        grid=(t // tm, f // tf),
        in_specs=[blk,
                  pl.BlockSpec((d, tf), lambda i, j: (0, j)),
                  pl.BlockSpec((d, tf), lambda i, j: (0, j)),
                  pl.BlockSpec((tf, d), lambda i, j: (j, 0)),
                  row, row],
        out_specs=blk,
        out_shape=jax.ShapeDtypeStruct((t, d), F32),
        compiler_params=_params(("arbitrary", "arbitrary"), VMEM_LIMIT_BIG_TILES),
        name="ffn",
    )(h, wg, wu, wd, g, b)


def _pad_cols(w, n):
    return jnp.pad(w, ((0, 0), (0, n - w.shape[1])))


def _pad_rows(w, n):
    return jnp.pad(w, ((0, n - w.shape[0]), (0, 0)))


def kernel(x, mem, ln_in_g, ln_in_b, w_in, conv_w, conv_b, lru_wa, lru_ba, lru_wx, lru_bx, lru_lambda, rw_mu, rw_w0, rw_wB, rw_a0, rw_aB, rw_gB, rw_kk, rw_ka, rw_rk, rw_gn_g, rw_gn_b, w_out, ln1_g, ln1_b, xa_wq, xa_wk, xa_wv, xa_wo, ln2_g, ln2_b, ffn_wg, ffn_wu, ffn_wd, ln3_g, ln3_b):
    bsz, t, d = x.shape
    depth = w_in.shape[0]
    assert bsz == 1 and d == D_MODEL and t % 512 == 0
    row = lambda p: p.reshape(1, -1)

    h = None
    for l in range(depth):
        w_t = jnp.swapaxes(w_in[l], 0, 1)
        n_cols = IN_BLOCKS * IN_TN
        mu = row(rw_mu[l])
        mu_steps = jnp.concatenate(
            [jnp.zeros((1, 2 * d), F32), mu[:, :3 * d], jnp.zeros((1, 2 * d), F32), mu[:, 3 * d:]], axis=1)
        col_params = jnp.concatenate([
            _pad_cols(mu_steps, n_cols),
            _pad_cols(conv_w[l], n_cols),
            _pad_cols(row(conv_b[l]), n_cols),
            jnp.zeros((CP_ROWS - CP_CONV_B - 1, n_cols), F32)], axis=0)
        if l == 0:
            zin = _in_proj(x[0], row(ln_in_g), row(ln_in_b), w_t, col_params)
        else:
            raise NotImplementedError("DEPTH > 1 is not part of this problem")

        ya = _lru(zin, lru_wa[l].astype(BF16), row(lru_ba[l]),
                  lru_wx[l].astype(BF16), row(lru_bx[l]), row(lru_lambda[l]))

        yb = _rwkv_rec(
            zin, row(rw_w0[l]),
            _pad_rows(rw_wB[l], D_LORA_PAD).astype(BF16), row(rw_a0[l]),
            _pad_rows(rw_aB[l], D_LORA_PAD).astype(BF16), rw_gB[l].astype(BF16),
            row(rw_kk[l]), row(rw_ka[l]), row(rw_rk[l]), row(rw_gn_g[l]), row(rw_gn_b[l]))

        kmem, vmem = _mem_kv(mem[0], xa_wk[l], xa_wv[l])
        h = _mix_xattn(ya, yb, x[0], row(ln_in_g), row(ln_in_b), w_out[l].astype(BF16),
                       row(ln1_g[l]), row(ln1_b[l]), xa_wq[l].astype(BF16), kmem, vmem,
                       xa_wo[l].astype(BF16), row(ln2_g[l]), row(ln2_b[l]))
        h = _ffn(h, ffn_wg[l], ffn_wu[l], ffn_wd[l], row(ln3_g[l]), row(ln3_b[l]))
    return h[None]
```

```python
import jax
import jax.numpy as jnp
from jax import lax
from jax.experimental import pallas as pl
from jax.experimental.pallas import tpu as pltpu

F32 = jnp.float32
BF16 = jnp.bfloat16

D_MODEL = 2048
LN_EPS = 1e-5
ALPHA = 2.0 ** 0.25

LRU_HEADS = 16
LRU_HEAD_DIM = 128
CONV_WIDTH = 4
LRU_C = 8.0

RWKV_HEAD_DIM = 64
D_LORA = 96
D_LORA_PAD = 128
D_GATE_LORA = 256
ZL_WIDTH = 512
GN_EPS = 64e-5
CHUNK = 64
GROUP = 256
N_GROUPS = D_MODEL // GROUP

XATTN_HEADS = 4
XATTN_HEAD_DIM = 512

VMEM_LIMIT = 56 * 1024 * 1024
VMEM_LIMIT_BIG_TILES = 60 * 1024 * 1024


def _dot(a, b):
    return jnp.dot(a, b, preferred_element_type=F32)


def _dot_nt(a, b):
    return lax.dot_general(a, b, (((1,), (1,)), ((), ())), preferred_element_type=F32)


def _dot_tn(a, b):
    return lax.dot_general(a, b, (((0,), (0,)), ((), ())), preferred_element_type=F32)


def _split2(x):
    hi = x.astype(BF16)
    lo = (x - hi.astype(F32)).astype(BF16)
    return hi, lo


def _split3(x):
    hi = x.astype(BF16)
    r1 = x - hi.astype(F32)
    mid = r1.astype(BF16)
    lo = (r1 - mid.astype(F32)).astype(BF16)
    return hi, mid, lo


def _sigmoid(x):
    return 1.0 / (1.0 + jnp.exp(-x))


def _softplus(x):
    return jnp.maximum(x, 0.0) + jnp.log1p(jnp.exp(-jnp.abs(x)))


def _softplus_log(x):
    return jnp.maximum(x, 0.0) + jnp.log(1.0 + jnp.exp(-jnp.abs(x)))


def _gelu_tanh(x):
    c = 0.7978845608028654
    return 0.5 * x * (1.0 + jnp.tanh(c * (x + 0.044715 * (x * x * x))))


def _layer_norm(x, g, b):
    mu = jnp.mean(x, axis=-1, keepdims=True)
    xc = x - mu
    var = jnp.mean(xc * xc, axis=-1, keepdims=True)
    return xc * lax.rsqrt(var + LN_EPS) * g + b


def _shift_rows(z, prev8, s):
    rolled = pltpu.roll(z, s, 0)
    row8 = lax.broadcasted_iota(jnp.int32, prev8.shape, 0)
    head = jnp.where(row8 < s, pltpu.roll(prev8, s, 0), rolled[:8])
    return jnp.concatenate([head, rolled[8:]], axis=0)


def _params(sem, vmem_limit=VMEM_LIMIT):
    return pltpu.CompilerParams(dimension_semantics=sem, vmem_limit_bytes=vmem_limit)


def _resident(shape):
    return pl.BlockSpec(shape, lambda *_: (0,) * len(shape), pipeline_mode=pl.Buffered(1))


IN_TN = 1024
IN_D_BLOCKS = D_MODEL // IN_TN
IN_MAIN_BLOCKS = 5 * IN_D_BLOCKS
IN_GATE_BLOCKS = 2 * IN_D_BLOCKS
IN_BLOCKS = IN_MAIN_BLOCKS + IN_GATE_BLOCKS + 1


def _in_weight_row(j, n_out):
    gates = n_out - 2 * D_MODEL + (j - IN_MAIN_BLOCKS) * IN_TN
    return jnp.where(j < IN_MAIN_BLOCKS, j * IN_TN,
                     jnp.where(j < IN_MAIN_BLOCKS + IN_GATE_BLOCKS, gates, 5 * D_MODEL))


def _in_out_block(j):
    b2 = 2 * D_MODEL // IN_TN
    return jnp.where(j < b2, j,
                     jnp.where(j < IN_MAIN_BLOCKS, j + b2,
                               jnp.where(j < IN_MAIN_BLOCKS + b2, j - (IN_MAIN_BLOCKS - b2), j)))


CP_MU, CP_CONV_W, CP_CONV_B, CP_ROWS = 0, 1, 1 + CONV_WIDTH, 8


def _in_proj_kernel(x_ref, g_ref, b_ref, w_ref, cp_ref, z_ref, hb_ref, carry_ref):
    i = pl.program_id(0)
    j = pl.program_id(1)
    nd = IN_D_BLOCKS

    @pl.when(j == 0)
    def _():
        hb_ref[...] = _layer_norm(x_ref[...], g_ref[...], b_ref[...]).astype(BF16)

    @pl.when((i == 0) & (j == 0))
    def _():
        carry_ref[...] = jnp.zeros_like(carry_ref)

    def keep_tail(z):
        carry_ref[j] = z[z.shape[0] - 8:, :]

    @pl.when(j < nd)
    def _():
        z = _dot_nt(hb_ref[...], w_ref[...].astype(BF16))
        prev8 = carry_ref[j]
        cw = CP_CONV_W + CONV_WIDTH - 1
        conv = cp_ref[CP_CONV_B:CP_CONV_B + 1, :] + z * cp_ref[cw:cw + 1, :]
        for s in range(1, CONV_WIDTH):
            conv = conv + _shift_rows(z, prev8, s) * cp_ref[cw - s:cw - s + 1, :]
        keep_tail(z)
        z_ref[...] = conv

    @pl.when((j >= nd) & (j < 2 * nd))
    def _():
        z_ref[...] = _gelu_tanh(_dot_nt(hb_ref[...], w_ref[...].astype(BF16)))

    def token_shift(z):
        n = z.shape[1]
        zp = _shift_rows(z, carry_ref[j][:, 0:n], 1)
        carry_ref[j, :, 0:n] = z[z.shape[0] - 8:, :]
        return z + (zp - z) * cp_ref[CP_MU:CP_MU + 1, 0:n]

    @pl.when((j >= 2 * nd) & (j < IN_MAIN_BLOCKS))
    def _():
        z_ref[...] = token_shift(_dot_nt(hb_ref[...], w_ref[...].astype(BF16)))

    @pl.when((j >= IN_MAIN_BLOCKS) & (j < IN_MAIN_BLOCKS + IN_GATE_BLOCKS))
    def _():
        z_ref[...] = _sigmoid(_dot_nt(hb_ref[...], w_ref[...].astype(BF16)))

    @pl.when(j >= IN_MAIN_BLOCKS + IN_GATE_BLOCKS)
    def _():
        z_ref[:, 0:ZL_WIDTH] = token_shift(_dot_nt(hb_ref[...], w_ref[0:ZL_WIDTH, :].astype(BF16)))
        z_ref[:, ZL_WIDTH:] = jnp.zeros((z_ref.shape[0], z_ref.shape[1] - ZL_WIDTH), F32)


def _in_proj(x, g, b, w_t, col_params, tm=1024):
    t, d = x.shape
    tn = IN_TN
    nb = IN_BLOCKS
    n_out = w_t.shape[0]
    tm = min(tm, t)
    return pl.pallas_call(
        _in_proj_kernel,
        grid=(t // tm, nb),
        in_specs=[
            pl.BlockSpec((tm, d), lambda i, j: (i, 0)),
            pl.BlockSpec((1, d), lambda i, j: (0, 0)),
            pl.BlockSpec((1, d), lambda i, j: (0, 0)),
            pl.BlockSpec((pl.Element(tn), pl.Element(d)),
                         lambda i, j: (pl.multiple_of(_in_weight_row(j, n_out), 8), 0)),
            pl.BlockSpec((CP_ROWS, tn), lambda i, j: (0, j)),
        ],
        out_specs=pl.BlockSpec((tm, tn), lambda i, j: (i, _in_out_block(j))),
        out_shape=jax.ShapeDtypeStruct((t, nb * tn), F32),
        scratch_shapes=[pltpu.VMEM((tm, d), BF16), pltpu.VMEM((nb, 8, tn), F32)],
        compiler_params=_params(("arbitrary", "arbitrary")),
        name="in_proj",
    )(x, g, b, w_t, col_params)


def _lru_kernel(u_ref, gate_ref, ga_ref, wa_ref, ba_ref, wx_ref, bx_ref,
                lam_ref, out_ref, hcarry):
    @pl.when(pl.program_id(0) == 0)
    def _():
        hcarry[...] = jnp.zeros_like(hcarry)

    conv = u_ref[...]
    tm = conv.shape[0]
    cb16 = conv.astype(BF16)
    r_parts, i_parts = [], []
    for g in range(LRU_HEADS):
        ug = cb16[:, g * LRU_HEAD_DIM:(g + 1) * LRU_HEAD_DIM]
        r_parts.append(_dot(ug, wa_ref[g]))
        i_parts.append(_dot(ug, wx_ref[g]))
    r = _sigmoid(jnp.concatenate(r_parts, axis=1) + ba_ref[...])
    ig = _sigmoid(jnp.concatenate(i_parts, axis=1) + bx_ref[...])

    log_a = (-LRU_C) * r * _softplus(-lam_ref[...])
    a_c = jnp.exp(log_a)
    om = -jnp.tanh(log_a) * (a_c * a_c + 1.0)
    b_c = jnp.where(om > 0.0, om * lax.rsqrt(om), 0.0) * (ig * conv)

    n8 = tm // 8
    a3 = a_c.reshape(n8, 8, a_c.shape[1])
    b3 = b_c.reshape(n8, 8, b_c.shape[1])
    sub = lax.broadcasted_iota(jnp.int32, a3.shape, 1)
    for d in (1, 2, 4):
        m = sub >= d
        a_sh = jnp.where(m, pltpu.roll(a3, d, 1), 1.0)
        b_sh = jnp.where(m, pltpu.roll(b3, d, 1), 0.0)
        b3 = a3 * b_sh + b3
        a3 = a3 * a_sh
    carry = hcarry[...]
    hs = []
    for i in range(n8):
        h_i = b3[i] + a3[i] * carry
        hs.append(h_i)
        carry = h_i[7:8, :]
    hcarry[...] = carry
    h = jnp.concatenate(hs, axis=0)

    out_ref[...] = ga_ref[...] * (gate_ref[...] * h)


def _lru(zin, wa, ba, wx, bx, lam, tm=256):
    t = zin.shape[0]
    d = D_MODEL
    row = lambda i: (0, 0)
    return pl.pallas_call(
        _lru_kernel,
        grid=(t // tm,),
        in_specs=[
            pl.BlockSpec((tm, d), lambda i: (i, 0)),
            pl.BlockSpec((tm, d), lambda i: (i, 1)),
            pl.BlockSpec((tm, d), lambda i: (i, 2)),
            pl.BlockSpec((LRU_HEADS, LRU_HEAD_DIM, LRU_HEAD_DIM), lambda i: (0, 0, 0)),
            pl.BlockSpec((1, d), row),
            pl.BlockSpec((LRU_HEADS, LRU_HEAD_DIM, LRU_HEAD_DIM), lambda i: (0, 0, 0)),
            pl.BlockSpec((1, d), row),
            pl.BlockSpec((1, d), row),
        ],
        out_specs=pl.BlockSpec((tm, d), lambda i: (i, 0)),
        out_shape=jax.ShapeDtypeStruct((t, d), F32),
        scratch_shapes=[pltpu.VMEM((1, d), F32)],
        compiler_params=_params(("arbitrary",)),
        name="lru",
    )(zin, zin, zin, wa, ba, wx, bx, lam)


def _block_ones(n, seg):
    r = lax.broadcasted_iota(jnp.int32, (n, n), 0) // seg
    c = lax.broadcasted_iota(jnp.int32, (n, n), 1) // seg
    return jnp.where(r == c, 1.0, 0.0).astype(BF16)


REC_ROWS = 256


def _round_robin(*gens):
    live = list(gens)
    while live:
        for gen in list(live):
            try:
                next(gen)
            except StopIteration:
                live.remove(gen)
        yield


def _interleave(*gens):
    for _ in _round_robin(*gens):
        pass


def _chain(*gens):
    for gen in gens:
        yield from gen


def _rwkv_rec_kernel(r_ref, k_ref, v_ref, zl_ref, zgb_ref, w0_ref, wb_ref, a0_ref, ab_ref, gb_ref,
                     kk_ref, ka_ref, rk_ref, gng_ref, gnb_ref, out_ref, s_ref):
    @pl.when(pl.program_id(0) == 0)
    def _():
        s_ref[...] = jnp.zeros_like(s_ref)

    c = CHUNK
    hd = RWKV_HEAD_DIM
    ng = N_GROUPS
    nch = r_ref.shape[0] // c
    sls = [slice(g * GROUP, (g + 1) * GROUP) for g in range(ng)]

    assert nch % 2 == 0
    rows = lax.broadcasted_iota(jnp.int32, (2 * c, 2 * c), 0)
    cols = lax.broadcasted_iota(jnp.int32, (2 * c, 2 * c), 1)
    tri2 = jnp.where((cols <= rows) & (cols // c == rows // c), 1.0, 0.0).astype(BF16)
    t_c = lax.broadcasted_iota(jnp.int32, (c, GROUP), 0)
    s_c = lax.broadcasted_iota(jnp.int32, (c, GROUP), 1) % hd
    strict = s_c < t_c
    incl = s_c <= t_c
    eye_c = jnp.where(s_c == t_c, 1.0, 0.0)
    bd = (lax.broadcasted_iota(jnp.int32, (GROUP, GROUP), 0) // hd
          == lax.broadcasted_iota(jnp.int32, (GROUP, GROUP), 1) // hd)
    ones_bd = _block_ones(GROUP, hd)

    blk4 = (t_c >> 2) == (s_c >> 2)
    diag3 = (t_c - s_c) == 3

    def inverse_blocks4(l):
        m = jnp.where(blk4, l, 0.0)
        r1 = pltpu.roll(m, GROUP - 1, 1)
        r2 = pltpu.roll(m, GROUP - 2, 1)
        d1 = pltpu.roll(m, 1, 0)
        d2 = pltpu.roll(m, 2, 0)
        d1r1 = pltpu.roll(d1, GROUP - 1, 1)
        corr = jnp.where(diag3, r2 * d1 + r1 * d2 - r2 * d1r1 * d2, r1 * d1)
        return eye_c - jnp.where(blk4, m - corr, 0.0)

    def expand(x_c):
        return jnp.where(bd, jnp.concatenate([x_c] * (GROUP // hd), axis=0), 0.0).astype(BF16)

    def seg_sums(xs):
        hi, lo = _split2(jnp.concatenate(xs, axis=0))
        s = _dot(jnp.concatenate([hi, lo], axis=0), ones_bd)
        n = c * len(xs)
        s = s[:n] + s[n:]
        return [s[i * c:(i + 1) * c] for i in range(len(xs))]

    state = [s_ref[g] for g in range(ng)]
    prep = [None] * nch

    pair_prep = {}

    def lora_stage(pi):
        rs_ = slice(2 * pi * c, (2 * pi + 2) * c)
        zl = zl_ref[rs_, :]
        nl = zl.shape[1]
        w_lo = zl[:, 0:D_LORA_PAD]
        a_lo = pltpu.roll(zl, nl - D_LORA, 1)[:, 0:D_LORA_PAD]
        g_lo = pltpu.roll(zl, nl - 2 * D_LORA, 1)[:, 0:D_GATE_LORA]
        w_log = -_softplus_log(-(w0_ref[...] + _dot(jnp.tanh(w_lo).astype(BF16), wb_ref[...]))) - 0.5
        lw = -jnp.exp(w_log)
        a = _sigmoid(a0_ref[...] + _dot(a_lo.astype(BF16), ab_ref[...]))
        gate = _dot(_sigmoid(g_lo).astype(BF16), gb_ref[...])
        hi, mid, lo = _split3(lw)
        cum = _dot(tri2, hi) + _dot(tri2, mid) + _dot(tri2, lo)
        pair_prep[pi] = dict(lw=lw, a=a, gate=gate, cum=cum)
        yield

    def phase_a(ci):
        rs_ = slice(ci * c, (ci + 1) * c)
        half = slice((ci % 2) * c, (ci % 2 + 1) * c)
        pp = pair_prep[ci // 2]
        lw, a, gate, cum = pp["lw"][half], pp["a"][half], pp["gate"][half], pp["cum"][half]
        tot = cum[c - 1:c, :]
        k = k_ref[rs_, :]
        kk = k * kk_ref[...]
        kk2 = kk * kk
        n2 = jnp.concatenate(seg_sums([kk2[:, sl] for sl in sls]), axis=1)
        kn = kk * lax.rsqrt(jnp.maximum(n2, 1e-24))
        bb = kn * a
        kf = k * (1.0 + (a - 1.0) * ka_ref[...])
        yield
        r = r_ref[rs_, :]
        v = v_ref[rs_, :]
        p_inv = jnp.exp(-cum)
        p_end = jnp.exp(tot - cum)
        rq = r * jnp.exp(cum)
        kap = kn * jnp.exp(cum - lw)
        bet = bb * p_inv
        kt = kf * p_inv
        lhs2 = [jnp.concatenate([kap[:, sl], rq[:, sl]], axis=0).astype(BF16) for sl in sls]
        amat = [_dot_nt(lhs2[g], jnp.concatenate([expand(bet[:, sls[g]]), expand(kt[:, sls[g]])], axis=0))
                for g in range(ng)]
        yield
        l_c = [jnp.where(strict, a[:c, :GROUP], 0.0) for a in amat]
        a_lo = [jnp.concatenate([jnp.where(strict, a[:c, GROUP:], 0.0),
                                 jnp.where(incl, a[c:, GROUP:], 0.0)], axis=0).astype(BF16) for a in amat]
        arb = [jnp.where(incl, a[c:, :GROUP], 0.0).astype(BF16) for a in amat]
        av = [_dot(a_lo[g], expand(v[:, sls[g]])) for g in range(ng)]
        x_c = [inverse_blocks4(l) for l in l_c]
        yield
        for lvl in range(3, 7):
            lmask = ((t_c >> lvl) == (s_c >> lvl)) & ((t_c >> (lvl - 1)) != (s_c >> (lvl - 1)))
            y_c = [_dot(x_c[g].astype(BF16), expand(jnp.where(lmask, l_c[g], 0.0))) for g in range(ng)]
            yield
            x_c = [x_c[g] - _dot(y_c[g].astype(BF16), expand(x_c[g])) for g in range(ng)]
            yield
        bhat_neg = -(bb * p_end)
        khat = kf * p_end
        upd_rhs = [jnp.concatenate([bhat_neg[:, sl], khat[:, sl]], axis=0).astype(BF16) for sl in sls]
        prep[ci] = dict(lhs2=lhs2, av=av, arb=arb, x=[x.astype(BF16) for x in x_c], v=v,
                        upd_rhs=upd_rhs, p_tot=jnp.exp(tot), rkk=r * kf * rk_ref[...], gate=gate)

    def phase_b(ci):
        p = prep[ci]
        rs_ = slice(ci * c, (ci + 1) * c)
        v = p["v"]
        rs = [_dot_nt(p["lhs2"][g], state[g].astype(BF16)) for g in range(ng)]
        yield
        u_c = [_dot(p["x"][g], expand(rs[g][:c] + p["av"][g][:c])) for g in range(ng)]
        yield
        o_c = [rs[g][c:] + p["av"][g][c:] - _dot(p["arb"][g], expand(u_c[g])) for g in range(ng)]
        for g in range(ng):
            upd = _dot_tn(jnp.concatenate([u_c[g], v[:, sls[g]]], axis=0).astype(BF16), p["upd_rhs"][g])
            state[g] = state[g] * p["p_tot"][:, sls[g]] + jnp.where(bd, upd, 0.0)
        yield
        sums = seg_sums(o_c + [p["rkk"][:, sl] for sl in sls])
        dev = [o_c[g] - sums[g] * (1.0 / hd) for g in range(ng)]
        yield
        var = seg_sums([dv * dv for dv in dev])
        o_parts = []
        for g in range(ng):
            o_n = dev[g] * lax.rsqrt(var[g] * (1.0 / hd) + GN_EPS) * gng_ref[:, sls[g]] + gnb_ref[:, sls[g]]
            o_parts.append(o_n + sums[ng + g] * v[:, sls[g]])
        o = jnp.concatenate(o_parts, axis=1)
        out_ref[rs_, :] = zgb_ref[rs_, :] * (o * p["gate"])

    def pair_a(pi):
        yield from lora_stage(pi)
        yield from _round_robin(phase_a(2 * pi), phase_a(2 * pi + 1))

    n_pairs = nch // 2
    _interleave(pair_a(0))
    for k in range(n_pairs):
        b_gen = _chain(phase_b(2 * k), phase_b(2 * k + 1))
        _interleave(b_gen, *([pair_a(k + 1)] if k + 1 < n_pairs else []))

    for g in range(ng):
        s_ref[g] = state[g]


def _rwkv_rec(zin, w0, wb, a0, ab, gb, k_k, k_a, rk, gng, gnb):
    t = zin.shape[0]
    d = D_MODEL
    dl = ZL_WIDTH
    br = min(REC_ROWS, t)
    zcol = lambda c: pl.BlockSpec((br, d), lambda i: (i, c))
    row = pl.BlockSpec((1, d), lambda i: (0, 0))
    return pl.pallas_call(
        _rwkv_rec_kernel,
        grid=(t // br,),
        in_specs=[zcol(4), zcol(5), zcol(6), pl.BlockSpec((br, dl), lambda i: (i, 7 * d // dl)), zcol(3),
                  row, _resident((D_LORA_PAD, d)), row, _resident((D_LORA_PAD, d)),
                  _resident((D_GATE_LORA, d)), row, row, row, row, row],
        out_specs=pl.BlockSpec((br, d), lambda i: (i, 0)),
        out_shape=jax.ShapeDtypeStruct((t, d), F32),
        scratch_shapes=[pltpu.VMEM((N_GROUPS, GROUP, GROUP), F32)],
        compiler_params=_params(("arbitrary",), VMEM_LIMIT_BIG_TILES),
        name="rwkv_rec",
    )(zin, zin, zin, zin, zin, w0, wb, a0, ab, gb, k_k, k_a, rk, gng, gnb)


def _mem_kv_kernel(mem_ref, wk_ref, wv_ref, k_ref, v_ref):
    m = mem_ref[...].astype(BF16)
    k_ref[...] = _dot(m, wk_ref[...].astype(BF16)).astype(BF16)
    v_ref[...] = _dot(m, wv_ref[...].astype(BF16)).astype(BF16)


def _mem_kv(mem, wk, wv, tn=512):
    n, d = mem.shape
    return pl.pallas_call(
        _mem_kv_kernel,
        grid=(d // tn,),
        in_specs=[pl.BlockSpec((n, d), lambda j: (0, 0)),
                  pl.BlockSpec((d, tn), lambda j: (0, j)),
                  pl.BlockSpec((d, tn), lambda j: (0, j))],
        out_specs=[pl.BlockSpec((n, tn), lambda j: (0, j))] * 2,
        out_shape=[jax.ShapeDtypeStruct((n, d), BF16)] * 2,
        compiler_params=_params(("arbitrary",)),
        name="mem_kv",
    )(mem, wk, wv)


def _mix_xattn_kernel(ya_ref, yb_ref, x_ref, gin_ref, bin_ref, wout_ref, g1_ref, b1_ref,
                      wq_ref, k_ref, v_ref, wo_ref, g_ref, b_ref, o_ref):
    y = (ya_ref[...] + yb_ref[...]).astype(BF16)
    h0 = _layer_norm(x_ref[...], gin_ref[...], bin_ref[...])
    mix = _dot(y, wout_ref[...])
    h = _layer_norm(ALPHA * h0 + mix, g1_ref[...], b1_ref[...])
    q = _dot(h.astype(BF16), wq_ref[...]).astype(BF16)
    scale = XATTN_HEAD_DIM ** -0.5
    sls = [slice(hh * XATTN_HEAD_DIM, (hh + 1) * XATTN_HEAD_DIM) for hh in range(XATTN_HEADS)]
    ss = [_dot_nt(q[:, sl], k_ref[:, sl]) * scale for sl in sls]
    es = [jnp.exp(s - jnp.max(s, axis=-1, keepdims=True)) for s in ss]
    ps = [(e / jnp.sum(e, axis=-1, keepdims=True)).astype(BF16) for e in es]
    outs = [_dot(p, v_ref[:, sl]) for p, sl in zip(ps, sls)]
    o = jnp.concatenate(outs, axis=1).astype(BF16)
    xa = _dot(o, wo_ref[...])
    o_ref[...] = _layer_norm(ALPHA * h + xa, g_ref[...], b_ref[...])


def _mix_xattn(ya, yb, x, g_in, b_in, w_out, g1, b1, wq, kmem, vmem, wo, g2, b2, tm=256):
    t, d = x.shape
    n = kmem.shape[0]
    blk = pl.BlockSpec((tm, d), lambda i: (i, 0))
    row = pl.BlockSpec((1, d), lambda i: (0, 0))
    return pl.pallas_call(
        _mix_xattn_kernel,
        grid=(t // tm,),
        in_specs=[blk, blk, blk, row, row, _resident((d, d)), row, row,
                  _resident((d, d)), _resident((n, d)), _resident((n, d)), _resident((d, d)), row, row],
        out_specs=blk,
        out_shape=jax.ShapeDtypeStruct((t, d), F32),
        compiler_params=_params(("arbitrary",)),
        name="mix_xattn",
    )(ya, yb, x, g_in, b_in, w_out, g1, b1, wq, kmem, vmem, wo, g2, b2)


def _ffn_kernel(h_ref, wg_ref, wu_ref, wd_ref, g_ref, b_ref, o_ref):
    j = pl.program_id(1)

    @pl.when(j == 0)
    def _():
        o_ref[...] = jnp.zeros_like(o_ref)

    hb = h_ref[...].astype(BF16)
    gate = _dot(hb, wg_ref[...].astype(BF16))
    up = _dot(hb, wu_ref[...].astype(BF16))
    act = (gate * _sigmoid(gate) * up).astype(BF16)
    o_ref[...] += _dot(act, wd_ref[...].astype(BF16))

    @pl.when(j == pl.num_programs(1) - 1)
    def _():
        o_ref[...] = _layer_norm(ALPHA * h_ref[...] + o_ref[...], g_ref[...], b_ref[...])


def _ffn(h, wg, wu, wd, g, b, tm=1024, tf=256):
    t, d = h.shape
    f = wg.shape[1]
    tm = min(tm, t)
    blk = pl.BlockSpec((tm, d), lambda i, j: (i, 0))
    row = pl.BlockSpec((1, d), lambda i, j: (0, 0))
    return pl.pallas_call(
        _ffn_kernel,
        grid=(t // tm, f // tf),
        in_specs=[blk,
                  pl.BlockSpec((d, tf), lambda i, j: (0, j)),
                  pl.BlockSpec((d, tf), lambda i, j: (0, j)),
                  pl.BlockSpec((tf, d), lambda i, j: (j, 0)),
                  row, row],
        out_specs=blk,
        out_shape=jax.ShapeDtypeStruct((t, d), F32),
        compiler_params=_params(("arbitrary", "arbitrary"), VMEM_LIMIT_BIG_TILES),
        name="ffn",
    )(h, wg, wu, wd, g, b)


def _pad_cols(w, n):
    return jnp.pad(w, ((0, 0), (0, n - w.shape[1])))


def _pad_rows(w, n):
    return jnp.pad(w, ((0, n - w.shape[0]), (0, 0)))


def kernel(x, mem, ln_in_g, ln_in_b, w_in, conv_w, conv_b, lru_wa, lru_ba, lru_wx, lru_bx, lru_lambda, rw_mu, rw_w0, rw_wB, rw_a0, rw_aB, rw_gB, rw_kk, rw_ka, rw_rk, rw_gn_g, rw_gn_b, w_out, ln1_g, ln1_b, xa_wq, xa_wk, xa_wv, xa_wo, ln2_g, ln2_b, ffn_wg, ffn_wu, ffn_wd, ln3_g, ln3_b):
    bsz, t, d = x.shape
    depth = w_in.shape[0]
    assert bsz == 1 and d == D_MODEL and t % 512 == 0
    row = lambda p: p.reshape(1, -1)

    h = None
    for l in range(depth):
        w_t = jnp.swapaxes(w_in[l], 0, 1)
        n_cols = IN_BLOCKS * IN_TN
        mu = row(rw_mu[l])
        mu_steps = jnp.concatenate(
            [jnp.zeros((1, 2 * d), F32), mu[:, :3 * d], jnp.zeros((1, 2 * d), F32), mu[:, 3 * d:]], axis=1)
        col_params = jnp.concatenate([
            _pad_cols(mu_steps, n_cols),
            _pad_cols(conv_w[l], n_cols),
            _pad_cols(row(conv_b[l]), n_cols),
            jnp.zeros((CP_ROWS - CP_CONV_B - 1, n_cols), F32)], axis=0)
        if l == 0:
            zin = _in_proj(x[0], row(ln_in_g), row(ln_in_b), w_t, col_params)
        else:
            raise NotImplementedError("DEPTH > 1 is not part of this problem")

        ya = _lru(zin, lru_wa[l].astype(BF16), row(lru_ba[l]),
                  lru_wx[l].astype(BF16), row(lru_bx[l]), row(lru_lambda[l]))

        yb = _rwkv_rec(
            zin, row(rw_w0[l]),
            _pad_rows(rw_wB[l], D_LORA_PAD).astype(BF16), row(rw_a0[l]),
            _pad_rows(rw_aB[l], D_LORA_PAD).astype(BF16), rw_gB[l].astype(BF16),
            row(rw_kk[l]), row(rw_ka[l]), row(rw_rk[l]), row(rw_gn_g[l]), row(rw_gn_b[l]))

        kmem, vmem = _mem_kv(mem[0], xa_wk[l], xa_wv[l])
        h = _mix_xattn(ya, yb, x[0], row(ln_in_g), row(ln_in_b), w_out[l].astype(BF16),
                       row(ln1_g[l]), row(ln1_b[l]), xa_wq[l].astype(BF16), kmem, vmem,
                       xa_wo[l].astype(BF16), row(ln2_g[l]), row(ln2_b[l]))
        h = _ffn(h, ffn_wg[l], ffn_wu[l], ffn_wd[l], row(ln3_g[l]), row(ln3_b[l]))
    return h[None]
```

```python
import jax
import jax.numpy as jnp
from jax import lax
from jax.experimental import pallas as pl
from jax.experimental.pallas import tpu as pltpu

F32 = jnp.float32
BF16 = jnp.bfloat16

D_MODEL = 2048
LN_EPS = 1e-5
ALPHA = 2.0 ** 0.25

LRU_HEADS = 16
LRU_HEAD_DIM = 128
CONV_WIDTH = 4
LRU_C = 8.0

RWKV_HEAD_DIM = 64
D_LORA = 96
D_LORA_PAD = 128
D_GATE_LORA = 256
ZL_WIDTH = 512
GN_EPS = 64e-5
CHUNK = 64
GROUP = 256
N_GROUPS = D_MODEL // GROUP

XATTN_HEADS = 4
XATTN_HEAD_DIM = 512

VMEM_LIMIT = 56 * 1024 * 1024
VMEM_LIMIT_BIG_TILES = 60 * 1024 * 1024


def _dot(a, b):
    return jnp.dot(a, b, preferred_element_type=F32)


def _dot_nt(a, b):
    return lax.dot_general(a, b, (((1,), (1,)), ((), ())), preferred_element_type=F32)


def _dot_tn(a, b):
    return lax.dot_general(a, b, (((0,), (0,)), ((), ())), preferred_element_type=F32)


def _split2(x):
    hi = x.astype(BF16)
    lo = (x - hi.astype(F32)).astype(BF16)
    return hi, lo


def _split3(x):
    hi = x.astype(BF16)
    r1 = x - hi.astype(F32)
    mid = r1.astype(BF16)
    lo = (r1 - mid.astype(F32)).astype(BF16)
    return hi, mid, lo


def _sigmoid(x):
    return 1.0 / (1.0 + jnp.exp(-x))


def _softplus(x):
    return jnp.maximum(x, 0.0) + jnp.log1p(jnp.exp(-jnp.abs(x)))


def _softplus_log(x):
    return jnp.maximum(x, 0.0) + jnp.log(1.0 + jnp.exp(-jnp.abs(x)))


def _gelu_tanh(x):
    c = 0.7978845608028654
    return 0.5 * x * (1.0 + jnp.tanh(c * (x + 0.044715 * (x * x * x))))


def _layer_norm(x, g, b):
    mu = jnp.mean(x, axis=-1, keepdims=True)
    xc = x - mu
    var = jnp.mean(xc * xc, axis=-1, keepdims=True)
    return xc * lax.rsqrt(var + LN_EPS) * g + b


def _shift_rows(z, prev8, s):
    rolled = pltpu.roll(z, s, 0)
    row8 = lax.broadcasted_iota(jnp.int32, prev8.shape, 0)
    head = jnp.where(row8 < s, pltpu.roll(prev8, s, 0), rolled[:8])
    return jnp.concatenate([head, rolled[8:]], axis=0)


def _params(sem, vmem_limit=VMEM_LIMIT):
    return pltpu.CompilerParams(dimension_semantics=sem, vmem_limit_bytes=vmem_limit)


def _resident(shape):
    return pl.BlockSpec(shape, lambda *_: (0,) * len(shape), pipeline_mode=pl.Buffered(1))


IN_TN = 1024
IN_D_BLOCKS = D_MODEL // IN_TN
IN_MAIN_BLOCKS = 5 * IN_D_BLOCKS
IN_GATE_BLOCKS = 2 * IN_D_BLOCKS
IN_BLOCKS = IN_MAIN_BLOCKS + IN_GATE_BLOCKS + 1


def _in_weight_row(j, n_out):
    gates = n_out - 2 * D_MODEL + (j - IN_MAIN_BLOCKS) * IN_TN
    return jnp.where(j < IN_MAIN_BLOCKS, j * IN_TN,
                     jnp.where(j < IN_MAIN_BLOCKS + IN_GATE_BLOCKS, gates, 5 * D_MODEL))


def _in_out_block(j):
    b2 = 2 * D_MODEL // IN_TN
    return jnp.where(j < b2, j,
                     jnp.where(j < IN_MAIN_BLOCKS, j + b2,
                               jnp.where(j < IN_MAIN_BLOCKS + b2, j - (IN_MAIN_BLOCKS - b2), j)))


CP_MU, CP_CONV_W, CP_CONV_B, CP_ROWS = 0, 1, 1 + CONV_WIDTH, 8


def _in_proj_kernel(x_ref, g_ref, b_ref, w_ref, cp_ref, z_ref, hb_ref, carry_ref):
    i = pl.program_id(0)
    j = pl.program_id(1)
    nd = IN_D_BLOCKS

    @pl.when(j == 0)
    def _():
        hb_ref[...] = _layer_norm(x_ref[...], g_ref[...], b_ref[...]).astype(BF16)

    @pl.when((i == 0) & (j == 0))
    def _():
        carry_ref[...] = jnp.zeros_like(carry_ref)

    def keep_tail(z):
        carry_ref[j] = z[z.shape[0] - 8:, :]

    @pl.when(j < nd)
    def _():
        z = _dot_nt(hb_ref[...], w_ref[...].astype(BF16))
        prev8 = carry_ref[j]
        cw = CP_CONV_W + CONV_WIDTH - 1
        conv = cp_ref[CP_CONV_B:CP_CONV_B + 1, :] + z * cp_ref[cw:cw + 1, :]
        for s in range(1, CONV_WIDTH):
            conv = conv + _shift_rows(z, prev8, s) * cp_ref[cw - s:cw - s + 1, :]
        keep_tail(z)
        z_ref[...] = conv

    @pl.when((j >= nd) & (j < 2 * nd))
    def _():
        z_ref[...] = _gelu_tanh(_dot_nt(hb_ref[...], w_ref[...].astype(BF16)))

    def token_shift(z):
        n = z.shape[1]
        zp = _shift_rows(z, carry_ref[j][:, 0:n], 1)
        carry_ref[j, :, 0:n] = z[z.shape[0] - 8:, :]
        return z + (zp - z) * cp_ref[CP_MU:CP_MU + 1, 0:n]

    @pl.when((j >= 2 * nd) & (j < IN_MAIN_BLOCKS))
    def _():
        z_ref[...] = token_shift(_dot_nt(hb_ref[...], w_ref[...].astype(BF16)))

    @pl.when((j >= IN_MAIN_BLOCKS) & (j < IN_MAIN_BLOCKS + IN_GATE_BLOCKS))
    def _():
        z_ref[...] = _sigmoid(_dot_nt(hb_ref[...], w_ref[...].astype(BF16)))

    @pl.when(j >= IN_MAIN_BLOCKS + IN_GATE_BLOCKS)
    def _():
        z_ref[:, 0:ZL_WIDTH] = token_shift(_dot_nt(hb_ref[...], w_ref[0:ZL_WIDTH, :].astype(BF16)))
        z_ref[:, ZL_WIDTH:] = jnp.zeros((z_ref.shape[0], z_ref.shape[1] - ZL_WIDTH), F32)


def _in_proj(x, g, b, w_t, col_params, tm=1024):
    t, d = x.shape
    tn = IN_TN
    nb = IN_BLOCKS
    n_out = w_t.shape[0]
    tm = min(tm, t)
    return pl.pallas_call(
        _in_proj_kernel,
        grid=(t // tm, nb),
        in_specs=[
            pl.BlockSpec((tm, d), lambda i, j: (i, 0)),
            pl.BlockSpec((1, d), lambda i, j: (0, 0)),
            pl.BlockSpec((1, d), lambda i, j: (0, 0)),
            pl.BlockSpec((pl.Element(tn), pl.Element(d)),
                         lambda i, j: (pl.multiple_of(_in_weight_row(j, n_out), 8), 0)),
            pl.BlockSpec((CP_ROWS, tn), lambda i, j: (0, j)),
        ],
        out_specs=pl.BlockSpec((tm, tn), lambda i, j: (i, _in_out_block(j))),
        out_shape=jax.ShapeDtypeStruct((t, nb * tn), F32),
        scratch_shapes=[pltpu.VMEM((tm, d), BF16), pltpu.VMEM((nb, 8, tn), F32)],
        compiler_params=_params(("arbitrary", "arbitrary")),
        name="in_proj",
    )(x, g, b, w_t, col_params)


def _lru_kernel(u_ref, gate_ref, ga_ref, wa_ref, ba_ref, wx_ref, bx_ref,
                lam_ref, out_ref, hcarry):
    @pl.when(pl.program_id(0) == 0)
    def _():
        hcarry[...] = jnp.zeros_like(hcarry)

    conv = u_ref[...]
    tm = conv.shape[0]
    cb16 = conv.astype(BF16)
    r_parts, i_parts = [], []
    for g in range(LRU_HEADS):
        ug = cb16[:, g * LRU_HEAD_DIM:(g + 1) * LRU_HEAD_DIM]
        r_parts.append(_dot(ug, wa_ref[g]))
        i_parts.append(_dot(ug, wx_ref[g]))
    r = _sigmoid(jnp.concatenate(r_parts, axis=1) + ba_ref[...])
    ig = _sigmoid(jnp.concatenate(i_parts, axis=1) + bx_ref[...])

    log_a = (-LRU_C) * r * _softplus(-lam_ref[...])
    a_c = jnp.exp(log_a)
    om = -jnp.tanh(log_a) * (a_c * a_c + 1.0)
    b_c = jnp.where(om > 0.0, om * lax.rsqrt(om), 0.0) * (ig * conv)

    n8 = tm // 8
    a3 = a_c.reshape(n8, 8, a_c.shape[1])
    b3 = b_c.reshape(n8, 8, b_c.shape[1])
    sub = lax.broadcasted_iota(jnp.int32, a3.shape, 1)
    for d in (1, 2, 4):
        m = sub >= d
        a_sh = jnp.where(m, pltpu.roll(a3, d, 1), 1.0)
        b_sh = jnp.where(m, pltpu.roll(b3, d, 1), 0.0)
        b3 = a3 * b_sh + b3
        a3 = a3 * a_sh
    carry = hcarry[...]
    hs = []
    for i in range(n8):
        h_i = b3[i] + a3[i] * carry
        hs.append(h_i)
        carry = h_i[7:8, :]
    hcarry[...] = carry
    h = jnp.concatenate(hs, axis=0)

    out_ref[...] = ga_ref[...] * (gate_ref[...] * h)


def _lru(zin, wa, ba, wx, bx, lam, tm=256):
    t = zin.shape[0]
    d = D_MODEL
    row = lambda i: (0, 0)
    return pl.pallas_call(
        _lru_kernel,
        grid=(t // tm,),
        in_specs=[
            pl.BlockSpec((tm, d), lambda i: (i, 0)),
            pl.BlockSpec((tm, d), lambda i: (i, 1)),
            pl.BlockSpec((tm, d), lambda i: (i, 2)),
            pl.BlockSpec((LRU_HEADS, LRU_HEAD_DIM, LRU_HEAD_DIM), lambda i: (0, 0, 0)),
            pl.BlockSpec((1, d), row),
            pl.BlockSpec((LRU_HEADS, LRU_HEAD_DIM, LRU_HEAD_DIM), lambda i: (0, 0, 0)),
            pl.BlockSpec((1, d), row),
            pl.BlockSpec((1, d), row),
        ],
        out_specs=pl.BlockSpec((tm, d), lambda i: (i, 0)),
        out_shape=jax.ShapeDtypeStruct((t, d), F32),
        scratch_shapes=[pltpu.VMEM((1, d), F32)],
        compiler_params=_params(("arbitrary",)),
        name="lru",
    )(zin, zin, zin, wa, ba, wx, bx, lam)


def _block_ones(n, seg):
    r = lax.broadcasted_iota(jnp.int32, (n, n), 0) // seg
    c = lax.broadcasted_iota(jnp.int32, (n, n), 1) // seg
    return jnp.where(r == c, 1.0, 0.0).astype(BF16)


REC_ROWS = 256


def _round_robin(*gens):
    live = list(gens)
    while live:
        for gen in list(live):
            try:
                next(gen)
            except StopIteration:
                live.remove(gen)
        yield


def _interleave(*gens):
    for _ in _round_robin(*gens):
        pass


def _chain(*gens):
    for gen in gens:
        yield from gen


def _rwkv_rec_kernel(r_ref, k_ref, v_ref, zl_ref, zgb_ref, w0_ref, wb_ref, a0_ref, ab_ref, gb_ref,
                     kk_ref, ka_ref, rk_ref, gng_ref, gnb_ref, out_ref, s_ref):
    @pl.when(pl.program_id(0) == 0)
    def _():
        s_ref[...] = jnp.zeros_like(s_ref)

    c = CHUNK
    hd = RWKV_HEAD_DIM
    ng = N_GROUPS
    nch = r_ref.shape[0] // c
    sls = [slice(g * GROUP, (g + 1) * GROUP) for g in range(ng)]

    assert nch % 2 == 0
    rows = lax.broadcasted_iota(jnp.int32, (2 * c, 2 * c), 0)
    cols = lax.broadcasted_iota(jnp.int32, (2 * c, 2 * c), 1)
    tri2 = jnp.where((cols <= rows) & (cols // c == rows // c), 1.0, 0.0).astype(BF16)
    t_c = lax.broadcasted_iota(jnp.int32, (c, GROUP), 0)
    s_c = lax.broadcasted_iota(jnp.int32, (c, GROUP), 1) % hd
    strict = s_c < t_c
    incl = s_c <= t_c
    eye_c = jnp.where(s_c == t_c, 1.0, 0.0)
    bd = (lax.broadcasted_iota(jnp.int32, (GROUP, GROUP), 0) // hd
          == lax.broadcasted_iota(jnp.int32, (GROUP, GROUP), 1) // hd)
    ones_bd = _block_ones(GROUP, hd)

    blk4 = (t_c >> 2) == (s_c >> 2)
    diag3 = (t_c - s_c) == 3

    def inverse_blocks4(l):
        m = jnp.where(blk4, l, 0.0)
        r1 = pltpu.roll(m, GROUP - 1, 1)
        r2 = pltpu.roll(m, GROUP - 2, 1)
        d1 = pltpu.roll(m, 1, 0)
        d2 = pltpu.roll(m, 2, 0)
        d1r1 = pltpu.roll(d1, GROUP - 1, 1)
        corr = jnp.where(diag3, r2 * d1 + r1 * d2 - r2 * d1r1 * d2, r1 * d1)
        return eye_c - jnp.where(blk4, m - corr, 0.0)

    def expand(x_c):
        return jnp.where(bd, jnp.concatenate([x_c] * (GROUP // hd), axis=0), 0.0).astype(BF16)

    def seg_sums(xs):
        hi, lo = _split2(jnp.concatenate(xs, axis=0))
        s = _dot(jnp.concatenate([hi, lo], axis=0), ones_bd)
        n = c * len(xs)
        s = s[:n] + s[n:]
        return [s[i * c:(i + 1) * c] for i in range(len(xs))]

    state = [s_ref[g] for g in range(ng)]
    prep = [None] * nch

    pair_prep = {}

    def lora_stage(pi):
        rs_ = slice(2 * pi * c, (2 * pi + 2) * c)
        zl = zl_ref[rs_, :]
        nl = zl.shape[1]
        w_lo = zl[:, 0:D_LORA_PAD]
        a_lo = pltpu.roll(zl, nl - D_LORA, 1)[:, 0:D_LORA_PAD]
        g_lo = pltpu.roll(zl, nl - 2 * D_LORA, 1)[:, 0:D_GATE_LORA]
        w_log = -_softplus_log(-(w0_ref[...] + _dot(jnp.tanh(w_lo).astype(BF16), wb_ref[...]))) - 0.5
        lw = -jnp.exp(w_log)
        a = _sigmoid(a0_ref[...] + _dot(a_lo.astype(BF16), ab_ref[...]))
        gate = _dot(_sigmoid(g_lo).astype(BF16), gb_ref[...])
        hi, mid, lo = _split3(lw)
        cum = _dot(tri2, hi) + _dot(tri2, mid) + _dot(tri2, lo)
        pair_prep[pi] = dict(lw=lw, a=a, gate=gate, cum=cum)
        yield

    def phase_a(ci):
        rs_ = slice(ci * c, (ci + 1) * c)
        half = slice((ci % 2) * c, (ci % 2 + 1) * c)
        pp = pair_prep[ci // 2]
        lw, a, gate, cum = pp["lw"][half], pp["a"][half], pp["gate"][half], pp["cum"][half]
        tot = cum[c - 1:c, :]
        k = k_ref[rs_, :]
        kk = k * kk_ref[...]
        kk2 = kk * kk
        n2 = jnp.concatenate(seg_sums([kk2[:, sl] for sl in sls]), axis=1)
        kn = kk * lax.rsqrt(jnp.maximum(n2, 1e-24))
        bb = kn * a
        kf = k * (1.0 + (a - 1.0) * ka_ref[...])
        yield
        r = r_ref[rs_, :]
        v = v_ref[rs_, :]
        p_inv = jnp.exp(-cum)
        p_end = jnp.exp(tot - cum)
        rq = r * jnp.exp(cum)
        kap = kn * jnp.exp(cum - lw)
        bet = bb * p_inv
        kt = kf * p_inv
        lhs2 = [jnp.concatenate([kap[:, sl], rq[:, sl]], axis=0).astype(BF16) for sl in sls]
        amat = [_dot_nt(lhs2[g], jnp.concatenate([expand(bet[:, sls[g]]), expand(kt[:, sls[g]])], axis=0))
                for g in range(ng)]
        yield
        l_c = [jnp.where(strict, a[:c, :GROUP], 0.0) for a in amat]
        a_lo = [jnp.concatenate([jnp.where(strict, a[:c, GROUP:], 0.0),
                                 jnp.where(incl, a[c:, GROUP:], 0.0)], axis=0).astype(BF16) for a in amat]
        arb = [jnp.where(incl, a[c:, :GROUP], 0.0).astype(BF16) for a in amat]
        av = [_dot(a_lo[g], expand(v[:, sls[g]])) for g in range(ng)]
        x_c = [inverse_blocks4(l) for l in l_c]
        yield
        for lvl in range(3, 7):
            lmask = ((t_c >> lvl) == (s_c >> lvl)) & ((t_c >> (lvl - 1)) != (s_c >> (lvl - 1)))
            y_c = [_dot(x_c[g].astype(BF16), expand(jnp.where(lmask, l_c[g], 0.0))) for g in range(ng)]
            yield
            x_c = [x_c[g] - _dot(y_c[g].astype(BF16), expand(x_c[g])) for g in range(ng)]
            yield
        bhat_neg = -(bb * p_end)
        khat = kf * p_end
        upd_rhs = [jnp.concatenate([bhat_neg[:, sl], khat[:, sl]], axis=0).astype(BF16) for sl in sls]
        prep[ci] = dict(lhs2=lhs2, av=av, arb=arb, x=[x.astype(BF16) for x in x_c], v=v,
                        upd_rhs=upd_rhs, p_tot=jnp.exp(tot), rkk=r * kf * rk_ref[...], gate=gate)

    def phase_b(ci):
        p = prep[ci]
        rs_ = slice(ci * c, (ci + 1) * c)
        v = p["v"]
        rs = [_dot_nt(p["lhs2"][g], state[g].astype(BF16)) for g in range(ng)]
        yield
        u_c = [_dot(p["x"][g], expand(rs[g][:c] + p["av"][g][:c])) for g in range(ng)]
        yield
        o_c = [rs[g][c:] + p["av"][g][c:] - _dot(p["arb"][g], expand(u_c[g])) for g in range(ng)]
        for g in range(ng):
            upd = _dot_tn(jnp.concatenate([u_c[g], v[:, sls[g]]], axis=0).astype(BF16), p["upd_rhs"][g])
            state[g] = state[g] * p["p_tot"][:, sls[g]] + jnp.where(bd, upd, 0.0)
        yield
        sums = seg_sums(o_c + [p["rkk"][:, sl] for sl in sls])
        dev = [o_c[g] - sums[g] * (1.0 / hd) for g in range(ng)]
        yield
        var = seg_sums([dv * dv for dv in dev])
        o_parts = []
        for g in range(ng):
            o_n = dev[g] * lax.rsqrt(var[g] * (1.0 / hd) + GN_EPS) * gng_ref[:, sls[g]] + gnb_ref[:, sls[g]]
            o_parts.append(o_n + sums[ng + g] * v[:, sls[g]])
        o = jnp.concatenate(o_parts, axis=1)
        out_ref[rs_, :] = zgb_ref[rs_, :] * (o * p["gate"])

    def pair_a(pi):
        yield from lora_stage(pi)
        yield from _round_robin(phase_a(2 * pi), phase_a(2 * pi + 1))

    n_pairs = nch // 2
    _interleave(pair_a(0))
    for k in range(n_pairs):
        b_gen = _chain(phase_b(2 * k), phase_b(2 * k + 1))
        _interleave(b_gen, *([pair_a(k + 1)] if k + 1 < n_pairs else []))

    for g in range(ng):
        s_ref[g] = state[g]


def _rwkv_rec(zin, w0, wb, a0, ab, gb, k_k, k_a, rk, gng, gnb):
    t = zin.shape[0]
    d = D_MODEL
    dl = ZL_WIDTH
    br = min(REC_ROWS, t)
    zcol = lambda c: pl.BlockSpec((br, d), lambda i: (i, c))
    row = pl.BlockSpec((1, d), lambda i: (0, 0))
    return pl.pallas_call(
        _rwkv_rec_kernel,
        grid=(t // br,),
        in_specs=[zcol(4), zcol(5), zcol(6), pl.BlockSpec((br, dl), lambda i: (i, 7 * d // dl)), zcol(3),
                  row, _resident((D_LORA_PAD, d)), row, _resident((D_LORA_PAD, d)),
                  _resident((D_GATE_LORA, d)), row, row, row, row, row],
        out_specs=pl.BlockSpec((br, d), lambda i: (i, 0)),
        out_shape=jax.ShapeDtypeStruct((t, d), F32),
        scratch_shapes=[pltpu.VMEM((N_GROUPS, GROUP, GROUP), F32)],
        compiler_params=_params(("arbitrary",), VMEM_LIMIT_BIG_TILES),
        name="rwkv_rec",
    )(zin, zin, zin, zin, zin, w0, wb, a0, ab, gb, k_k, k_a, rk, gng, gnb)


def _mem_kv_kernel(mem_ref, wk_ref, wv_ref, k_ref, v_ref):
    m = mem_ref[...].astype(BF16)
    k_ref[...] = _dot(m, wk_ref[...].astype(BF16)).astype(BF16)
    v_ref[...] = _dot(m, wv_ref[...].astype(BF16)).astype(BF16)


def _mem_kv(mem, wk, wv, tn=512):
    n, d = mem.shape
    return pl.pallas_call(
        _mem_kv_kernel,
        grid=(d // tn,),
        in_specs=[pl.BlockSpec((n, d), lambda j: (0, 0)),
                  pl.BlockSpec((d, tn), lambda j: (0, j)),
                  pl.BlockSpec((d, tn), lambda j: (0, j))],
        out_specs=[pl.BlockSpec((n, tn), lambda j: (0, j))] * 2,
        out_shape=[jax.ShapeDtypeStruct((n, d), BF16)] * 2,
        compiler_params=_params(("arbitrary",)),
        name="mem_kv",
    )(mem, wk, wv)


def _mix_xattn_kernel(ya_ref, yb_ref, x_ref, gin_ref, bin_ref, wout_ref, g1_ref, b1_ref,
                      wq_ref, k_ref, v_ref, wo_ref, g_ref, b_ref, o_ref):
    y = (ya_ref[...] + yb_ref[...]).astype(BF16)
    mix = _dot(y, wout_ref[...])
    h0 = _layer_norm(x_ref[...], gin_ref[...], bin_ref[...])
    h = _layer_norm(ALPHA * h0 + mix, g1_ref[...], b1_ref[...])
    q = _dot(h.astype(BF16), wq_ref[...]).astype(BF16)
    scale = XATTN_HEAD_DIM ** -0.5
    sls = [slice(hh * XATTN_HEAD_DIM, (hh + 1) * XATTN_HEAD_DIM) for hh in range(XATTN_HEADS)]
    ss = [_dot_nt(q[:, sl], k_ref[:, sl]) * scale for sl in sls]
    es = [jnp.exp(s - jnp.max(s, axis=-1, keepdims=True)) for s in ss]
    ps = [(e / jnp.sum(e, axis=-1, keepdims=True)).astype(BF16) for e in es]
    outs = [_dot(p, v_ref[:, sl]) for p, sl in zip(ps, sls)]
    o = jnp.concatenate(outs, axis=1).astype(BF16)
    xa = _dot(o, wo_ref[...])
    o_ref[...] = _layer_norm(ALPHA * h + xa, g_ref[...], b_ref[...])


def _mix_xattn(ya, yb, x, g_in, b_in, w_out, g1, b1, wq, kmem, vmem, wo, g2, b2, tm=256):
    t, d = x.shape
    n = kmem.shape[0]
    blk = pl.BlockSpec((tm, d), lambda i: (i, 0))
    row = pl.BlockSpec((1, d), lambda i: (0, 0))
    return pl.pallas_call(
        _mix_xattn_kernel,
        grid=(t // tm,),
        in_specs=[blk, blk, blk, row, row, _resident((d, d)), row, row,
                  _resident((d, d)), _resident((n, d)), _resident((n, d)), _resident((d, d)), row, row],
        out_specs=blk,
        out_shape=jax.ShapeDtypeStruct((t, d), F32),
        compiler_params=_params(("arbitrary",)),
        name="mix_xattn",
    )(ya, yb, x, g_in, b_in, w_out, g1, b1, wq, kmem, vmem, wo, g2, b2)


def _ffn_kernel(h_ref, wg_ref, wu_ref, wd_ref, g_ref, b_ref, o_ref):
    j = pl.program_id(1)

    @pl.when(j == 0)
    def _():
        o_ref[...] = jnp.zeros_like(o_ref)

    hb = h_ref[...].astype(BF16)
    gate = _dot(hb, wg_ref[...].astype(BF16))
    silu = gate * _sigmoid(gate)
    up = _dot(hb, wu_ref[...].astype(BF16))
    act = (silu * up).astype(BF16)
    o_ref[...] += _dot(act, wd_ref[...].astype(BF16))

    @pl.when(j == pl.num_programs(1) - 1)
    def _():
        o_ref[...] = _layer_norm(ALPHA * h_ref[...] + o_ref[...], g_ref[...], b_ref[...])


def _ffn(h, wg, wu, wd, g, b, tm=1024, tf=256):
    t, d = h.shape
    f = wg.shape[1]
    tm = min(tm, t)
    blk = pl.BlockSpec((tm, d), lambda i, j: (i, 0))
    row = pl.BlockSpec((1, d), lambda i, j: (0, 0))
    return pl.pallas_call(
        _ffn_kernel,
        grid=(t // tm, f // tf),
        in_specs=[blk,
                  pl.BlockSpec((d, tf), lambda i, j: (0, j)),
                  pl.BlockSpec((d, tf), lambda i, j: (0, j)),
                  pl.BlockSpec((tf, d), lambda i, j: (j, 0)),
                  row, row],
        out_specs=blk,
        out_shape=jax.ShapeDtypeStruct((t, d), F32),
        compiler_params=_params(("arbitrary", "arbitrary"), VMEM_LIMIT_BIG_TILES),
        name="ffn",
    )(h, wg, wu, wd, g, b)


def _pad_cols(w, n):
    return jnp.pad(w, ((0, 0), (0, n - w.shape[1])))


def _pad_rows(w, n):
    return jnp.pad(w, ((0, n - w.shape[0]), (0, 0)))


def kernel(x, mem, ln_in_g, ln_in_b, w_in, conv_w, conv_b, lru_wa, lru_ba, lru_wx, lru_bx, lru_lambda, rw_mu, rw_w0, rw_wB, rw_a0, rw_aB, rw_gB, rw_kk, rw_ka, rw_rk, rw_gn_g, rw_gn_b, w_out, ln1_g, ln1_b, xa_wq, xa_wk, xa_wv, xa_wo, ln2_g, ln2_b, ffn_wg, ffn_wu, ffn_wd, ln3_g, ln3_b):
    bsz, t, d = x.shape
    depth = w_in.shape[0]
    assert bsz == 1 and d == D_MODEL and t % 512 == 0
    row = lambda p: p.reshape(1, -1)

    h = None
    for l in range(depth):
        w_t = jnp.swapaxes(w_in[l], 0, 1)
        n_cols = IN_BLOCKS * IN_TN
        mu = row(rw_mu[l])
        mu_steps = jnp.concatenate(
            [jnp.zeros((1, 2 * d), F32), mu[:, :3 * d], jnp.zeros((1, 2 * d), F32), mu[:, 3 * d:]], axis=1)
        col_params = jnp.concatenate([
            _pad_cols(mu_steps, n_cols),
            _pad_cols(conv_w[l], n_cols),
            _pad_cols(row(conv_b[l]), n_cols),
            jnp.zeros((CP_ROWS - CP_CONV_B - 1, n_cols), F32)], axis=0)
        if l == 0:
            zin = _in_proj(x[0], row(ln_in_g), row(ln_in_b), w_t, col_params)
        else:
            raise NotImplementedError("DEPTH > 1 is not part of this problem")

        ya = _lru(zin, lru_wa[l].astype(BF16), row(lru_ba[l]),
                  lru_wx[l].astype(BF16), row(lru_bx[l]), row(lru_lambda[l]))

        yb = _rwkv_rec(
            zin, row(rw_w0[l]),
            _pad_rows(rw_wB[l], D_LORA_PAD).astype(BF16), row(rw_a0[l]),
            _pad_rows(rw_aB[l], D_LORA_PAD).astype(BF16), rw_gB[l].astype(BF16),
            row(rw_kk[l]), row(rw_ka[l]), row(rw_rk[l]), row(rw_gn_g[l]), row(rw_gn_b[l]))

        kmem, vmem = _mem_kv(mem[0], xa_wk[l], xa_wv[l])
        h = _mix_xattn(ya, yb, x[0], row(ln_in_g), row(ln_in_b), w_out[l].astype(BF16),
                       row(ln1_g[l]), row(ln1_b[l]), xa_wq[l].astype(BF16), kmem, vmem,
                       xa_wo[l].astype(BF16), row(ln2_g[l]), row(ln2_b[l]))
        h = _ffn(h, ffn_wg[l], ffn_wu[l], ffn_wd[l], row(ln3_g[l]), row(ln3_b[l]))
    return h[None]
```

```python
import jax
import jax.numpy as jnp
from jax import lax
from jax.experimental import pallas as pl
from jax.experimental.pallas import tpu as pltpu

F32 = jnp.float32
BF16 = jnp.bfloat16

D_MODEL = 2048
LN_EPS = 1e-5
ALPHA = 2.0 ** 0.25

LRU_HEADS = 16
LRU_HEAD_DIM = 128
CONV_WIDTH = 4
LRU_C = 8.0

RWKV_HEAD_DIM = 64
D_LORA = 96
D_LORA_PAD = 128
D_GATE_LORA = 256
ZL_WIDTH = 512
GN_EPS = 64e-5
CHUNK = 64
GROUP = 256
N_GROUPS = D_MODEL // GROUP

XATTN_HEADS = 4
XATTN_HEAD_DIM = 512

VMEM_LIMIT = 56 * 1024 * 1024
VMEM_LIMIT_BIG_TILES = 60 * 1024 * 1024


def _dot(a, b):
    return jnp.dot(a, b, preferred_element_type=F32)


def _dot_nt(a, b):
    return lax.dot_general(a, b, (((1,), (1,)), ((), ())), preferred_element_type=F32)


def _dot_tn(a, b):
    return lax.dot_general(a, b, (((0,), (0,)), ((), ())), preferred_element_type=F32)


def _split2(x):
    hi = x.astype(BF16)
    lo = (x - hi.astype(F32)).astype(BF16)
    return hi, lo


def _split3(x):
    hi = x.astype(BF16)
    r1 = x - hi.astype(F32)
    mid = r1.astype(BF16)
    lo = (r1 - mid.astype(F32)).astype(BF16)
    return hi, mid, lo


def _sigmoid(x):
    return 1.0 / (1.0 + jnp.exp(-x))


def _softplus(x):
    return jnp.maximum(x, 0.0) + jnp.log1p(jnp.exp(-jnp.abs(x)))


def _softplus_log(x):
    return jnp.maximum(x, 0.0) + jnp.log(1.0 + jnp.exp(-jnp.abs(x)))


def _gelu_tanh(x):
    c = 0.7978845608028654
    return 0.5 * x * (1.0 + jnp.tanh(c * (x + 0.044715 * (x * x * x))))


def _layer_norm(x, g, b):
    mu = jnp.mean(x, axis=-1, keepdims=True)
    xc = x - mu
    var = jnp.mean(xc * xc, axis=-1, keepdims=True)
    return xc * lax.rsqrt(var + LN_EPS) * g + b


def _shift_rows(z, prev8, s):
    rolled = pltpu.roll(z, s, 0)
    row8 = lax.broadcasted_iota(jnp.int32, prev8.shape, 0)
    head = jnp.where(row8 < s, pltpu.roll(prev8, s, 0), rolled[:8])
    return jnp.concatenate([head, rolled[8:]], axis=0)


def _params(sem, vmem_limit=VMEM_LIMIT):
    return pltpu.CompilerParams(dimension_semantics=sem, vmem_limit_bytes=vmem_limit)


def _resident(shape):
    return pl.BlockSpec(shape, lambda *_: (0,) * len(shape), pipeline_mode=pl.Buffered(1))


IN_TN = 1024
IN_D_BLOCKS = D_MODEL // IN_TN
IN_MAIN_BLOCKS = 5 * IN_D_BLOCKS
IN_GATE_BLOCKS = 2 * IN_D_BLOCKS
IN_BLOCKS = IN_MAIN_BLOCKS + IN_GATE_BLOCKS + 1


def _in_weight_row(j, n_out):
    gates = n_out - 2 * D_MODEL + (j - IN_MAIN_BLOCKS) * IN_TN
    return jnp.where(j < IN_MAIN_BLOCKS, j * IN_TN,
                     jnp.where(j < IN_MAIN_BLOCKS + IN_GATE_BLOCKS, gates, 5 * D_MODEL))


def _in_out_block(j):
    b2 = 2 * D_MODEL // IN_TN
    return jnp.where(j < b2, j,
                     jnp.where(j < IN_MAIN_BLOCKS, j + b2,
                               jnp.where(j < IN_MAIN_BLOCKS + b2, j - (IN_MAIN_BLOCKS - b2), j)))


CP_MU, CP_CONV_W, CP_CONV_B, CP_ROWS = 0, 1, 1 + CONV_WIDTH, 8


def _in_proj_kernel(x_ref, g_ref, b_ref, w_ref, cp_ref, z_ref, hb_ref, carry_ref):
    i = pl.program_id(0)
    j = pl.program_id(1)
    nd = IN_D_BLOCKS

    @pl.when(j == 0)
    def _():
        hb_ref[...] = _layer_norm(x_ref[...], g_ref[...], b_ref[...]).astype(BF16)

    @pl.when((i == 0) & (j == 0))
    def _():
        carry_ref[...] = jnp.zeros_like(carry_ref)

    def keep_tail(z):
        carry_ref[j] = z[z.shape[0] - 8:, :]

    @pl.when(j < nd)
    def _():
        z = _dot_nt(hb_ref[...], w_ref[...].astype(BF16))
        prev8 = carry_ref[j]
        cw = CP_CONV_W + CONV_WIDTH - 1
        conv = cp_ref[CP_CONV_B:CP_CONV_B + 1, :] + z * cp_ref[cw:cw + 1, :]
        for s in range(1, CONV_WIDTH):
            conv = conv + _shift_rows(z, prev8, s) * cp_ref[cw - s:cw - s + 1, :]
        keep_tail(z)
        z_ref[...] = conv

    @pl.when((j >= nd) & (j < 2 * nd))
    def _():
        z_ref[...] = _gelu_tanh(_dot_nt(hb_ref[...], w_ref[...].astype(BF16)))

    def token_shift(z):
        n = z.shape[1]
        zp = _shift_rows(z, carry_ref[j][:, 0:n], 1)
        carry_ref[j, :, 0:n] = z[z.shape[0] - 8:, :]
        return z + (zp - z) * cp_ref[CP_MU:CP_MU + 1, 0:n]

    @pl.when((j >= 2 * nd) & (j < IN_MAIN_BLOCKS))
    def _():
        z_ref[...] = token_shift(_dot_nt(hb_ref[...], w_ref[...].astype(BF16)))

    @pl.when((j >= IN_MAIN_BLOCKS) & (j < IN_MAIN_BLOCKS + IN_GATE_BLOCKS))
    def _():
        z_ref[...] = _sigmoid(_dot_nt(hb_ref[...], w_ref[...].astype(BF16)))

    @pl.when(j >= IN_MAIN_BLOCKS + IN_GATE_BLOCKS)
    def _():
        z_ref[:, 0:ZL_WIDTH] = token_shift(_dot_nt(hb_ref[...], w_ref[0:ZL_WIDTH, :].astype(BF16)))
        z_ref[:, ZL_WIDTH:] = jnp.zeros((z_ref.shape[0], z_ref.shape[1] - ZL_WIDTH), F32)


def _in_proj(x, g, b, w_t, col_params, tm=1024):
    t, d = x.shape
    tn = IN_TN
    nb = IN_BLOCKS
    n_out = w_t.shape[0]
    tm = min(tm, t)
    return pl.pallas_call(
        _in_proj_kernel,
        grid=(t // tm, nb),
        in_specs=[
            pl.BlockSpec((tm, d), lambda i, j: (i, 0)),
            pl.BlockSpec((1, d), lambda i, j: (0, 0)),
            pl.BlockSpec((1, d), lambda i, j: (0, 0)),
            pl.BlockSpec((pl.Element(tn), pl.Element(d)),
                         lambda i, j: (pl.multiple_of(_in_weight_row(j, n_out), 8), 0)),
            pl.BlockSpec((CP_ROWS, tn), lambda i, j: (0, j)),
        ],
        out_specs=pl.BlockSpec((tm, tn), lambda i, j: (i, _in_out_block(j))),
        out_shape=jax.ShapeDtypeStruct((t, nb * tn), F32),
        scratch_shapes=[pltpu.VMEM((tm, d), BF16), pltpu.VMEM((nb, 8, tn), F32)],
        compiler_params=_params(("arbitrary", "arbitrary")),
        name="in_proj",
    )(x, g, b, w_t, col_params)


def _lru_kernel(u_ref, gate_ref, ga_ref, wa_ref, ba_ref, wx_ref, bx_ref,
                lam_ref, out_ref, hcarry):
    @pl.when(pl.program_id(0) == 0)
    def _():
        hcarry[...] = jnp.zeros_like(hcarry)

    sp_lam = _softplus(-lam_ref[...])
    carry = hcarry[...]
    for ci in range(u_ref.shape[0] // LRU_ROWS):
        rs = slice(ci * LRU_ROWS, (ci + 1) * LRU_ROWS)
        conv = u_ref[rs, :]
        cb16 = conv.astype(BF16)
        r_parts, i_parts = [], []
        for g in range(LRU_HEADS):
            ug = cb16[:, g * LRU_HEAD_DIM:(g + 1) * LRU_HEAD_DIM]
            r_parts.append(_dot(ug, wa_ref[g]))
            i_parts.append(_dot(ug, wx_ref[g]))
        r = _sigmoid(jnp.concatenate(r_parts, axis=1) + ba_ref[...])
        ig = _sigmoid(jnp.concatenate(i_parts, axis=1) + bx_ref[...])

        log_a = (-LRU_C) * r * sp_lam
        a_c = jnp.exp(log_a)
        om = -jnp.tanh(log_a) * (a_c * a_c + 1.0)
        b_c = jnp.where(om > 0.0, om * lax.rsqrt(om), 0.0) * (ig * conv)

        n8 = LRU_ROWS // 8
        a3 = a_c.reshape(n8, 8, a_c.shape[1])
        b3 = b_c.reshape(n8, 8, b_c.shape[1])
        sub = lax.broadcasted_iota(jnp.int32, a3.shape, 1)
        for d in (1, 2, 4):
            m = sub >= d
            a_sh = jnp.where(m, pltpu.roll(a3, d, 1), 1.0)
            b_sh = jnp.where(m, pltpu.roll(b3, d, 1), 0.0)
            b3 = a3 * b_sh + b3
            a3 = a3 * a_sh
        hs = []
        for i in range(n8):
            h_i = b3[i] + a3[i] * carry
            hs.append(h_i)
            carry = h_i[7:8, :]
        out_ref[rs, :] = ga_ref[rs, :] * (gate_ref[rs, :] * jnp.concatenate(hs, axis=0))
    hcarry[...] = carry


LRU_ROWS = 64


def _lru(zin, wa, ba, wx, bx, lam, tm=512):
    t = zin.shape[0]
    d = D_MODEL
    row = lambda i: (0, 0)
    return pl.pallas_call(
        _lru_kernel,
        grid=(t // tm,),
        in_specs=[
            pl.BlockSpec((tm, d), lambda i: (i, 0)),
            pl.BlockSpec((tm, d), lambda i: (i, 1)),
            pl.BlockSpec((tm, d), lambda i: (i, 2)),
            pl.BlockSpec((LRU_HEADS, LRU_HEAD_DIM, LRU_HEAD_DIM), lambda i: (0, 0, 0)),
            pl.BlockSpec((1, d), row),
            pl.BlockSpec((LRU_HEADS, LRU_HEAD_DIM, LRU_HEAD_DIM), lambda i: (0, 0, 0)),
            pl.BlockSpec((1, d), row),
            pl.BlockSpec((1, d), row),
        ],
        out_specs=pl.BlockSpec((tm, d), lambda i: (i, 0)),
        out_shape=jax.ShapeDtypeStruct((t, d), F32),
        scratch_shapes=[pltpu.VMEM((1, d), F32)],
        compiler_params=_params(("arbitrary",)),
        name="lru",
    )(zin, zin, zin, wa, ba, wx, bx, lam)


def _block_ones(n, seg):
    r = lax.broadcasted_iota(jnp.int32, (n, n), 0) // seg
    c = lax.broadcasted_iota(jnp.int32, (n, n), 1) // seg
    return jnp.where(r == c, 1.0, 0.0).astype(BF16)


REC_ROWS = 256


def _round_robin(*gens):
    live = list(gens)
    while live:
        for gen in list(live):
            try:
                next(gen)
            except StopIteration:
                live.remove(gen)
        yield


def _interleave(*gens):
    for _ in _round_robin(*gens):
        pass


def _chain(*gens):
    for gen in gens:
        yield from gen


def _rwkv_rec_kernel(r_ref, k_ref, v_ref, zl_ref, zgb_ref, w0_ref, wb_ref, a0_ref, ab_ref, gb_ref,
                     kk_ref, ka_ref, rk_ref, gng_ref, gnb_ref, out_ref, s_ref):
    @pl.when(pl.program_id(0) == 0)
    def _():
        s_ref[...] = jnp.zeros_like(s_ref)

    c = CHUNK
    hd = RWKV_HEAD_DIM
    ng = N_GROUPS
    nch = r_ref.shape[0] // c
    sls = [slice(g * GROUP, (g + 1) * GROUP) for g in range(ng)]

    assert nch % 2 == 0
    rows = lax.broadcasted_iota(jnp.int32, (2 * c, 2 * c), 0)
    cols = lax.broadcasted_iota(jnp.int32, (2 * c, 2 * c), 1)
    tri2 = jnp.where((cols <= rows) & (cols // c == rows // c), 1.0, 0.0).astype(BF16)
    t_c = lax.broadcasted_iota(jnp.int32, (c, GROUP), 0)
    s_c = lax.broadcasted_iota(jnp.int32, (c, GROUP), 1) % hd
    strict = s_c < t_c
    incl = s_c <= t_c
    eye_c = jnp.where(s_c == t_c, 1.0, 0.0)
    bd = (lax.broadcasted_iota(jnp.int32, (GROUP, GROUP), 0) // hd
          == lax.broadcasted_iota(jnp.int32, (GROUP, GROUP), 1) // hd)
    ones_bd = _block_ones(GROUP, hd)

    blk4 = (t_c >> 2) == (s_c >> 2)
    diag3 = (t_c - s_c) == 3

    def inverse_blocks4(l):
        m = jnp.where(blk4, l, 0.0)
        r1 = pltpu.roll(m, GROUP - 1, 1)
        r2 = pltpu.roll(m, GROUP - 2, 1)
        d1 = pltpu.roll(m, 1, 0)
        d2 = pltpu.roll(m, 2, 0)
        d1r1 = pltpu.roll(d1, GROUP - 1, 1)
        corr = jnp.where(diag3, r2 * d1 + r1 * d2 - r2 * d1r1 * d2, r1 * d1)
        return eye_c - jnp.where(blk4, m - corr, 0.0)

    def expand(x_c):
        return jnp.where(bd, jnp.concatenate([x_c] * (GROUP // hd), axis=0), 0.0).astype(BF16)

    def seg_sums(xs):
        hi, lo = _split2(jnp.concatenate(xs, axis=0))
        s = _dot(jnp.concatenate([hi, lo], axis=0), ones_bd)
        n = c * len(xs)
        s = s[:n] + s[n:]
        return [s[i * c:(i + 1) * c] for i in range(len(xs))]

    state = [s_ref[g] for g in range(ng)]
    prep = [None] * nch

    pair_prep = {}

    def lora_stage(pi):
        rs_ = slice(2 * pi * c, (2 * pi + 2) * c)
        zl = zl_ref[rs_, :]
        nl = zl.shape[1]
        w_lo = zl[:, 0:D_LORA_PAD]
        a_lo = pltpu.roll(zl, nl - D_LORA, 1)[:, 0:D_LORA_PAD]
        g_lo = pltpu.roll(zl, nl - 2 * D_LORA, 1)[:, 0:D_GATE_LORA]
        w_log = -_softplus_log(-(w0_ref[...] + _dot(jnp.tanh(w_lo).astype(BF16), wb_ref[...]))) - 0.5
        lw = -jnp.exp(w_log)
        a = _sigmoid(a0_ref[...] + _dot(a_lo.astype(BF16), ab_ref[...]))
        gate = _dot(_sigmoid(g_lo).astype(BF16), gb_ref[...])
        hi, mid, lo = _split3(lw)
        cum = _dot(tri2, hi) + _dot(tri2, mid) + _dot(tri2, lo)
        pair_prep[pi] = dict(lw=lw, a=a, gate=gate, cum=cum)
        yield

    def phase_a(ci):
        rs_ = slice(ci * c, (ci + 1) * c)
        half = slice((ci % 2) * c, (ci % 2 + 1) * c)
        pp = pair_prep[ci // 2]
        lw, a, gate, cum = pp["lw"][half], pp["a"][half], pp["gate"][half], pp["cum"][half]
        tot = cum[c - 1:c, :]
        k = k_ref[rs_, :]
        kk = k * kk_ref[...]
        kk2 = kk * kk
        n2 = jnp.concatenate(seg_sums([kk2[:, sl] for sl in sls]), axis=1)
        kn = kk * lax.rsqrt(jnp.maximum(n2, 1e-24))
        bb = kn * a
        kf = k * (1.0 + (a - 1.0) * ka_ref[...])
        yield
        r = r_ref[rs_, :]
        v = v_ref[rs_, :]
        p_inv = jnp.exp(-cum)
        p_end = jnp.exp(tot - cum)
        rq = r * jnp.exp(cum)
        kap = kn * jnp.exp(cum - lw)
        bet = bb * p_inv
        kt = kf * p_inv
        lhs2 = [jnp.concatenate([kap[:, sl], rq[:, sl]], axis=0).astype(BF16) for sl in sls]
        amat = [_dot_nt(lhs2[g], jnp.concatenate([expand(bet[:, sls[g]]), expand(kt[:, sls[g]])], axis=0))
                for g in range(ng)]
        yield
        l_c = [jnp.where(strict, a[:c, :GROUP], 0.0) for a in amat]
        a_lo = [jnp.concatenate([jnp.where(strict, a[:c, GROUP:], 0.0),
                                 jnp.where(incl, a[c:, GROUP:], 0.0)], axis=0).astype(BF16) for a in amat]
        arb = [jnp.where(incl, a[c:, :GROUP], 0.0).astype(BF16) for a in amat]
        av = [_dot(a_lo[g], expand(v[:, sls[g]])) for g in range(ng)]
        x_c = [inverse_blocks4(l) for l in l_c]
        yield
        for lvl in range(3, 7):
            lmask = ((t_c >> lvl) == (s_c >> lvl)) & ((t_c >> (lvl - 1)) != (s_c >> (lvl - 1)))
            y_c = [_dot(x_c[g].astype(BF16), expand(jnp.where(lmask, l_c[g], 0.0))) for g in range(ng)]
            yield
            x_c = [x_c[g] - _dot(y_c[g].astype(BF16), expand(x_c[g])) for g in range(ng)]
            yield
        bhat_neg = -(bb * p_end)
        khat = kf * p_end
        upd_rhs = [jnp.concatenate([bhat_neg[:, sl], khat[:, sl]], axis=0).astype(BF16) for sl in sls]
        prep[ci] = dict(lhs2=lhs2, av=av, arb=arb, x=[x.astype(BF16) for x in x_c], v=v,
                        upd_rhs=upd_rhs, p_tot=jnp.exp(tot), rkk=r * kf * rk_ref[...], gate=gate)

    def phase_b(ci):
        p = prep[ci]
        rs_ = slice(ci * c, (ci + 1) * c)
        v = p["v"]
        rs = [_dot_nt(p["lhs2"][g], state[g].astype(BF16)) for g in range(ng)]
        yield
        u_c = [_dot(p["x"][g], expand(rs[g][:c] + p["av"][g][:c])) for g in range(ng)]
        yield
        o_c = [rs[g][c:] + p["av"][g][c:] - _dot(p["arb"][g], expand(u_c[g])) for g in range(ng)]
        for g in range(ng):
            upd = _dot_tn(jnp.concatenate([u_c[g], v[:, sls[g]]], axis=0).astype(BF16), p["upd_rhs"][g])
            state[g] = state[g] * p["p_tot"][:, sls[g]] + jnp.where(bd, upd, 0.0)
        yield
        sums = seg_sums(o_c + [p["rkk"][:, sl] for sl in sls])
        dev = [o_c[g] - sums[g] * (1.0 / hd) for g in range(ng)]
        yield
        var = seg_sums([dv * dv for dv in dev])
        o_parts = []
        for g in range(ng):
            o_n = dev[g] * lax.rsqrt(var[g] * (1.0 / hd) + GN_EPS) * gng_ref[:, sls[g]] + gnb_ref[:, sls[g]]
            o_parts.append(o_n + sums[ng + g] * v[:, sls[g]])
        o = jnp.concatenate(o_parts, axis=1)
        out_ref[rs_, :] = zgb_ref[rs_, :] * (o * p["gate"])

    def pair_a(pi):
        yield from lora_stage(pi)
        yield from _round_robin(phase_a(2 * pi), phase_a(2 * pi + 1))

    n_pairs = nch // 2
    _interleave(pair_a(0))
    for k in range(n_pairs):
        b_gen = _chain(phase_b(2 * k), phase_b(2 * k + 1))
        _interleave(b_gen, *([pair_a(k + 1)] if k + 1 < n_pairs else []))

    for g in range(ng):
        s_ref[g] = state[g]


def _rwkv_rec(zin, w0, wb, a0, ab, gb, k_k, k_a, rk, gng, gnb):
    t = zin.shape[0]
    d = D_MODEL
    dl = ZL_WIDTH
    br = min(REC_ROWS, t)
    zcol = lambda c: pl.BlockSpec((br, d), lambda i: (i, c))
    row = pl.BlockSpec((1, d), lambda i: (0, 0))
    return pl.pallas_call(
        _rwkv_rec_kernel,
        grid=(t // br,),
        in_specs=[zcol(4), zcol(5), zcol(6), pl.BlockSpec((br, dl), lambda i: (i, 7 * d // dl)), zcol(3),
                  row, _resident((D_LORA_PAD, d)), row, _resident((D_LORA_PAD, d)),
                  _resident((D_GATE_LORA, d)), row, row, row, row, row],
        out_specs=pl.BlockSpec((br, d), lambda i: (i, 0)),
        out_shape=jax.ShapeDtypeStruct((t, d), F32),
        scratch_shapes=[pltpu.VMEM((N_GROUPS, GROUP, GROUP), F32)],
        compiler_params=_params(("arbitrary",), VMEM_LIMIT_BIG_TILES),
        name="rwkv_rec",
    )(zin, zin, zin, zin, zin, w0, wb, a0, ab, gb, k_k, k_a, rk, gng, gnb)


def _mem_kv_kernel(mem_ref, wk_ref, wv_ref, k_ref, v_ref):
    m = mem_ref[...].astype(BF16)
    k_ref[...] = _dot(m, wk_ref[...].astype(BF16)).astype(BF16)
    v_ref[...] = _dot(m, wv_ref[...].astype(BF16)).astype(BF16)


def _mem_kv(mem, wk, wv, tn=512):
    n, d = mem.shape
    return pl.pallas_call(
        _mem_kv_kernel,
        grid=(d // tn,),
        in_specs=[pl.BlockSpec((n, d), lambda j: (0, 0)),
                  pl.BlockSpec((d, tn), lambda j: (0, j)),
                  pl.BlockSpec((d, tn), lambda j: (0, j))],
        out_specs=[pl.BlockSpec((n, tn), lambda j: (0, j))] * 2,
        out_shape=[jax.ShapeDtypeStruct((n, d), BF16)] * 2,
        compiler_params=_params(("arbitrary",)),
        name="mem_kv",
    )(mem, wk, wv)


def _mix_xattn_kernel(ya_ref, yb_ref, x_ref, gin_ref, bin_ref, wout_ref, g1_ref, b1_ref,
                      wq_ref, k_ref, v_ref, wo_ref, g_ref, b_ref, o_ref):
    y = (ya_ref[...] + yb_ref[...]).astype(BF16)
    mix = _dot(y, wout_ref[...])
    h0 = _layer_norm(x_ref[...], gin_ref[...], bin_ref[...])
    h = _layer_norm(ALPHA * h0 + mix, g1_ref[...], b1_ref[...])
    q = _dot(h.astype(BF16), wq_ref[...]).astype(BF16)
    scale = XATTN_HEAD_DIM ** -0.5
    sls = [slice(hh * XATTN_HEAD_DIM, (hh + 1) * XATTN_HEAD_DIM) for hh in range(XATTN_HEADS)]
    ss = [_dot_nt(q[:, sl], k_ref[:, sl]) * scale for sl in sls]
    es = [jnp.exp(s - jnp.max(s, axis=-1, keepdims=True)) for s in ss]
    ps = [(e / jnp.sum(e, axis=-1, keepdims=True)).astype(BF16) for e in es]
    outs = [_dot(p, v_ref[:, sl]) for p, sl in zip(ps, sls)]
    o = jnp.concatenate(outs, axis=1).astype(BF16)
    xa = _dot(o, wo_ref[...])
    o_ref[...] = _layer_norm(ALPHA * h + xa, g_ref[...], b_ref[...])


def _mix_xattn(ya, yb, x, g_in, b_in, w_out, g1, b1, wq, kmem, vmem, wo, g2, b2, tm=256):
    t, d = x.shape
    n = kmem.shape[0]
    blk = pl.BlockSpec((tm, d), lambda i: (i, 0))
    row = pl.BlockSpec((1, d), lambda i: (0, 0))
    return pl.pallas_call(
        _mix_xattn_kernel,
        grid=(t // tm,),
        in_specs=[blk, blk, blk, row, row, _resident((d, d)), row, row,
                  _resident((d, d)), _resident((n, d)), _resident((n, d)), _resident((d, d)), row, row],
        out_specs=blk,
        out_shape=jax.ShapeDtypeStruct((t, d), F32),
        compiler_params=_params(("arbitrary",)),
        name="mix_xattn",
    )(ya, yb, x, g_in, b_in, w_out, g1, b1, wq, kmem, vmem, wo, g2, b2)


def _ffn_kernel(h_ref, wg_ref, wu_ref, wd_ref, g_ref, b_ref, o_ref):
    j = pl.program_id(1)

    @pl.when(j == 0)
    def _():
        o_ref[...] = jnp.zeros_like(o_ref)

    hb = h_ref[...].astype(BF16)
    gate = _dot(hb, wg_ref[...].astype(BF16))
    up = _dot(hb, wu_ref[...].astype(BF16))
    act = (gate * _sigmoid(gate) * up).astype(BF16)
    o_ref[...] += _dot(act, wd_ref[...].astype(BF16))

    @pl.when(j == pl.num_programs(1) - 1)
    def _():
        o_ref[...] = _layer_norm(ALPHA * h_ref[...] + o_ref[...], g_ref[...], b_ref[...])


def _ffn(h, wg, wu, wd, g, b, tm=1024, tf=256):
    t, d = h.shape
    f = wg.shape[1]
    tm = min(tm, t)
    blk = pl.BlockSpec((tm, d), lambda i, j: (i, 0))
    row = pl.BlockSpec((1, d), lambda i, j: (0, 0))
    return pl.pallas_call(
        _ffn_kernel,
        grid=(t // tm, f // tf),
        in_specs=[blk,
                  pl.BlockSpec((d, tf), lambda i, j: (0, j)),
                  pl.BlockSpec((d, tf), lambda i, j: (0, j)),
                  pl.BlockSpec((tf, d), lambda i, j: (j, 0)),
                  row, row],
        out_specs=blk,
        out_shape=jax.ShapeDtypeStruct((t, d), F32),
        compiler_params=_params(("arbitrary", "arbitrary"), VMEM_LIMIT_BIG_TILES),
        name="ffn",
    )(h, wg, wu, wd, g, b)


def _pad_cols(w, n):
    return jnp.pad(w, ((0, 0), (0, n - w.shape[1])))


def _pad_rows(w, n):
    return jnp.pad(w, ((0, n - w.shape[0]), (0, 0)))


def kernel(x, mem, ln_in_g, ln_in_b, w_in, conv_w, conv_b, lru_wa, lru_ba, lru_wx, lru_bx, lru_lambda, rw_mu, rw_w0, rw_wB, rw_a0, rw_aB, rw_gB, rw_kk, rw_ka, rw_rk, rw_gn_g, rw_gn_b, w_out, ln1_g, ln1_b, xa_wq, xa_wk, xa_wv, xa_wo, ln2_g, ln2_b, ffn_wg, ffn_wu, ffn_wd, ln3_g, ln3_b):
    bsz, t, d = x.shape
    depth = w_in.shape[0]
    assert bsz == 1 and d == D_MODEL and t % 512 == 0
    row = lambda p: p.reshape(1, -1)

    h = None
    for l in range(depth):
        w_t = jnp.swapaxes(w_in[l], 0, 1)
        n_cols = IN_BLOCKS * IN_TN
        mu = row(rw_mu[l])
        mu_steps = jnp.concatenate(
            [jnp.zeros((1, 2 * d), F32), mu[:, :3 * d], jnp.zeros((1, 2 * d), F32), mu[:, 3 * d:]], axis=1)
        col_params = jnp.concatenate([
            _pad_cols(mu_steps, n_cols),
            _pad_cols(conv_w[l], n_cols),
            _pad_cols(row(conv_b[l]), n_cols),
            jnp.zeros((CP_ROWS - CP_CONV_B - 1, n_cols), F32)], axis=0)
        if l == 0:
            zin = _in_proj(x[0], row(ln_in_g), row(ln_in_b), w_t, col_params)
        else:
            raise NotImplementedError("DEPTH > 1 is not part of this problem")

        ya = _lru(zin, lru_wa[l].astype(BF16), row(lru_ba[l]),
                  lru_wx[l].astype(BF16), row(lru_bx[l]), row(lru_lambda[l]))

        yb = _rwkv_rec(
            zin, row(rw_w0[l]),
            _pad_rows(rw_wB[l], D_LORA_PAD).astype(BF16), row(rw_a0[l]),
            _pad_rows(rw_aB[l], D_LORA_PAD).astype(BF16), rw_gB[l].astype(BF16),
            row(rw_kk[l]), row(rw_ka[l]), row(rw_rk[l]), row(rw_gn_g[l]), row(rw_gn_b[l]))

        kmem, vmem = _mem_kv(mem[0], xa_wk[l], xa_wv[l])
        h = _mix_xattn(ya, yb, x[0], row(ln_in_g), row(ln_in_b), w_out[l].astype(BF16),
                       row(ln1_g[l]), row(ln1_b[l]), xa_wq[l].astype(BF16), kmem, vmem,
                       xa_wo[l].astype(BF16), row(ln2_g[l]), row(ln2_b[l]))
        h = _ffn(h, ffn_wg[l], ffn_wu[l], ffn_wd[l], row(ln3_g[l]), row(ln3_b[l]))
    return h[None]
```
